```python
import jax, jax.numpy as jnp
from jax import lax
import numpy as np

D_MODEL = 1024
BATCH = 32
SEQ = 2048
DEPTH = 2

CTX_LEN = 256
GRID_W = 64
NORM_EPS = 1e-6
POOL_WIDTH = 1024
POOL_WINDOWS = (2, 4, 8, 16)
POOL_GROUP = POOL_WIDTH // len(POOL_WINDOWS)
GLA_HEADS = 4
GLA_DK = 128
GLA_DV = 256
GLA_GATE_RANK = 16
GLA_TAU = 16.0
GLA_CHUNK = 64
MLA_HEADS = 8
MLA_Q_RANK = 384
MLA_KV_RANK = 256
MLA_NOPE = 128
MLA_ROPE = 64
MLA_V = 128
ROPE_BASE = 10000.0
Q_BLOCK = 128
N_BRANCH = 3
BRANCH_WIDTH = 1024
PEER_HEADS = 8
PEER_N_KEYS = 128
PEER_N_EXPERTS = PEER_N_KEYS * PEER_N_KEYS
PEER_TOPK = 16
PEER_QDIM = 256
PEER_HALF = PEER_QDIM // 2
PEER_BLOCK = 128
IN_SPLITS = (POOL_WIDTH, GLA_HEADS * GLA_DK, GLA_HEADS * GLA_DK, GLA_HEADS * GLA_DV, GLA_HEADS * GLA_DV,
             GLA_GATE_RANK, GLA_GATE_RANK, MLA_Q_RANK, MLA_KV_RANK, MLA_ROPE, N_BRANCH * D_MODEL)
IN_COLS = sum(IN_SPLITS)

kernel_name = "hybrid_pool_gla_mla_peer_diffusion_block"

F32 = jnp.float32


def rmsnorm(x, g):
    xf = x.astype(F32)
    y = xf * lax.rsqrt(jnp.mean(xf * xf, axis=-1, keepdims=True) + NORM_EPS)
    return (y * g.astype(F32)).astype(x.dtype)


def split_cols(z):
    idx, acc = [], 0
    for s in IN_SPLITS[:-1]:
        acc += s
        idx.append(acc)
    return jnp.split(z, idx, axis=-1)


def flip(a):
    return jnp.flip(a, axis=1)


def axial_angles(T):
    n_rows = T // GRID_W
    row = jnp.repeat(jnp.arange(n_rows), GRID_W).astype(F32)
    col = jnp.tile(jnp.arange(GRID_W), n_rows).astype(F32)
    half = MLA_ROPE // 2
    inv = ROPE_BASE ** (-jnp.arange(0, half, 2, dtype=F32) / half)
    return row[:, None] * inv, col[:, None] * inv


def rotate(x, ang):
    m = x.shape[-1] // 2
    x1, x2 = x[..., :m], x[..., m:]
    cos, sin = jnp.cos(ang).astype(x.dtype), jnp.sin(ang).astype(x.dtype)
    return jnp.concatenate([x1 * cos - x2 * sin, x1 * sin + x2 * cos], axis=-1)


def axial_rope(x, ang_row, ang_col):
    h = x.shape[-1] // 2
    return jnp.concatenate([rotate(x[..., :h], ang_row), rotate(x[..., h:], ang_col)], axis=-1)


def multiscale_pool(u, w, scale):
    T = u.shape[1]
    uf = u.astype(F32)
    cs = jnp.concatenate([jnp.zeros_like(uf[:, :1]), jnp.cumsum(uf, axis=1)], axis=1)
    t = jnp.arange(T)
    outs = []
    for gi, win in enumerate(POOL_WINDOWS):
        lo = jnp.clip(t - win // 2, 0, T)
        hi = jnp.clip(t + win - win // 2, 0, T)
        sl = slice(gi * POOL_GROUP, (gi + 1) * POOL_GROUP)
        csg = cs[..., sl]
        mean = (csg[:, hi] - csg[:, lo]) / (hi - lo).astype(F32)[:, None]
        diff = (mean - uf[..., sl]).astype(u.dtype)
        outs.append(diff @ w[gi])
    return jnp.concatenate(outs, axis=-1) * scale


def gla_chunk_scan(q, k, v, g, s0):
    Bn, T, H, _ = q.shape
    dv = v.shape[-1]
    n = T // GLA_CHUNK

    def to_chunks(a):
        return a.astype(F32).reshape(Bn, n, GLA_CHUNK, H, a.shape[-1]).transpose(1, 0, 3, 2, 4)

    mask = jnp.tril(jnp.ones((GLA_CHUNK, GLA_CHUNK), dtype=bool))

    def step(S, inp):
        qc, kc, vc, gc = inp
        b = jnp.cumsum(gc, axis=2)
        b_last = b[:, :, -1:, :]
        qe = qc * jnp.exp(b)
        ke = kc * jnp.exp(-b)
        kd = kc * jnp.exp(b_last - b)
        att = jnp.where(mask, jnp.einsum('bhtd,bhsd->bhts', qe, ke), 0.0)
        o = jnp.einsum('bhts,bhsv->bhtv', att, vc) + jnp.einsum('bhtd,bhdv->bhtv', qe, S)
        S = jnp.exp(b_last[:, :, 0, :])[..., None] * S + jnp.einsum('bhsd,bhsv->bhdv', kd, vc)
        return S, o

    S, o = lax.scan(step, s0, (to_chunks(q), to_chunks(k), to_chunks(v), to_chunks(g)))
    o = o.transpose(1, 0, 3, 2, 4).reshape(Bn, T, H, dv)
    return o.astype(q.dtype), S


def gla_inputs(q, k, v, lr_f, lr_b, gate_w, gate_b):
    shp = q.shape[:2]
    q = q.reshape(shp + (GLA_HEADS, GLA_DK)) * GLA_DK ** -0.5
    k = k.reshape(shp + (GLA_HEADS, GLA_DK))
    v = v.reshape(shp + (GLA_HEADS, GLA_DV))

    def log_gate(lr, d):
        z = (lr @ gate_w[d] + gate_b[d]).astype(F32)
        return (jax.nn.log_sigmoid(z) / GLA_TAU).reshape(shp + (GLA_HEADS, GLA_DK))

    return q, k, v, log_gate(lr_f, 0), log_gate(lr_b, 1)


def gla_finish(o, r, g_norm):
    shp = o.shape[:2]
    return rmsnorm(o, g_norm).reshape(shp + (GLA_HEADS * GLA_DV,)) * jax.nn.silu(r)


def mla_inputs(q_dn, kv_dn, k_rope, q_g, kv_g, w_uq, w_ukv, angles):
    shp = q_dn.shape[:2]
    q = (rmsnorm(q_dn, q_g) @ w_uq).reshape(shp + (MLA_HEADS, MLA_NOPE + MLA_ROPE))
    kv = (rmsnorm(kv_dn, kv_g) @ w_ukv).reshape(shp + (MLA_HEADS, MLA_NOPE + MLA_V))
    qn, qr = q[..., :MLA_NOPE], q[..., MLA_NOPE:]
    kn, vv = kv[..., :MLA_NOPE], kv[..., MLA_NOPE:]
    if angles is not None:
        ar, ac = angles
        qr = axial_rope(qr, ar[:, None, :], ac[:, None, :])
        k_rope = axial_rope(k_rope, ar, ac)
    return qn, qr, kn, k_rope, vv


def block_attention(qn, qr, kn, kr, v):
    Bn, T, H, dn = qn.shape
    nb = T // Q_BLOCK
    scale = (MLA_NOPE + MLA_ROPE) ** -0.5
    qn_b = qn.reshape(Bn, nb, Q_BLOCK, H, dn).swapaxes(0, 1)
    qr_b = qr.reshape(Bn, nb, Q_BLOCK, H, qr.shape[-1]).swapaxes(0, 1)

    def one(args):
        qnb, qrb = args
        s = jnp.einsum('bqhd,bkhd->bhqk', qnb, kn) + jnp.einsum('bqhr,bkr->bhqk', qrb, kr)
        p = jax.nn.softmax(s.astype(F32) * scale, axis=-1).astype(v.dtype)
        return jnp.einsum('bhqk,bkhd->bqhd', p, v)

    o = lax.map(one, (qn_b, qr_b))
    return o.swapaxes(0, 1).reshape(Bn, T, H * v.shape[-1])


def merge(pool_o, gla_o, mla_o, gz, branch_w, w_out):
    shp = gz.shape[:2]
    gates = jax.nn.sigmoid(gz.astype(F32)).astype(gz.dtype).reshape(shp + (N_BRANCH, D_MODEL))
    m = (gates[..., 0, :] * (pool_o @ branch_w[0]) + gates[..., 1, :] * (gla_o @ branch_w[1])
         + gates[..., 2, :] * (mla_o @ branch_w[2]))
    return m @ w_out


def token_mixer(h, hc, w_in, pool_w, pool_scale, gate_w, gate_b, gla_g, q_g, kv_g, w_uq, w_ukv,
                branch_w, w_out, with_ctx):
    Bn, T, _ = h.shape
    angles = axial_angles(T)
    (p_in, q, k, v, r, lr_f, lr_b, q_dn, kv_dn, k_rope, gz) = split_cols(h @ w_in)
    (p_c, q_c, k_c, v_c, r_c, lr_fc, lr_bc, q_dnc, kv_dnc, k_ropec, gzc) = split_cols(hc @ w_in)

    gq, gk, gv, gf, gb = gla_inputs(q, k, v, lr_f, lr_b, gate_w, gate_b)
    cq, ck, cv, cf, cb = gla_inputs(q_c, k_c, v_c, lr_fc, lr_bc, gate_w, gate_b)
    s0 = jnp.zeros((Bn, GLA_HEADS, GLA_DK, GLA_DV), F32)
    oc_f, s_f = gla_chunk_scan(cq, ck, cv, cf, s0)
    oc_b, s_b = gla_chunk_scan(flip(cq), flip(ck), flip(cv), flip(cb), s0)
    o_f, _ = gla_chunk_scan(gq, gk, gv, gf, s_f)
    o_b, _ = gla_chunk_scan(flip(gq), flip(gk), flip(gv), flip(gb), s_b)
    gla_lat = gla_finish(o_f + flip(o_b), r, gla_g)

    qn, qr, kn, kr, vv = mla_inputs(q_dn, kv_dn, k_rope, q_g, kv_g, w_uq, w_ukv, angles)
    cqn, cqr, ckn, ckr, cvv = mla_inputs(q_dnc, kv_dnc, k_ropec, q_g, kv_g, w_uq, w_ukv, None)
    mla_lat = block_attention(qn, qr, jnp.concatenate([ckn, kn], axis=1),
                              jnp.concatenate([ckr, kr], axis=1), jnp.concatenate([cvv, vv], axis=1))

    pool_lat = multiscale_pool(p_in, pool_w, pool_scale)
    y = merge(pool_lat, gla_lat, mla_lat, gz, branch_w, w_out)
    if not with_ctx:
        return y, None
    gla_ctx = gla_finish(oc_f + flip(oc_b), r_c, gla_g)
    mla_ctx = block_attention(cqn, cqr, ckn, ckr, cvv)
    pool_ctx = multiscale_pool(p_c, pool_w, pool_scale)
    yc = merge(pool_ctx, gla_ctx, mla_ctx, gzc, branch_w, w_out)
    return y, yc


def peer(h, wq, keys, u, v):
    shp = h.shape
    xb = h.reshape(-1, PEER_BLOCK, D_MODEL)

    def one(xs):
        q = (xs @ wq).reshape(PEER_BLOCK, PEER_HEADS, 2, PEER_HALF)
        s1 = jnp.einsum('nhd,hkd->nhk', q[:, :, 0], keys[0]).astype(F32)
        s2 = jnp.einsum('nhd,hkd->nhk', q[:, :, 1], keys[1]).astype(F32)
        t1, i1 = lax.top_k(s1, PEER_TOPK)
        t2, i2 = lax.top_k(s2, PEER_TOPK)
        cand = (t1[..., :, None] + t2[..., None, :]).reshape(PEER_BLOCK, PEER_HEADS, PEER_TOPK * PEER_TOPK)
        cid = (i1[..., :, None] * PEER_N_KEYS + i2[..., None, :]).reshape(PEER_BLOCK, PEER_HEADS,
                                                                          PEER_TOPK * PEER_TOPK)
        top, j = lax.top_k(cand, PEER_TOPK)
        eid = jnp.take_along_axis(cid, j, axis=-1)
        g = jax.nn.softmax(top, axis=-1).astype(xs.dtype)
        a = jax.nn.gelu(jnp.einsum('nd,nhjd->nhj', xs, u[eid]), approximate=False)
        return jnp.einsum('nhj,nhjd->nd', g * a, v[eid])

    return lax.map(one, xb).reshape(shp)


def setup_inputs(seed: int = 0) -> dict:
    key = jax.random.key(seed)
    ks = jax.random.split(key, 32)
    L, D = DEPTH, D_MODEL

    def nrm(k, shape, s):
        return jax.random.normal(k, shape, F32) * s

    return {
        "x": nrm(ks[0], (BATCH, SEQ, D), 1.0),
        "c": nrm(ks[1], (BATCH, D), 1.0),
        "ctx": nrm(ks[2], (BATCH, CTX_LEN, D), 1.0),
        "c_ctx": nrm(ks[3], (D,), 1.0),
        "ada_w": nrm(ks[4], (L, D, 6 * D), 0.5 * D ** -0.5),
        "ada_b": nrm(ks[5], (L, 6 * D), 0.02),
        "norm1_g": 1.0 + nrm(ks[6], (L, D), 0.02),
        "norm2_g": 1.0 + nrm(ks[7], (L, D), 0.02),
        "w_in": nrm(ks[8], (L, D, IN_COLS), D ** -0.5),
        "pool_w": nrm(ks[9], (L, len(POOL_WINDOWS), POOL_GROUP, POOL_GROUP), POOL_GROUP ** -0.5),
        "pool_scale": 1.0 + nrm(ks[10], (L, POOL_WIDTH), 0.1),
        "gla_gate_w": nrm(ks[11], (L, 2, GLA_GATE_RANK, GLA_HEADS * GLA_DK), GLA_GATE_RANK ** -0.5),
        "gla_gate_b": nrm(ks[12], (L, 2, GLA_HEADS * GLA_DK), 0.1),
        "gla_norm_g": 1.0 + nrm(ks[13], (L, GLA_DV), 0.02),
        "mla_q_norm_g": 1.0 + nrm(ks[14], (L, MLA_Q_RANK), 0.02),
        "mla_kv_norm_g": 1.0 + nrm(ks[15], (L, MLA_KV_RANK), 0.02),
        "mla_w_uq": nrm(ks[16], (L, MLA_Q_RANK, MLA_HEADS * (MLA_NOPE + MLA_ROPE)), MLA_Q_RANK ** -0.5),
        "mla_w_ukv": nrm(ks[17], (L, MLA_KV_RANK, MLA_HEADS * (MLA_NOPE + MLA_V)), MLA_KV_RANK ** -0.5),
        "branch_w": nrm(ks[18], (L, N_BRANCH, BRANCH_WIDTH, D), BRANCH_WIDTH ** -0.5),
        "w_out": nrm(ks[19], (L, D, D), D ** -0.5),
        "peer_wq": nrm(ks[20], (L, D, PEER_HEADS * PEER_QDIM), D ** -0.5),
        "peer_keys": nrm(ks[21], (L, 2, PEER_HEADS, PEER_N_KEYS, PEER_HALF), PEER_HALF ** -0.5),
        "peer_u": nrm(ks[22], (L, PEER_N_EXPERTS, D), D ** -0.5),
        "peer_v": nrm(ks[23], (L, PEER_N_EXPERTS, D), 1.0),
        "final_norm_g": 1.0 + nrm(ks[24], (D,), 0.02),
    }


def reference(x, c, ctx, c_ctx, ada_w, ada_b, norm1_g, norm2_g, w_in, pool_w, pool_scale, gla_gate_w,
              gla_gate_b, gla_norm_g, mla_q_norm_g, mla_kv_norm_g, mla_w_uq, mla_w_ukv, branch_w, w_out,
              peer_wq, peer_keys, peer_u, peer_v, final_norm_g):
    x_lat, x_ctx = x, ctx
    for i in range(DEPTH):
        with_ctx = i < DEPTH - 1
        mod = jax.nn.silu(c) @ ada_w[i] + ada_b[i]
        modc = jax.nn.silu(c_ctx) @ ada_w[i] + ada_b[i]
        sh1, sc1, g1, sh2, sc2, g2 = jnp.split(mod[:, None, :], 6, axis=-1)
        csh1, csc1, cg1, csh2, csc2, cg2 = jnp.split(modc, 6)
        h = rmsnorm(x_lat, norm1_g[i]) * (1.0 + sc1) + sh1
        hc = rmsnorm(x_ctx, norm1_g[i]) * (1.0 + csc1) + csh1
        y, yc = token_mixer(h, hc, w_in[i], pool_w[i], pool_scale[i], gla_gate_w[i], gla_gate_b[i],
                            gla_norm_g[i], mla_q_norm_g[i], mla_kv_norm_g[i], mla_w_uq[i], mla_w_ukv[i],
                            branch_w[i], w_out[i], with_ctx)
        x_lat = x_lat + g1 * y
        h2 = rmsnorm(x_lat, norm2_g[i]) * (1.0 + sc2) + sh2
        x_lat = x_lat + g2 * peer(h2, peer_wq[i], peer_keys[i], peer_u[i], peer_v[i])
        if with_ctx:
            x_ctx = x_ctx + cg1 * yc
            hc2 = rmsnorm(x_ctx, norm2_g[i]) * (1.0 + csc2) + csh2
            x_ctx = x_ctx + cg2 * peer(hc2, peer_wq[i], peer_keys[i], peer_u[i], peer_v[i])
    return rmsnorm(x_lat, final_norm_g)
```

```python
import functools

import jax
import jax.numpy as jnp
import numpy as np
from jax import lax
from jax.experimental import pallas as pl
from jax.experimental.pallas import tpu as pltpu

F32 = jnp.float32
BF16 = jnp.bfloat16

D_MODEL = 1024
DEPTH = 2
CTX_LEN = 256
GRID_W = 64
NORM_EPS = 1e-6
POOL_WIDTH = 1024
POOL_WINDOWS = (2, 4, 8, 16)
POOL_GROUP = POOL_WIDTH // len(POOL_WINDOWS)
GLA_HEADS = 4
GLA_DK = 128
GLA_DV = 256
GLA_GATE_RANK = 16
GLA_TAU = 16.0
GLA_CHUNK = 64
MLA_HEADS = 8
MLA_Q_RANK = 384
MLA_KV_RANK = 256
MLA_NOPE = 128
MLA_ROPE = 64
MLA_V = 128
ROPE_BASE = 10000.0
N_BRANCH = 3
PEER_HEADS = 8
PEER_N_KEYS = 128
PEER_TOPK = 16
PEER_QDIM = 256
PEER_HALF = PEER_QDIM // 2
IN_SPLITS = (POOL_WIDTH, GLA_HEADS * GLA_DK, GLA_HEADS * GLA_DK, GLA_HEADS * GLA_DV, GLA_HEADS * GLA_DV,
             GLA_GATE_RANK, GLA_GATE_RANK, MLA_Q_RANK, MLA_KV_RANK, MLA_ROPE, N_BRANCH * D_MODEL)

LANES = 128
TM = 256
VMEM_LIMIT = 56 * 1024 * 1024

COL_P = 0
COL_Q = 1024
COL_K = 1536
COL_V = 2048
COL_R = 3072
COL_GZ = 4096
COL_KVDN = 7168
COL_MISC = 7424
COL_SW = 7552
COL_QDN = 7680
Z_COLS = 8192
IN_NBLK = 2048
MISC_LRF = MLA_ROPE
MISC_LRB = MLA_ROPE + GLA_GATE_RANK
MQ = MLA_NOPE + LANES


def _cparams(*sem):
    return pltpu.CompilerParams(dimension_semantics=sem, vmem_limit_bytes=VMEM_LIMIT)


def _rms(x, g):
    return x * lax.rsqrt(jnp.mean(x * x, axis=-1, keepdims=True) + NORM_EPS) * g


def _dot(a, b):
    return jnp.dot(a, b, preferred_element_type=F32)


def _dot_nt(a, b):
    return lax.dot_general(a, b, (((1,), (1,)), ((), ())), preferred_element_type=F32)


def _dot_tn(a, b):
    return lax.dot_general(a, b, (((0,), (0,)), ((), ())), preferred_element_type=F32)


def _ada_kernel(c_ref, w_ref, b_ref, o_ref):
    c = c_ref[...]
    a = (c * jax.nn.sigmoid(c)).astype(BF16)
    o_ref[0] = _dot(a, w_ref[0].astype(BF16)) + b_ref[0]


def _ada(cc, ada_w, ada_b):
    L, D, N = ada_w.shape
    rows = cc.shape[0]
    nb = N // D
    return pl.pallas_call(
        _ada_kernel,
        grid=(L, nb),
        in_specs=[pl.BlockSpec((rows, D), lambda l, j: (0, 0)),
                  pl.BlockSpec((1, D, D), lambda l, j: (l, 0, j)),
                  pl.BlockSpec((1, 1, D), lambda l, j: (l, 0, j))],
        out_specs=pl.BlockSpec((1, rows, D), lambda l, j: (l, 0, j)),
        out_shape=jax.ShapeDtypeStruct((L, rows, N), F32),
        compiler_params=_cparams("parallel", "parallel"),
        name="ada_mod",
    )(cc, ada_w, ada_b.reshape(L, 1, N))


def _in_proj_kernel(x_ref, mod_ref, g_ref, w_ref, z_ref):
    D = D_MODEL
    x = x_ref[0]
    mod = mod_ref[0]
    h = _rms(x, g_ref[...]) * (1.0 + mod[:, D:2 * D]) + mod[:, 0:D]
    z_ref[0] = _dot(h.astype(BF16), w_ref[...]).astype(BF16)


def _in_proj(xall, modrows, g, w):
    B, R, D = xall.shape
    nt = R // TM
    nn = Z_COLS // IN_NBLK
    return pl.pallas_call(
        _in_proj_kernel,
        grid=(nn, B, nt),
        in_specs=[pl.BlockSpec((1, TM, D), lambda n, b, t: (b, t, 0)),
                  pl.BlockSpec((1, 1, 6 * D), lambda n, b, t: (2 * b + jnp.minimum(t, 1), 0, 0)),
                  pl.BlockSpec((1, D), lambda n, b, t: (0, 0)),
                  pl.BlockSpec((D, IN_NBLK), lambda n, b, t: (0, n))],
        out_specs=pl.BlockSpec((1, TM, IN_NBLK), lambda n, b, t: (b, t, n)),
        out_shape=jax.ShapeDtypeStruct((B, R, Z_COLS), BF16),
        compiler_params=_cparams("parallel", "parallel", "parallel"),
        name="in_proj",
    )(xall, modrows, g, w)


def _pool_kernel(u_ref, w_ref, sc_ref, o_ref):
    R = u_ref.shape[1]
    row = lax.broadcasted_iota(jnp.int32, (R, POOL_GROUP), 0)
    seg_lo = jnp.where(row < CTX_LEN, 0, CTX_LEN)
    seg_hi = jnp.where(row < CTX_LEN, CTX_LEN, R)
    for gi, win in enumerate(POOL_WINDOWS):
        sl = slice(gi * POOL_GROUP, (gi + 1) * POOL_GROUP)
        u = u_ref[0, :, sl].astype(F32)
        lo_off, hi_off = win // 2, win - win // 2
        acc = jnp.zeros_like(u)
        for d in range(-lo_off, hi_off):
            shifted = u if d == 0 else pltpu.roll(u, (R - d) % R, axis=0)
            ok = (row + d >= seg_lo) & (row + d < seg_hi)
            acc = acc + jnp.where(ok, shifted, 0.0)
        cnt = (jnp.minimum(row + hi_off, seg_hi) - jnp.maximum(row - lo_off, seg_lo)).astype(F32)
        diff = (acc / cnt - u).astype(BF16)
        o_ref[0, :, sl] = (_dot(diff, w_ref[gi]) * sc_ref[:, sl]).astype(BF16)


def _pool(z, pool_w, pool_scale):
    B, R, _ = z.shape
    return pl.pallas_call(
        _pool_kernel,
        grid=(B,),
        in_specs=[pl.BlockSpec((1, R, POOL_WIDTH), lambda b: (b, 0, COL_P // POOL_WIDTH)),
                  pl.BlockSpec(pool_w.shape, lambda b: (0, 0, 0)),
                  pl.BlockSpec((1, POOL_WIDTH), lambda b: (0, 0))],
        out_specs=pl.BlockSpec((1, R, POOL_WIDTH), lambda b: (b, 0, 0)),
        out_shape=jax.ShapeDtypeStruct((B, R, POOL_WIDTH), BF16),
        compiler_params=_cparams("parallel"),
        name="pool",
    )(z, pool_w, pool_scale)


def _split3(x):
    a = x.astype(BF16)
    r = x - a.astype(F32)
    b = r.astype(BF16)
    c = (r - b.astype(F32)).astype(BF16)
    return a, b, c


def _gla_kernel(*refs, fwd):
    if fwd:
        q_ref, k_ref, v_ref, misc_ref, gw_ref, gb_ref, ob_ref, r_ref, gn_ref, o_ref, st_ref = refs
    else:
        q_ref, k_ref, v_ref, misc_ref, gw_ref, gb_ref, o_ref, st_ref = refs
    C = GLA_CHUNK
    nchunk = TM // C
    HK = GLA_HEADS * GLA_DK

    @pl.when(pl.program_id(1) == 0)
    def _():
        st_ref[...] = jnp.zeros_like(st_ref)

    zg = _dot(misc_ref[0], gw_ref[...]) + gb_ref[...]
    g = (jnp.minimum(zg, 0.0) - jnp.log1p(jnp.exp(-jnp.abs(zg)))) * (1.0 / GLA_TAU)

    row = lax.broadcasted_iota(jnp.int32, (TM, TM), 0)
    col = lax.broadcasted_iota(jnp.int32, (TM, TM), 1)
    shift = C.bit_length() - 1
    same = (row >> shift) == (col >> shift)
    tri = same & ((col <= row) if fwd else (col >= row))
    ones = jnp.concatenate([jnp.where(tri, 1.0, 0.0), jnp.where(same, 1.0, 0.0)], axis=0).astype(BF16)
    g1, g2, g3 = _split3(g)
    both = _dot(ones, g1) + _dot(ones, g2) + _dot(ones, g3)
    b = both[:TM]
    bt = both[TM:]

    qe = q_ref[0].astype(F32) * (GLA_DK ** -0.5) * jnp.exp(b)
    ke = (k_ref[0].astype(F32) * jnp.exp(-b)).astype(BF16)
    kd = (k_ref[0].astype(F32) * jnp.exp(bt - b)).astype(BF16)
    qe = qe.astype(BF16)
    dec = jnp.exp(bt)
    v = v_ref[0]

    order = range(nchunk) if fwd else range(nchunk - 1, -1, -1)
    for h in range(GLA_HEADS):
        ks = slice(h * GLA_DK, (h + 1) * GLA_DK)
        vs = slice(h * GLA_DV, (h + 1) * GLA_DV)
        att = jnp.where(tri, _dot_nt(qe[:, ks], ke[:, ks]), 0.0).astype(BF16)
        o_in = _dot(att, v[:, vs])
        st = st_ref[h]
        parts = [None] * nchunk
        for j in order:
            rs = slice(j * C, (j + 1) * C)
            parts[j] = o_in[rs] + _dot_nt(qe[rs, ks], st.astype(BF16))
            st = st * dec[j * C:j * C + 1, ks] + _dot_tn(v[rs, vs], kd[rs, ks])
        st_ref[h] = st
        o = jnp.concatenate(parts, axis=0)
        if fwd:
            o = o + ob_ref[0, :, vs].astype(F32)
            r = r_ref[0, :, vs].astype(F32)
            o = _rms(o, gn_ref[...]) * (r * jax.nn.sigmoid(r))
        o_ref[0, :, vs] = o.astype(BF16)


def _gla(z, gw, gb, *, fwd, ob=None, gn=None):
    B, R, _ = z.shape
    nt = R // TM
    HK, HV = GLA_HEADS * GLA_DK, GLA_HEADS * GLA_DV
    if fwd:
        tile = lambda c: c
    else:
        tile = lambda c: jnp.where(c == 0, 0, nt - c)
    in_specs = [pl.BlockSpec((1, TM, HK), lambda b, c: (b, tile(c), COL_Q // HK)),
                pl.BlockSpec((1, TM, HK), lambda b, c: (b, tile(c), COL_K // HK)),
                pl.BlockSpec((1, TM, HV), lambda b, c: (b, tile(c), COL_V // HV)),
                pl.BlockSpec((1, TM, LANES), lambda b, c: (b, tile(c), COL_MISC // LANES)),
                pl.BlockSpec((LANES, HK), lambda b, c: (0, 0)),
                pl.BlockSpec((1, HK), lambda b, c: (0, 0))]
    args = [z, z, z, z, gw, gb]
    if fwd:
        in_specs += [pl.BlockSpec((1, TM, HV), lambda b, c: (b, tile(c), 0)),
                     pl.BlockSpec((1, TM, HV), lambda b, c: (b, tile(c), COL_R // HV)),
                     pl.BlockSpec((1, GLA_DV), lambda b, c: (0, 0))]
        args += [ob, z, gn]
    return pl.pallas_call(
        functools.partial(_gla_kernel, fwd=fwd),
        grid=(B, nt),
        in_specs=in_specs,
        out_specs=pl.BlockSpec((1, TM, HV), lambda b, c: (b, tile(c), 0)),
        out_shape=jax.ShapeDtypeStruct((B, R, HV), BF16),
        scratch_shapes=[pltpu.VMEM((GLA_HEADS, GLA_DV, GLA_DK), F32)],
        compiler_params=_cparams("parallel", "arbitrary"),
        name="gla_fwd" if fwd else "gla_bwd",
    )(*args)


def _mla_up_kernel(qdn_ref, kvdn_ref, misc_ref, sw_ref, cos_ref, sin_ref, qg_ref, kvg_ref, wq_ref, wkv_ref,
                   q_ref, k_ref, v_ref):
    H = MLA_HEADS
    scale = (MLA_NOPE + MLA_ROPE) ** -0.5
    cos = cos_ref[...]
    sin = sin_ref[...]
    qn = _rms(qdn_ref[0].astype(F32), qg_ref[...]).astype(BF16)
    qall = _dot(qn, wq_ref[...])
    for h in range(H):
        base = h * MQ
        q_ref[0, :, base:base + MLA_NOPE] = (qall[:, base:base + MLA_NOPE] * scale).astype(BF16)
        rot = (qall[:, base + MLA_NOPE:base + MQ] * cos
               + qall[:, H * MQ + h * LANES:H * MQ + (h + 1) * LANES] * sin)
        q_ref[0, :, base + MLA_NOPE:base + MQ] = (rot * scale).astype(BF16)
    kvn = _rms(kvdn_ref[0].astype(F32), kvg_ref[...]).astype(BF16)
    kvall = _dot(kvn, wkv_ref[...])
    kr = (misc_ref[0].astype(F32) * cos + sw_ref[0].astype(F32) * sin).astype(BF16)
    for h in range(H):
        src = h * (MLA_NOPE + MLA_V)
        k_ref[0, :, h * MQ:h * MQ + MLA_NOPE] = kvall[:, src:src + MLA_NOPE].astype(BF16)
        k_ref[0, :, h * MQ + MLA_NOPE:(h + 1) * MQ] = kr
        v_ref[0, :, h * MLA_V:(h + 1) * MLA_V] = kvall[:, src + MLA_NOPE:src + MLA_NOPE + MLA_V].astype(BF16)


def _mla_up(z, cos, sin, qg, kvg, wq, wkv):
    B, R, _ = z.shape
    nt = R // TM
    H = MLA_HEADS
    const = lambda b, t: (0, 0)
    return pl.pallas_call(
        _mla_up_kernel,
        grid=(B, nt),
        in_specs=[pl.BlockSpec((1, TM, MLA_Q_RANK), lambda b, t: (b, t, COL_QDN // MLA_Q_RANK)),
                  pl.BlockSpec((1, TM, MLA_KV_RANK), lambda b, t: (b, t, COL_KVDN // MLA_KV_RANK)),
                  pl.BlockSpec((1, TM, LANES), lambda b, t: (b, t, COL_MISC // LANES)),
                  pl.BlockSpec((1, TM, LANES), lambda b, t: (b, t, COL_SW // LANES)),
                  pl.BlockSpec((TM, LANES), lambda b, t: (t, 0)),
                  pl.BlockSpec((TM, LANES), lambda b, t: (t, 0)),
                  pl.BlockSpec(qg.shape, const),
                  pl.BlockSpec(kvg.shape, const),
                  pl.BlockSpec(wq.shape, const),
                  pl.BlockSpec(wkv.shape, const)],
        out_specs=[pl.BlockSpec((1, TM, H * MQ), lambda b, t: (b, t, 0)),
                   pl.BlockSpec((1, TM, H * MQ), lambda b, t: (b, t, 0)),
                   pl.BlockSpec((1, TM, H * MLA_V), lambda b, t: (b, t, 0))],
        out_shape=[jax.ShapeDtypeStruct((B, R, H * MQ), BF16),
                   jax.ShapeDtypeStruct((B, R, H * MQ), BF16),
                   jax.ShapeDtypeStruct((B, R, H * MLA_V), BF16)],
        compiler_params=_cparams("parallel", "parallel"),
        name="mla_up",
    )(z, z, z, z, cos, sin, qg, kvg, wq, wkv)


def _attn_kernel(q_ref, k_ref, v_ref, o_ref, *, q_off):
    R = k_ref.shape[1]

    def attend(nk):
        s = _dot_nt(q_ref[0], k_ref[0, :nk])
        p = jnp.exp(s - jnp.max(s, axis=-1, keepdims=True))
        l = jnp.sum(p, axis=-1, keepdims=True)
        o_ref[0] = (_dot(p.astype(BF16), v_ref[0, :nk]) / l).astype(BF16)

    if q_off == 0:
        qi = pl.program_id(2)

        @pl.when(qi == 0)
        def _():
            attend(CTX_LEN)

        @pl.when(qi > 0)
        def _():
            attend(R)
    else:
        attend(R)


def _attn(q, k, v, *, with_ctx):
    B, R, _ = q.shape
    q_off = 0 if with_ctx else CTX_LEN // TM
    nq = R // TM - q_off
    return pl.pallas_call(
        functools.partial(_attn_kernel, q_off=q_off),
        grid=(B, MLA_HEADS, nq),
        in_specs=[pl.BlockSpec((1, TM, MQ), lambda b, h, i: (b, i + q_off, h)),
                  pl.BlockSpec((1, R, MQ), lambda b, h, i: (b, 0, h)),
                  pl.BlockSpec((1, R, MLA_V), lambda b, h, i: (b, 0, h))],
        out_specs=pl.BlockSpec((1, TM, MLA_V), lambda b, h, i: (b, i + q_off, h)),
        out_shape=jax.ShapeDtypeStruct((B, R, MLA_HEADS * MLA_V), BF16),
        compiler_params=_cparams("parallel", "parallel", "arbitrary"),
        name="mla_attn",
    )(q, k, v)


def _merge_kernel(po_ref, go_ref, mo_ref, gz0_ref, gz1_ref, gz2_ref, x_ref, mod_ref, bw_ref, wo_ref, g_ref,
                  x1_ref, h2_ref):
    D = D_MODEL
    m = None
    for i, (o_ref, gz_ref) in enumerate(((po_ref, gz0_ref), (go_ref, gz1_ref), (mo_ref, gz2_ref))):
        t = jax.nn.sigmoid(gz_ref[0].astype(F32)) * _dot(o_ref[0], bw_ref[i])
        m = t if m is None else m + t
    y = _dot(m.astype(BF16), wo_ref[...])
    mod = mod_ref[0]
    x1 = x_ref[0] + mod[:, 2 * D:3 * D] * y
    x1_ref[0] = x1
    h2 = _rms(x1, g_ref[...]) * (1.0 + mod[:, 4 * D:5 * D]) + mod[:, 3 * D:4 * D]
    h2_ref[0] = h2.astype(BF16)


def _merge(pool_o, gla_o, mla_o, z, xall, modrows, bw, wo, g, *, with_ctx):
    B, R, D = xall.shape
    t_off = 0 if with_ctx else CTX_LEN // TM
    nt = R // TM - t_off
    tok = lambda b, t: (b, t + t_off, 0)
    gzb = COL_GZ // D
    return pl.pallas_call(
        _merge_kernel,
        grid=(B, nt),
        in_specs=[pl.BlockSpec((1, TM, D), tok),
                  pl.BlockSpec((1, TM, D), tok),
                  pl.BlockSpec((1, TM, D), tok),
                  pl.BlockSpec((1, TM, D), lambda b, t: (b, t + t_off, gzb)),
                  pl.BlockSpec((1, TM, D), lambda b, t: (b, t + t_off, gzb + 1)),
                  pl.BlockSpec((1, TM, D), lambda b, t: (b, t + t_off, gzb + 2)),
                  pl.BlockSpec((1, TM, D), tok),
                  pl.BlockSpec((1, 1, 6 * D), lambda b, t: (2 * b + jnp.minimum(t + t_off, 1), 0, 0)),
                  pl.BlockSpec(bw.shape, lambda b, t: (0, 0, 0)),
                  pl.BlockSpec(wo.shape, lambda b, t: (0, 0)),
                  pl.BlockSpec((1, D), lambda b, t: (0, 0))],
        out_specs=[pl.BlockSpec((1, TM, D), lambda b, t: (b, t, 0)),
                   pl.BlockSpec((1, TM, D), lambda b, t: (b, t, 0))],
        out_shape=[jax.ShapeDtypeStruct((B, nt * TM, D), F32),
                   jax.ShapeDtypeStruct((B, nt * TM, D), BF16)],
        compiler_params=_cparams("parallel", "parallel"),
        name="merge",
    )(pool_o, gla_o, mla_o, z, z, z, xall, modrows, bw, wo, g)


def _top16(s):
    nk, n = s.shape
    iota = lax.broadcasted_iota(jnp.int32, (nk, n), 0).astype(F32)
    slot = lax.broadcasted_iota(jnp.int32, (PEER_TOPK, n), 0)
    rank = jnp.full((nk, n), float(PEER_TOPK), F32)
    vals = jnp.zeros((PEER_TOPK, n), F32)
    for j in range(PEER_TOPK):
        m = jnp.max(s, axis=0, keepdims=True)
        idx = jnp.min(jnp.where(s == m, iota, float(nk)), axis=0, keepdims=True)
        hit = iota == idx
        rank = jnp.where(hit, float(j), rank)
        s = jnp.where(hit, -jnp.inf, s)
        vals = jnp.where(slot == j, m, vals)
    return vals, rank


def _pair_top16(t1, t2):
    K = PEER_TOPK
    n = t1.shape[1]
    short = K // 2
    cand = jnp.concatenate([t1[0:1] + t2] + [t1[i:i + 1] + t2[0:short] for i in range(1, K)], axis=0)
    nr = cand.shape[0]
    r = lax.broadcasted_iota(jnp.int32, (nr, n), 0)
    sh = short.bit_length() - 1
    i_of = jnp.where(r < K, 0, ((r - K) >> sh) + 1)
    j_of = jnp.where(r < K, r, (r - K) & (short - 1))
    flat = (i_of * K + j_of).astype(F32)
    cand = jnp.where((i_of + 1) * (j_of + 1) <= K, cand, -jnp.inf)
    irow = lax.broadcasted_iota(jnp.int32, (K, n), 0).astype(F32)
    cnt = jnp.zeros((K, n), F32)
    zsum = jnp.zeros((1, n), F32)
    best = None
    for j in range(K):
        m = jnp.max(cand, axis=0, keepdims=True)
        idx = jnp.min(jnp.where(cand == m, flat, float(K * K)), axis=0, keepdims=True)
        cand = jnp.where(flat == idx, -jnp.inf, cand)
        cnt = cnt + jnp.where(irow == jnp.floor(idx * (1.0 / K)), 1.0, 0.0)
        if j == 0:
            best = m
        zsum = zsum + jnp.exp(m - best)
    return cnt, zsum


def _peer_score_kernel(h2_ref, wqt_ref, keys_ref, r2_ref, lr_ref, p2_ref, ka_ref, qt_ref):
    K = PEER_TOPK
    qt_ref[...] = _dot_nt(wqt_ref[...], h2_ref[...]).astype(BF16)

    def head(h, carry):
        base = pl.multiple_of(h * PEER_QDIM, PEER_QDIM)
        s1 = _dot(keys_ref[h], qt_ref[pl.ds(base, PEER_HALF), :])
        s2 = _dot(keys_ref[PEER_HEADS + h], qt_ref[pl.ds(base + PEER_HALF, PEER_HALF), :])
        t1, rank1 = _top16(s1)
        t2, rank2 = _top16(s2)
        cnt, zsum = _pair_top16(t1, t2)
        rowcnt = jnp.zeros_like(rank1)
        for j in range(K):
            rowcnt = jnp.where(rank1 == float(j), cnt[j:j + 1], rowcnt)
        r2_ref[h] = rank2
        lr_ref[h] = rowcnt
        p2_ref[h] = jnp.exp(s2 - t2[0:1])
        ka_ref[h] = jnp.exp(s1 - t1[0:1]) / zsum
        return carry

    lax.fori_loop(0, PEER_HEADS, head, 0)


def _peer_score(h2, wqt, keys):
    ntok, D = h2.shape
    tn = LANES
    shp = jax.ShapeDtypeStruct((PEER_HEADS, PEER_N_KEYS, ntok), F32)
    ospec = pl.BlockSpec((PEER_HEADS, PEER_N_KEYS, tn), lambda i: (0, 0, i))
    return pl.pallas_call(
        _peer_score_kernel,
        grid=(ntok // tn,),
        in_specs=[pl.BlockSpec((tn, D), lambda i: (i, 0)),
                  pl.BlockSpec(wqt.shape, lambda i: (0, 0)),
                  pl.BlockSpec(keys.shape, lambda i: (0, 0, 0))],
        out_specs=[ospec] * 4,
        out_shape=[shp] * 4,
        scratch_shapes=[pltpu.VMEM((PEER_HEADS * PEER_QDIM, tn), BF16)],
        compiler_params=_cparams("parallel"),
        name="peer_score",
    )(h2, wqt, keys)


PEER_ETILE = 2 * PEER_N_KEYS


def _peer_dense_kernel(h2_ref, r2_ref, lr_ref, p2_ref, ka_ref, u_ref, vt_ref, x1_ref, modc_ref, modl_ref, fg_ref,
                       o_ref, acc_ref, *, ctx_rows, rows_per_sample, final):
    D = D_MODEL
    tn = h2_ref.shape[0]
    e = pl.program_id(1)

    @pl.when(e == 0)
    def _():
        acc_ref[...] = jnp.zeros_like(acc_ref)

    pre = _dot_nt(u_ref[...], h2_ref[...])
    act = 0.5 * pre * (1.0 + lax.erf(pre * (2.0 ** -0.5)))
    parts = []
    for half in range(PEER_ETILE // PEER_N_KEYS):
        a = e * (PEER_ETILE // PEER_N_KEYS) + half
        w = jnp.zeros((PEER_N_KEYS, tn), F32)
        for h in range(PEER_HEADS):
            sel = r2_ref[h] < lr_ref[h, pl.ds(a, 1), :]
            w = w + jnp.where(sel, p2_ref[h], 0.0) * ka_ref[h, pl.ds(a, 1), :]
        parts.append((act[half * PEER_N_KEYS:(half + 1) * PEER_N_KEYS] * w).astype(BF16))
    acc_ref[...] += _dot(vt_ref[...], jnp.concatenate(parts, axis=0))

    @pl.when(e == pl.num_programs(1) - 1)
    def _():
        y = acc_ref[...].T
        g_lat = modl_ref[0][:, 5 * D:6 * D]
        if ctx_rows:
            g_ctx = modc_ref[0][:, 5 * D:6 * D]
            tiles = rows_per_sample // tn
            r0 = (pl.program_id(0) % tiles) * tn
            row = r0 + lax.broadcasted_iota(jnp.int32, (tn, D), 0)
            gate = jnp.where(row < ctx_rows, g_ctx, g_lat)
        else:
            gate = g_lat
        x2 = x1_ref[...] + gate * y
        if final:
            x2 = _rms(x2, fg_ref[...])
        o_ref[...] = x2


def _peer_dense(h2, sel, u, vt, x1, modrows, fg, *, tn, rows_per_sample, ctx_rows, final):
    ntok, D = h2.shape
    ne = u.shape[0] // PEER_ETILE
    tiles = rows_per_sample // tn
    sspec = pl.BlockSpec((PEER_HEADS, PEER_N_KEYS, tn), lambda i, e: (0, 0, i))
    return pl.pallas_call(
        functools.partial(_peer_dense_kernel, ctx_rows=ctx_rows, rows_per_sample=rows_per_sample, final=final),
        grid=(ntok // tn, ne),
        in_specs=[pl.BlockSpec((tn, D), lambda i, e: (i, 0)),
                  sspec, sspec, sspec, sspec,
                  pl.BlockSpec((PEER_ETILE, D), lambda i, e: (e, 0)),
                  pl.BlockSpec((D, PEER_ETILE), lambda i, e: (0, e)),
                  pl.BlockSpec((tn, D), lambda i, e: (i, 0)),
                  pl.BlockSpec((1, 1, 6 * D), lambda i, e: (2 * (i // tiles), 0, 0)),
                  pl.BlockSpec((1, 1, 6 * D), lambda i, e: (2 * (i // tiles) + 1, 0, 0)),
                  pl.BlockSpec((1, D), lambda i, e: (0, 0))],
        out_specs=pl.BlockSpec((tn, D), lambda i, e: (i, 0)),
        out_shape=jax.ShapeDtypeStruct((ntok, D), F32),
        scratch_shapes=[pltpu.VMEM((D, tn), F32)],
        compiler_params=_cparams("parallel", "arbitrary"),
        name="peer_dense",
    )(h2, *sel, u, vt, x1, modrows, modrows, fg)


def _peer_tile(rows_per_sample):
    for tn in (768, 512, 256):
        if rows_per_sample % tn == 0:
            return tn
    raise ValueError(rows_per_sample)


_SWAP = np.concatenate([np.arange(16, 32), np.arange(0, 16), np.arange(48, 64), np.arange(32, 48)])


def _layout_w_in(w):
    D = w.shape[0]
    idx = np.cumsum(IN_SPLITS)[:-1]
    p, q, k, v, r, lrf, lrb, qdn, kvdn, krope, gz = jnp.split(w, idx, axis=1)
    zeros = lambda n: jnp.zeros((D, n), w.dtype)
    misc = jnp.concatenate([krope, lrf, lrb, zeros(LANES - MLA_ROPE - 2 * GLA_GATE_RANK)], axis=1)
    sw = jnp.concatenate([krope[:, _SWAP], zeros(LANES - MLA_ROPE)], axis=1)
    out = jnp.concatenate([p, q, k, v, r, gz, kvdn, misc, sw, qdn], axis=1)
    return jnp.concatenate([out, zeros(Z_COLS - out.shape[1])], axis=1).astype(BF16)


def _layout_gate_w(gate_w, row0):
    HK = GLA_HEADS * GLA_DK
    pad = jnp.zeros((LANES, HK), gate_w.dtype)
    return pad.at[row0:row0 + GLA_GATE_RANK].set(gate_w).astype(BF16)


def _layout_w_uq(w):
    H = MLA_HEADS
    w3 = w.reshape(MLA_Q_RANK, H, MLA_NOPE + MLA_ROPE)
    qn, qr = w3[..., :MLA_NOPE], w3[..., MLA_NOPE:]
    z = jnp.zeros((MLA_Q_RANK, H, LANES - MLA_ROPE), w.dtype)
    main = jnp.concatenate([qn, qr, z], axis=-1).reshape(MLA_Q_RANK, H * MQ)
    swp = jnp.concatenate([qr[..., _SWAP], z], axis=-1).reshape(MLA_Q_RANK, H * LANES)
    return jnp.concatenate([main, swp], axis=1).astype(BF16)


def _rope_tables(seq):
    half = MLA_ROPE // 2
    t = jnp.arange(seq)
    inv = ROPE_BASE ** (-jnp.arange(0, half, 2, dtype=F32) / half)
    ar = (t // GRID_W).astype(F32)[:, None] * inv
    ac = (t % GRID_W).astype(F32)[:, None] * inv
    cos = jnp.concatenate([jnp.cos(ar), jnp.cos(ar), jnp.cos(ac), jnp.cos(ac)], axis=1)
    sin = jnp.concatenate([-jnp.sin(ar), jnp.sin(ar), -jnp.sin(ac), jnp.sin(ac)], axis=1)
    cos = jnp.concatenate([jnp.ones((CTX_LEN, MLA_ROPE), F32), cos], axis=0)
    sin = jnp.concatenate([jnp.zeros((CTX_LEN, MLA_ROPE), F32), sin], axis=0)
    pad = jnp.zeros((CTX_LEN + seq, LANES - MLA_ROPE), F32)
    return jnp.concatenate([cos, pad], axis=1), jnp.concatenate([sin, pad], axis=1)


def kernel(x, c, ctx, c_ctx, ada_w, ada_b, norm1_g, norm2_g, w_in, pool_w, pool_scale, gla_gate_w, gla_gate_b, gla_norm_g, mla_q_norm_g, mla_kv_norm_g, mla_w_uq, mla_w_ukv, branch_w, w_out, peer_wq, peer_keys, peer_u, peer_v, final_norm_g):
    B, T, D = x.shape
    R = CTX_LEN + T
    assert D == D_MODEL and ctx.shape[1] == CTX_LEN == TM and T % TM == 0 and T % GRID_W == 0

    nrow = -(-(B + 1) // 8) * 8
    cc = jnp.concatenate([c, c_ctx[None], jnp.zeros((nrow - B - 1, D), F32)], axis=0)
    mod = _ada(cc, ada_w, ada_b)
    modrows = jnp.stack([jnp.broadcast_to(mod[:, B:B + 1], (DEPTH, B, 6 * D)), mod[:, :B]], axis=2)
    modrows = modrows.reshape(DEPTH, 2 * B, 1, 6 * D)

    cos, sin = _rope_tables(T)
    xall = jnp.concatenate([ctx, x], axis=1)
    row = lambda a: a.reshape(1, -1)

    for i in range(DEPTH):
        with_ctx = i < DEPTH - 1
        final = i == DEPTH - 1
        z = _in_proj(xall, modrows[i], row(norm1_g[i]), _layout_w_in(w_in[i]))

        pool_o = _pool(z, pool_w[i].astype(BF16), row(pool_scale[i]))

        o_b = _gla(z, _layout_gate_w(gla_gate_w[i, 1], MISC_LRB), row(gla_gate_b[i, 1]), fwd=False)
        gla_o = _gla(z, _layout_gate_w(gla_gate_w[i, 0], MISC_LRF), row(gla_gate_b[i, 0]), fwd=True,
                     ob=o_b, gn=row(gla_norm_g[i]))

        q, k, v = _mla_up(z, cos, sin, row(mla_q_norm_g[i]), row(mla_kv_norm_g[i]),
                          _layout_w_uq(mla_w_uq[i]), mla_w_ukv[i].astype(BF16))
        mla_o = _attn(q, k, v, with_ctx=with_ctx)

        x1, h2 = _merge(pool_o, gla_o, mla_o, z, xall, modrows[i], branch_w[i].astype(BF16),
                        w_out[i].astype(BF16), row(norm2_g[i]), with_ctx=with_ctx)

        rows = x1.shape[1]
        h2f = h2.reshape(B * rows, D)
        keys = peer_keys[i].reshape(2 * PEER_HEADS, PEER_N_KEYS, PEER_HALF).astype(BF16)
        sel = _peer_score(h2f, peer_wq[i].T.astype(BF16), keys)
        xall = _peer_dense(h2f, sel, peer_u[i].astype(BF16), peer_v[i].T.astype(BF16), x1.reshape(B * rows, D),
                           modrows[i], row(final_norm_g), tn=_peer_tile(rows), rows_per_sample=rows,
                           ctx_rows=CTX_LEN if with_ctx else 0, final=final).reshape(B, rows, D)
    return xall
```

```python
import functools

import jax
import jax.numpy as jnp
import numpy as np
from jax import lax
from jax.experimental import pallas as pl
from jax.experimental.pallas import tpu as pltpu

F32 = jnp.float32
BF16 = jnp.bfloat16

D_MODEL = 1024
DEPTH = 2
CTX_LEN = 256
GRID_W = 64
NORM_EPS = 1e-6
POOL_WIDTH = 1024
POOL_WINDOWS = (2, 4, 8, 16)
POOL_GROUP = POOL_WIDTH // len(POOL_WINDOWS)
GLA_HEADS = 4
GLA_DK = 128
GLA_DV = 256
GLA_GATE_RANK = 16
GLA_TAU = 16.0
GLA_CHUNK = 64
MLA_HEADS = 8
MLA_Q_RANK = 384
MLA_KV_RANK = 256
MLA_NOPE = 128
MLA_ROPE = 64
MLA_V = 128
ROPE_BASE = 10000.0
N_BRANCH = 3
PEER_HEADS = 8
PEER_N_KEYS = 128
PEER_TOPK = 16
PEER_QDIM = 256
PEER_HALF = PEER_QDIM // 2
IN_SPLITS = (POOL_WIDTH, GLA_HEADS * GLA_DK, GLA_HEADS * GLA_DK, GLA_HEADS * GLA_DV, GLA_HEADS * GLA_DV,
             GLA_GATE_RANK, GLA_GATE_RANK, MLA_Q_RANK, MLA_KV_RANK, MLA_ROPE, N_BRANCH * D_MODEL)

LANES = 128
BF16_ROWS = 16
TM = 256
VMEM_LIMIT = 56 * 1024 * 1024

COL_P = 0
COL_Q = 1024
COL_K = 1536
COL_V = 2048
COL_R = 3072
COL_GZ = 4096
COL_KVDN = 7168
COL_MISC = 7424
COL_SW = 7552
COL_QDN = 7680
Z_COLS = 8192
IN_NBLK = 2048
MISC_LRF = MLA_ROPE
MISC_LRB = MLA_ROPE + GLA_GATE_RANK
MQ = MLA_NOPE + LANES


def _cparams(*sem):
    return pltpu.CompilerParams(dimension_semantics=sem, vmem_limit_bytes=VMEM_LIMIT)


def _rms(x, g):
    return x * lax.rsqrt(jnp.mean(x * x, axis=-1, keepdims=True) + NORM_EPS) * g


def _dot(a, b):
    return jnp.dot(a, b, preferred_element_type=F32)


def _dot_nt(a, b):
    return lax.dot_general(a, b, (((1,), (1,)), ((), ())), preferred_element_type=F32)


def _dot_tn(a, b):
    return lax.dot_general(a, b, (((0,), (0,)), ((), ())), preferred_element_type=F32)


def _ada_kernel(c_ref, w_ref, b_ref, o_ref):
    c = c_ref[...]
    a = (c * jax.nn.sigmoid(c)).astype(BF16)
    o_ref[0] = _dot(a, w_ref[0].astype(BF16)) + b_ref[0]


def _ada(cc, ada_w, ada_b):
    L, D, N = ada_w.shape
    rows = cc.shape[0]
    nb = N // D
    return pl.pallas_call(
        _ada_kernel,
        grid=(L, nb),
        in_specs=[pl.BlockSpec((rows, D), lambda l, j: (0, 0)),
                  pl.BlockSpec((1, D, D), lambda l, j: (l, 0, j)),
                  pl.BlockSpec((1, 1, D), lambda l, j: (l, 0, j))],
        out_specs=pl.BlockSpec((1, rows, D), lambda l, j: (l, 0, j)),
        out_shape=jax.ShapeDtypeStruct((L, rows, N), F32),
        compiler_params=_cparams("parallel", "parallel"),
        name="ada_mod",
    )(cc, ada_w, ada_b.reshape(L, 1, N))


def _in_proj_kernel(x_ref, mod_ref, g_ref, w_ref, z_ref):
    D = D_MODEL
    x = x_ref[0]
    mod = mod_ref[0]
    h = _rms(x, g_ref[...]) * (1.0 + mod[:, D:2 * D]) + mod[:, 0:D]
    z_ref[0] = _dot(h.astype(BF16), w_ref[...]).astype(BF16)


def _in_proj(xall, modrows, g, w):
    B, R, D = xall.shape
    nt = R // TM
    nn = Z_COLS // IN_NBLK
    return pl.pallas_call(
        _in_proj_kernel,
        grid=(nn, B, nt),
        in_specs=[pl.BlockSpec((1, TM, D), lambda n, b, t: (b, t, 0)),
                  pl.BlockSpec((1, 1, 6 * D), lambda n, b, t: (2 * b + jnp.minimum(t, 1), 0, 0)),
                  pl.BlockSpec((1, D), lambda n, b, t: (0, 0)),
                  pl.BlockSpec((D, IN_NBLK), lambda n, b, t: (0, n))],
        out_specs=pl.BlockSpec((1, TM, IN_NBLK), lambda n, b, t: (b, t, n)),
        out_shape=jax.ShapeDtypeStruct((B, R, Z_COLS), BF16),
        compiler_params=_cparams("parallel", "parallel", "parallel"),
        name="in_proj",
    )(xall, modrows, g, w)


def _pool_kernel(u_ref, w_ref, sc_ref, o_ref):
    R = u_ref.shape[1]
    row = lax.broadcasted_iota(jnp.int32, (R, POOL_GROUP), 0)
    seg_lo = jnp.where(row < CTX_LEN, 0, CTX_LEN)
    seg_hi = jnp.where(row < CTX_LEN, CTX_LEN, R)
    for gi, win in enumerate(POOL_WINDOWS):
        sl = slice(gi * POOL_GROUP, (gi + 1) * POOL_GROUP)
        u = u_ref[0, :, sl].astype(F32)
        lo_off, hi_off = win // 2, win - win // 2
        acc = jnp.zeros_like(u)
        for d in range(-lo_off, hi_off):
            shifted = u if d == 0 else pltpu.roll(u, (R - d) % R, axis=0)
            ok = (row + d >= seg_lo) & (row + d < seg_hi)
            acc = acc + jnp.where(ok, shifted, 0.0)
        cnt = (jnp.minimum(row + hi_off, seg_hi) - jnp.maximum(row - lo_off, seg_lo)).astype(F32)
        diff = (acc / cnt - u).astype(BF16)
        o_ref[0, :, sl] = (_dot(diff, w_ref[gi]) * sc_ref[:, sl]).astype(BF16)


def _pool(z, pool_w, pool_scale):
    B, R, _ = z.shape
    return pl.pallas_call(
        _pool_kernel,
        grid=(B,),
        in_specs=[pl.BlockSpec((1, R, POOL_WIDTH), lambda b: (b, 0, COL_P // POOL_WIDTH)),
                  pl.BlockSpec(pool_w.shape, lambda b: (0, 0, 0)),
                  pl.BlockSpec((1, POOL_WIDTH), lambda b: (0, 0))],
        out_specs=pl.BlockSpec((1, R, POOL_WIDTH), lambda b: (b, 0, 0)),
        out_shape=jax.ShapeDtypeStruct((B, R, POOL_WIDTH), BF16),
        compiler_params=_cparams("parallel"),
        name="pool",
    )(z, pool_w, pool_scale)


def _split3(x):
    a = x.astype(BF16)
    r = x - a.astype(F32)
    b = r.astype(BF16)
    c = (r - b.astype(F32)).astype(BF16)
    return a, b, c


def _gla_kernel(*refs, fwd):
    if fwd:
        q_ref, k_ref, v_ref, misc_ref, gw_ref, gb_ref, ob_ref, r_ref, gn_ref, o_ref, st_ref = refs
    else:
        q_ref, k_ref, v_ref, misc_ref, gw_ref, gb_ref, o_ref, st_ref = refs
    C = GLA_CHUNK
    nchunk = TM // C
    HK = GLA_HEADS * GLA_DK

    @pl.when(pl.program_id(1) == 0)
    def _():
        st_ref[...] = jnp.zeros_like(st_ref)

    zg = _dot(misc_ref[0], gw_ref[...]) + gb_ref[...]
    g = (jnp.minimum(zg, 0.0) - jnp.log1p(jnp.exp(-jnp.abs(zg)))) * (1.0 / GLA_TAU)

    row = lax.broadcasted_iota(jnp.int32, (TM, TM), 0)
    col = lax.broadcasted_iota(jnp.int32, (TM, TM), 1)
    shift = C.bit_length() - 1
    same = (row >> shift) == (col >> shift)
    tri = same & ((col <= row) if fwd else (col >= row))
    ones = jnp.concatenate([jnp.where(tri, 1.0, 0.0), jnp.where(same, 1.0, 0.0)], axis=0).astype(BF16)
    g1, g2, g3 = _split3(g)
    both = _dot(ones, g1) + _dot(ones, g2) + _dot(ones, g3)
    b = both[:TM]
    bt = both[TM:]

    qe = q_ref[0].astype(F32) * (GLA_DK ** -0.5) * jnp.exp(b)
    ke = (k_ref[0].astype(F32) * jnp.exp(-b)).astype(BF16)
    kd = (k_ref[0].astype(F32) * jnp.exp(bt - b)).astype(BF16)
    qe = qe.astype(BF16)
    dec = jnp.exp(bt)
    v = v_ref[0]

    order = range(nchunk) if fwd else range(nchunk - 1, -1, -1)
    for h in range(GLA_HEADS):
        ks = slice(h * GLA_DK, (h + 1) * GLA_DK)
        vs = slice(h * GLA_DV, (h + 1) * GLA_DV)
        att = jnp.where(tri, _dot_nt(qe[:, ks], ke[:, ks]), 0.0).astype(BF16)
        o_in = _dot(att, v[:, vs])
        st = st_ref[h]
        parts = [None] * nchunk
        for j in order:
            rs = slice(j * C, (j + 1) * C)
            parts[j] = o_in[rs] + _dot_nt(qe[rs, ks], st.astype(BF16))
            st = st * dec[j * C:j * C + 1, ks] + _dot_tn(v[rs, vs], kd[rs, ks])
        st_ref[h] = st
        o = jnp.concatenate(parts, axis=0)
        if fwd:
            o = o + ob_ref[0, :, vs].astype(F32)
            r = r_ref[0, :, vs].astype(F32)
            o = _rms(o, gn_ref[...]) * (r * jax.nn.sigmoid(r))
        o_ref[0, :, vs] = o.astype(BF16)


def _gla(z, gw, gb, *, fwd, ob=None, gn=None):
    B, R, _ = z.shape
    nt = R // TM
    HK, HV = GLA_HEADS * GLA_DK, GLA_HEADS * GLA_DV
    if fwd:
        tile = lambda c: c
    else:
        tile = lambda c: jnp.where(c == 0, 0, nt - c)
    in_specs = [pl.BlockSpec((1, TM, HK), lambda b, c: (b, tile(c), COL_Q // HK)),
                pl.BlockSpec((1, TM, HK), lambda b, c: (b, tile(c), COL_K // HK)),
                pl.BlockSpec((1, TM, HV), lambda b, c: (b, tile(c), COL_V // HV)),
                pl.BlockSpec((1, TM, LANES), lambda b, c: (b, tile(c), COL_MISC // LANES)),
                pl.BlockSpec((LANES, HK), lambda b, c: (0, 0)),
                pl.BlockSpec((1, HK), lambda b, c: (0, 0))]
    args = [z, z, z, z, gw, gb]
    if fwd:
        in_specs += [pl.BlockSpec((1, TM, HV), lambda b, c: (b, tile(c), 0)),
                     pl.BlockSpec((1, TM, HV), lambda b, c: (b, tile(c), COL_R // HV)),
                     pl.BlockSpec((1, GLA_DV), lambda b, c: (0, 0))]
        args += [ob, z, gn]
    return pl.pallas_call(
        functools.partial(_gla_kernel, fwd=fwd),
        grid=(B, nt),
        in_specs=in_specs,
        out_specs=pl.BlockSpec((1, TM, HV), lambda b, c: (b, tile(c), 0)),
        out_shape=jax.ShapeDtypeStruct((B, R, HV), BF16),
        scratch_shapes=[pltpu.VMEM((GLA_HEADS, GLA_DV, GLA_DK), F32)],
        compiler_params=_cparams("parallel", "arbitrary"),
        name="gla_fwd" if fwd else "gla_bwd",
    )(*args)


def _mla_up_kernel(qdn_ref, kvdn_ref, misc_ref, sw_ref, cos_ref, sin_ref, qg_ref, kvg_ref, wq_ref, wkv_ref,
                   q_ref, k_ref, v_ref):
    H = MLA_HEADS
    scale = (MLA_NOPE + MLA_ROPE) ** -0.5
    cos = cos_ref[...]
    sin = sin_ref[...]
    qn = _rms(qdn_ref[0].astype(F32), qg_ref[...]).astype(BF16)
    qall = _dot(qn, wq_ref[...])
    for h in range(H):
        base = h * MQ
        q_ref[0, :, base:base + MLA_NOPE] = (qall[:, base:base + MLA_NOPE] * scale).astype(BF16)
        rot = (qall[:, base + MLA_NOPE:base + MQ] * cos
               + qall[:, H * MQ + h * LANES:H * MQ + (h + 1) * LANES] * sin)
        q_ref[0, :, base + MLA_NOPE:base + MQ] = (rot * scale).astype(BF16)
    kvn = _rms(kvdn_ref[0].astype(F32), kvg_ref[...]).astype(BF16)
    kvall = _dot(kvn, wkv_ref[...])
    kr = (misc_ref[0].astype(F32) * cos + sw_ref[0].astype(F32) * sin).astype(BF16)
    for h in range(H):
        src = h * (MLA_NOPE + MLA_V)
        k_ref[0, :, h * MQ:h * MQ + MLA_NOPE] = kvall[:, src:src + MLA_NOPE].astype(BF16)
        k_ref[0, :, h * MQ + MLA_NOPE:(h + 1) * MQ] = kr
        v_ref[0, :, h * MLA_V:(h + 1) * MLA_V] = kvall[:, src + MLA_NOPE:src + MLA_NOPE + MLA_V].astype(BF16)


def _mla_up(z, cos, sin, qg, kvg, wq, wkv):
    B, R, _ = z.shape
    nt = R // TM
    H = MLA_HEADS
    const = lambda b, t: (0, 0)
    return pl.pallas_call(
        _mla_up_kernel,
        grid=(B, nt),
        in_specs=[pl.BlockSpec((1, TM, MLA_Q_RANK), lambda b, t: (b, t, COL_QDN // MLA_Q_RANK)),
                  pl.BlockSpec((1, TM, MLA_KV_RANK), lambda b, t: (b, t, COL_KVDN // MLA_KV_RANK)),
                  pl.BlockSpec((1, TM, LANES), lambda b, t: (b, t, COL_MISC // LANES)),
                  pl.BlockSpec((1, TM, LANES), lambda b, t: (b, t, COL_SW // LANES)),
                  pl.BlockSpec((TM, LANES), lambda b, t: (t, 0)),
                  pl.BlockSpec((TM, LANES), lambda b, t: (t, 0)),
                  pl.BlockSpec(qg.shape, const),
                  pl.BlockSpec(kvg.shape, const),
                  pl.BlockSpec(wq.shape, const),
                  pl.BlockSpec(wkv.shape, const)],
        out_specs=[pl.BlockSpec((1, TM, H * MQ), lambda b, t: (b, t, 0)),
                   pl.BlockSpec((1, TM, H * MQ), lambda b, t: (b, t, 0)),
                   pl.BlockSpec((1, TM, H * MLA_V), lambda b, t: (b, t, 0))],
        out_shape=[jax.ShapeDtypeStruct((B, R, H * MQ), BF16),
                   jax.ShapeDtypeStruct((B, R, H * MQ), BF16),
                   jax.ShapeDtypeStruct((B, R, H * MLA_V), BF16)],
        compiler_params=_cparams("parallel", "parallel"),
        name="mla_up",
    )(z, z, z, z, cos, sin, qg, kvg, wq, wkv)


def _attn_kernel(q_ref, k_ref, v_ref, o_ref, *, q_off):
    R = k_ref.shape[1]

    def attend(nk):
        s = _dot_nt(q_ref[0], k_ref[0, :nk])
        p = jnp.exp(s - jnp.max(s, axis=-1, keepdims=True))
        l = jnp.sum(p, axis=-1, keepdims=True)
        o_ref[0] = (_dot(p.astype(BF16), v_ref[0, :nk]) / l).astype(BF16)

    if q_off == 0:
        qi = pl.program_id(2)

        @pl.when(qi == 0)
        def _():
            attend(CTX_LEN)

        @pl.when(qi > 0)
        def _():
            attend(R)
    else:
        attend(R)


def _attn(q, k, v, *, with_ctx):
    B, R, _ = q.shape
    q_off = 0 if with_ctx else CTX_LEN // TM
    nq = R // TM - q_off
    return pl.pallas_call(
        functools.partial(_attn_kernel, q_off=q_off),
        grid=(B, MLA_HEADS, nq),
        in_specs=[pl.BlockSpec((1, TM, MQ), lambda b, h, i: (b, i + q_off, h)),
                  pl.BlockSpec((1, R, MQ), lambda b, h, i: (b, 0, h)),
                  pl.BlockSpec((1, R, MLA_V), lambda b, h, i: (b, 0, h))],
        out_specs=pl.BlockSpec((1, TM, MLA_V), lambda b, h, i: (b, i + q_off, h)),
        out_shape=jax.ShapeDtypeStruct((B, R, MLA_HEADS * MLA_V), BF16),
        compiler_params=_cparams("parallel", "parallel", "arbitrary"),
        name="mla_attn",
    )(q, k, v)


def _merge_kernel(po_ref, go_ref, mo_ref, gz0_ref, gz1_ref, gz2_ref, x_ref, mod_ref, bw_ref, wo_ref, g_ref,
                  x1_ref, h2_ref):
    D = D_MODEL
    m = None
    for i, (o_ref, gz_ref) in enumerate(((po_ref, gz0_ref), (go_ref, gz1_ref), (mo_ref, gz2_ref))):
        t = jax.nn.sigmoid(gz_ref[0].astype(F32)) * _dot(o_ref[0], bw_ref[i])
        m = t if m is None else m + t
    y = _dot(m.astype(BF16), wo_ref[...])
    mod = mod_ref[0]
    x1 = x_ref[0] + mod[:, 2 * D:3 * D] * y
    x1_ref[0] = x1
    h2 = _rms(x1, g_ref[...]) * (1.0 + mod[:, 4 * D:5 * D]) + mod[:, 3 * D:4 * D]
    h2_ref[0] = h2.astype(BF16)


def _merge(pool_o, gla_o, mla_o, z, xall, modrows, bw, wo, g, *, with_ctx):
    B, R, D = xall.shape
    t_off = 0 if with_ctx else CTX_LEN // TM
    nt = R // TM - t_off
    tok = lambda b, t: (b, t + t_off, 0)
    gzb = COL_GZ // D
    return pl.pallas_call(
        _merge_kernel,
        grid=(B, nt),
        in_specs=[pl.BlockSpec((1, TM, D), tok),
                  pl.BlockSpec((1, TM, D), tok),
                  pl.BlockSpec((1, TM, D), tok),
                  pl.BlockSpec((1, TM, D), lambda b, t: (b, t + t_off, gzb)),
                  pl.BlockSpec((1, TM, D), lambda b, t: (b, t + t_off, gzb + 1)),
                  pl.BlockSpec((1, TM, D), lambda b, t: (b, t + t_off, gzb + 2)),
                  pl.BlockSpec((1, TM, D), tok),
                  pl.BlockSpec((1, 1, 6 * D), lambda b, t: (2 * b + jnp.minimum(t + t_off, 1), 0, 0)),
                  pl.BlockSpec(bw.shape, lambda b, t: (0, 0, 0)),
                  pl.BlockSpec(wo.shape, lambda b, t: (0, 0)),
                  pl.BlockSpec((1, D), lambda b, t: (0, 0))],
        out_specs=[pl.BlockSpec((1, TM, D), lambda b, t: (b, t, 0)),
                   pl.BlockSpec((1, TM, D), lambda b, t: (b, t, 0))],
        out_shape=[jax.ShapeDtypeStruct((B, nt * TM, D), F32),
                   jax.ShapeDtypeStruct((B, nt * TM, D), BF16)],
        compiler_params=_cparams("parallel", "parallel"),
        name="merge",
    )(pool_o, gla_o, mla_o, z, z, z, xall, modrows, bw, wo, g)


def _top16(s):
    nk, n = s.shape
    iota = lax.broadcasted_iota(jnp.int32, (nk, n), 0).astype(F32)
    slot = lax.broadcasted_iota(jnp.int32, (PEER_TOPK, n), 0)
    rank = jnp.full((nk, n), float(PEER_TOPK), F32)
    vals = jnp.zeros((PEER_TOPK, n), F32)
    for j in range(PEER_TOPK):
        m = jnp.max(s, axis=0, keepdims=True)
        idx = jnp.min(jnp.where(s == m, iota, float(nk)), axis=0, keepdims=True)
        hit = iota == idx
        rank = jnp.where(hit, float(j), rank)
        s = jnp.where(hit, -jnp.inf, s)
        vals = jnp.where(slot == j, m, vals)
    return vals, rank


PAIR_SHORT = PEER_TOPK // 2
PAIR_ROWS = PEER_TOPK + (PEER_TOPK - 1) * PAIR_SHORT
PAIR_VALID = sum(PEER_TOPK // (i + 1) for i in range(PEER_TOPK))


def _pair_rows(t2):
    return jnp.concatenate([t2] + [t2[0:PAIR_SHORT]] * (PEER_TOPK - 1), axis=0)


def _pair_candidates(t1, t2):
    K = PEER_TOPK
    n = t1.shape[1]
    first = jnp.concatenate([jnp.broadcast_to(t1[0:1], (K, n))]
                            + [jnp.broadcast_to(t1[i:i + 1], (PAIR_SHORT, n)) for i in range(1, K)], axis=0)
    r = lax.broadcasted_iota(jnp.int32, (PAIR_ROWS, n), 0)
    sh = PAIR_SHORT.bit_length() - 1
    i_of = jnp.where(r < K, 0, ((r - K) >> sh) + 1)
    j_of = jnp.where(r < K, r, (r - K) & (PAIR_SHORT - 1))
    cand = jnp.where((i_of + 1) * (j_of + 1) <= K, first + _pair_rows(t2), -jnp.inf)
    return cand, (i_of * K + j_of).astype(F32)


def _top16_values(s, with_rank):
    n = s.shape[1]
    slot = lax.broadcasted_iota(jnp.int32, (PEER_TOPK, n), 0)
    vals = jnp.zeros((PEER_TOPK, n), F32)
    rank = jnp.full(s.shape, float(PEER_TOPK), F32) if with_rank else None
    for j in range(PEER_TOPK):
        m = jnp.max(s, axis=0, keepdims=True)
        hit = s == m
        if with_rank:
            rank = jnp.where(hit, float(j), rank)
        s = jnp.where(hit, -jnp.inf, s)
        vals = jnp.where(slot == j, m, vals)
    used = jnp.sum(jnp.where(s == -jnp.inf, 1.0, 0.0), axis=0, keepdims=True)
    return vals, rank, jnp.where(used == float(PEER_TOPK), 1.0, 0.0)


def _pair_counts(t1, t2):
    K = PEER_TOPK
    n = t1.shape[1]
    cand0, _ = _pair_candidates(t1, t2)
    cand = cand0
    zsum = jnp.zeros((1, n), F32)
    best = m = None
    for j in range(K):
        m = jnp.max(cand, axis=0, keepdims=True)
        cand = jnp.where(cand == m, -jnp.inf, cand)
        if j == 0:
            best = m
        zsum = zsum + jnp.exp(m - best)
    used = jnp.sum(jnp.where(cand == -jnp.inf, 1.0, 0.0), axis=0, keepdims=True)
    exact = jnp.where(used == float(PAIR_ROWS - PAIR_VALID + K), 1.0, 0.0)
    picked = jnp.where(cand0 >= m, 1.0, 0.0)
    slot = lax.broadcasted_iota(jnp.int32, (K, n), 0)
    cnt = jnp.zeros((K, n), F32)
    for i in range(K):
        blk = picked[0:K] if i == 0 else picked[K + (i - 1) * PAIR_SHORT:K + i * PAIR_SHORT]
        cnt = jnp.where(slot == i, jnp.sum(blk, axis=0, keepdims=True), cnt)
    return cnt, zsum, exact


def _pair_top16(t1, t2):
    K = PEER_TOPK
    n = t1.shape[1]
    cand, flat = _pair_candidates(t1, t2)
    irow = lax.broadcasted_iota(jnp.int32, (K, n), 0).astype(F32)
    cnt = jnp.zeros((K, n), F32)
    zsum = jnp.zeros((1, n), F32)
    best = None
    for j in range(K):
        m = jnp.max(cand, axis=0, keepdims=True)
        idx = jnp.min(jnp.where(cand == m, flat, float(K * K)), axis=0, keepdims=True)
        cand = jnp.where(flat == idx, -jnp.inf, cand)
        cnt = cnt + jnp.where(irow == jnp.floor(idx * (1.0 / K)), 1.0, 0.0)
        if j == 0:
            best = m
        zsum = zsum + jnp.exp(m - best)
    return cnt, zsum


def _peer_score_kernel(h2_ref, wqt_ref, keys_ref, c2_ref, lim_ref, p2_ref, ka_ref, qt_ref):
    K = PEER_TOPK
    qt_ref[...] = _dot_nt(wqt_ref[...], h2_ref[...]).astype(BF16)

    def store(h, cs, s1, s2, t1, rank1, t2, rank2, cnt, zsum):
        rowcnt = jnp.zeros_like(s1)
        for j in range(K):
            in_row = (s1 == t1[j:j + 1]) if rank1 is None else (rank1 == float(j))
            rowcnt = jnp.where(in_row, cnt[j:j + 1], rowcnt)
        c2_ref[h, :, cs] = rank2.astype(BF16)
        lim_ref[h, :, cs] = rowcnt - 1.0
        p2_ref[h, :, cs] = jnp.exp(s2 - t2[0:1]).astype(BF16)
        ka_ref[h, :, cs] = jnp.exp(s1 - t1[0:1]) / zsum

    def head(h, carry):
        base = pl.multiple_of(h * PEER_QDIM, PEER_QDIM)
        s1_all = _dot(keys_ref[h], qt_ref[pl.ds(base, PEER_HALF), :])
        s2_all = _dot(keys_ref[PEER_HEADS + h], qt_ref[pl.ds(base + PEER_HALF, PEER_HALF), :])
        chunks = []
        for c in range(s1_all.shape[1] // LANES):
            cs = slice(c * LANES, (c + 1) * LANES)
            s1, s2 = s1_all[:, cs], s2_all[:, cs]
            t1, _, ok1 = _top16_values(s1, with_rank=False)
            t2, rank2, ok2 = _top16_values(s2, with_rank=True)
            cnt, zsum, ok3 = _pair_counts(t1, t2)
            store(h, cs, s1, s2, t1, None, t2, rank2, cnt, zsum)
            chunks.append((cs, s1, s2, jnp.min(ok1 * ok2 * ok3)))

        for cs, s1, s2, exact in chunks:
            @pl.when(exact < 0.5)
            def _(cs=cs, s1=s1, s2=s2):
                t1x, rank1x = _top16(s1)
                t2x, rank2x = _top16(s2)
                cntx, zx = _pair_top16(t1x, t2x)
                store(h, cs, s1, s2, t1x, rank1x, t2x, rank2x, cntx, zx)

        return carry

    lax.fori_loop(0, PEER_HEADS, head, 0)


PEER_SCORE_TN = 2 * LANES


def _peer_score(h2, wqt, keys):
    ntok, D = h2.shape
    tn = PEER_SCORE_TN
    ospec = pl.BlockSpec((PEER_HEADS, PEER_N_KEYS, tn), lambda i: (0, 0, i))
    tbl = lambda dt: jax.ShapeDtypeStruct((PEER_HEADS, PEER_N_KEYS, ntok), dt)
    return pl.pallas_call(
        _peer_score_kernel,
        grid=(ntok // tn,),
        in_specs=[pl.BlockSpec((tn, D), lambda i: (i, 0)),
                  pl.BlockSpec(wqt.shape, lambda i: (0, 0)),
                  pl.BlockSpec(keys.shape, lambda i: (0, 0, 0))],
        out_specs=[ospec] * 4,
        out_shape=[tbl(BF16), tbl(F32), tbl(BF16), tbl(F32)],
        scratch_shapes=[pltpu.VMEM((PEER_HEADS * PEER_QDIM, tn), BF16)],
        compiler_params=_cparams("parallel"),
        name="peer_score",
    )(h2, wqt, keys)


PEER_KEYS_PER_STEP = 4
PEER_ETILE = PEER_KEYS_PER_STEP * PEER_N_KEYS


def _peer_dense_kernel(h2_ref, c2_ref, lim_ref, p2_ref, ka_ref, u_ref, vt_ref, x1_ref, modc_ref, modl_ref, fg_ref,
                       o_ref, acc_ref, pre_ref, hid_ref, c2s_ref, p2s_ref, *, ctx_rows, rows_per_sample, final):
    D = D_MODEL
    NK = PEER_N_KEYS
    tn = h2_ref.shape[0]
    e = pl.program_id(1)

    @pl.when(e == 0)
    def _():
        acc_ref[...] = jnp.zeros_like(acc_ref)
        c2s_ref[...] = c2_ref[...]
        p2s_ref[...] = p2_ref[...]

    pre_ref[...] = _dot_nt(u_ref[...], h2_ref[...])
    for ka_i in range(PEER_KEYS_PER_STEP):
        a = e * PEER_KEYS_PER_STEP + ka_i
        rows = slice(ka_i * NK, (ka_i + 1) * NK)
        rep = lambda ref, h: jnp.broadcast_to(ref[h, pl.ds(a, 1), :], (BF16_ROWS, tn)).astype(BF16)
        lim = [rep(lim_ref, h) for h in range(PEER_HEADS)]
        kaw = [rep(ka_ref, h) for h in range(PEER_HEADS)]
        for c in range(tn // LANES):
            cs = slice(c * LANES, (c + 1) * LANES)
            w = None
            for h in range(PEER_HEADS):
                sel = c2s_ref[h, :, :, cs] <= lim[h][:, cs][None]
                t = jnp.where(sel, p2s_ref[h, :, :, cs], 0.0) * kaw[h][:, cs][None]
                w = t if w is None else w + t
            x = pre_ref[rows, cs]
            act = 0.5 * x * (1.0 + lax.erf(x * (2.0 ** -0.5)))
            hid_ref[rows, cs] = act.astype(BF16) * w.reshape(NK, LANES)
    acc_ref[...] += _dot(vt_ref[...], hid_ref[...])

    @pl.when(e == pl.num_programs(1) - 1)
    def _():
        y = acc_ref[...].T
        g_lat = modl_ref[0][:, 5 * D:6 * D]
        if ctx_rows:
            g_ctx = modc_ref[0][:, 5 * D:6 * D]
            tiles = rows_per_sample // tn
            r0 = (pl.program_id(0) % tiles) * tn
            row = r0 + lax.broadcasted_iota(jnp.int32, (tn, D), 0)
            gate = jnp.where(row < ctx_rows, g_ctx, g_lat)
        else:
            gate = g_lat
        x2 = x1_ref[...] + gate * y
        if final:
            x2 = _rms(x2, fg_ref[...])
        o_ref[...] = x2


def _peer_dense(h2, sel, u, vt, x1, modrows, fg, *, tn, rows_per_sample, ctx_rows, final):
    ntok, D = h2.shape
    ne = u.shape[0] // PEER_ETILE
    tiles = rows_per_sample // tn
    c2, lim, p2, ka = sel
    packed = lambda t: t.reshape(PEER_HEADS, PEER_N_KEYS // BF16_ROWS, BF16_ROWS, ntok)
    sel = (packed(c2), lim, packed(p2), ka)
    rspec = pl.BlockSpec((PEER_HEADS, PEER_N_KEYS, tn), lambda i, e: (0, 0, i))
    pspec = pl.BlockSpec((PEER_HEADS, PEER_N_KEYS // BF16_ROWS, BF16_ROWS, tn), lambda i, e: (0, 0, 0, i))
    return pl.pallas_call(
        functools.partial(_peer_dense_kernel, ctx_rows=ctx_rows, rows_per_sample=rows_per_sample, final=final),
        grid=(ntok // tn, ne),
        in_specs=[pl.BlockSpec((tn, D), lambda i, e: (i, 0)),
                  pspec, rspec, pspec, rspec,
                  pl.BlockSpec((PEER_ETILE, D), lambda i, e: (e, 0)),
                  pl.BlockSpec((D, PEER_ETILE), lambda i, e: (0, e)),
                  pl.BlockSpec((tn, D), lambda i, e: (i, 0)),
                  pl.BlockSpec((1, 1, 6 * D), lambda i, e: (2 * (i // tiles), 0, 0)),
                  pl.BlockSpec((1, 1, 6 * D), lambda i, e: (2 * (i // tiles) + 1, 0, 0)),
                  pl.BlockSpec((1, D), lambda i, e: (0, 0))],
        out_specs=pl.BlockSpec((tn, D), lambda i, e: (i, 0)),
        out_shape=jax.ShapeDtypeStruct((ntok, D), F32),
        scratch_shapes=[pltpu.VMEM((D, tn), F32),
                        pltpu.VMEM((PEER_ETILE, tn), F32),
                        pltpu.VMEM((PEER_ETILE, tn), BF16),
                        pltpu.VMEM((PEER_HEADS, PEER_N_KEYS // BF16_ROWS, BF16_ROWS, tn), BF16),
                        pltpu.VMEM((PEER_HEADS, PEER_N_KEYS // BF16_ROWS, BF16_ROWS, tn), BF16)],
        compiler_params=_cparams("parallel", "arbitrary"),
        name="peer_dense",
    )(h2, *sel, u, vt, x1, modrows, modrows, fg)


def _peer_tile(rows_per_sample):
    for tn in (768, 512, 256):
        if rows_per_sample % tn == 0:
            return tn
    raise ValueError(rows_per_sample)


_SWAP = np.concatenate([np.arange(16, 32), np.arange(0, 16), np.arange(48, 64), np.arange(32, 48)])


def _layout_w_in(w):
    D = w.shape[0]
    idx = np.cumsum(IN_SPLITS)[:-1]
    p, q, k, v, r, lrf, lrb, qdn, kvdn, krope, gz = jnp.split(w, idx, axis=1)
    zeros = lambda n: jnp.zeros((D, n), w.dtype)
    misc = jnp.concatenate([krope, lrf, lrb, zeros(LANES - MLA_ROPE - 2 * GLA_GATE_RANK)], axis=1)
    sw = jnp.concatenate([krope[:, _SWAP], zeros(LANES - MLA_ROPE)], axis=1)
    out = jnp.concatenate([p, q, k, v, r, gz, kvdn, misc, sw, qdn], axis=1)
    return jnp.concatenate([out, zeros(Z_COLS - out.shape[1])], axis=1).astype(BF16)


def _layout_gate_w(gate_w, row0):
    HK = GLA_HEADS * GLA_DK
    pad = jnp.zeros((LANES, HK), gate_w.dtype)
    return pad.at[row0:row0 + GLA_GATE_RANK].set(gate_w).astype(BF16)


def _layout_w_uq(w):
    H = MLA_HEADS
    w3 = w.reshape(MLA_Q_RANK, H, MLA_NOPE + MLA_ROPE)
    qn, qr = w3[..., :MLA_NOPE], w3[..., MLA_NOPE:]
    z = jnp.zeros((MLA_Q_RANK, H, LANES - MLA_ROPE), w.dtype)
    main = jnp.concatenate([qn, qr, z], axis=-1).reshape(MLA_Q_RANK, H * MQ)
    swp = jnp.concatenate([qr[..., _SWAP], z], axis=-1).reshape(MLA_Q_RANK, H * LANES)
    return jnp.concatenate([main, swp], axis=1).astype(BF16)


def _rope_tables(seq):
    half = MLA_ROPE // 2
    t = jnp.arange(seq)
    inv = ROPE_BASE ** (-jnp.arange(0, half, 2, dtype=F32) / half)
    ar = (t // GRID_W).astype(F32)[:, None] * inv
    ac = (t % GRID_W).astype(F32)[:, None] * inv
    cos = jnp.concatenate([jnp.cos(ar), jnp.cos(ar), jnp.cos(ac), jnp.cos(ac)], axis=1)
    sin = jnp.concatenate([-jnp.sin(ar), jnp.sin(ar), -jnp.sin(ac), jnp.sin(ac)], axis=1)
    cos = jnp.concatenate([jnp.ones((CTX_LEN, MLA_ROPE), F32), cos], axis=0)
    sin = jnp.concatenate([jnp.zeros((CTX_LEN, MLA_ROPE), F32), sin], axis=0)
    pad = jnp.zeros((CTX_LEN + seq, LANES - MLA_ROPE), F32)
    return jnp.concatenate([cos, pad], axis=1), jnp.concatenate([sin, pad], axis=1)


def kernel(x, c, ctx, c_ctx, ada_w, ada_b, norm1_g, norm2_g, w_in, pool_w, pool_scale, gla_gate_w, gla_gate_b, gla_norm_g, mla_q_norm_g, mla_kv_norm_g, mla_w_uq, mla_w_ukv, branch_w, w_out, peer_wq, peer_keys, peer_u, peer_v, final_norm_g):
    B, T, D = x.shape
    R = CTX_LEN + T
    assert D == D_MODEL and ctx.shape[1] == CTX_LEN == TM and T % TM == 0 and T % GRID_W == 0

    nrow = -(-(B + 1) // 8) * 8
    cc = jnp.concatenate([c, c_ctx[None], jnp.zeros((nrow - B - 1, D), F32)], axis=0)
    mod = _ada(cc, ada_w, ada_b)
    modrows = jnp.stack([jnp.broadcast_to(mod[:, B:B + 1], (DEPTH, B, 6 * D)), mod[:, :B]], axis=2)
    modrows = modrows.reshape(DEPTH, 2 * B, 1, 6 * D)

    cos, sin = _rope_tables(T)
    xall = jnp.concatenate([ctx, x], axis=1)
    row = lambda a: a.reshape(1, -1)

    for i in range(DEPTH):
        with_ctx = i < DEPTH - 1
        final = i == DEPTH - 1
        z = _in_proj(xall, modrows[i], row(norm1_g[i]), _layout_w_in(w_in[i]))

        pool_o = _pool(z, pool_w[i].astype(BF16), row(pool_scale[i]))

        o_b = _gla(z, _layout_gate_w(gla_gate_w[i, 1], MISC_LRB), row(gla_gate_b[i, 1]), fwd=False)
        gla_o = _gla(z, _layout_gate_w(gla_gate_w[i, 0], MISC_LRF), row(gla_gate_b[i, 0]), fwd=True,
                     ob=o_b, gn=row(gla_norm_g[i]))

        q, k, v = _mla_up(z, cos, sin, row(mla_q_norm_g[i]), row(mla_kv_norm_g[i]),
                          _layout_w_uq(mla_w_uq[i]), mla_w_ukv[i].astype(BF16))
        mla_o = _attn(q, k, v, with_ctx=with_ctx)

        x1, h2 = _merge(pool_o, gla_o, mla_o, z, xall, modrows[i], branch_w[i].astype(BF16),
                        w_out[i].astype(BF16), row(norm2_g[i]), with_ctx=with_ctx)

        rows = x1.shape[1]
        h2f = h2.reshape(B * rows, D)
        keys = peer_keys[i].reshape(2 * PEER_HEADS, PEER_N_KEYS, PEER_HALF).astype(BF16)
        sel = _peer_score(h2f, peer_wq[i].T.astype(BF16), keys)
        xall = _peer_dense(h2f, sel, peer_u[i].astype(BF16), peer_v[i].T.astype(BF16), x1.reshape(B * rows, D),
                           modrows[i], row(final_norm_g), tn=_peer_tile(rows), rows_per_sample=rows,
                           ctx_rows=CTX_LEN if with_ctx else 0, final=final).reshape(B, rows, D)
    return xall
```

```python
import functools

import jax
import jax.numpy as jnp
import numpy as np
from jax import lax
from jax.experimental import pallas as pl
from jax.experimental.pallas import tpu as pltpu

F32 = jnp.float32
BF16 = jnp.bfloat16

D_MODEL = 1024
DEPTH = 2
CTX_LEN = 256
GRID_W = 64
NORM_EPS = 1e-6
POOL_WIDTH = 1024
POOL_WINDOWS = (2, 4, 8, 16)
POOL_GROUP = POOL_WIDTH // len(POOL_WINDOWS)
GLA_HEADS = 4
GLA_DK = 128
GLA_DV = 256
GLA_GATE_RANK = 16
GLA_TAU = 16.0
GLA_CHUNK = 64
MLA_HEADS = 8
MLA_Q_RANK = 384
MLA_KV_RANK = 256
MLA_NOPE = 128
MLA_ROPE = 64
MLA_V = 128
ROPE_BASE = 10000.0
N_BRANCH = 3
PEER_HEADS = 8
PEER_N_KEYS = 128
PEER_TOPK = 16
PEER_QDIM = 256
PEER_HALF = PEER_QDIM // 2
IN_SPLITS = (POOL_WIDTH, GLA_HEADS * GLA_DK, GLA_HEADS * GLA_DK, GLA_HEADS * GLA_DV, GLA_HEADS * GLA_DV,
             GLA_GATE_RANK, GLA_GATE_RANK, MLA_Q_RANK, MLA_KV_RANK, MLA_ROPE, N_BRANCH * D_MODEL)

LANES = 128
BF16_ROWS = 16
TM = 256
VMEM_LIMIT = 56 * 1024 * 1024

COL_P = 0
COL_Q = 1024
COL_K = 1536
COL_V = 2048
COL_R = 3072
COL_GZ = 4096
COL_KVDN = 7168
COL_MISC = 7424
COL_SW = 7552
COL_QDN = 7680
Z_COLS = 8192
IN_NBLK = 2048
MISC_LRF = MLA_ROPE
MISC_LRB = MLA_ROPE + GLA_GATE_RANK
MQ = MLA_NOPE + LANES


def _cparams(*sem):
    return pltpu.CompilerParams(dimension_semantics=sem, vmem_limit_bytes=VMEM_LIMIT)


def _rms(x, g):
    return x * lax.rsqrt(jnp.mean(x * x, axis=-1, keepdims=True) + NORM_EPS) * g


def _dot(a, b):
    return jnp.dot(a, b, preferred_element_type=F32)


def _dot_nt(a, b):
    return lax.dot_general(a, b, (((1,), (1,)), ((), ())), preferred_element_type=F32)


def _dot_tn(a, b):
    return lax.dot_general(a, b, (((0,), (0,)), ((), ())), preferred_element_type=F32)


def _ada_kernel(c_ref, w_ref, b_ref, o_ref):
    c = c_ref[...]
    a = (c * jax.nn.sigmoid(c)).astype(BF16)
    o_ref[0] = _dot(a, w_ref[0].astype(BF16)) + b_ref[0]


def _ada(cc, ada_w, ada_b):
    L, D, N = ada_w.shape
    rows = cc.shape[0]
    nb = N // D
    return pl.pallas_call(
        _ada_kernel,
        grid=(L, nb),
        in_specs=[pl.BlockSpec((rows, D), lambda l, j: (0, 0)),
                  pl.BlockSpec((1, D, D), lambda l, j: (l, 0, j)),
                  pl.BlockSpec((1, 1, D), lambda l, j: (l, 0, j))],
        out_specs=pl.BlockSpec((1, rows, D), lambda l, j: (l, 0, j)),
        out_shape=jax.ShapeDtypeStruct((L, rows, N), F32),
        compiler_params=_cparams("parallel", "parallel"),
        name="ada_mod",
    )(cc, ada_w, ada_b.reshape(L, 1, N))


def _in_proj_kernel(x_ref, mod_ref, g_ref, w_ref, z_ref):
    D = D_MODEL
    x = x_ref[0]
    mod = mod_ref[0]
    h = _rms(x, g_ref[...]) * (1.0 + mod[:, D:2 * D]) + mod[:, 0:D]
    z_ref[0] = _dot(h.astype(BF16), w_ref[...]).astype(BF16)


def _in_proj(xall, modrows, g, w):
    B, R, D = xall.shape
    nt = R // TM
    nn = Z_COLS // IN_NBLK
    return pl.pallas_call(
        _in_proj_kernel,
        grid=(nn, B, nt),
        in_specs=[pl.BlockSpec((1, TM, D), lambda n, b, t: (b, t, 0)),
                  pl.BlockSpec((1, 1, 6 * D), lambda n, b, t: (2 * b + jnp.minimum(t, 1), 0, 0)),
                  pl.BlockSpec((1, D), lambda n, b, t: (0, 0)),
                  pl.BlockSpec((D, IN_NBLK), lambda n, b, t: (0, n))],
        out_specs=pl.BlockSpec((1, TM, IN_NBLK), lambda n, b, t: (b, t, n)),
        out_shape=jax.ShapeDtypeStruct((B, R, Z_COLS), BF16),
        compiler_params=_cparams("parallel", "parallel", "parallel"),
        name="in_proj",
    )(xall, modrows, g, w)


def _pool_kernel(u_ref, w_ref, sc_ref, o_ref):
    R = u_ref.shape[1]
    row = lax.broadcasted_iota(jnp.int32, (R, POOL_GROUP), 0)
    seg_lo = jnp.where(row < CTX_LEN, 0, CTX_LEN)
    seg_hi = jnp.where(row < CTX_LEN, CTX_LEN, R)
    for gi, win in enumerate(POOL_WINDOWS):
        sl = slice(gi * POOL_GROUP, (gi + 1) * POOL_GROUP)
        u = u_ref[0, :, sl].astype(F32)
        lo_off, hi_off = win // 2, win - win // 2
        acc = jnp.zeros_like(u)
        for d in range(-lo_off, hi_off):
            shifted = u if d == 0 else pltpu.roll(u, (R - d) % R, axis=0)
            ok = (row + d >= seg_lo) & (row + d < seg_hi)
            acc = acc + jnp.where(ok, shifted, 0.0)
        cnt = (jnp.minimum(row + hi_off, seg_hi) - jnp.maximum(row - lo_off, seg_lo)).astype(F32)
        diff = (acc / cnt - u).astype(BF16)
        o_ref[0, :, sl] = (_dot(diff, w_ref[gi]) * sc_ref[:, sl]).astype(BF16)


def _pool(z, pool_w, pool_scale):
    B, R, _ = z.shape
    return pl.pallas_call(
        _pool_kernel,
        grid=(B,),
        in_specs=[pl.BlockSpec((1, R, POOL_WIDTH), lambda b: (b, 0, COL_P // POOL_WIDTH)),
                  pl.BlockSpec(pool_w.shape, lambda b: (0, 0, 0)),
                  pl.BlockSpec((1, POOL_WIDTH), lambda b: (0, 0))],
        out_specs=pl.BlockSpec((1, R, POOL_WIDTH), lambda b: (b, 0, 0)),
        out_shape=jax.ShapeDtypeStruct((B, R, POOL_WIDTH), BF16),
        compiler_params=_cparams("parallel"),
        name="pool",
    )(z, pool_w, pool_scale)


def _split3(x):
    a = x.astype(BF16)
    r = x - a.astype(F32)
    b = r.astype(BF16)
    c = (r - b.astype(F32)).astype(BF16)
    return a, b, c


def _gla_kernel(*refs, fwd):
    if fwd:
        q_ref, k_ref, v_ref, misc_ref, gw_ref, gb_ref, ob_ref, r_ref, gn_ref, o_ref, st_ref = refs
    else:
        q_ref, k_ref, v_ref, misc_ref, gw_ref, gb_ref, o_ref, st_ref = refs
    C = GLA_CHUNK
    nchunk = TM // C
    HK = GLA_HEADS * GLA_DK

    @pl.when(pl.program_id(1) == 0)
    def _():
        st_ref[...] = jnp.zeros_like(st_ref)

    zg = _dot(misc_ref[0], gw_ref[...]) + gb_ref[...]
    g = (jnp.minimum(zg, 0.0) - jnp.log1p(jnp.exp(-jnp.abs(zg)))) * (1.0 / GLA_TAU)

    row = lax.broadcasted_iota(jnp.int32, (TM, TM), 0)
    col = lax.broadcasted_iota(jnp.int32, (TM, TM), 1)
    shift = C.bit_length() - 1
    same = (row >> shift) == (col >> shift)
    tri = same & ((col <= row) if fwd else (col >= row))
    ones = jnp.concatenate([jnp.where(tri, 1.0, 0.0), jnp.where(same, 1.0, 0.0)], axis=0).astype(BF16)
    g1, g2, g3 = _split3(g)
    both = _dot(ones, g1) + _dot(ones, g2) + _dot(ones, g3)
    b = both[:TM]
    bt = both[TM:]

    qe = q_ref[0].astype(F32) * (GLA_DK ** -0.5) * jnp.exp(b)
    ke = (k_ref[0].astype(F32) * jnp.exp(-b)).astype(BF16)
    kd = (k_ref[0].astype(F32) * jnp.exp(bt - b)).astype(BF16)
    qe = qe.astype(BF16)
    dec = jnp.exp(bt)
    v = v_ref[0]

    order = range(nchunk) if fwd else range(nchunk - 1, -1, -1)
    for h in range(GLA_HEADS):
        ks = slice(h * GLA_DK, (h + 1) * GLA_DK)
        vs = slice(h * GLA_DV, (h + 1) * GLA_DV)
        att = jnp.where(tri, _dot_nt(qe[:, ks], ke[:, ks]), 0.0).astype(BF16)
        o_in = _dot(att, v[:, vs])
        st = st_ref[h]
        parts = [None] * nchunk
        for j in order:
            rs = slice(j * C, (j + 1) * C)
            parts[j] = o_in[rs] + _dot_nt(qe[rs, ks], st.astype(BF16))
            st = st * dec[j * C:j * C + 1, ks] + _dot_tn(v[rs, vs], kd[rs, ks])
        st_ref[h] = st
        o = jnp.concatenate(parts, axis=0)
        if fwd:
            o = o + ob_ref[0, :, vs].astype(F32)
            r = r_ref[0, :, vs].astype(F32)
            o = _rms(o, gn_ref[...]) * (r * jax.nn.sigmoid(r))
        o_ref[0, :, vs] = o.astype(BF16)


def _gla(z, gw, gb, *, fwd, ob=None, gn=None):
    B, R, _ = z.shape
    nt = R // TM
    HK, HV = GLA_HEADS * GLA_DK, GLA_HEADS * GLA_DV
    if fwd:
        tile = lambda c: c
    else:
        tile = lambda c: jnp.where(c == 0, 0, nt - c)
    in_specs = [pl.BlockSpec((1, TM, HK), lambda b, c: (b, tile(c), COL_Q // HK)),
                pl.BlockSpec((1, TM, HK), lambda b, c: (b, tile(c), COL_K // HK)),
                pl.BlockSpec((1, TM, HV), lambda b, c: (b, tile(c), COL_V // HV)),
                pl.BlockSpec((1, TM, LANES), lambda b, c: (b, tile(c), COL_MISC // LANES)),
                pl.BlockSpec((LANES, HK), lambda b, c: (0, 0)),
                pl.BlockSpec((1, HK), lambda b, c: (0, 0))]
    args = [z, z, z, z, gw, gb]
    if fwd:
        in_specs += [pl.BlockSpec((1, TM, HV), lambda b, c: (b, tile(c), 0)),
                     pl.BlockSpec((1, TM, HV), lambda b, c: (b, tile(c), COL_R // HV)),
                     pl.BlockSpec((1, GLA_DV), lambda b, c: (0, 0))]
        args += [ob, z, gn]
    return pl.pallas_call(
        functools.partial(_gla_kernel, fwd=fwd),
        grid=(B, nt),
        in_specs=in_specs,
        out_specs=pl.BlockSpec((1, TM, HV), lambda b, c: (b, tile(c), 0)),
        out_shape=jax.ShapeDtypeStruct((B, R, HV), BF16),
        scratch_shapes=[pltpu.VMEM((GLA_HEADS, GLA_DV, GLA_DK), F32)],
        compiler_params=_cparams("parallel", "arbitrary"),
        name="gla_fwd" if fwd else "gla_bwd",
    )(*args)


def _mla_up_kernel(qdn_ref, kvdn_ref, misc_ref, sw_ref, cos_ref, sin_ref, qg_ref, kvg_ref, wq_ref, wkv_ref,
                   q_ref, k_ref, v_ref):
    H = MLA_HEADS
    scale = (MLA_NOPE + MLA_ROPE) ** -0.5
    cos = cos_ref[...]
    sin = sin_ref[...]
    qn = _rms(qdn_ref[0].astype(F32), qg_ref[...]).astype(BF16)
    qall = _dot(qn, wq_ref[...])
    for h in range(H):
        base = h * MQ
        q_ref[0, :, base:base + MLA_NOPE] = (qall[:, base:base + MLA_NOPE] * scale).astype(BF16)
        rot = (qall[:, base + MLA_NOPE:base + MQ] * cos
               + qall[:, H * MQ + h * LANES:H * MQ + (h + 1) * LANES] * sin)
        q_ref[0, :, base + MLA_NOPE:base + MQ] = (rot * scale).astype(BF16)
    kvn = _rms(kvdn_ref[0].astype(F32), kvg_ref[...]).astype(BF16)
    kvall = _dot(kvn, wkv_ref[...])
    kr = (misc_ref[0].astype(F32) * cos + sw_ref[0].astype(F32) * sin).astype(BF16)
    for h in range(H):
        src = h * (MLA_NOPE + MLA_V)
        k_ref[0, :, h * MQ:h * MQ + MLA_NOPE] = kvall[:, src:src + MLA_NOPE].astype(BF16)
        k_ref[0, :, h * MQ + MLA_NOPE:(h + 1) * MQ] = kr
        v_ref[0, :, h * MLA_V:(h + 1) * MLA_V] = kvall[:, src + MLA_NOPE:src + MLA_NOPE + MLA_V].astype(BF16)


def _mla_up(z, cos, sin, qg, kvg, wq, wkv):
    B, R, _ = z.shape
    nt = R // TM
    H = MLA_HEADS
    const = lambda b, t: (0, 0)
    return pl.pallas_call(
        _mla_up_kernel,
        grid=(B, nt),
        in_specs=[pl.BlockSpec((1, TM, MLA_Q_RANK), lambda b, t: (b, t, COL_QDN // MLA_Q_RANK)),
                  pl.BlockSpec((1, TM, MLA_KV_RANK), lambda b, t: (b, t, COL_KVDN // MLA_KV_RANK)),
                  pl.BlockSpec((1, TM, LANES), lambda b, t: (b, t, COL_MISC // LANES)),
                  pl.BlockSpec((1, TM, LANES), lambda b, t: (b, t, COL_SW // LANES)),
                  pl.BlockSpec((TM, LANES), lambda b, t: (t, 0)),
                  pl.BlockSpec((TM, LANES), lambda b, t: (t, 0)),
                  pl.BlockSpec(qg.shape, const),
                  pl.BlockSpec(kvg.shape, const),
                  pl.BlockSpec(wq.shape, const),
                  pl.BlockSpec(wkv.shape, const)],
        out_specs=[pl.BlockSpec((1, TM, H * MQ), lambda b, t: (b, t, 0)),
                   pl.BlockSpec((1, TM, H * MQ), lambda b, t: (b, t, 0)),
                   pl.BlockSpec((1, TM, H * MLA_V), lambda b, t: (b, t, 0))],
        out_shape=[jax.ShapeDtypeStruct((B, R, H * MQ), BF16),
                   jax.ShapeDtypeStruct((B, R, H * MQ), BF16),
                   jax.ShapeDtypeStruct((B, R, H * MLA_V), BF16)],
        compiler_params=_cparams("parallel", "parallel"),
        name="mla_up",
    )(z, z, z, z, cos, sin, qg, kvg, wq, wkv)


def _attn_kernel(q_ref, k_ref, v_ref, o_ref, *, q_off):
    R = k_ref.shape[1]

    def attend(nk):
        s = _dot_nt(q_ref[0], k_ref[0, :nk])
        p = jnp.exp(s - jnp.max(s, axis=-1, keepdims=True))
        l = jnp.sum(p, axis=-1, keepdims=True)
        o_ref[0] = (_dot(p.astype(BF16), v_ref[0, :nk]) / l).astype(BF16)

    if q_off == 0:
        qi = pl.program_id(2)

        @pl.when(qi == 0)
        def _():
            attend(CTX_LEN)

        @pl.when(qi > 0)
        def _():
            attend(R)
    else:
        attend(R)


def _attn(q, k, v, *, with_ctx):
    B, R, _ = q.shape
    q_off = 0 if with_ctx else CTX_LEN // TM
    nq = R // TM - q_off
    return pl.pallas_call(
        functools.partial(_attn_kernel, q_off=q_off),
        grid=(B, MLA_HEADS, nq),
        in_specs=[pl.BlockSpec((1, TM, MQ), lambda b, h, i: (b, i + q_off, h)),
                  pl.BlockSpec((1, R, MQ), lambda b, h, i: (b, 0, h)),
                  pl.BlockSpec((1, R, MLA_V), lambda b, h, i: (b, 0, h))],
        out_specs=pl.BlockSpec((1, TM, MLA_V), lambda b, h, i: (b, i + q_off, h)),
        out_shape=jax.ShapeDtypeStruct((B, R, MLA_HEADS * MLA_V), BF16),
        compiler_params=_cparams("parallel", "parallel", "arbitrary"),
        name="mla_attn",
    )(q, k, v)


def _merge_kernel(po_ref, go_ref, mo_ref, gz0_ref, gz1_ref, gz2_ref, x_ref, mod_ref, bw_ref, wo_ref, g_ref,
                  x1_ref, h2_ref):
    D = D_MODEL
    m = None
    for i, (o_ref, gz_ref) in enumerate(((po_ref, gz0_ref), (go_ref, gz1_ref), (mo_ref, gz2_ref))):
        t = jax.nn.sigmoid(gz_ref[0].astype(F32)) * _dot(o_ref[0], bw_ref[i])
        m = t if m is None else m + t
    y = _dot(m.astype(BF16), wo_ref[...])
    mod = mod_ref[0]
    x1 = x_ref[0] + mod[:, 2 * D:3 * D] * y
    x1_ref[0] = x1
    h2 = _rms(x1, g_ref[...]) * (1.0 + mod[:, 4 * D:5 * D]) + mod[:, 3 * D:4 * D]
    h2_ref[0] = h2.astype(BF16)


def _merge(pool_o, gla_o, mla_o, z, xall, modrows, bw, wo, g, *, with_ctx):
    B, R, D = xall.shape
    t_off = 0 if with_ctx else CTX_LEN // TM
    nt = R // TM - t_off
    tok = lambda b, t: (b, t + t_off, 0)
    gzb = COL_GZ // D
    return pl.pallas_call(
        _merge_kernel,
        grid=(B, nt),
        in_specs=[pl.BlockSpec((1, TM, D), tok),
                  pl.BlockSpec((1, TM, D), tok),
                  pl.BlockSpec((1, TM, D), tok),
                  pl.BlockSpec((1, TM, D), lambda b, t: (b, t + t_off, gzb)),
                  pl.BlockSpec((1, TM, D), lambda b, t: (b, t + t_off, gzb + 1)),
                  pl.BlockSpec((1, TM, D), lambda b, t: (b, t + t_off, gzb + 2)),
                  pl.BlockSpec((1, TM, D), tok),
                  pl.BlockSpec((1, 1, 6 * D), lambda b, t: (2 * b + jnp.minimum(t + t_off, 1), 0, 0)),
                  pl.BlockSpec(bw.shape, lambda b, t: (0, 0, 0)),
                  pl.BlockSpec(wo.shape, lambda b, t: (0, 0)),
                  pl.BlockSpec((1, D), lambda b, t: (0, 0))],
        out_specs=[pl.BlockSpec((1, TM, D), lambda b, t: (b, t, 0)),
                   pl.BlockSpec((1, TM, D), lambda b, t: (b, t, 0))],
        out_shape=[jax.ShapeDtypeStruct((B, nt * TM, D), F32),
                   jax.ShapeDtypeStruct((B, nt * TM, D), BF16)],
        compiler_params=_cparams("parallel", "parallel"),
        name="merge",
    )(pool_o, gla_o, mla_o, z, z, z, xall, modrows, bw, wo, g)


def _top16(s):
    nk, n = s.shape
    iota = lax.broadcasted_iota(jnp.int32, (nk, n), 0).astype(F32)
    slot = lax.broadcasted_iota(jnp.int32, (PEER_TOPK, n), 0)
    rank = jnp.full((nk, n), float(PEER_TOPK), F32)
    vals = jnp.zeros((PEER_TOPK, n), F32)
    for j in range(PEER_TOPK):
        m = jnp.max(s, axis=0, keepdims=True)
        idx = jnp.min(jnp.where(s == m, iota, float(nk)), axis=0, keepdims=True)
        hit = iota == idx
        rank = jnp.where(hit, float(j), rank)
        s = jnp.where(hit, -jnp.inf, s)
        vals = jnp.where(slot == j, m, vals)
    return vals, rank


PAIR_SHORT = PEER_TOPK // 2
PAIR_ROWS = PEER_TOPK + (PEER_TOPK - 1) * PAIR_SHORT
PAIR_VALID = sum(PEER_TOPK // (i + 1) for i in range(PEER_TOPK))


def _pair_rows(t2):
    return jnp.concatenate([t2] + [t2[0:PAIR_SHORT]] * (PEER_TOPK - 1), axis=0)


def _pair_candidates(t1, t2):
    K = PEER_TOPK
    n = t1.shape[1]
    first = jnp.concatenate([jnp.broadcast_to(t1[0:1], (K, n))]
                            + [jnp.broadcast_to(t1[i:i + 1], (PAIR_SHORT, n)) for i in range(1, K)], axis=0)
    r = lax.broadcasted_iota(jnp.int32, (PAIR_ROWS, n), 0)
    sh = PAIR_SHORT.bit_length() - 1
    i_of = jnp.where(r < K, 0, ((r - K) >> sh) + 1)
    j_of = jnp.where(r < K, r, (r - K) & (PAIR_SHORT - 1))
    cand = jnp.where((i_of + 1) * (j_of + 1) <= K, first + _pair_rows(t2), -jnp.inf)
    return cand, (i_of * K + j_of).astype(F32)


def _top16_values(s, with_rank):
    n = s.shape[1]
    slot = lax.broadcasted_iota(jnp.int32, (PEER_TOPK, n), 0)
    vals = jnp.zeros((PEER_TOPK, n), F32)
    rank = jnp.full(s.shape, float(PEER_TOPK), F32) if with_rank else None
    for j in range(PEER_TOPK):
        m = jnp.max(s, axis=0, keepdims=True)
        hit = s == m
        if with_rank:
            rank = jnp.where(hit, float(j), rank)
        s = jnp.where(hit, -jnp.inf, s)
        vals = jnp.where(slot == j, m, vals)
    used = jnp.sum(jnp.where(s == -jnp.inf, 1.0, 0.0), axis=0, keepdims=True)
    return vals, rank, jnp.where(used == float(PEER_TOPK), 1.0, 0.0)


def _pair_counts(t1, t2):
    K = PEER_TOPK
    n = t1.shape[1]
    cand0, _ = _pair_candidates(t1, t2)
    cand = cand0
    zsum = jnp.zeros((1, n), F32)
    best = m = None
    for j in range(K):
        m = jnp.max(cand, axis=0, keepdims=True)
        cand = jnp.where(cand == m, -jnp.inf, cand)
        if j == 0:
            best = m
        zsum = zsum + jnp.exp(m - best)
    used = jnp.sum(jnp.where(cand == -jnp.inf, 1.0, 0.0), axis=0, keepdims=True)
    exact = jnp.where(used == float(PAIR_ROWS - PAIR_VALID + K), 1.0, 0.0)
    picked = jnp.where(cand0 >= m, 1.0, 0.0)
    slot = lax.broadcasted_iota(jnp.int32, (K, n), 0)
    cnt = jnp.zeros((K, n), F32)
    for i in range(K):
        blk = picked[0:K] if i == 0 else picked[K + (i - 1) * PAIR_SHORT:K + i * PAIR_SHORT]
        cnt = jnp.where(slot == i, jnp.sum(blk, axis=0, keepdims=True), cnt)
    return cnt, zsum, exact


def _pair_top16(t1, t2):
    K = PEER_TOPK
    n = t1.shape[1]
    cand, flat = _pair_candidates(t1, t2)
    irow = lax.broadcasted_iota(jnp.int32, (K, n), 0).astype(F32)
    cnt = jnp.zeros((K, n), F32)
    zsum = jnp.zeros((1, n), F32)
    best = None
    for j in range(K):
        m = jnp.max(cand, axis=0, keepdims=True)
        idx = jnp.min(jnp.where(cand == m, flat, float(K * K)), axis=0, keepdims=True)
        cand = jnp.where(flat == idx, -jnp.inf, cand)
        cnt = cnt + jnp.where(irow == jnp.floor(idx * (1.0 / K)), 1.0, 0.0)
        if j == 0:
            best = m
        zsum = zsum + jnp.exp(m - best)
    return cnt, zsum


CNT_BITS = 0xFF


def _words_per_row():
    return 4 // jnp.dtype(BF16).itemsize


def _peer_score_kernel(h2_ref, wqt_ref, keys_ref, c2_ref, p2_ref, ak_ref, qt_ref):
    K = PEER_TOPK
    qt_ref[...] = _dot_nt(wqt_ref[...], h2_ref[...]).astype(BF16)

    def store(h, cs, s1, s2, t1, rank1, t2, rank2, cnt, zsum):
        rowcnt = jnp.zeros_like(s1)
        for j in range(K):
            in_row = (s1 == t1[j:j + 1]) if rank1 is None else (rank1 == float(j))
            rowcnt = jnp.where(in_row, cnt[j:j + 1], rowcnt)
        c2_ref[h, :, cs] = pltpu.bitcast(rank2.astype(BF16), jnp.int32)
        p2_ref[h, :, cs] = pltpu.bitcast(jnp.exp(s2 - t2[0:1]).astype(BF16), jnp.int32)
        ka = (jnp.exp(s1 - t1[0:1]) / zsum).astype(BF16).astype(F32)
        ak_ref[h, :, cs] = (lax.bitcast_convert_type(ka, jnp.int32) & ~CNT_BITS) | rowcnt.astype(jnp.int32)

    def head(h, carry):
        base = pl.multiple_of(h * PEER_QDIM, PEER_QDIM)
        s1_all = _dot(keys_ref[h], qt_ref[pl.ds(base, PEER_HALF), :])
        s2_all = _dot(keys_ref[PEER_HEADS + h], qt_ref[pl.ds(base + PEER_HALF, PEER_HALF), :])
        chunks = []
        for c in range(s1_all.shape[1] // LANES):
            cs = slice(c * LANES, (c + 1) * LANES)
            s1, s2 = s1_all[:, cs], s2_all[:, cs]
            t1, _, ok1 = _top16_values(s1, with_rank=False)
            t2, rank2, ok2 = _top16_values(s2, with_rank=True)
            cnt, zsum, ok3 = _pair_counts(t1, t2)
            store(h, cs, s1, s2, t1, None, t2, rank2, cnt, zsum)
            chunks.append((cs, s1, s2, jnp.min(ok1 * ok2 * ok3)))

        for cs, s1, s2, exact in chunks:
            @pl.when(exact < 0.5)
            def _(cs=cs, s1=s1, s2=s2):
                t1x, rank1x = _top16(s1)
                t2x, rank2x = _top16(s2)
                cntx, zx = _pair_top16(t1x, t2x)
                store(h, cs, s1, s2, t1x, rank1x, t2x, rank2x, cntx, zx)

        return carry

    lax.fori_loop(0, PEER_HEADS, head, 0)


PEER_SCORE_TN = 2 * LANES


def _peer_score(h2, wqt, keys):
    ntok, D = h2.shape
    tn = PEER_SCORE_TN
    nw = PEER_N_KEYS // _words_per_row()
    tbl = lambda rows: jax.ShapeDtypeStruct((PEER_HEADS, rows, ntok), jnp.int32)
    ospec = lambda rows: pl.BlockSpec((PEER_HEADS, rows, tn), lambda i: (0, 0, i))
    return pl.pallas_call(
        _peer_score_kernel,
        grid=(ntok // tn,),
        in_specs=[pl.BlockSpec((tn, D), lambda i: (i, 0)),
                  pl.BlockSpec(wqt.shape, lambda i: (0, 0)),
                  pl.BlockSpec(keys.shape, lambda i: (0, 0, 0))],
        out_specs=[ospec(nw), ospec(nw), ospec(PEER_N_KEYS)],
        out_shape=[tbl(nw), tbl(nw), tbl(PEER_N_KEYS)],
        scratch_shapes=[pltpu.VMEM((PEER_HEADS * PEER_QDIM, tn), BF16)],
        compiler_params=_cparams("parallel"),
        name="peer_score",
    )(h2, wqt, keys)


PEER_KEYS_PER_STEP = 4
PEER_ETILE = PEER_KEYS_PER_STEP * PEER_N_KEYS


def _peer_dense_kernel(h2_ref, c2_ref, p2_ref, ak_ref, u_ref, vt_ref, x1_ref, modc_ref, modl_ref, fg_ref,
                       o_ref, acc_ref, pre_ref, hid_ref, *, ctx_rows, rows_per_sample, final):
    D = D_MODEL
    NK = PEER_N_KEYS
    tn = h2_ref.shape[0]
    e = pl.program_id(1)

    @pl.when(e == 0)
    def _():
        acc_ref[...] = jnp.zeros_like(acc_ref)

    pre_ref[...] = _dot_nt(u_ref[...], h2_ref[...])
    rep = lambda row: jnp.broadcast_to(row, (BF16_ROWS, tn)).astype(BF16)
    cnt, kaw = [], []
    for i in range(PEER_KEYS_PER_STEP):
        words = [ak_ref[h, pl.ds(e * PEER_KEYS_PER_STEP + i, 1), :] for h in range(PEER_HEADS)]
        cnt.append([rep((wd & CNT_BITS).astype(F32)) for wd in words])
        kaw.append([rep(lax.bitcast_convert_type(wd & ~CNT_BITS, F32)) for wd in words])
    for c in range(tn // LANES):
        cs = slice(c * LANES, (c + 1) * LANES)
        w = [None] * PEER_KEYS_PER_STEP
        for h in range(PEER_HEADS):
            c2 = pltpu.bitcast(c2_ref[h, :, cs], BF16).reshape(NK // BF16_ROWS, BF16_ROWS, LANES)
            p2 = pltpu.bitcast(p2_ref[h, :, cs], BF16).reshape(NK // BF16_ROWS, BF16_ROWS, LANES)
            for i in range(PEER_KEYS_PER_STEP):
                t = jnp.where(c2 < cnt[i][h][:, cs][None], p2, 0.0) * kaw[i][h][:, cs][None]
                w[i] = t if w[i] is None else w[i] + t
        for i in range(PEER_KEYS_PER_STEP):
            rows = slice(i * NK, (i + 1) * NK)
            x = pre_ref[rows, cs]
            act = 0.5 * x * (1.0 + lax.erf(x * (2.0 ** -0.5)))
            hid_ref[rows, cs] = act.astype(BF16) * w[i].reshape(NK, LANES)
    acc_ref[...] += _dot(vt_ref[...], hid_ref[...])

    @pl.when(e == pl.num_programs(1) - 1)
    def _():
        y = acc_ref[...].T
        g_lat = modl_ref[0][:, 5 * D:6 * D]
        if ctx_rows:
            g_ctx = modc_ref[0][:, 5 * D:6 * D]
            tiles = rows_per_sample // tn
            r0 = (pl.program_id(0) % tiles) * tn
            row = r0 + lax.broadcasted_iota(jnp.int32, (tn, D), 0)
            gate = jnp.where(row < ctx_rows, g_ctx, g_lat)
        else:
            gate = g_lat
        x2 = x1_ref[...] + gate * y
        if final:
            x2 = _rms(x2, fg_ref[...])
        o_ref[...] = x2


def _peer_dense(h2, sel, u, vt, x1, modrows, fg, *, tn, rows_per_sample, ctx_rows, final):
    ntok, D = h2.shape
    ne = u.shape[0] // PEER_ETILE
    tiles = rows_per_sample // tn
    tspec = lambda t: pl.BlockSpec((PEER_HEADS, t.shape[1], tn), lambda i, e: (0, 0, i))
    return pl.pallas_call(
        functools.partial(_peer_dense_kernel, ctx_rows=ctx_rows, rows_per_sample=rows_per_sample, final=final),
        grid=(ntok // tn, ne),
        in_specs=[pl.BlockSpec((tn, D), lambda i, e: (i, 0)),
                  tspec(sel[0]), tspec(sel[1]), tspec(sel[2]),
                  pl.BlockSpec((PEER_ETILE, D), lambda i, e: (e, 0)),
                  pl.BlockSpec((D, PEER_ETILE), lambda i, e: (0, e)),
                  pl.BlockSpec((tn, D), lambda i, e: (i, 0), pipeline_mode=pl.Buffered(1)),
                  pl.BlockSpec((1, 1, 6 * D), lambda i, e: (2 * (i // tiles), 0, 0)),
                  pl.BlockSpec((1, 1, 6 * D), lambda i, e: (2 * (i // tiles) + 1, 0, 0)),
                  pl.BlockSpec((1, D), lambda i, e: (0, 0))],
        out_specs=pl.BlockSpec((tn, D), lambda i, e: (i, 0)),
        out_shape=jax.ShapeDtypeStruct((ntok, D), F32),
        scratch_shapes=[pltpu.VMEM((D, tn), F32),
                        pltpu.VMEM((PEER_ETILE, tn), F32),
                        pltpu.VMEM((PEER_ETILE, tn), BF16)],
        compiler_params=_cparams("parallel", "arbitrary"),
        name="peer_dense",
    )(h2, *sel, u, vt, x1, modrows, modrows, fg)


def _peer_tile(rows_per_sample):
    for tn in (1024, 768, 512, 256):
        if rows_per_sample % tn == 0:
            return tn
    raise ValueError(rows_per_sample)


_SWAP = np.concatenate([np.arange(16, 32), np.arange(0, 16), np.arange(48, 64), np.arange(32, 48)])


def _layout_w_in(w):
    D = w.shape[0]
    idx = np.cumsum(IN_SPLITS)[:-1]
    p, q, k, v, r, lrf, lrb, qdn, kvdn, krope, gz = jnp.split(w, idx, axis=1)
    zeros = lambda n: jnp.zeros((D, n), w.dtype)
    misc = jnp.concatenate([krope, lrf, lrb, zeros(LANES - MLA_ROPE - 2 * GLA_GATE_RANK)], axis=1)
    sw = jnp.concatenate([krope[:, _SWAP], zeros(LANES - MLA_ROPE)], axis=1)
    out = jnp.concatenate([p, q, k, v, r, gz, kvdn, misc, sw, qdn], axis=1)
    return jnp.concatenate([out, zeros(Z_COLS - out.shape[1])], axis=1).astype(BF16)


def _layout_gate_w(gate_w, row0):
    HK = GLA_HEADS * GLA_DK
    pad = jnp.zeros((LANES, HK), gate_w.dtype)
    return pad.at[row0:row0 + GLA_GATE_RANK].set(gate_w).astype(BF16)


def _layout_w_uq(w):
    H = MLA_HEADS
    w3 = w.reshape(MLA_Q_RANK, H, MLA_NOPE + MLA_ROPE)
    qn, qr = w3[..., :MLA_NOPE], w3[..., MLA_NOPE:]
    z = jnp.zeros((MLA_Q_RANK, H, LANES - MLA_ROPE), w.dtype)
    main = jnp.concatenate([qn, qr, z], axis=-1).reshape(MLA_Q_RANK, H * MQ)
    swp = jnp.concatenate([qr[..., _SWAP], z], axis=-1).reshape(MLA_Q_RANK, H * LANES)
    return jnp.concatenate([main, swp], axis=1).astype(BF16)


def _rope_tables(seq):
    half = MLA_ROPE // 2
    t = jnp.arange(seq)
    inv = ROPE_BASE ** (-jnp.arange(0, half, 2, dtype=F32) / half)
    ar = (t // GRID_W).astype(F32)[:, None] * inv
    ac = (t % GRID_W).astype(F32)[:, None] * inv
    cos = jnp.concatenate([jnp.cos(ar), jnp.cos(ar), jnp.cos(ac), jnp.cos(ac)], axis=1)
    sin = jnp.concatenate([-jnp.sin(ar), jnp.sin(ar), -jnp.sin(ac), jnp.sin(ac)], axis=1)
    cos = jnp.concatenate([jnp.ones((CTX_LEN, MLA_ROPE), F32), cos], axis=0)
    sin = jnp.concatenate([jnp.zeros((CTX_LEN, MLA_ROPE), F32), sin], axis=0)
    pad = jnp.zeros((CTX_LEN + seq, LANES - MLA_ROPE), F32)
    return jnp.concatenate([cos, pad], axis=1), jnp.concatenate([sin, pad], axis=1)


def kernel(x, c, ctx, c_ctx, ada_w, ada_b, norm1_g, norm2_g, w_in, pool_w, pool_scale, gla_gate_w, gla_gate_b, gla_norm_g, mla_q_norm_g, mla_kv_norm_g, mla_w_uq, mla_w_ukv, branch_w, w_out, peer_wq, peer_keys, peer_u, peer_v, final_norm_g):
    B, T, D = x.shape
    R = CTX_LEN + T
    assert D == D_MODEL and ctx.shape[1] == CTX_LEN == TM and T % TM == 0 and T % GRID_W == 0

    nrow = -(-(B + 1) // 8) * 8
    cc = jnp.concatenate([c, c_ctx[None], jnp.zeros((nrow - B - 1, D), F32)], axis=0)
    mod = _ada(cc, ada_w, ada_b)
    modrows = jnp.stack([jnp.broadcast_to(mod[:, B:B + 1], (DEPTH, B, 6 * D)), mod[:, :B]], axis=2)
    modrows = modrows.reshape(DEPTH, 2 * B, 1, 6 * D)

    cos, sin = _rope_tables(T)
    xall = jnp.concatenate([ctx, x], axis=1)
    row = lambda a: a.reshape(1, -1)

    for i in range(DEPTH):
        with_ctx = i < DEPTH - 1
        final = i == DEPTH - 1
        z = _in_proj(xall, modrows[i], row(norm1_g[i]), _layout_w_in(w_in[i]))

        pool_o = _pool(z, pool_w[i].astype(BF16), row(pool_scale[i]))

        o_b = _gla(z, _layout_gate_w(gla_gate_w[i, 1], MISC_LRB), row(gla_gate_b[i, 1]), fwd=False)
        gla_o = _gla(z, _layout_gate_w(gla_gate_w[i, 0], MISC_LRF), row(gla_gate_b[i, 0]), fwd=True,
                     ob=o_b, gn=row(gla_norm_g[i]))

        q, k, v = _mla_up(z, cos, sin, row(mla_q_norm_g[i]), row(mla_kv_norm_g[i]),
                          _layout_w_uq(mla_w_uq[i]), mla_w_ukv[i].astype(BF16))
        mla_o = _attn(q, k, v, with_ctx=with_ctx)

        x1, h2 = _merge(pool_o, gla_o, mla_o, z, xall, modrows[i], branch_w[i].astype(BF16),
                        w_out[i].astype(BF16), row(norm2_g[i]), with_ctx=with_ctx)

        rows = x1.shape[1]
        h2f = h2.reshape(B * rows, D)
        keys = peer_keys[i].reshape(2 * PEER_HEADS, PEER_N_KEYS, PEER_HALF).astype(BF16)
        sel = _peer_score(h2f, peer_wq[i].T.astype(BF16), keys)
        xall = _peer_dense(h2f, sel, peer_u[i].astype(BF16), peer_v[i].T.astype(BF16), x1.reshape(B * rows, D),
                           modrows[i], row(final_norm_g), tn=_peer_tile(rows), rows_per_sample=rows,
                           ctx_rows=CTX_LEN if with_ctx else 0, final=final).reshape(B, rows, D)
    return xall
```

```python
import functools

import jax
import jax.numpy as jnp
import numpy as np
from jax import lax
from jax.experimental import pallas as pl
from jax.experimental.pallas import tpu as pltpu

F32 = jnp.float32
BF16 = jnp.bfloat16

D_MODEL = 1024
DEPTH = 2
CTX_LEN = 256
GRID_W = 64
NORM_EPS = 1e-6
POOL_WIDTH = 1024
POOL_WINDOWS = (2, 4, 8, 16)
POOL_GROUP = POOL_WIDTH // len(POOL_WINDOWS)
GLA_HEADS = 4
GLA_DK = 128
GLA_DV = 256
GLA_GATE_RANK = 16
GLA_TAU = 16.0
GLA_CHUNK = 64
MLA_HEADS = 8
MLA_Q_RANK = 384
MLA_KV_RANK = 256
MLA_NOPE = 128
MLA_ROPE = 64
MLA_V = 128
ROPE_BASE = 10000.0
N_BRANCH = 3
PEER_HEADS = 8
PEER_N_KEYS = 128
PEER_TOPK = 16
PEER_QDIM = 256
PEER_HALF = PEER_QDIM // 2
IN_SPLITS = (POOL_WIDTH, GLA_HEADS * GLA_DK, GLA_HEADS * GLA_DK, GLA_HEADS * GLA_DV, GLA_HEADS * GLA_DV,
             GLA_GATE_RANK, GLA_GATE_RANK, MLA_Q_RANK, MLA_KV_RANK, MLA_ROPE, N_BRANCH * D_MODEL)

LANES = 128
BF16_ROWS = 16
TM = 256
VMEM_LIMIT = 56 * 1024 * 1024

COL_P = 0
COL_Q = 1024
COL_K = 1536
COL_V = 2048
COL_R = 3072
COL_GZ = 4096
COL_KVDN = 7168
COL_MISC = 7424
COL_SW = 7552
COL_QDN = 7680
Z_COLS = 8192
IN_NBLK = 2048
MISC_LRF = MLA_ROPE
MISC_LRB = MLA_ROPE + GLA_GATE_RANK
MQ = MLA_NOPE + LANES


def _cparams(*sem):
    return pltpu.CompilerParams(dimension_semantics=sem, vmem_limit_bytes=VMEM_LIMIT)


def _rms(x, g):
    return x * lax.rsqrt(jnp.mean(x * x, axis=-1, keepdims=True) + NORM_EPS) * g


def _dot(a, b):
    return jnp.dot(a, b, preferred_element_type=F32)


def _dot_nt(a, b):
    return lax.dot_general(a, b, (((1,), (1,)), ((), ())), preferred_element_type=F32)


def _dot_tn(a, b):
    return lax.dot_general(a, b, (((0,), (0,)), ((), ())), preferred_element_type=F32)


def _ada_kernel(c_ref, w_ref, b_ref, o_ref):
    c = c_ref[...]
    a = (c * jax.nn.sigmoid(c)).astype(BF16)
    o_ref[0] = _dot(a, w_ref[0].astype(BF16)) + b_ref[0]


def _ada(cc, ada_w, ada_b):
    L, D, N = ada_w.shape
    rows = cc.shape[0]
    nb = N // D
    return pl.pallas_call(
        _ada_kernel,
        grid=(L, nb),
        in_specs=[pl.BlockSpec((rows, D), lambda l, j: (0, 0)),
                  pl.BlockSpec((1, D, D), lambda l, j: (l, 0, j)),
                  pl.BlockSpec((1, 1, D), lambda l, j: (l, 0, j))],
        out_specs=pl.BlockSpec((1, rows, D), lambda l, j: (l, 0, j)),
        out_shape=jax.ShapeDtypeStruct((L, rows, N), F32),
        compiler_params=_cparams("parallel", "parallel"),
        name="ada_mod",
    )(cc, ada_w, ada_b.reshape(L, 1, N))


def _in_proj_kernel(x_ref, mod_ref, g_ref, w_ref, z_ref):
    D = D_MODEL
    x = x_ref[0]
    mod = mod_ref[0]
    h = _rms(x, g_ref[...]) * (1.0 + mod[:, D:2 * D]) + mod[:, 0:D]
    z_ref[0] = _dot(h.astype(BF16), w_ref[...]).astype(BF16)


def _in_proj(xall, modrows, g, w):
    B, R, D = xall.shape
    nt = R // TM
    nn = Z_COLS // IN_NBLK
    return pl.pallas_call(
        _in_proj_kernel,
        grid=(nn, B, nt),
        in_specs=[pl.BlockSpec((1, TM, D), lambda n, b, t: (b, t, 0)),
                  pl.BlockSpec((1, 1, 6 * D), lambda n, b, t: (2 * b + jnp.minimum(t, 1), 0, 0)),
                  pl.BlockSpec((1, D), lambda n, b, t: (0, 0)),
                  pl.BlockSpec((D, IN_NBLK), lambda n, b, t: (0, n))],
        out_specs=pl.BlockSpec((1, TM, IN_NBLK), lambda n, b, t: (b, t, n)),
        out_shape=jax.ShapeDtypeStruct((B, R, Z_COLS), BF16),
        compiler_params=_cparams("parallel", "parallel", "parallel"),
        name="in_proj",
    )(xall, modrows, g, w)


def _pool_kernel(u_ref, w_ref, sc_ref, o_ref):
    R = u_ref.shape[1]
    row = lax.broadcasted_iota(jnp.int32, (R, POOL_GROUP), 0)
    seg_lo = jnp.where(row < CTX_LEN, 0, CTX_LEN)
    seg_hi = jnp.where(row < CTX_LEN, CTX_LEN, R)
    for gi, win in enumerate(POOL_WINDOWS):
        sl = slice(gi * POOL_GROUP, (gi + 1) * POOL_GROUP)
        u = u_ref[0, :, sl].astype(F32)
        lo_off, hi_off = win // 2, win - win // 2
        acc = jnp.zeros_like(u)
        for d in range(-lo_off, hi_off):
            shifted = u if d == 0 else pltpu.roll(u, (R - d) % R, axis=0)
            ok = (row + d >= seg_lo) & (row + d < seg_hi)
            acc = acc + jnp.where(ok, shifted, 0.0)
        cnt = (jnp.minimum(row + hi_off, seg_hi) - jnp.maximum(row - lo_off, seg_lo)).astype(F32)
        diff = (acc / cnt - u).astype(BF16)
        o_ref[0, :, sl] = (_dot(diff, w_ref[gi]) * sc_ref[:, sl]).astype(BF16)


def _pool(z, pool_w, pool_scale):
    B, R, _ = z.shape
    return pl.pallas_call(
        _pool_kernel,
        grid=(B,),
        in_specs=[pl.BlockSpec((1, R, POOL_WIDTH), lambda b: (b, 0, COL_P // POOL_WIDTH)),
                  pl.BlockSpec(pool_w.shape, lambda b: (0, 0, 0)),
                  pl.BlockSpec((1, POOL_WIDTH), lambda b: (0, 0))],
        out_specs=pl.BlockSpec((1, R, POOL_WIDTH), lambda b: (b, 0, 0)),
        out_shape=jax.ShapeDtypeStruct((B, R, POOL_WIDTH), BF16),
        compiler_params=_cparams("parallel"),
        name="pool",
    )(z, pool_w, pool_scale)


def _split3(x):
    a = x.astype(BF16)
    r = x - a.astype(F32)
    b = r.astype(BF16)
    c = (r - b.astype(F32)).astype(BF16)
    return a, b, c


def _gla_kernel(*refs, fwd):
    if fwd:
        q_ref, k_ref, v_ref, misc_ref, gw_ref, gb_ref, ob_ref, r_ref, gn_ref, o_ref, st_ref = refs
    else:
        q_ref, k_ref, v_ref, misc_ref, gw_ref, gb_ref, o_ref, st_ref = refs
    C = GLA_CHUNK
    nchunk = TM // C
    HK = GLA_HEADS * GLA_DK

    @pl.when(pl.program_id(1) == 0)
    def _():
        st_ref[...] = jnp.zeros_like(st_ref)

    zg = _dot(misc_ref[0], gw_ref[...]) + gb_ref[...]
    g = (jnp.minimum(zg, 0.0) - jnp.log1p(jnp.exp(-jnp.abs(zg)))) * (1.0 / GLA_TAU)

    row = lax.broadcasted_iota(jnp.int32, (TM, TM), 0)
    col = lax.broadcasted_iota(jnp.int32, (TM, TM), 1)
    shift = C.bit_length() - 1
    same = (row >> shift) == (col >> shift)
    tri = same & ((col <= row) if fwd else (col >= row))
    ones = jnp.concatenate([jnp.where(tri, 1.0, 0.0), jnp.where(same, 1.0, 0.0)], axis=0).astype(BF16)
    g1, g2, g3 = _split3(g)
    both = _dot(ones, g1) + _dot(ones, g2) + _dot(ones, g3)
    b = both[:TM]
    bt = both[TM:]

    qe = q_ref[0].astype(F32) * (GLA_DK ** -0.5) * jnp.exp(b)
    ke = (k_ref[0].astype(F32) * jnp.exp(-b)).astype(BF16)
    kd = (k_ref[0].astype(F32) * jnp.exp(bt - b)).astype(BF16)
    qe = qe.astype(BF16)
    dec = jnp.exp(bt)
    v = v_ref[0]

    order = range(nchunk) if fwd else range(nchunk - 1, -1, -1)
    for h in range(GLA_HEADS):
        ks = slice(h * GLA_DK, (h + 1) * GLA_DK)
        vs = slice(h * GLA_DV, (h + 1) * GLA_DV)
        att = jnp.where(tri, _dot_nt(qe[:, ks], ke[:, ks]), 0.0).astype(BF16)
        o_in = _dot(att, v[:, vs])
        st = st_ref[h]
        parts = [None] * nchunk
        for j in order:
            rs = slice(j * C, (j + 1) * C)
            parts[j] = o_in[rs] + _dot_nt(qe[rs, ks], st.astype(BF16))
            st = st * dec[j * C:j * C + 1, ks] + _dot_tn(v[rs, vs], kd[rs, ks])
        st_ref[h] = st
        o = jnp.concatenate(parts, axis=0)
        if fwd:
            o = o + ob_ref[0, :, vs].astype(F32)
            r = r_ref[0, :, vs].astype(F32)
            o = _rms(o, gn_ref[...]) * (r * jax.nn.sigmoid(r))
        o_ref[0, :, vs] = o.astype(BF16)


def _gla(z, gw, gb, *, fwd, ob=None, gn=None):
    B, R, _ = z.shape
    nt = R // TM
    HK, HV = GLA_HEADS * GLA_DK, GLA_HEADS * GLA_DV
    if fwd:
        tile = lambda c: c
    else:
        tile = lambda c: jnp.where(c == 0, 0, nt - c)
    in_specs = [pl.BlockSpec((1, TM, HK), lambda b, c: (b, tile(c), COL_Q // HK)),
                pl.BlockSpec((1, TM, HK), lambda b, c: (b, tile(c), COL_K // HK)),
                pl.BlockSpec((1, TM, HV), lambda b, c: (b, tile(c), COL_V // HV)),
                pl.BlockSpec((1, TM, LANES), lambda b, c: (b, tile(c), COL_MISC // LANES)),
                pl.BlockSpec((LANES, HK), lambda b, c: (0, 0)),
                pl.BlockSpec((1, HK), lambda b, c: (0, 0))]
    args = [z, z, z, z, gw, gb]
    if fwd:
        in_specs += [pl.BlockSpec((1, TM, HV), lambda b, c: (b, tile(c), 0)),
                     pl.BlockSpec((1, TM, HV), lambda b, c: (b, tile(c), COL_R // HV)),
                     pl.BlockSpec((1, GLA_DV), lambda b, c: (0, 0))]
        args += [ob, z, gn]
    return pl.pallas_call(
        functools.partial(_gla_kernel, fwd=fwd),
        grid=(B, nt),
        in_specs=in_specs,
        out_specs=pl.BlockSpec((1, TM, HV), lambda b, c: (b, tile(c), 0)),
        out_shape=jax.ShapeDtypeStruct((B, R, HV), BF16),
        scratch_shapes=[pltpu.VMEM((GLA_HEADS, GLA_DV, GLA_DK), F32)],
        compiler_params=_cparams("parallel", "arbitrary"),
        name="gla_fwd" if fwd else "gla_bwd",
    )(*args)


def _mla_up_kernel(qdn_ref, kvdn_ref, misc_ref, sw_ref, cos_ref, sin_ref, qg_ref, kvg_ref, wq_ref, wkv_ref,
                   q_ref, k_ref, v_ref):
    H = MLA_HEADS
    scale = (MLA_NOPE + MLA_ROPE) ** -0.5 * float(np.log2(np.e))
    cos = cos_ref[...]
    sin = sin_ref[...]
    qn = _rms(qdn_ref[0].astype(F32), qg_ref[...]).astype(BF16)
    qall = _dot(qn, wq_ref[...])
    for h in range(H):
        base = h * MQ
        q_ref[0, :, base:base + MLA_NOPE] = (qall[:, base:base + MLA_NOPE] * scale).astype(BF16)
        rot = (qall[:, base + MLA_NOPE:base + MQ] * cos
               + qall[:, H * MQ + h * LANES:H * MQ + (h + 1) * LANES] * sin)
        q_ref[0, :, base + MLA_NOPE:base + MQ] = (rot * scale).astype(BF16)
    kvn = _rms(kvdn_ref[0].astype(F32), kvg_ref[...]).astype(BF16)
    kvall = _dot(kvn, wkv_ref[...])
    kr = (misc_ref[0].astype(F32) * cos + sw_ref[0].astype(F32) * sin).astype(BF16)
    for h in range(H):
        src = h * (MLA_NOPE + MLA_V)
        k_ref[0, :, h * MQ:h * MQ + MLA_NOPE] = kvall[:, src:src + MLA_NOPE].astype(BF16)
        k_ref[0, :, h * MQ + MLA_NOPE:(h + 1) * MQ] = kr
        v_ref[0, :, h * MLA_V:(h + 1) * MLA_V] = kvall[:, src + MLA_NOPE:src + MLA_NOPE + MLA_V].astype(BF16)


def _mla_up(z, cos, sin, qg, kvg, wq, wkv):
    B, R, _ = z.shape
    nt = R // TM
    H = MLA_HEADS
    const = lambda b, t: (0, 0)
    return pl.pallas_call(
        _mla_up_kernel,
        grid=(B, nt),
        in_specs=[pl.BlockSpec((1, TM, MLA_Q_RANK), lambda b, t: (b, t, COL_QDN // MLA_Q_RANK)),
                  pl.BlockSpec((1, TM, MLA_KV_RANK), lambda b, t: (b, t, COL_KVDN // MLA_KV_RANK)),
                  pl.BlockSpec((1, TM, LANES), lambda b, t: (b, t, COL_MISC // LANES)),
                  pl.BlockSpec((1, TM, LANES), lambda b, t: (b, t, COL_SW // LANES)),
                  pl.BlockSpec((TM, LANES), lambda b, t: (t, 0)),
                  pl.BlockSpec((TM, LANES), lambda b, t: (t, 0)),
                  pl.BlockSpec(qg.shape, const),
                  pl.BlockSpec(kvg.shape, const),
                  pl.BlockSpec(wq.shape, const),
                  pl.BlockSpec(wkv.shape, const)],
        out_specs=[pl.BlockSpec((1, TM, H * MQ), lambda b, t: (b, t, 0)),
                   pl.BlockSpec((1, TM, H * MQ), lambda b, t: (b, t, 0)),
                   pl.BlockSpec((1, TM, H * MLA_V), lambda b, t: (b, t, 0))],
        out_shape=[jax.ShapeDtypeStruct((B, R, H * MQ), BF16),
                   jax.ShapeDtypeStruct((B, R, H * MQ), BF16),
                   jax.ShapeDtypeStruct((B, R, H * MLA_V), BF16)],
        compiler_params=_cparams("parallel", "parallel"),
        name="mla_up",
    )(z, z, z, z, cos, sin, qg, kvg, wq, wkv)


def _attn_kernel(q_ref, k_ref, v_ref, o_ref, *, q_off):
    R = k_ref.shape[1]

    def attend(nk):
        s = _dot_nt(q_ref[0], k_ref[0, :nk])
        p = jnp.exp2(s - jnp.max(s, axis=-1, keepdims=True))
        l = jnp.sum(p, axis=-1, keepdims=True)
        o_ref[0] = (_dot(p.astype(BF16), v_ref[0, :nk]) / l).astype(BF16)

    if q_off == 0:
        qi = pl.program_id(2)

        @pl.when(qi == 0)
        def _():
            attend(CTX_LEN)

        @pl.when(qi > 0)
        def _():
            attend(R)
    else:
        attend(R)


def _attn(q, k, v, *, with_ctx):
    B, R, _ = q.shape
    q_off = 0 if with_ctx else CTX_LEN // TM
    nq = R // TM - q_off
    return pl.pallas_call(
        functools.partial(_attn_kernel, q_off=q_off),
        grid=(B, MLA_HEADS, nq),
        in_specs=[pl.BlockSpec((1, TM, MQ), lambda b, h, i: (b, i + q_off, h)),
                  pl.BlockSpec((1, R, MQ), lambda b, h, i: (b, 0, h)),
                  pl.BlockSpec((1, R, MLA_V), lambda b, h, i: (b, 0, h))],
        out_specs=pl.BlockSpec((1, TM, MLA_V), lambda b, h, i: (b, i + q_off, h)),
        out_shape=jax.ShapeDtypeStruct((B, R, MLA_HEADS * MLA_V), BF16),
        compiler_params=_cparams("parallel", "parallel", "arbitrary"),
        name="mla_attn",
    )(q, k, v)


def _merge_kernel(po_ref, go_ref, mo_ref, gz0_ref, gz1_ref, gz2_ref, x_ref, mod_ref, bw_ref, wo_ref, g_ref,
                  x1_ref, h2_ref):
    D = D_MODEL
    m = None
    for i, (o_ref, gz_ref) in enumerate(((po_ref, gz0_ref), (go_ref, gz1_ref), (mo_ref, gz2_ref))):
        t = jax.nn.sigmoid(gz_ref[0].astype(F32)) * _dot(o_ref[0], bw_ref[i])
        m = t if m is None else m + t
    y = _dot(m.astype(BF16), wo_ref[...])
    mod = mod_ref[0]
    x1 = x_ref[0] + mod[:, 2 * D:3 * D] * y
    x1_ref[0] = x1
    h2 = _rms(x1, g_ref[...]) * (1.0 + mod[:, 4 * D:5 * D]) + mod[:, 3 * D:4 * D]
    h2_ref[0] = h2.astype(BF16)


def _merge(pool_o, gla_o, mla_o, z, xall, modrows, bw, wo, g, *, with_ctx):
    B, R, D = xall.shape
    t_off = 0 if with_ctx else CTX_LEN // TM
    nt = R // TM - t_off
    tok = lambda b, t: (b, t + t_off, 0)
    gzb = COL_GZ // D
    return pl.pallas_call(
        _merge_kernel,
        grid=(B, nt),
        in_specs=[pl.BlockSpec((1, TM, D), tok),
                  pl.BlockSpec((1, TM, D), tok),
                  pl.BlockSpec((1, TM, D), tok),
                  pl.BlockSpec((1, TM, D), lambda b, t: (b, t + t_off, gzb)),
                  pl.BlockSpec((1, TM, D), lambda b, t: (b, t + t_off, gzb + 1)),
                  pl.BlockSpec((1, TM, D), lambda b, t: (b, t + t_off, gzb + 2)),
                  pl.BlockSpec((1, TM, D), tok),
                  pl.BlockSpec((1, 1, 6 * D), lambda b, t: (2 * b + jnp.minimum(t + t_off, 1), 0, 0)),
                  pl.BlockSpec(bw.shape, lambda b, t: (0, 0, 0)),
                  pl.BlockSpec(wo.shape, lambda b, t: (0, 0)),
                  pl.BlockSpec((1, D), lambda b, t: (0, 0))],
        out_specs=[pl.BlockSpec((1, TM, D), lambda b, t: (b, t, 0)),
                   pl.BlockSpec((1, TM, D), lambda b, t: (b, t, 0))],
        out_shape=[jax.ShapeDtypeStruct((B, nt * TM, D), F32),
                   jax.ShapeDtypeStruct((B, nt * TM, D), BF16)],
        compiler_params=_cparams("parallel", "parallel"),
        name="merge",
    )(pool_o, gla_o, mla_o, z, z, z, xall, modrows, bw, wo, g)


def _top16(s):
    nk, n = s.shape
    iota = lax.broadcasted_iota(jnp.int32, (nk, n), 0).astype(F32)
    slot = lax.broadcasted_iota(jnp.int32, (PEER_TOPK, n), 0)
    rank = jnp.full((nk, n), float(PEER_TOPK), F32)
    vals = jnp.zeros((PEER_TOPK, n), F32)
    for j in range(PEER_TOPK):
        m = jnp.max(s, axis=0, keepdims=True)
        idx = jnp.min(jnp.where(s == m, iota, float(nk)), axis=0, keepdims=True)
        hit = iota == idx
        rank = jnp.where(hit, float(j), rank)
        s = jnp.where(hit, -jnp.inf, s)
        vals = jnp.where(slot == j, m, vals)
    return vals, rank


PAIR_SHORT = PEER_TOPK // 2
PAIR_ROWS = PEER_TOPK + (PEER_TOPK - 1) * PAIR_SHORT
PAIR_VALID = sum(PEER_TOPK // (i + 1) for i in range(PEER_TOPK))


def _pair_rows(t2):
    return jnp.concatenate([t2] + [t2[0:PAIR_SHORT]] * (PEER_TOPK - 1), axis=0)


def _pair_candidates(t1, t2):
    K = PEER_TOPK
    n = t1.shape[1]
    first = jnp.concatenate([jnp.broadcast_to(t1[0:1], (K, n))]
                            + [jnp.broadcast_to(t1[i:i + 1], (PAIR_SHORT, n)) for i in range(1, K)], axis=0)
    r = lax.broadcasted_iota(jnp.int32, (PAIR_ROWS, n), 0)
    sh = PAIR_SHORT.bit_length() - 1
    i_of = jnp.where(r < K, 0, ((r - K) >> sh) + 1)
    j_of = jnp.where(r < K, r, (r - K) & (PAIR_SHORT - 1))
    cand = jnp.where((i_of + 1) * (j_of + 1) <= K, first + _pair_rows(t2), -jnp.inf)
    return cand, (i_of * K + j_of).astype(F32)


def _top16_values(s, with_rank):
    n = s.shape[1]
    slot = lax.broadcasted_iota(jnp.int32, (PEER_TOPK, n), 0)
    vals = jnp.zeros((PEER_TOPK, n), F32)
    rank = jnp.full(s.shape, float(PEER_TOPK), F32) if with_rank else None
    for j in range(PEER_TOPK):
        m = jnp.max(s, axis=0, keepdims=True)
        hit = s == m
        if with_rank:
            rank = jnp.where(hit, float(j), rank)
        s = jnp.where(hit, -jnp.inf, s)
        vals = jnp.where(slot == j, m, vals)
    used = jnp.sum(jnp.where(s == -jnp.inf, 1.0, 0.0), axis=0, keepdims=True)
    return vals, rank, jnp.where(used == float(PEER_TOPK), 1.0, 0.0)


def _pair_counts(t1, t2):
    K = PEER_TOPK
    n = t1.shape[1]
    cand0, _ = _pair_candidates(t1, t2)
    cand = cand0
    zsum = jnp.zeros((1, n), F32)
    best = m = None
    for j in range(K):
        m = jnp.max(cand, axis=0, keepdims=True)
        cand = jnp.where(cand == m, -jnp.inf, cand)
        if j == 0:
            best = m
        zsum = zsum + jnp.exp(m - best)
    used = jnp.sum(jnp.where(cand == -jnp.inf, 1.0, 0.0), axis=0, keepdims=True)
    exact = jnp.where(used == float(PAIR_ROWS - PAIR_VALID + K), 1.0, 0.0)
    picked = jnp.where(cand0 >= m, 1.0, 0.0)
    slot = lax.broadcasted_iota(jnp.int32, (K, n), 0)
    cnt = jnp.zeros((K, n), F32)
    for i in range(K):
        blk = picked[0:K] if i == 0 else picked[K + (i - 1) * PAIR_SHORT:K + i * PAIR_SHORT]
        cnt = jnp.where(slot == i, jnp.sum(blk, axis=0, keepdims=True), cnt)
    return cnt, zsum, exact


def _pair_top16(t1, t2):
    K = PEER_TOPK
    n = t1.shape[1]
    cand, flat = _pair_candidates(t1, t2)
    irow = lax.broadcasted_iota(jnp.int32, (K, n), 0).astype(F32)
    cnt = jnp.zeros((K, n), F32)
    zsum = jnp.zeros((1, n), F32)
    best = None
    for j in range(K):
        m = jnp.max(cand, axis=0, keepdims=True)
        idx = jnp.min(jnp.where(cand == m, flat, float(K * K)), axis=0, keepdims=True)
        cand = jnp.where(flat == idx, -jnp.inf, cand)
        cnt = cnt + jnp.where(irow == jnp.floor(idx * (1.0 / K)), 1.0, 0.0)
        if j == 0:
            best = m
        zsum = zsum + jnp.exp(m - best)
    return cnt, zsum


CNT_BITS = 0xFF


def _words_per_row():
    return 4 // jnp.dtype(BF16).itemsize


def _peer_score_kernel(h2_ref, wqt_ref, keys_ref, c2_ref, p2_ref, ak_ref, qt_ref):
    K = PEER_TOPK
    qt_ref[...] = _dot_nt(wqt_ref[...], h2_ref[...]).astype(BF16)

    def store(h, cs, s1, s2, t1, rank1, t2, rank2, cnt, zsum):
        rowcnt = jnp.zeros_like(s1)
        for j in range(K):
            in_row = (s1 == t1[j:j + 1]) if rank1 is None else (rank1 == float(j))
            rowcnt = jnp.where(in_row, cnt[j:j + 1], rowcnt)
        c2_ref[h, :, cs] = pltpu.bitcast(rank2.astype(BF16), jnp.int32)
        p2_ref[h, :, cs] = pltpu.bitcast(jnp.exp(s2 - t2[0:1]).astype(BF16), jnp.int32)
        ka = (jnp.exp(s1 - t1[0:1]) / zsum).astype(BF16).astype(F32)
        ak_ref[h, :, cs] = (lax.bitcast_convert_type(ka, jnp.int32) & ~CNT_BITS) | rowcnt.astype(jnp.int32)

    def head(h, carry):
        base = pl.multiple_of(h * PEER_QDIM, PEER_QDIM)
        s1_all = _dot(keys_ref[h], qt_ref[pl.ds(base, PEER_HALF), :])
        s2_all = _dot(keys_ref[PEER_HEADS + h], qt_ref[pl.ds(base + PEER_HALF, PEER_HALF), :])
        chunks = []
        for c in range(s1_all.shape[1] // LANES):
            cs = slice(c * LANES, (c + 1) * LANES)
            s1, s2 = s1_all[:, cs], s2_all[:, cs]
            t1, _, ok1 = _top16_values(s1, with_rank=False)
            t2, rank2, ok2 = _top16_values(s2, with_rank=True)
            cnt, zsum, ok3 = _pair_counts(t1, t2)
            store(h, cs, s1, s2, t1, None, t2, rank2, cnt, zsum)
            chunks.append((cs, s1, s2, jnp.min(ok1 * ok2 * ok3)))

        for cs, s1, s2, exact in chunks:
            @pl.when(exact < 0.5)
            def _(cs=cs, s1=s1, s2=s2):
                t1x, rank1x = _top16(s1)
                t2x, rank2x = _top16(s2)
                cntx, zx = _pair_top16(t1x, t2x)
                store(h, cs, s1, s2, t1x, rank1x, t2x, rank2x, cntx, zx)

        return carry

    lax.fori_loop(0, PEER_HEADS, head, 0)


PEER_SCORE_TN = 2 * LANES


def _peer_score(h2, wqt, keys):
    ntok, D = h2.shape
    tn = PEER_SCORE_TN
    nw = PEER_N_KEYS // _words_per_row()
    tbl = lambda rows: jax.ShapeDtypeStruct((PEER_HEADS, rows, ntok), jnp.int32)
    ospec = lambda rows: pl.BlockSpec((PEER_HEADS, rows, tn), lambda i: (0, 0, i))
    return pl.pallas_call(
        _peer_score_kernel,
        grid=(ntok // tn,),
        in_specs=[pl.BlockSpec((tn, D), lambda i: (i, 0)),
                  pl.BlockSpec(wqt.shape, lambda i: (0, 0)),
                  pl.BlockSpec(keys.shape, lambda i: (0, 0, 0))],
        out_specs=[ospec(nw), ospec(nw), ospec(PEER_N_KEYS)],
        out_shape=[tbl(nw), tbl(nw), tbl(PEER_N_KEYS)],
        scratch_shapes=[pltpu.VMEM((PEER_HEADS * PEER_QDIM, tn), BF16)],
        compiler_params=_cparams("parallel"),
        name="peer_score",
    )(h2, wqt, keys)


PEER_KEYS_PER_STEP = 4
PEER_ETILE = PEER_KEYS_PER_STEP * PEER_N_KEYS
PEER_KEY_GROUP = 2


def _peer_dense_kernel(h2_ref, c2_ref, p2_ref, ak_ref, u_ref, vt_ref, x1_ref, modc_ref, modl_ref, fg_ref,
                       o_ref, acc_ref, pre_ref, hid_ref, *, ctx_rows, rows_per_sample, final):
    D = D_MODEL
    NK = PEER_N_KEYS
    tn = h2_ref.shape[0]
    e = pl.program_id(1)

    @pl.when(e == 0)
    def _():
        acc_ref[...] = jnp.zeros_like(acc_ref)

    pre_ref[...] = _dot_nt(pltpu.bitcast(u_ref[...], BF16), h2_ref[...])

    words = [[ak_ref[h, pl.ds(e * PEER_KEYS_PER_STEP + i, 1), :] for h in range(PEER_HEADS)]
             for i in range(PEER_KEYS_PER_STEP)]

    def key_rows(i, h, cs):
        wd = words[i][h][:, cs]
        rep = lambda row: jnp.broadcast_to(row, (BF16_ROWS, LANES)).astype(BF16)
        return rep((wd & CNT_BITS).astype(F32)), rep(lax.bitcast_convert_type(wd & ~CNT_BITS, F32))

    for c in range(tn // LANES):
        cs = slice(c * LANES, (c + 1) * LANES)
        for g in range(0, PEER_KEYS_PER_STEP, PEER_KEY_GROUP):
            group = range(g, g + PEER_KEY_GROUP)
            w = {i: None for i in group}
            for h in range(PEER_HEADS):
                c2 = pltpu.bitcast(c2_ref[h, :, cs], BF16).reshape(NK // BF16_ROWS, BF16_ROWS, LANES)
                p2 = pltpu.bitcast(p2_ref[h, :, cs], BF16).reshape(NK // BF16_ROWS, BF16_ROWS, LANES)
                for i in group:
                    cnt, kaw = key_rows(i, h, cs)
                    t = jnp.where(c2 < cnt[None], p2, 0.0) * kaw[None]
                    w[i] = t if w[i] is None else w[i] + t
            for i in group:
                rows = slice(i * NK, (i + 1) * NK)
                x = pre_ref[rows, cs].astype(BF16)
                act = 0.5 * x * (1.0 + lax.erf(x * (2.0 ** -0.5)))
                hid_ref[rows, cs] = act * w[i].reshape(NK, LANES)

    acc_ref[...] += _dot(pltpu.bitcast(vt_ref[...], BF16), hid_ref[...])

    @pl.when(e == pl.num_programs(1) - 1)
    def _():
        y = acc_ref[...].T
        g_lat = modl_ref[0][:, 5 * D:6 * D]
        if ctx_rows:
            g_ctx = modc_ref[0][:, 5 * D:6 * D]
            tiles = rows_per_sample // tn
            r0 = (pl.program_id(0) % tiles) * tn
            row = r0 + lax.broadcasted_iota(jnp.int32, (tn, D), 0)
            gate = jnp.where(row < ctx_rows, g_ctx, g_lat)
        else:
            gate = g_lat
        x2 = x1_ref[...] + gate * y
        if final:
            x2 = _rms(x2, fg_ref[...])
        o_ref[...] = x2


def _peer_dense(h2, sel, u, vt, x1, modrows, fg, *, tn, rows_per_sample, ctx_rows, final):
    ntok, D = h2.shape
    wpr = _words_per_row()
    ne = u.shape[0] * wpr // PEER_ETILE
    tiles = rows_per_sample // tn
    tspec = lambda t: pl.BlockSpec((PEER_HEADS, t.shape[1], tn), lambda i, s: (0, 0, i))
    return pl.pallas_call(
        functools.partial(_peer_dense_kernel, ctx_rows=ctx_rows, rows_per_sample=rows_per_sample, final=final),
        grid=(ntok // tn, ne),
        in_specs=[pl.BlockSpec((tn, D), lambda i, s: (i, 0)),
                  tspec(sel[0]), tspec(sel[1]), tspec(sel[2]),
                  pl.BlockSpec((PEER_ETILE // wpr, D), lambda i, s: (s, 0)),
                  pl.BlockSpec((D // wpr, PEER_ETILE), lambda i, s: (0, s)),
                  pl.BlockSpec((tn, D), lambda i, s: (i, 0), pipeline_mode=pl.Buffered(1)),
                  pl.BlockSpec((1, 1, 6 * D), lambda i, s: (2 * (i // tiles), 0, 0)),
                  pl.BlockSpec((1, 1, 6 * D), lambda i, s: (2 * (i // tiles) + 1, 0, 0)),
                  pl.BlockSpec((1, D), lambda i, s: (0, 0))],
        out_specs=pl.BlockSpec((tn, D), lambda i, s: (i, 0)),
        out_shape=jax.ShapeDtypeStruct((ntok, D), F32),
        scratch_shapes=[pltpu.VMEM((D, tn), F32),
                        pltpu.VMEM((PEER_ETILE, tn), F32),
                        pltpu.VMEM((PEER_ETILE, tn), BF16)],
        compiler_params=_cparams("parallel", "arbitrary"),
        name="peer_dense",
    )(h2, *sel, u, vt, x1, modrows, modrows, fg)


def _pack_rows_kernel(x_ref, o_ref):
    o_ref[...] = pltpu.bitcast(x_ref[...].astype(BF16), jnp.int32)


def _pack_rows(x):
    R, C = x.shape
    wpr = _words_per_row()
    br, bc = 512, 1024
    return pl.pallas_call(
        _pack_rows_kernel,
        grid=(R // br, C // bc),
        in_specs=[pl.BlockSpec((br, bc), lambda i, j: (i, j))],
        out_specs=pl.BlockSpec((br // wpr, bc), lambda i, j: (i, j)),
        out_shape=jax.ShapeDtypeStruct((R // wpr, C), jnp.int32),
        compiler_params=_cparams("parallel", "parallel"),
        name="pack_rows",
    )(x)


def _peer_tile(rows_per_sample):
    for tn in (1024, 768, 512, 256):
        if rows_per_sample % tn == 0:
            return tn
    raise ValueError(rows_per_sample)


_SWAP = np.concatenate([np.arange(16, 32), np.arange(0, 16), np.arange(48, 64), np.arange(32, 48)])


def _layout_w_in(w):
    D = w.shape[0]
    idx = np.cumsum(IN_SPLITS)[:-1]
    p, q, k, v, r, lrf, lrb, qdn, kvdn, krope, gz = jnp.split(w, idx, axis=1)
    zeros = lambda n: jnp.zeros((D, n), w.dtype)
    misc = jnp.concatenate([krope, lrf, lrb, zeros(LANES - MLA_ROPE - 2 * GLA_GATE_RANK)], axis=1)
    sw = jnp.concatenate([krope[:, _SWAP], zeros(LANES - MLA_ROPE)], axis=1)
    out = jnp.concatenate([p, q, k, v, r, gz, kvdn, misc, sw, qdn], axis=1)
    return jnp.concatenate([out, zeros(Z_COLS - out.shape[1])], axis=1).astype(BF16)


def _layout_gate_w(gate_w, row0):
    HK = GLA_HEADS * GLA_DK
    pad = jnp.zeros((LANES, HK), gate_w.dtype)
    return pad.at[row0:row0 + GLA_GATE_RANK].set(gate_w).astype(BF16)


def _layout_w_uq(w):
    H = MLA_HEADS
    w3 = w.reshape(MLA_Q_RANK, H, MLA_NOPE + MLA_ROPE)
    qn, qr = w3[..., :MLA_NOPE], w3[..., MLA_NOPE:]
    z = jnp.zeros((MLA_Q_RANK, H, LANES - MLA_ROPE), w.dtype)
    main = jnp.concatenate([qn, qr, z], axis=-1).reshape(MLA_Q_RANK, H * MQ)
    swp = jnp.concatenate([qr[..., _SWAP], z], axis=-1).reshape(MLA_Q_RANK, H * LANES)
    return jnp.concatenate([main, swp], axis=1).astype(BF16)


def _rope_tables(seq):
    half = MLA_ROPE // 2
    t = jnp.arange(seq)
    inv = ROPE_BASE ** (-jnp.arange(0, half, 2, dtype=F32) / half)
    ar = (t // GRID_W).astype(F32)[:, None] * inv
    ac = (t % GRID_W).astype(F32)[:, None] * inv
    cos = jnp.concatenate([jnp.cos(ar), jnp.cos(ar), jnp.cos(ac), jnp.cos(ac)], axis=1)
    sin = jnp.concatenate([-jnp.sin(ar), jnp.sin(ar), -jnp.sin(ac), jnp.sin(ac)], axis=1)
    cos = jnp.concatenate([jnp.ones((CTX_LEN, MLA_ROPE), F32), cos], axis=0)
    sin = jnp.concatenate([jnp.zeros((CTX_LEN, MLA_ROPE), F32), sin], axis=0)
    pad = jnp.zeros((CTX_LEN + seq, LANES - MLA_ROPE), F32)
    return jnp.concatenate([cos, pad], axis=1), jnp.concatenate([sin, pad], axis=1)


def kernel(x, c, ctx, c_ctx, ada_w, ada_b, norm1_g, norm2_g, w_in, pool_w, pool_scale, gla_gate_w, gla_gate_b, gla_norm_g, mla_q_norm_g, mla_kv_norm_g, mla_w_uq, mla_w_ukv, branch_w, w_out, peer_wq, peer_keys, peer_u, peer_v, final_norm_g):
    B, T, D = x.shape
    R = CTX_LEN + T
    assert D == D_MODEL and ctx.shape[1] == CTX_LEN == TM and T % TM == 0 and T % GRID_W == 0

    nrow = -(-(B + 1) // 8) * 8
    cc = jnp.concatenate([c, c_ctx[None], jnp.zeros((nrow - B - 1, D), F32)], axis=0)
    mod = _ada(cc, ada_w, ada_b)
    modrows = jnp.stack([jnp.broadcast_to(mod[:, B:B + 1], (DEPTH, B, 6 * D)), mod[:, :B]], axis=2)
    modrows = modrows.reshape(DEPTH, 2 * B, 1, 6 * D)

    cos, sin = _rope_tables(T)
    xall = jnp.concatenate([ctx, x], axis=1)
    row = lambda a: a.reshape(1, -1)

    for i in range(DEPTH):
        with_ctx = i < DEPTH - 1
        final = i == DEPTH - 1
        z = _in_proj(xall, modrows[i], row(norm1_g[i]), _layout_w_in(w_in[i]))

        pool_o = _pool(z, pool_w[i].astype(BF16), row(pool_scale[i]))

        o_b = _gla(z, _layout_gate_w(gla_gate_w[i, 1], MISC_LRB), row(gla_gate_b[i, 1]), fwd=False)
        gla_o = _gla(z, _layout_gate_w(gla_gate_w[i, 0], MISC_LRF), row(gla_gate_b[i, 0]), fwd=True,
                     ob=o_b, gn=row(gla_norm_g[i]))

        q, k, v = _mla_up(z, cos, sin, row(mla_q_norm_g[i]), row(mla_kv_norm_g[i]),
                          _layout_w_uq(mla_w_uq[i]), mla_w_ukv[i].astype(BF16))
        mla_o = _attn(q, k, v, with_ctx=with_ctx)

        x1, h2 = _merge(pool_o, gla_o, mla_o, z, xall, modrows[i], branch_w[i].astype(BF16),
                        w_out[i].astype(BF16), row(norm2_g[i]), with_ctx=with_ctx)

        rows = x1.shape[1]
        h2f = h2.reshape(B * rows, D)
        keys = peer_keys[i].reshape(2 * PEER_HEADS, PEER_N_KEYS, PEER_HALF).astype(BF16)
        sel = _peer_score(h2f, peer_wq[i].T.astype(BF16), keys)
        xall = _peer_dense(h2f, sel, _pack_rows(peer_u[i]), _pack_rows(peer_v[i].T), x1.reshape(B * rows, D),
                           modrows[i], row(final_norm_g), tn=_peer_tile(rows), rows_per_sample=rows,
                           ctx_rows=CTX_LEN if with_ctx else 0, final=final).reshape(B, rows, D)
    return xall
```

```python
import functools

import jax
import jax.numpy as jnp
import numpy as np
from jax import lax
from jax.experimental import pallas as pl
from jax.experimental.pallas import tpu as pltpu

F32 = jnp.float32
BF16 = jnp.bfloat16

D_MODEL = 1024
DEPTH = 2
CTX_LEN = 256
GRID_W = 64
NORM_EPS = 1e-6
POOL_WIDTH = 1024
POOL_WINDOWS = (2, 4, 8, 16)
POOL_GROUP = POOL_WIDTH // len(POOL_WINDOWS)
GLA_HEADS = 4
GLA_DK = 128
GLA_DV = 256
GLA_GATE_RANK = 16
GLA_TAU = 16.0
GLA_CHUNK = 64
MLA_HEADS = 8
MLA_Q_RANK = 384
MLA_KV_RANK = 256
MLA_NOPE = 128
MLA_ROPE = 64
MLA_V = 128
ROPE_BASE = 10000.0
N_BRANCH = 3
PEER_HEADS = 8
PEER_N_KEYS = 128
PEER_TOPK = 16
PEER_QDIM = 256
PEER_HALF = PEER_QDIM // 2
IN_SPLITS = (POOL_WIDTH, GLA_HEADS * GLA_DK, GLA_HEADS * GLA_DK, GLA_HEADS * GLA_DV, GLA_HEADS * GLA_DV,
             GLA_GATE_RANK, GLA_GATE_RANK, MLA_Q_RANK, MLA_KV_RANK, MLA_ROPE, N_BRANCH * D_MODEL)

LANES = 128
BF16_ROWS = 16
TM = 256
VMEM_LIMIT = 56 * 1024 * 1024

COL_P = 0
COL_Q = 1024
COL_K = 1536
COL_V = 2048
COL_R = 3072
COL_GZ = 4096
COL_KVDN = 7168
COL_MISC = 7424
COL_SW = 7552
COL_QDN = 7680
Z_COLS = 8192
IN_NBLK = 2048
MISC_LRF = MLA_ROPE
MISC_LRB = MLA_ROPE + GLA_GATE_RANK
MQ = MLA_NOPE + LANES


def _cparams(*sem):
    return pltpu.CompilerParams(dimension_semantics=sem, vmem_limit_bytes=VMEM_LIMIT)


def _rms(x, g):
    return x * lax.rsqrt(jnp.mean(x * x, axis=-1, keepdims=True) + NORM_EPS) * g


def _dot(a, b):
    return jnp.dot(a, b, preferred_element_type=F32)


def _dot_nt(a, b):
    return lax.dot_general(a, b, (((1,), (1,)), ((), ())), preferred_element_type=F32)


def _dot_tn(a, b):
    return lax.dot_general(a, b, (((0,), (0,)), ((), ())), preferred_element_type=F32)


def _ada_kernel(c_ref, w_ref, b_ref, o_ref):
    c = c_ref[...]
    a = (c * jax.nn.sigmoid(c)).astype(BF16)
    o_ref[0] = _dot(a, w_ref[0].astype(BF16)) + b_ref[0]


def _ada(cc, ada_w, ada_b):
    L, D, N = ada_w.shape
    rows = cc.shape[0]
    nb = N // D
    return pl.pallas_call(
        _ada_kernel,
        grid=(L, nb),
        in_specs=[pl.BlockSpec((rows, D), lambda l, j: (0, 0)),
                  pl.BlockSpec((1, D, D), lambda l, j: (l, 0, j)),
                  pl.BlockSpec((1, 1, D), lambda l, j: (l, 0, j))],
        out_specs=pl.BlockSpec((1, rows, D), lambda l, j: (l, 0, j)),
        out_shape=jax.ShapeDtypeStruct((L, rows, N), F32),
        compiler_params=_cparams("parallel", "parallel"),
        name="ada_mod",
    )(cc, ada_w, ada_b.reshape(L, 1, N))


def _in_proj_kernel(x_ref, mod_ref, g_ref, w_ref, z_ref):
    D = D_MODEL
    x = x_ref[0]
    mod = mod_ref[0]
    h = _rms(x, g_ref[...]) * (1.0 + mod[:, D:2 * D]) + mod[:, 0:D]
    z_ref[0] = _dot(h.astype(BF16), w_ref[...]).astype(BF16)


def _in_proj(xall, modrows, g, w):
    B, R, D = xall.shape
    nt = R // TM
    nn = Z_COLS // IN_NBLK
    return pl.pallas_call(
        _in_proj_kernel,
        grid=(nn, B, nt),
        in_specs=[pl.BlockSpec((1, TM, D), lambda n, b, t: (b, t, 0)),
                  pl.BlockSpec((1, 1, 6 * D), lambda n, b, t: (2 * b + jnp.minimum(t, 1), 0, 0)),
                  pl.BlockSpec((1, D), lambda n, b, t: (0, 0)),
                  pl.BlockSpec((D, IN_NBLK), lambda n, b, t: (0, n))],
        out_specs=pl.BlockSpec((1, TM, IN_NBLK), lambda n, b, t: (b, t, n)),
        out_shape=jax.ShapeDtypeStruct((B, R, Z_COLS), BF16),
        compiler_params=_cparams("parallel", "parallel", "parallel"),
        name="in_proj",
    )(xall, modrows, g, w)


def _pool_kernel(u_ref, w_ref, sc_ref, o_ref):
    R = u_ref.shape[1]
    row = lax.broadcasted_iota(jnp.int32, (R, POOL_GROUP), 0)
    seg_lo = jnp.where(row < CTX_LEN, 0, CTX_LEN)
    seg_hi = jnp.where(row < CTX_LEN, CTX_LEN, R)
    for gi, win in enumerate(POOL_WINDOWS):
        sl = slice(gi * POOL_GROUP, (gi + 1) * POOL_GROUP)
        u = u_ref[0, :, sl].astype(F32)
        lo_off, hi_off = win // 2, win - win // 2
        acc = jnp.zeros_like(u)
        for d in range(-lo_off, hi_off):
            shifted = u if d == 0 else pltpu.roll(u, (R - d) % R, axis=0)
            ok = (row + d >= seg_lo) & (row + d < seg_hi)
            acc = acc + jnp.where(ok, shifted, 0.0)
        cnt = (jnp.minimum(row + hi_off, seg_hi) - jnp.maximum(row - lo_off, seg_lo)).astype(F32)
        diff = (acc / cnt - u).astype(BF16)
        o_ref[0, :, sl] = (_dot(diff, w_ref[gi]) * sc_ref[:, sl]).astype(BF16)


def _pool(z, pool_w, pool_scale):
    B, R, _ = z.shape
    return pl.pallas_call(
        _pool_kernel,
        grid=(B,),
        in_specs=[pl.BlockSpec((1, R, POOL_WIDTH), lambda b: (b, 0, COL_P // POOL_WIDTH)),
                  pl.BlockSpec(pool_w.shape, lambda b: (0, 0, 0)),
                  pl.BlockSpec((1, POOL_WIDTH), lambda b: (0, 0))],
        out_specs=pl.BlockSpec((1, R, POOL_WIDTH), lambda b: (b, 0, 0)),
        out_shape=jax.ShapeDtypeStruct((B, R, POOL_WIDTH), BF16),
        compiler_params=_cparams("parallel"),
        name="pool",
    )(z, pool_w, pool_scale)


def _split3(x):
    a = x.astype(BF16)
    r = x - a.astype(F32)
    b = r.astype(BF16)
    c = (r - b.astype(F32)).astype(BF16)
    return a, b, c


def _gla_kernel(*refs, fwd):
    if fwd:
        q_ref, k_ref, v_ref, misc_ref, gw_ref, gb_ref, ob_ref, r_ref, gn_ref, o_ref, st_ref = refs
    else:
        q_ref, k_ref, v_ref, misc_ref, gw_ref, gb_ref, o_ref, st_ref = refs
    C = GLA_CHUNK
    nchunk = TM // C
    HK = GLA_HEADS * GLA_DK

    @pl.when(pl.program_id(1) == 0)
    def _():
        st_ref[...] = jnp.zeros_like(st_ref)

    zg = _dot(misc_ref[0], gw_ref[...]) + gb_ref[...]
    g = (jnp.minimum(zg, 0.0) - jnp.log1p(jnp.exp(-jnp.abs(zg)))) * (1.0 / GLA_TAU)

    row = lax.broadcasted_iota(jnp.int32, (TM, TM), 0)
    col = lax.broadcasted_iota(jnp.int32, (TM, TM), 1)
    shift = C.bit_length() - 1
    same = (row >> shift) == (col >> shift)
    tri = same & ((col <= row) if fwd else (col >= row))
    ones = jnp.concatenate([jnp.where(tri, 1.0, 0.0), jnp.where(same, 1.0, 0.0)], axis=0).astype(BF16)
    g1, g2, g3 = _split3(g)
    both = _dot(ones, g1) + _dot(ones, g2) + _dot(ones, g3)
    b = both[:TM]
    bt = both[TM:]

    qe = q_ref[0].astype(F32) * (GLA_DK ** -0.5) * jnp.exp(b)
    ke = (k_ref[0].astype(F32) * jnp.exp(-b)).astype(BF16)
    kd = (k_ref[0].astype(F32) * jnp.exp(bt - b)).astype(BF16)
    qe = qe.astype(BF16)
    dec = jnp.exp(bt)
    v = v_ref[0]

    order = range(nchunk) if fwd else range(nchunk - 1, -1, -1)
    for h in range(GLA_HEADS):
        ks = slice(h * GLA_DK, (h + 1) * GLA_DK)
        vs = slice(h * GLA_DV, (h + 1) * GLA_DV)
        att = jnp.where(tri, _dot_nt(qe[:, ks], ke[:, ks]), 0.0).astype(BF16)
        o_in = _dot(att, v[:, vs])
        st = st_ref[h]
        parts = [None] * nchunk
        for j in order:
            rs = slice(j * C, (j + 1) * C)
            parts[j] = o_in[rs] + _dot_nt(qe[rs, ks], st.astype(BF16))
            st = st * dec[j * C:j * C + 1, ks] + _dot_tn(v[rs, vs], kd[rs, ks])
        st_ref[h] = st
        o = jnp.concatenate(parts, axis=0)
        if fwd:
            o = o + ob_ref[0, :, vs].astype(F32)
            r = r_ref[0, :, vs].astype(F32)
            o = _rms(o, gn_ref[...]) * (r * jax.nn.sigmoid(r))
        o_ref[0, :, vs] = o.astype(BF16)


def _gla(z, gw, gb, *, fwd, ob=None, gn=None):
    B, R, _ = z.shape
    nt = R // TM
    HK, HV = GLA_HEADS * GLA_DK, GLA_HEADS * GLA_DV
    if fwd:
        tile = lambda c: c
    else:
        tile = lambda c: jnp.where(c == 0, 0, nt - c)
    in_specs = [pl.BlockSpec((1, TM, HK), lambda b, c: (b, tile(c), COL_Q // HK)),
                pl.BlockSpec((1, TM, HK), lambda b, c: (b, tile(c), COL_K // HK)),
                pl.BlockSpec((1, TM, HV), lambda b, c: (b, tile(c), COL_V // HV)),
                pl.BlockSpec((1, TM, LANES), lambda b, c: (b, tile(c), COL_MISC // LANES)),
                pl.BlockSpec((LANES, HK), lambda b, c: (0, 0)),
                pl.BlockSpec((1, HK), lambda b, c: (0, 0))]
    args = [z, z, z, z, gw, gb]
    if fwd:
        in_specs += [pl.BlockSpec((1, TM, HV), lambda b, c: (b, tile(c), 0)),
                     pl.BlockSpec((1, TM, HV), lambda b, c: (b, tile(c), COL_R // HV)),
                     pl.BlockSpec((1, GLA_DV), lambda b, c: (0, 0))]
        args += [ob, z, gn]
    return pl.pallas_call(
        functools.partial(_gla_kernel, fwd=fwd),
        grid=(B, nt),
        in_specs=in_specs,
        out_specs=pl.BlockSpec((1, TM, HV), lambda b, c: (b, tile(c), 0)),
        out_shape=jax.ShapeDtypeStruct((B, R, HV), BF16),
        scratch_shapes=[pltpu.VMEM((GLA_HEADS, GLA_DV, GLA_DK), F32)],
        compiler_params=_cparams("parallel", "arbitrary"),
        name="gla_fwd" if fwd else "gla_bwd",
    )(*args)


def _mla_up_kernel(qdn_ref, kvdn_ref, misc_ref, sw_ref, cos_ref, sin_ref, qg_ref, kvg_ref, wq_ref, wkv_ref,
                   q_ref, k_ref, v_ref):
    H = MLA_HEADS
    scale = (MLA_NOPE + MLA_ROPE) ** -0.5 * float(np.log2(np.e))
    cos = cos_ref[...]
    sin = sin_ref[...]
    qn = _rms(qdn_ref[0].astype(F32), qg_ref[...]).astype(BF16)
    qall = _dot(qn, wq_ref[...])
    for h in range(H):
        base = h * MQ
        q_ref[0, :, base:base + MLA_NOPE] = (qall[:, base:base + MLA_NOPE] * scale).astype(BF16)
        rot = (qall[:, base + MLA_NOPE:base + MQ] * cos
               + qall[:, H * MQ + h * LANES:H * MQ + (h + 1) * LANES] * sin)
        q_ref[0, :, base + MLA_NOPE:base + MQ] = (rot * scale).astype(BF16)
    kvn = _rms(kvdn_ref[0].astype(F32), kvg_ref[...]).astype(BF16)
    kvall = _dot(kvn, wkv_ref[...])
    kr = (misc_ref[0].astype(F32) * cos + sw_ref[0].astype(F32) * sin).astype(BF16)
    for h in range(H):
        src = h * (MLA_NOPE + MLA_V)
        k_ref[0, :, h * MQ:h * MQ + MLA_NOPE] = kvall[:, src:src + MLA_NOPE].astype(BF16)
        k_ref[0, :, h * MQ + MLA_NOPE:(h + 1) * MQ] = kr
        v_ref[0, :, h * MLA_V:(h + 1) * MLA_V] = kvall[:, src + MLA_NOPE:src + MLA_NOPE + MLA_V].astype(BF16)


def _mla_up(z, cos, sin, qg, kvg, wq, wkv):
    B, R, _ = z.shape
    nt = R // TM
    H = MLA_HEADS
    const = lambda b, t: (0, 0)
    return pl.pallas_call(
        _mla_up_kernel,
        grid=(B, nt),
        in_specs=[pl.BlockSpec((1, TM, MLA_Q_RANK), lambda b, t: (b, t, COL_QDN // MLA_Q_RANK)),
                  pl.BlockSpec((1, TM, MLA_KV_RANK), lambda b, t: (b, t, COL_KVDN // MLA_KV_RANK)),
                  pl.BlockSpec((1, TM, LANES), lambda b, t: (b, t, COL_MISC // LANES)),
                  pl.BlockSpec((1, TM, LANES), lambda b, t: (b, t, COL_SW // LANES)),
                  pl.BlockSpec((TM, LANES), lambda b, t: (t, 0)),
                  pl.BlockSpec((TM, LANES), lambda b, t: (t, 0)),
                  pl.BlockSpec(qg.shape, const),
                  pl.BlockSpec(kvg.shape, const),
                  pl.BlockSpec(wq.shape, const),
                  pl.BlockSpec(wkv.shape, const)],
        out_specs=[pl.BlockSpec((1, TM, H * MQ), lambda b, t: (b, t, 0)),
                   pl.BlockSpec((1, TM, H * MQ), lambda b, t: (b, t, 0)),
                   pl.BlockSpec((1, TM, H * MLA_V), lambda b, t: (b, t, 0))],
        out_shape=[jax.ShapeDtypeStruct((B, R, H * MQ), BF16),
                   jax.ShapeDtypeStruct((B, R, H * MQ), BF16),
                   jax.ShapeDtypeStruct((B, R, H * MLA_V), BF16)],
        compiler_params=_cparams("parallel", "parallel"),
        name="mla_up",
    )(z, z, z, z, cos, sin, qg, kvg, wq, wkv)


def _attn_kernel(q_ref, k_ref, v_ref, o_ref, *, q_off):
    R = k_ref.shape[1]

    def attend(nk):
        s = _dot_nt(q_ref[0], k_ref[0, :nk])
        p = jnp.exp2(s - jnp.max(s, axis=-1, keepdims=True))
        l = jnp.sum(p, axis=-1, keepdims=True)
        o_ref[0] = (_dot(p.astype(BF16), v_ref[0, :nk]) / l).astype(BF16)

    if q_off == 0:
        qi = pl.program_id(2)

        @pl.when(qi == 0)
        def _():
            attend(CTX_LEN)

        @pl.when(qi > 0)
        def _():
            attend(R)
    else:
        attend(R)


def _attn(q, k, v, *, with_ctx):
    B, R, _ = q.shape
    q_off = 0 if with_ctx else CTX_LEN // TM
    nq = R // TM - q_off
    return pl.pallas_call(
        functools.partial(_attn_kernel, q_off=q_off),
        grid=(B, MLA_HEADS, nq),
        in_specs=[pl.BlockSpec((1, TM, MQ), lambda b, h, i: (b, i + q_off, h)),
                  pl.BlockSpec((1, R, MQ), lambda b, h, i: (b, 0, h)),
                  pl.BlockSpec((1, R, MLA_V), lambda b, h, i: (b, 0, h))],
        out_specs=pl.BlockSpec((1, TM, MLA_V), lambda b, h, i: (b, i + q_off, h)),
        out_shape=jax.ShapeDtypeStruct((B, R, MLA_HEADS * MLA_V), BF16),
        compiler_params=_cparams("parallel", "parallel", "arbitrary"),
        name="mla_attn",
    )(q, k, v)


def _merge_kernel(po_ref, go_ref, mo_ref, gz0_ref, gz1_ref, gz2_ref, x_ref, mod_ref, bw_ref, wo_ref, g_ref,
                  x1_ref, h2_ref):
    D = D_MODEL
    m = None
    for i, (o_ref, gz_ref) in enumerate(((po_ref, gz0_ref), (go_ref, gz1_ref), (mo_ref, gz2_ref))):
        t = jax.nn.sigmoid(gz_ref[0].astype(F32)) * _dot(o_ref[0], bw_ref[i])
        m = t if m is None else m + t
    y = _dot(m.astype(BF16), wo_ref[...])
    mod = mod_ref[0]
    x1 = x_ref[0] + mod[:, 2 * D:3 * D] * y
    x1_ref[0] = x1
    h2 = _rms(x1, g_ref[...]) * (1.0 + mod[:, 4 * D:5 * D]) + mod[:, 3 * D:4 * D]
    h2_ref[0] = h2.astype(BF16)


def _merge(pool_o, gla_o, mla_o, z, xall, modrows, bw, wo, g, *, with_ctx):
    B, R, D = xall.shape
    t_off = 0 if with_ctx else CTX_LEN // TM
    nt = R // TM - t_off
    tok = lambda b, t: (b, t + t_off, 0)
    gzb = COL_GZ // D
    return pl.pallas_call(
        _merge_kernel,
        grid=(B, nt),
        in_specs=[pl.BlockSpec((1, TM, D), tok),
                  pl.BlockSpec((1, TM, D), tok),
                  pl.BlockSpec((1, TM, D), tok),
                  pl.BlockSpec((1, TM, D), lambda b, t: (b, t + t_off, gzb)),
                  pl.BlockSpec((1, TM, D), lambda b, t: (b, t + t_off, gzb + 1)),
                  pl.BlockSpec((1, TM, D), lambda b, t: (b, t + t_off, gzb + 2)),
                  pl.BlockSpec((1, TM, D), tok),
                  pl.BlockSpec((1, 1, 6 * D), lambda b, t: (2 * b + jnp.minimum(t + t_off, 1), 0, 0)),
                  pl.BlockSpec(bw.shape, lambda b, t: (0, 0, 0)),
                  pl.BlockSpec(wo.shape, lambda b, t: (0, 0)),
                  pl.BlockSpec((1, D), lambda b, t: (0, 0))],
        out_specs=[pl.BlockSpec((1, TM, D), lambda b, t: (b, t, 0)),
                   pl.BlockSpec((1, TM, D), lambda b, t: (b, t, 0))],
        out_shape=[jax.ShapeDtypeStruct((B, nt * TM, D), F32),
                   jax.ShapeDtypeStruct((B, nt * TM, D), BF16)],
        compiler_params=_cparams("parallel", "parallel"),
        name="merge",
    )(pool_o, gla_o, mla_o, z, z, z, xall, modrows, bw, wo, g)


def _top16(s):
    nk, n = s.shape
    iota = lax.broadcasted_iota(jnp.int32, (nk, n), 0).astype(F32)
    slot = lax.broadcasted_iota(jnp.int32, (PEER_TOPK, n), 0)
    rank = jnp.full((nk, n), float(PEER_TOPK), F32)
    vals = jnp.zeros((PEER_TOPK, n), F32)
    for j in range(PEER_TOPK):
        m = jnp.max(s, axis=0, keepdims=True)
        idx = jnp.min(jnp.where(s == m, iota, float(nk)), axis=0, keepdims=True)
        hit = iota == idx
        rank = jnp.where(hit, float(j), rank)
        s = jnp.where(hit, -jnp.inf, s)
        vals = jnp.where(slot == j, m, vals)
    return vals, rank


PAIR_SHORT = PEER_TOPK // 2
PAIR_ROWS = PEER_TOPK + (PEER_TOPK - 1) * PAIR_SHORT
PAIR_VALID = sum(PEER_TOPK // (i + 1) for i in range(PEER_TOPK))


def _pair_rows(t2):
    return jnp.concatenate([t2] + [t2[0:PAIR_SHORT]] * (PEER_TOPK - 1), axis=0)


def _pair_candidates(t1, t2):
    K = PEER_TOPK
    n = t1.shape[1]
    first = jnp.concatenate([jnp.broadcast_to(t1[0:1], (K, n))]
                            + [jnp.broadcast_to(t1[i:i + 1], (PAIR_SHORT, n)) for i in range(1, K)], axis=0)
    r = lax.broadcasted_iota(jnp.int32, (PAIR_ROWS, n), 0)
    sh = PAIR_SHORT.bit_length() - 1
    i_of = jnp.where(r < K, 0, ((r - K) >> sh) + 1)
    j_of = jnp.where(r < K, r, (r - K) & (PAIR_SHORT - 1))
    cand = jnp.where((i_of + 1) * (j_of + 1) <= K, first + _pair_rows(t2), -jnp.inf)
    return cand, (i_of * K + j_of).astype(F32)


def _top16_values(s, with_rank):
    n = s.shape[1]
    slot = lax.broadcasted_iota(jnp.int32, (PEER_TOPK, n), 0)
    vals = jnp.zeros((PEER_TOPK, n), F32)
    rank = jnp.full(s.shape, float(PEER_TOPK), F32) if with_rank else None
    for j in range(PEER_TOPK):
        m = jnp.max(s, axis=0, keepdims=True)
        hit = s == m
        if with_rank:
            rank = jnp.where(hit, float(j), rank)
        s = jnp.where(hit, -jnp.inf, s)
        vals = jnp.where(slot == j, m, vals)
    used = jnp.sum(jnp.where(s == -jnp.inf, 1.0, 0.0), axis=0, keepdims=True)
    return vals, rank, jnp.where(used == float(PEER_TOPK), 1.0, 0.0)


def _pair_counts(t1, t2):
    K = PEER_TOPK
    n = t1.shape[1]
    cand0, _ = _pair_candidates(t1, t2)
    cand = cand0
    zsum = jnp.zeros((1, n), F32)
    best = m = None
    for j in range(K):
        m = jnp.max(cand, axis=0, keepdims=True)
        cand = jnp.where(cand == m, -jnp.inf, cand)
        if j == 0:
            best = m
        zsum = zsum + jnp.exp(m - best)
    used = jnp.sum(jnp.where(cand == -jnp.inf, 1.0, 0.0), axis=0, keepdims=True)
    exact = jnp.where(used == float(PAIR_ROWS - PAIR_VALID + K), 1.0, 0.0)
    picked = jnp.where(cand0 >= m, 1.0, 0.0)
    slot = lax.broadcasted_iota(jnp.int32, (K, n), 0)
    cnt = jnp.zeros((K, n), F32)
    for i in range(K):
        blk = picked[0:K] if i == 0 else picked[K + (i - 1) * PAIR_SHORT:K + i * PAIR_SHORT]
        cnt = jnp.where(slot == i, jnp.sum(blk, axis=0, keepdims=True), cnt)
    return cnt, zsum, exact


def _pair_top16(t1, t2):
    K = PEER_TOPK
    n = t1.shape[1]
    cand, flat = _pair_candidates(t1, t2)
    irow = lax.broadcasted_iota(jnp.int32, (K, n), 0).astype(F32)
    cnt = jnp.zeros((K, n), F32)
    zsum = jnp.zeros((1, n), F32)
    best = None
    for j in range(K):
        m = jnp.max(cand, axis=0, keepdims=True)
        idx = jnp.min(jnp.where(cand == m, flat, float(K * K)), axis=0, keepdims=True)
        cand = jnp.where(flat == idx, -jnp.inf, cand)
        cnt = cnt + jnp.where(irow == jnp.floor(idx * (1.0 / K)), 1.0, 0.0)
        if j == 0:
            best = m
        zsum = zsum + jnp.exp(m - best)
    return cnt, zsum


CNT_BITS = 0xFF


def _words_per_row():
    return 4 // jnp.dtype(BF16).itemsize


def _peer_score_kernel(h2_ref, wqt_ref, keys_ref, c2_ref, p2_ref, ak_ref, qt_ref):
    K = PEER_TOPK
    qt_ref[...] = _dot_nt(wqt_ref[...], h2_ref[...]).astype(BF16)

    def store(h, cs, s1, s2, t1, rank1, t2, rank2, cnt, zsum):
        rowcnt = jnp.zeros_like(s1)
        for j in range(K):
            in_row = (s1 == t1[j:j + 1]) if rank1 is None else (rank1 == float(j))
            rowcnt = jnp.where(in_row, cnt[j:j + 1], rowcnt)
        c2_ref[h, :, cs] = pltpu.bitcast(rank2.astype(BF16), jnp.int32)
        p2_ref[h, :, cs] = pltpu.bitcast(jnp.exp(s2 - t2[0:1]).astype(BF16), jnp.int32)
        ka = (jnp.exp(s1 - t1[0:1]) / zsum).astype(BF16).astype(F32)
        ak_ref[h, :, cs] = (lax.bitcast_convert_type(ka, jnp.int32) & ~CNT_BITS) | rowcnt.astype(jnp.int32)

    def head(h, carry):
        base = pl.multiple_of(h * PEER_QDIM, PEER_QDIM)
        s1_all = _dot(keys_ref[h], qt_ref[pl.ds(base, PEER_HALF), :])
        s2_all = _dot(keys_ref[PEER_HEADS + h], qt_ref[pl.ds(base + PEER_HALF, PEER_HALF), :])
        chunks = []
        for c in range(s1_all.shape[1] // LANES):
            cs = slice(c * LANES, (c + 1) * LANES)
            s1, s2 = s1_all[:, cs], s2_all[:, cs]
            t1, _, ok1 = _top16_values(s1, with_rank=False)
            t2, rank2, ok2 = _top16_values(s2, with_rank=True)
            cnt, zsum, ok3 = _pair_counts(t1, t2)
            store(h, cs, s1, s2, t1, None, t2, rank2, cnt, zsum)
            chunks.append((cs, s1, s2, jnp.min(ok1 * ok2 * ok3)))

        for cs, s1, s2, exact in chunks:
            @pl.when(exact < 0.5)
            def _(cs=cs, s1=s1, s2=s2):
                t1x, rank1x = _top16(s1)
                t2x, rank2x = _top16(s2)
                cntx, zx = _pair_top16(t1x, t2x)
                store(h, cs, s1, s2, t1x, rank1x, t2x, rank2x, cntx, zx)

        return carry

    lax.fori_loop(0, PEER_HEADS, head, 0)


PEER_SCORE_TN = 2 * LANES


def _peer_score(h2, wqt, keys):
    ntok, D = h2.shape
    tn = PEER_SCORE_TN
    nw = PEER_N_KEYS // _words_per_row()
    tbl = lambda rows: jax.ShapeDtypeStruct((PEER_HEADS, rows, ntok), jnp.int32)
    ospec = lambda rows: pl.BlockSpec((PEER_HEADS, rows, tn), lambda i: (0, 0, i))
    return pl.pallas_call(
        _peer_score_kernel,
        grid=(ntok // tn,),
        in_specs=[pl.BlockSpec((tn, D), lambda i: (i, 0)),
                  pl.BlockSpec(wqt.shape, lambda i: (0, 0)),
                  pl.BlockSpec(keys.shape, lambda i: (0, 0, 0))],
        out_specs=[ospec(nw), ospec(nw), ospec(PEER_N_KEYS)],
        out_shape=[tbl(nw), tbl(nw), tbl(PEER_N_KEYS)],
        scratch_shapes=[pltpu.VMEM((PEER_HEADS * PEER_QDIM, tn), BF16)],
        compiler_params=_cparams("parallel"),
        name="peer_score",
    )(h2, wqt, keys)


PEER_KEYS_PER_STEP = 4
PEER_ETILE = PEER_KEYS_PER_STEP * PEER_N_KEYS
PEER_KEY_GROUP = 2
PEER_LAG = 2


def _peer_dense_kernel(h2_ref, c2_ref, p2_ref, ak_ref, u_ref, vt_ref, x1_ref, modc_ref, modl_ref, fg_ref,
                       o_ref, acc_ref, pre_ref, hid_ref, *, ctx_rows, rows_per_sample, final):
    D = D_MODEL
    NK = PEER_N_KEYS
    tn = h2_ref.shape[0]
    s = pl.program_id(1)
    last_tile = pl.num_programs(1) - PEER_LAG - 1
    cur = s % 2
    prev = 1 - cur

    @pl.when(s == 0)
    def _():
        acc_ref[...] = jnp.zeros_like(acc_ref)
        pre_ref[...] = jnp.zeros_like(pre_ref)
        hid_ref[...] = jnp.zeros_like(hid_ref)

    pre_ref[cur] = _dot_nt(pltpu.bitcast(u_ref[...], BF16), h2_ref[...])

    key0 = jnp.clip(s - 1, 0, last_tile) * PEER_KEYS_PER_STEP
    words = [[ak_ref[h, pl.ds(key0 + i, 1), :] for h in range(PEER_HEADS)]
             for i in range(PEER_KEYS_PER_STEP)]

    def key_rows(i, h, cs):
        wd = words[i][h][:, cs]
        rep = lambda row: jnp.broadcast_to(row, (BF16_ROWS, LANES)).astype(BF16)
        return rep((wd & CNT_BITS).astype(F32)), rep(lax.bitcast_convert_type(wd & ~CNT_BITS, F32))

    for c in range(tn // LANES):
        cs = slice(c * LANES, (c + 1) * LANES)
        for g in range(0, PEER_KEYS_PER_STEP, PEER_KEY_GROUP):
            group = range(g, g + PEER_KEY_GROUP)
            w = {i: None for i in group}
            for h in range(PEER_HEADS):
                c2 = pltpu.bitcast(c2_ref[h, :, cs], BF16).reshape(NK // BF16_ROWS, BF16_ROWS, LANES)
                p2 = pltpu.bitcast(p2_ref[h, :, cs], BF16).reshape(NK // BF16_ROWS, BF16_ROWS, LANES)
                for i in group:
                    cnt, kaw = key_rows(i, h, cs)
                    t = jnp.where(c2 < cnt[None], p2, 0.0) * kaw[None]
                    w[i] = t if w[i] is None else w[i] + t
            for i in group:
                rows = slice(i * NK, (i + 1) * NK)
                x = pre_ref[prev, rows, cs].astype(BF16)
                act = 0.5 * x * (1.0 + lax.erf(x * (2.0 ** -0.5)))
                hid_ref[prev, rows, cs] = act * w[i].reshape(NK, LANES)

    acc_ref[...] += _dot(pltpu.bitcast(vt_ref[...], BF16), hid_ref[cur])

    @pl.when(s == pl.num_programs(1) - 1)
    def _():
        y = acc_ref[...].T
        g_lat = modl_ref[0][:, 5 * D:6 * D]
        if ctx_rows:
            g_ctx = modc_ref[0][:, 5 * D:6 * D]
            tiles = rows_per_sample // tn
            r0 = (pl.program_id(0) % tiles) * tn
            row = r0 + lax.broadcasted_iota(jnp.int32, (tn, D), 0)
            gate = jnp.where(row < ctx_rows, g_ctx, g_lat)
        else:
            gate = g_lat
        x2 = x1_ref[...] + gate * y
        if final:
            x2 = _rms(x2, fg_ref[...])
        o_ref[...] = x2


def _peer_dense(h2, sel, u, vt, x1, modrows, fg, *, tn, rows_per_sample, ctx_rows, final):
    ntok, D = h2.shape
    wpr = _words_per_row()
    ne = u.shape[0] * wpr // PEER_ETILE
    tiles = rows_per_sample // tn
    tspec = lambda t: pl.BlockSpec((PEER_HEADS, t.shape[1], tn), lambda i, s: (0, 0, i))
    return pl.pallas_call(
        functools.partial(_peer_dense_kernel, ctx_rows=ctx_rows, rows_per_sample=rows_per_sample, final=final),
        grid=(ntok // tn, ne + PEER_LAG),
        in_specs=[pl.BlockSpec((tn, D), lambda i, s: (i, 0)),
                  tspec(sel[0]), tspec(sel[1]), tspec(sel[2]),
                  pl.BlockSpec((PEER_ETILE // wpr, D), lambda i, s: (jnp.minimum(s, ne - 1), 0)),
                  pl.BlockSpec((D // wpr, PEER_ETILE), lambda i, s: (0, jnp.clip(s - PEER_LAG, 0, ne - 1))),
                  pl.BlockSpec((tn, D), lambda i, s: (i, 0), pipeline_mode=pl.Buffered(1)),
                  pl.BlockSpec((1, 1, 6 * D), lambda i, s: (2 * (i // tiles), 0, 0)),
                  pl.BlockSpec((1, 1, 6 * D), lambda i, s: (2 * (i // tiles) + 1, 0, 0)),
                  pl.BlockSpec((1, D), lambda i, s: (0, 0))],
        out_specs=pl.BlockSpec((tn, D), lambda i, s: (i, 0)),
        out_shape=jax.ShapeDtypeStruct((ntok, D), F32),
        scratch_shapes=[pltpu.VMEM((D, tn), F32),
                        pltpu.VMEM((2, PEER_ETILE, tn), F32),
                        pltpu.VMEM((2, PEER_ETILE, tn), BF16)],
        compiler_params=_cparams("parallel", "arbitrary"),
        name="peer_dense",
    )(h2, *sel, u, vt, x1, modrows, modrows, fg)


def _pack_rows_kernel(x_ref, o_ref):
    o_ref[...] = pltpu.bitcast(x_ref[...].astype(BF16), jnp.int32)


def _pack_rows(x):
    R, C = x.shape
    wpr = _words_per_row()
    br, bc = 512, 1024
    return pl.pallas_call(
        _pack_rows_kernel,
        grid=(R // br, C // bc),
        in_specs=[pl.BlockSpec((br, bc), lambda i, j: (i, j))],
        out_specs=pl.BlockSpec((br // wpr, bc), lambda i, j: (i, j)),
        out_shape=jax.ShapeDtypeStruct((R // wpr, C), jnp.int32),
        compiler_params=_cparams("parallel", "parallel"),
        name="pack_rows",
    )(x)


def _peer_tile(rows_per_sample):
    for tn in (1024, 768, 512, 256):
        if rows_per_sample % tn == 0:
            return tn
    raise ValueError(rows_per_sample)


_SWAP = np.concatenate([np.arange(16, 32), np.arange(0, 16), np.arange(48, 64), np.arange(32, 48)])


def _layout_w_in(w):
    D = w.shape[0]
    idx = np.cumsum(IN_SPLITS)[:-1]
    p, q, k, v, r, lrf, lrb, qdn, kvdn, krope, gz = jnp.split(w, idx, axis=1)
    zeros = lambda n: jnp.zeros((D, n), w.dtype)
    misc = jnp.concatenate([krope, lrf, lrb, zeros(LANES - MLA_ROPE - 2 * GLA_GATE_RANK)], axis=1)
    sw = jnp.concatenate([krope[:, _SWAP], zeros(LANES - MLA_ROPE)], axis=1)
    out = jnp.concatenate([p, q, k, v, r, gz, kvdn, misc, sw, qdn], axis=1)
    return jnp.concatenate([out, zeros(Z_COLS - out.shape[1])], axis=1).astype(BF16)


def _layout_gate_w(gate_w, row0):
    HK = GLA_HEADS * GLA_DK
    pad = jnp.zeros((LANES, HK), gate_w.dtype)
    return pad.at[row0:row0 + GLA_GATE_RANK].set(gate_w).astype(BF16)


def _layout_w_uq(w):
    H = MLA_HEADS
    w3 = w.reshape(MLA_Q_RANK, H, MLA_NOPE + MLA_ROPE)
    qn, qr = w3[..., :MLA_NOPE], w3[..., MLA_NOPE:]
    z = jnp.zeros((MLA_Q_RANK, H, LANES - MLA_ROPE), w.dtype)
    main = jnp.concatenate([qn, qr, z], axis=-1).reshape(MLA_Q_RANK, H * MQ)
    swp = jnp.concatenate([qr[..., _SWAP], z], axis=-1).reshape(MLA_Q_RANK, H * LANES)
    return jnp.concatenate([main, swp], axis=1).astype(BF16)


def _rope_tables(seq):
    half = MLA_ROPE // 2
    t = jnp.arange(seq)
    inv = ROPE_BASE ** (-jnp.arange(0, half, 2, dtype=F32) / half)
    ar = (t // GRID_W).astype(F32)[:, None] * inv
    ac = (t % GRID_W).astype(F32)[:, None] * inv
    cos = jnp.concatenate([jnp.cos(ar), jnp.cos(ar), jnp.cos(ac), jnp.cos(ac)], axis=1)
    sin = jnp.concatenate([-jnp.sin(ar), jnp.sin(ar), -jnp.sin(ac), jnp.sin(ac)], axis=1)
    cos = jnp.concatenate([jnp.ones((CTX_LEN, MLA_ROPE), F32), cos], axis=0)
    sin = jnp.concatenate([jnp.zeros((CTX_LEN, MLA_ROPE), F32), sin], axis=0)
    pad = jnp.zeros((CTX_LEN + seq, LANES - MLA_ROPE), F32)
    return jnp.concatenate([cos, pad], axis=1), jnp.concatenate([sin, pad], axis=1)


def kernel(x, c, ctx, c_ctx, ada_w, ada_b, norm1_g, norm2_g, w_in, pool_w, pool_scale, gla_gate_w, gla_gate_b, gla_norm_g, mla_q_norm_g, mla_kv_norm_g, mla_w_uq, mla_w_ukv, branch_w, w_out, peer_wq, peer_keys, peer_u, peer_v, final_norm_g):
    B, T, D = x.shape
    R = CTX_LEN + T
    assert D == D_MODEL and ctx.shape[1] == CTX_LEN == TM and T % TM == 0 and T % GRID_W == 0

    nrow = -(-(B + 1) // 8) * 8
    cc = jnp.concatenate([c, c_ctx[None], jnp.zeros((nrow - B - 1, D), F32)], axis=0)
    mod = _ada(cc, ada_w, ada_b)
    modrows = jnp.stack([jnp.broadcast_to(mod[:, B:B + 1], (DEPTH, B, 6 * D)), mod[:, :B]], axis=2)
    modrows = modrows.reshape(DEPTH, 2 * B, 1, 6 * D)

    cos, sin = _rope_tables(T)
    xall = jnp.concatenate([ctx, x], axis=1)
    row = lambda a: a.reshape(1, -1)

    for i in range(DEPTH):
        with_ctx = i < DEPTH - 1
        final = i == DEPTH - 1
        z = _in_proj(xall, modrows[i], row(norm1_g[i]), _layout_w_in(w_in[i]))

        pool_o = _pool(z, pool_w[i].astype(BF16), row(pool_scale[i]))

        o_b = _gla(z, _layout_gate_w(gla_gate_w[i, 1], MISC_LRB), row(gla_gate_b[i, 1]), fwd=False)
        gla_o = _gla(z, _layout_gate_w(gla_gate_w[i, 0], MISC_LRF), row(gla_gate_b[i, 0]), fwd=True,
                     ob=o_b, gn=row(gla_norm_g[i]))

        q, k, v = _mla_up(z, cos, sin, row(mla_q_norm_g[i]), row(mla_kv_norm_g[i]),
                          _layout_w_uq(mla_w_uq[i]), mla_w_ukv[i].astype(BF16))
        mla_o = _attn(q, k, v, with_ctx=with_ctx)

        x1, h2 = _merge(pool_o, gla_o, mla_o, z, xall, modrows[i], branch_w[i].astype(BF16),
                        w_out[i].astype(BF16), row(norm2_g[i]), with_ctx=with_ctx)

        rows = x1.shape[1]
        h2f = h2.reshape(B * rows, D)
        keys = peer_keys[i].reshape(2 * PEER_HEADS, PEER_N_KEYS, PEER_HALF).astype(BF16)
        sel = _peer_score(h2f, peer_wq[i].T.astype(BF16), keys)
        xall = _peer_dense(h2f, sel, _pack_rows(peer_u[i]), _pack_rows(peer_v[i].T), x1.reshape(B * rows, D),
                           modrows[i], row(final_norm_g), tn=_peer_tile(rows), rows_per_sample=rows,
                           ctx_rows=CTX_LEN if with_ctx else 0, final=final).reshape(B, rows, D)
    return xall
```

```python
import functools

import jax
import jax.numpy as jnp
import numpy as np
from jax import lax
from jax.experimental import pallas as pl
from jax.experimental.pallas import tpu as pltpu

F32 = jnp.float32
BF16 = jnp.bfloat16

D_MODEL = 1024
DEPTH = 2
CTX_LEN = 256
GRID_W = 64
NORM_EPS = 1e-6
POOL_WIDTH = 1024
POOL_WINDOWS = (2, 4, 8, 16)
POOL_GROUP = POOL_WIDTH // len(POOL_WINDOWS)
GLA_HEADS = 4
GLA_DK = 128
GLA_DV = 256
GLA_GATE_RANK = 16
GLA_TAU = 16.0
GLA_CHUNK = 64
MLA_HEADS = 8
MLA_Q_RANK = 384
MLA_KV_RANK = 256
MLA_NOPE = 128
MLA_ROPE = 64
MLA_V = 128
ROPE_BASE = 10000.0
N_BRANCH = 3
PEER_HEADS = 8
PEER_N_KEYS = 128
PEER_TOPK = 16
PEER_QDIM = 256
PEER_HALF = PEER_QDIM // 2
IN_SPLITS = (POOL_WIDTH, GLA_HEADS * GLA_DK, GLA_HEADS * GLA_DK, GLA_HEADS * GLA_DV, GLA_HEADS * GLA_DV,
             GLA_GATE_RANK, GLA_GATE_RANK, MLA_Q_RANK, MLA_KV_RANK, MLA_ROPE, N_BRANCH * D_MODEL)

LANES = 128
BF16_ROWS = 16
TM = 256
VMEM_LIMIT = 56 * 1024 * 1024

COL_P = 0
COL_Q = 1024
COL_K = 1536
COL_V = 2048
COL_R = 3072
COL_GZ = 4096
COL_KVDN = 7168
COL_MISC = 7424
COL_SW = 7552
COL_QDN = 7680
Z_COLS = 8192
IN_NBLK = 2048
MISC_LRF = MLA_ROPE
MISC_LRB = MLA_ROPE + GLA_GATE_RANK
MQ = MLA_NOPE + LANES


def _cparams(*sem):
    return pltpu.CompilerParams(dimension_semantics=sem, vmem_limit_bytes=VMEM_LIMIT)


def _rms(x, g):
    return x * lax.rsqrt(jnp.mean(x * x, axis=-1, keepdims=True) + NORM_EPS) * g


def _dot(a, b):
    return jnp.dot(a, b, preferred_element_type=F32)


def _dot_nt(a, b):
    return lax.dot_general(a, b, (((1,), (1,)), ((), ())), preferred_element_type=F32)


def _dot_tn(a, b):
    return lax.dot_general(a, b, (((0,), (0,)), ((), ())), preferred_element_type=F32)


def _ada_kernel(c_ref, w_ref, b_ref, o_ref):
    c = c_ref[...]
    a = (c * jax.nn.sigmoid(c)).astype(BF16)
    o_ref[0] = _dot(a, w_ref[0].astype(BF16)) + b_ref[0]


def _ada(cc, ada_w, ada_b):
    L, D, N = ada_w.shape
    rows = cc.shape[0]
    nb = N // D
    return pl.pallas_call(
        _ada_kernel,
        grid=(L, nb),
        in_specs=[pl.BlockSpec((rows, D), lambda l, j: (0, 0)),
                  pl.BlockSpec((1, D, D), lambda l, j: (l, 0, j)),
                  pl.BlockSpec((1, 1, D), lambda l, j: (l, 0, j))],
        out_specs=pl.BlockSpec((1, rows, D), lambda l, j: (l, 0, j)),
        out_shape=jax.ShapeDtypeStruct((L, rows, N), F32),
        compiler_params=_cparams("parallel", "parallel"),
        name="ada_mod",
    )(cc, ada_w, ada_b.reshape(L, 1, N))


def _in_proj_kernel(x_ref, mod_ref, g_ref, w_ref, z_ref):
    D = D_MODEL
    x = x_ref[0]
    mod = mod_ref[0]
    h = _rms(x, g_ref[...]) * (1.0 + mod[:, D:2 * D]) + mod[:, 0:D]
    z_ref[0] = _dot(h.astype(BF16), w_ref[...]).astype(BF16)


def _in_proj(xall, modrows, g, w):
    B, R, D = xall.shape
    nt = R // TM
    nn = Z_COLS // IN_NBLK
    return pl.pallas_call(
        _in_proj_kernel,
        grid=(nn, B, nt),
        in_specs=[pl.BlockSpec((1, TM, D), lambda n, b, t: (b, t, 0)),
                  pl.BlockSpec((1, 1, 6 * D), lambda n, b, t: (2 * b + jnp.minimum(t, 1), 0, 0)),
                  pl.BlockSpec((1, D), lambda n, b, t: (0, 0)),
                  pl.BlockSpec((D, IN_NBLK), lambda n, b, t: (0, n))],
        out_specs=pl.BlockSpec((1, TM, IN_NBLK), lambda n, b, t: (b, t, n)),
        out_shape=jax.ShapeDtypeStruct((B, R, Z_COLS), BF16),
        compiler_params=_cparams("parallel", "parallel", "parallel"),
        name="in_proj",
    )(xall, modrows, g, w)


def _pool_kernel(u_ref, w_ref, sc_ref, o_ref):
    R = u_ref.shape[1]
    row = lax.broadcasted_iota(jnp.int32, (R, POOL_GROUP), 0)
    seg_lo = jnp.where(row < CTX_LEN, 0, CTX_LEN)
    seg_hi = jnp.where(row < CTX_LEN, CTX_LEN, R)
    for gi, win in enumerate(POOL_WINDOWS):
        sl = slice(gi * POOL_GROUP, (gi + 1) * POOL_GROUP)
        u = u_ref[0, :, sl].astype(F32)
        lo_off, hi_off = win // 2, win - win // 2
        acc = jnp.zeros_like(u)
        for d in range(-lo_off, hi_off):
            shifted = u if d == 0 else pltpu.roll(u, (R - d) % R, axis=0)
            ok = (row + d >= seg_lo) & (row + d < seg_hi)
            acc = acc + jnp.where(ok, shifted, 0.0)
        cnt = (jnp.minimum(row + hi_off, seg_hi) - jnp.maximum(row - lo_off, seg_lo)).astype(F32)
        diff = (acc / cnt - u).astype(BF16)
        o_ref[0, :, sl] = (_dot(diff, w_ref[gi]) * sc_ref[:, sl]).astype(BF16)


def _pool(z, pool_w, pool_scale):
    B, R, _ = z.shape
    return pl.pallas_call(
        _pool_kernel,
        grid=(B,),
        in_specs=[pl.BlockSpec((1, R, POOL_WIDTH), lambda b: (b, 0, COL_P // POOL_WIDTH)),
                  pl.BlockSpec(pool_w.shape, lambda b: (0, 0, 0)),
                  pl.BlockSpec((1, POOL_WIDTH), lambda b: (0, 0))],
        out_specs=pl.BlockSpec((1, R, POOL_WIDTH), lambda b: (b, 0, 0)),
        out_shape=jax.ShapeDtypeStruct((B, R, POOL_WIDTH), BF16),
        compiler_params=_cparams("parallel"),
        name="pool",
    )(z, pool_w, pool_scale)


def _split3(x):
    a = x.astype(BF16)
    r = x - a.astype(F32)
    b = r.astype(BF16)
    c = (r - b.astype(F32)).astype(BF16)
    return a, b, c


def _gla_kernel(*refs, fwd):
    if fwd:
        q_ref, k_ref, v_ref, misc_ref, gw_ref, gb_ref, ob_ref, r_ref, gn_ref, o_ref, st_ref = refs
    else:
        q_ref, k_ref, v_ref, misc_ref, gw_ref, gb_ref, o_ref, st_ref = refs
    C = GLA_CHUNK
    nchunk = TM // C
    HK = GLA_HEADS * GLA_DK

    @pl.when(pl.program_id(1) == 0)
    def _():
        st_ref[...] = jnp.zeros_like(st_ref)

    zg = _dot(misc_ref[0], gw_ref[...]) + gb_ref[...]
    g = (jnp.minimum(zg, 0.0) - jnp.log1p(jnp.exp(-jnp.abs(zg)))) * (1.0 / GLA_TAU)

    row = lax.broadcasted_iota(jnp.int32, (TM, TM), 0)
    col = lax.broadcasted_iota(jnp.int32, (TM, TM), 1)
    shift = C.bit_length() - 1
    same = (row >> shift) == (col >> shift)
    tri = same & ((col <= row) if fwd else (col >= row))
    ones = jnp.concatenate([jnp.where(tri, 1.0, 0.0), jnp.where(same, 1.0, 0.0)], axis=0).astype(BF16)
    g1, g2, g3 = _split3(g)
    both = _dot(ones, g1) + _dot(ones, g2) + _dot(ones, g3)
    b = both[:TM]
    bt = both[TM:]

    qe = q_ref[0].astype(F32) * (GLA_DK ** -0.5) * jnp.exp(b)
    ke = (k_ref[0].astype(F32) * jnp.exp(-b)).astype(BF16)
    kd = (k_ref[0].astype(F32) * jnp.exp(bt - b)).astype(BF16)
    qe = qe.astype(BF16)
    dec = jnp.exp(bt)
    v = v_ref[0]

    order = range(nchunk) if fwd else range(nchunk - 1, -1, -1)
    for h in range(GLA_HEADS):
        ks = slice(h * GLA_DK, (h + 1) * GLA_DK)
        vs = slice(h * GLA_DV, (h + 1) * GLA_DV)
        att = jnp.where(tri, _dot_nt(qe[:, ks], ke[:, ks]), 0.0).astype(BF16)
        o_in = _dot(att, v[:, vs])
        st = st_ref[h]
        parts = [None] * nchunk
        for j in order:
            rs = slice(j * C, (j + 1) * C)
            parts[j] = o_in[rs] + _dot_nt(qe[rs, ks], st.astype(BF16))
            st = st * dec[j * C:j * C + 1, ks] + _dot_tn(v[rs, vs], kd[rs, ks])
        st_ref[h] = st
        o = jnp.concatenate(parts, axis=0)
        if fwd:
            o = o + ob_ref[0, :, vs].astype(F32)
            r = r_ref[0, :, vs].astype(F32)
            o = _rms(o, gn_ref[...]) * (r * jax.nn.sigmoid(r))
        o_ref[0, :, vs] = o.astype(BF16)


def _gla(z, gw, gb, *, fwd, ob=None, gn=None):
    B, R, _ = z.shape
    nt = R // TM
    HK, HV = GLA_HEADS * GLA_DK, GLA_HEADS * GLA_DV
    if fwd:
        tile = lambda c: c
    else:
        tile = lambda c: jnp.where(c == 0, 0, nt - c)
    in_specs = [pl.BlockSpec((1, TM, HK), lambda b, c: (b, tile(c), COL_Q // HK)),
                pl.BlockSpec((1, TM, HK), lambda b, c: (b, tile(c), COL_K // HK)),
                pl.BlockSpec((1, TM, HV), lambda b, c: (b, tile(c), COL_V // HV)),
                pl.BlockSpec((1, TM, LANES), lambda b, c: (b, tile(c), COL_MISC // LANES)),
                pl.BlockSpec((LANES, HK), lambda b, c: (0, 0)),
                pl.BlockSpec((1, HK), lambda b, c: (0, 0))]
    args = [z, z, z, z, gw, gb]
    if fwd:
        in_specs += [pl.BlockSpec((1, TM, HV), lambda b, c: (b, tile(c), 0)),
                     pl.BlockSpec((1, TM, HV), lambda b, c: (b, tile(c), COL_R // HV)),
                     pl.BlockSpec((1, GLA_DV), lambda b, c: (0, 0))]
        args += [ob, z, gn]
    return pl.pallas_call(
        functools.partial(_gla_kernel, fwd=fwd),
        grid=(B, nt),
        in_specs=in_specs,
        out_specs=pl.BlockSpec((1, TM, HV), lambda b, c: (b, tile(c), 0)),
        out_shape=jax.ShapeDtypeStruct((B, R, HV), BF16),
        scratch_shapes=[pltpu.VMEM((GLA_HEADS, GLA_DV, GLA_DK), F32)],
        compiler_params=_cparams("parallel", "arbitrary"),
        name="gla_fwd" if fwd else "gla_bwd",
    )(*args)


def _mla_up_kernel(qdn_ref, kvdn_ref, misc_ref, sw_ref, cos_ref, sin_ref, qg_ref, kvg_ref, wq_ref, wkv_ref,
                   q_ref, k_ref, v_ref):
    H = MLA_HEADS
    scale = (MLA_NOPE + MLA_ROPE) ** -0.5 * float(np.log2(np.e))
    cos = cos_ref[...]
    sin = sin_ref[...]
    qn = _rms(qdn_ref[0].astype(F32), qg_ref[...]).astype(BF16)
    qall = _dot(qn, wq_ref[...])
    for h in range(H):
        base = h * MQ
        q_ref[0, :, base:base + MLA_NOPE] = (qall[:, base:base + MLA_NOPE] * scale).astype(BF16)
        rot = (qall[:, base + MLA_NOPE:base + MQ] * cos
               + qall[:, H * MQ + h * LANES:H * MQ + (h + 1) * LANES] * sin)
        q_ref[0, :, base + MLA_NOPE:base + MQ] = (rot * scale).astype(BF16)
    kvn = _rms(kvdn_ref[0].astype(F32), kvg_ref[...]).astype(BF16)
    kvall = _dot(kvn, wkv_ref[...])
    kr = (misc_ref[0].astype(F32) * cos + sw_ref[0].astype(F32) * sin).astype(BF16)
    for h in range(H):
        src = h * (MLA_NOPE + MLA_V)
        k_ref[0, :, h * MQ:h * MQ + MLA_NOPE] = kvall[:, src:src + MLA_NOPE].astype(BF16)
        k_ref[0, :, h * MQ + MLA_NOPE:(h + 1) * MQ] = kr
        v_ref[0, :, h * MLA_V:(h + 1) * MLA_V] = kvall[:, src + MLA_NOPE:src + MLA_NOPE + MLA_V].astype(BF16)


def _mla_up(z, cos, sin, qg, kvg, wq, wkv):
    B, R, _ = z.shape
    nt = R // TM
    H = MLA_HEADS
    const = lambda b, t: (0, 0)
    return pl.pallas_call(
        _mla_up_kernel,
        grid=(B, nt),
        in_specs=[pl.BlockSpec((1, TM, MLA_Q_RANK), lambda b, t: (b, t, COL_QDN // MLA_Q_RANK)),
                  pl.BlockSpec((1, TM, MLA_KV_RANK), lambda b, t: (b, t, COL_KVDN // MLA_KV_RANK)),
                  pl.BlockSpec((1, TM, LANES), lambda b, t: (b, t, COL_MISC // LANES)),
                  pl.BlockSpec((1, TM, LANES), lambda b, t: (b, t, COL_SW // LANES)),
                  pl.BlockSpec((TM, LANES), lambda b, t: (t, 0)),
                  pl.BlockSpec((TM, LANES), lambda b, t: (t, 0)),
                  pl.BlockSpec(qg.shape, const),
                  pl.BlockSpec(kvg.shape, const),
                  pl.BlockSpec(wq.shape, const),
                  pl.BlockSpec(wkv.shape, const)],
        out_specs=[pl.BlockSpec((1, TM, H * MQ), lambda b, t: (b, t, 0)),
                   pl.BlockSpec((1, TM, H * MQ), lambda b, t: (b, t, 0)),
                   pl.BlockSpec((1, TM, H * MLA_V), lambda b, t: (b, t, 0))],
        out_shape=[jax.ShapeDtypeStruct((B, R, H * MQ), BF16),
                   jax.ShapeDtypeStruct((B, R, H * MQ), BF16),
                   jax.ShapeDtypeStruct((B, R, H * MLA_V), BF16)],
        compiler_params=_cparams("parallel", "parallel"),
        name="mla_up",
    )(z, z, z, z, cos, sin, qg, kvg, wq, wkv)


def _attn_kernel(q_ref, k_ref, v_ref, o_ref, *, q_off):
    R = k_ref.shape[1]

    def attend(nk):
        s = _dot_nt(q_ref[0], k_ref[0, :nk])
        p = jnp.exp2(s - jnp.max(s, axis=-1, keepdims=True))
        l = jnp.sum(p, axis=-1, keepdims=True)
        o_ref[0] = (_dot(p.astype(BF16), v_ref[0, :nk]) / l).astype(BF16)

    if q_off == 0:
        qi = pl.program_id(2)

        @pl.when(qi == 0)
        def _():
            attend(CTX_LEN)

        @pl.when(qi > 0)
        def _():
            attend(R)
    else:
        attend(R)


def _attn(q, k, v, *, with_ctx):
    B, R, _ = q.shape
    q_off = 0 if with_ctx else CTX_LEN // TM
    nq = R // TM - q_off
    return pl.pallas_call(
        functools.partial(_attn_kernel, q_off=q_off),
        grid=(B, MLA_HEADS, nq),
        in_specs=[pl.BlockSpec((1, TM, MQ), lambda b, h, i: (b, i + q_off, h)),
                  pl.BlockSpec((1, R, MQ), lambda b, h, i: (b, 0, h)),
                  pl.BlockSpec((1, R, MLA_V), lambda b, h, i: (b, 0, h))],
        out_specs=pl.BlockSpec((1, TM, MLA_V), lambda b, h, i: (b, i + q_off, h)),
        out_shape=jax.ShapeDtypeStruct((B, R, MLA_HEADS * MLA_V), BF16),
        compiler_params=_cparams("parallel", "parallel", "arbitrary"),
        name="mla_attn",
    )(q, k, v)


def _merge_kernel(po_ref, go_ref, mo_ref, gz0_ref, gz1_ref, gz2_ref, x_ref, mod_ref, bw_ref, wo_ref, g_ref,
                  x1_ref, h2t_ref):
    D = D_MODEL
    m = None
    for i, (o_ref, gz_ref) in enumerate(((po_ref, gz0_ref), (go_ref, gz1_ref), (mo_ref, gz2_ref))):
        t = jax.nn.sigmoid(gz_ref[0].astype(F32)) * _dot(o_ref[0], bw_ref[i])
        m = t if m is None else m + t
    y = _dot(m.astype(BF16), wo_ref[...])
    mod = mod_ref[0]
    x1 = x_ref[0] + mod[:, 2 * D:3 * D] * y
    x1_ref[0] = x1
    h2 = _rms(x1, g_ref[...]) * (1.0 + mod[:, 4 * D:5 * D]) + mod[:, 3 * D:4 * D]
    h2t_ref[...] = pltpu.bitcast(h2.T.astype(BF16), jnp.int32)


def _merge(pool_o, gla_o, mla_o, z, xall, modrows, bw, wo, g, *, with_ctx):
    B, R, D = xall.shape
    t_off = 0 if with_ctx else CTX_LEN // TM
    nt = R // TM - t_off
    tok = lambda b, t: (b, t + t_off, 0)
    gzb = COL_GZ // D
    wpr = _words_per_row()
    return pl.pallas_call(
        _merge_kernel,
        grid=(B, nt),
        in_specs=[pl.BlockSpec((1, TM, D), tok),
                  pl.BlockSpec((1, TM, D), tok),
                  pl.BlockSpec((1, TM, D), tok),
                  pl.BlockSpec((1, TM, D), lambda b, t: (b, t + t_off, gzb)),
                  pl.BlockSpec((1, TM, D), lambda b, t: (b, t + t_off, gzb + 1)),
                  pl.BlockSpec((1, TM, D), lambda b, t: (b, t + t_off, gzb + 2)),
                  pl.BlockSpec((1, TM, D), tok),
                  pl.BlockSpec((1, 1, 6 * D), lambda b, t: (2 * b + jnp.minimum(t + t_off, 1), 0, 0)),
                  pl.BlockSpec(bw.shape, lambda b, t: (0, 0, 0)),
                  pl.BlockSpec(wo.shape, lambda b, t: (0, 0)),
                  pl.BlockSpec((1, D), lambda b, t: (0, 0))],
        out_specs=[pl.BlockSpec((1, TM, D), lambda b, t: (b, t, 0)),
                   pl.BlockSpec((D // wpr, TM), lambda b, t: (0, b * nt + t))],
        out_shape=[jax.ShapeDtypeStruct((B, nt * TM, D), F32),
                   jax.ShapeDtypeStruct((D // wpr, B * nt * TM), jnp.int32)],
        compiler_params=_cparams("parallel", "parallel"),
        name="merge",
    )(pool_o, gla_o, mla_o, z, z, z, xall, modrows, bw, wo, g)


def _top16(s):
    nk, n = s.shape
    iota = lax.broadcasted_iota(jnp.int32, (nk, n), 0).astype(F32)
    slot = lax.broadcasted_iota(jnp.int32, (PEER_TOPK, n), 0)
    rank = jnp.full((nk, n), float(PEER_TOPK), F32)
    vals = jnp.zeros((PEER_TOPK, n), F32)
    for j in range(PEER_TOPK):
        m = jnp.max(s, axis=0, keepdims=True)
        idx = jnp.min(jnp.where(s == m, iota, float(nk)), axis=0, keepdims=True)
        hit = iota == idx
        rank = jnp.where(hit, float(j), rank)
        s = jnp.where(hit, -jnp.inf, s)
        vals = jnp.where(slot == j, m, vals)
    return vals, rank


PAIR_SHORT = PEER_TOPK // 2
PAIR_ROWS = PEER_TOPK + (PEER_TOPK - 1) * PAIR_SHORT
PAIR_VALID = sum(PEER_TOPK // (i + 1) for i in range(PEER_TOPK))


def _pair_rows(t2):
    return jnp.concatenate([t2] + [t2[0:PAIR_SHORT]] * (PEER_TOPK - 1), axis=0)


def _pair_candidates(t1, t2):
    K = PEER_TOPK
    n = t1.shape[1]
    first = jnp.concatenate([jnp.broadcast_to(t1[0:1], (K, n))]
                            + [jnp.broadcast_to(t1[i:i + 1], (PAIR_SHORT, n)) for i in range(1, K)], axis=0)
    r = lax.broadcasted_iota(jnp.int32, (PAIR_ROWS, n), 0)
    sh = PAIR_SHORT.bit_length() - 1
    i_of = jnp.where(r < K, 0, ((r - K) >> sh) + 1)
    j_of = jnp.where(r < K, r, (r - K) & (PAIR_SHORT - 1))
    cand = jnp.where((i_of + 1) * (j_of + 1) <= K, first + _pair_rows(t2), -jnp.inf)
    return cand, (i_of * K + j_of).astype(F32)


def _top16_values(s, with_rank):
    n = s.shape[1]
    slot = lax.broadcasted_iota(jnp.int32, (PEER_TOPK, n), 0)
    vals = jnp.zeros((PEER_TOPK, n), F32)
    rank = jnp.full(s.shape, float(PEER_TOPK), F32) if with_rank else None
    for j in range(PEER_TOPK):
        m = jnp.max(s, axis=0, keepdims=True)
        hit = s == m
        if with_rank:
            rank = jnp.where(hit, float(j), rank)
        s = jnp.where(hit, -jnp.inf, s)
        vals = jnp.where(slot == j, m, vals)
    used = jnp.sum(jnp.where(s == -jnp.inf, 1.0, 0.0), axis=0, keepdims=True)
    return vals, rank, jnp.where(used == float(PEER_TOPK), 1.0, 0.0)


def _pair_counts(t1, t2):
    K = PEER_TOPK
    n = t1.shape[1]
    cand0, _ = _pair_candidates(t1, t2)
    cand = cand0
    zsum = jnp.zeros((1, n), F32)
    best = m = None
    for j in range(K):
        m = jnp.max(cand, axis=0, keepdims=True)
        cand = jnp.where(cand == m, -jnp.inf, cand)
        if j == 0:
            best = m
        zsum = zsum + jnp.exp(m - best)
    used = jnp.sum(jnp.where(cand == -jnp.inf, 1.0, 0.0), axis=0, keepdims=True)
    exact = jnp.where(used == float(PAIR_ROWS - PAIR_VALID + K), 1.0, 0.0)
    picked = jnp.where(cand0 >= m, 1.0, 0.0)
    slot = lax.broadcasted_iota(jnp.int32, (K, n), 0)
    cnt = jnp.zeros((K, n), F32)
    for i in range(K):
        blk = picked[0:K] if i == 0 else picked[K + (i - 1) * PAIR_SHORT:K + i * PAIR_SHORT]
        cnt = jnp.where(slot == i, jnp.sum(blk, axis=0, keepdims=True), cnt)
    return cnt, zsum, exact


def _pair_top16(t1, t2):
    K = PEER_TOPK
    n = t1.shape[1]
    cand, flat = _pair_candidates(t1, t2)
    irow = lax.broadcasted_iota(jnp.int32, (K, n), 0).astype(F32)
    cnt = jnp.zeros((K, n), F32)
    zsum = jnp.zeros((1, n), F32)
    best = None
    for j in range(K):
        m = jnp.max(cand, axis=0, keepdims=True)
        idx = jnp.min(jnp.where(cand == m, flat, float(K * K)), axis=0, keepdims=True)
        cand = jnp.where(flat == idx, -jnp.inf, cand)
        cnt = cnt + jnp.where(irow == jnp.floor(idx * (1.0 / K)), 1.0, 0.0)
        if j == 0:
            best = m
        zsum = zsum + jnp.exp(m - best)
    return cnt, zsum


CNT_BITS = 0xFF


def _words_per_row():
    return 4 // jnp.dtype(BF16).itemsize


def _peer_score_kernel(h2t_ref, wqt_ref, keys_ref, c2_ref, p2_ref, ak_ref, qt_ref):
    K = PEER_TOPK
    qt_ref[...] = _dot(wqt_ref[...], pltpu.bitcast(h2t_ref[...], BF16)).astype(BF16)

    def store(h, cs, s1, s2, t1, rank1, t2, rank2, cnt, zsum):
        rowcnt = jnp.zeros_like(s1)
        for j in range(K):
            in_row = (s1 == t1[j:j + 1]) if rank1 is None else (rank1 == float(j))
            rowcnt = jnp.where(in_row, cnt[j:j + 1], rowcnt)
        c2_ref[h, :, cs] = pltpu.bitcast(rank2.astype(BF16), jnp.int32)
        p2_ref[h, :, cs] = pltpu.bitcast(jnp.exp(s2 - t2[0:1]).astype(BF16), jnp.int32)
        ka = (jnp.exp(s1 - t1[0:1]) / zsum).astype(BF16).astype(F32)
        ak_ref[h, :, cs] = (lax.bitcast_convert_type(ka, jnp.int32) & ~CNT_BITS) | rowcnt.astype(jnp.int32)

    def head(h, carry):
        base = pl.multiple_of(h * PEER_QDIM, PEER_QDIM)
        s1_all = _dot(keys_ref[h], qt_ref[pl.ds(base, PEER_HALF), :])
        s2_all = _dot(keys_ref[PEER_HEADS + h], qt_ref[pl.ds(base + PEER_HALF, PEER_HALF), :])
        chunks = []
        for c in range(s1_all.shape[1] // LANES):
            cs = slice(c * LANES, (c + 1) * LANES)
            s1, s2 = s1_all[:, cs], s2_all[:, cs]
            t1, _, ok1 = _top16_values(s1, with_rank=False)
            t2, rank2, ok2 = _top16_values(s2, with_rank=True)
            cnt, zsum, ok3 = _pair_counts(t1, t2)
            store(h, cs, s1, s2, t1, None, t2, rank2, cnt, zsum)
            chunks.append((cs, s1, s2, jnp.min(ok1 * ok2 * ok3)))

        for cs, s1, s2, exact in chunks:
            @pl.when(exact < 0.5)
            def _(cs=cs, s1=s1, s2=s2):
                t1x, rank1x = _top16(s1)
                t2x, rank2x = _top16(s2)
                cntx, zx = _pair_top16(t1x, t2x)
                store(h, cs, s1, s2, t1x, rank1x, t2x, rank2x, cntx, zx)

        return carry

    lax.fori_loop(0, PEER_HEADS, head, 0)


PEER_SCORE_TN = 2 * LANES


def _peer_score(h2t, wqt, keys):
    ntok = h2t.shape[1]
    tn = PEER_SCORE_TN
    nw = PEER_N_KEYS // _words_per_row()
    tbl = lambda rows: jax.ShapeDtypeStruct((PEER_HEADS, rows, ntok), jnp.int32)
    ospec = lambda rows: pl.BlockSpec((PEER_HEADS, rows, tn), lambda i: (0, 0, i))
    return pl.pallas_call(
        _peer_score_kernel,
        grid=(ntok // tn,),
        in_specs=[pl.BlockSpec((h2t.shape[0], tn), lambda i: (0, i)),
                  pl.BlockSpec(wqt.shape, lambda i: (0, 0)),
                  pl.BlockSpec(keys.shape, lambda i: (0, 0, 0))],
        out_specs=[ospec(nw), ospec(nw), ospec(PEER_N_KEYS)],
        out_shape=[tbl(nw), tbl(nw), tbl(PEER_N_KEYS)],
        scratch_shapes=[pltpu.VMEM((PEER_HEADS * PEER_QDIM, tn), BF16)],
        compiler_params=_cparams("parallel"),
        name="peer_score",
    )(h2t, wqt, keys)


PEER_KEYS_PER_STEP = 4
PEER_ETILE = PEER_KEYS_PER_STEP * PEER_N_KEYS
PEER_KEY_GROUP = 2


def _peer_dense_kernel(h2t_ref, c2_ref, p2_ref, ak_ref, u_ref, vt_ref, x1_ref, modc_ref, modl_ref, fg_ref,
                       o_ref, acc_ref, pre_ref, hid_ref, *, ctx_rows, rows_per_sample, final):
    D = D_MODEL
    NK = PEER_N_KEYS
    tn = h2t_ref.shape[1]
    e = pl.program_id(1)

    @pl.when(e == 0)
    def _():
        acc_ref[...] = jnp.zeros_like(acc_ref)

    pre_ref[...] = _dot(pltpu.bitcast(u_ref[...], BF16), pltpu.bitcast(h2t_ref[...], BF16))

    words = [[ak_ref[h, pl.ds(e * PEER_KEYS_PER_STEP + i, 1), :] for h in range(PEER_HEADS)]
             for i in range(PEER_KEYS_PER_STEP)]

    def key_rows(i, h, cs):
        wd = words[i][h][:, cs]
        rep = lambda row: jnp.broadcast_to(row, (BF16_ROWS, LANES)).astype(BF16)
        return rep((wd & CNT_BITS).astype(F32)), rep(lax.bitcast_convert_type(wd & ~CNT_BITS, F32))

    for c in range(tn // LANES):
        cs = slice(c * LANES, (c + 1) * LANES)
        for g in range(0, PEER_KEYS_PER_STEP, PEER_KEY_GROUP):
            group = range(g, g + PEER_KEY_GROUP)
            w = {i: None for i in group}
            for h in range(PEER_HEADS):
                c2 = pltpu.bitcast(c2_ref[h, :, cs], BF16).reshape(NK // BF16_ROWS, BF16_ROWS, LANES)
                p2 = pltpu.bitcast(p2_ref[h, :, cs], BF16).reshape(NK // BF16_ROWS, BF16_ROWS, LANES)
                for i in group:
                    cnt, kaw = key_rows(i, h, cs)
                    t = jnp.where(c2 < cnt[None], p2, 0.0) * kaw[None]
                    w[i] = t if w[i] is None else w[i] + t
            for i in group:
                rows = slice(i * NK, (i + 1) * NK)
                x = pre_ref[rows, cs].astype(BF16)
                act = 0.5 * x * (1.0 + lax.erf(x * (2.0 ** -0.5)))
                hid_ref[rows, cs] = act * w[i].reshape(NK, LANES)

    acc_ref[...] += _dot(pltpu.bitcast(vt_ref[...], BF16), hid_ref[...])

    @pl.when(e == pl.num_programs(1) - 1)
    def _():
        y = acc_ref[...].T
        g_lat = modl_ref[0][:, 5 * D:6 * D]
        if ctx_rows:
            g_ctx = modc_ref[0][:, 5 * D:6 * D]
            tiles = rows_per_sample // tn
            r0 = (pl.program_id(0) % tiles) * tn
            row = r0 + lax.broadcasted_iota(jnp.int32, (tn, D), 0)
            gate = jnp.where(row < ctx_rows, g_ctx, g_lat)
        else:
            gate = g_lat
        x2 = x1_ref[...] + gate * y
        if final:
            x2 = _rms(x2, fg_ref[...])
        o_ref[...] = x2


def _peer_dense(h2t, sel, u, vt, x1, modrows, fg, *, tn, rows_per_sample, ctx_rows, final):
    ntok, D = x1.shape
    wpr = _words_per_row()
    ne = u.shape[0] * wpr // PEER_ETILE
    tiles = rows_per_sample // tn
    tspec = lambda t: pl.BlockSpec((PEER_HEADS, t.shape[1], tn), lambda i, s: (0, 0, i))
    return pl.pallas_call(
        functools.partial(_peer_dense_kernel, ctx_rows=ctx_rows, rows_per_sample=rows_per_sample, final=final),
        grid=(ntok // tn, ne),
        in_specs=[pl.BlockSpec((D // wpr, tn), lambda i, s: (0, i)),
                  tspec(sel[0]), tspec(sel[1]), tspec(sel[2]),
                  pl.BlockSpec((PEER_ETILE // wpr, D), lambda i, s: (s, 0)),
                  pl.BlockSpec((D // wpr, PEER_ETILE), lambda i, s: (0, s)),
                  pl.BlockSpec((tn, D), lambda i, s: (i, 0), pipeline_mode=pl.Buffered(1)),
                  pl.BlockSpec((1, 1, 6 * D), lambda i, s: (2 * (i // tiles), 0, 0)),
                  pl.BlockSpec((1, 1, 6 * D), lambda i, s: (2 * (i // tiles) + 1, 0, 0)),
                  pl.BlockSpec((1, D), lambda i, s: (0, 0))],
        out_specs=pl.BlockSpec((tn, D), lambda i, s: (i, 0)),
        out_shape=jax.ShapeDtypeStruct((ntok, D), F32),
        scratch_shapes=[pltpu.VMEM((D, tn), F32),
                        pltpu.VMEM((PEER_ETILE, tn), F32),
                        pltpu.VMEM((PEER_ETILE, tn), BF16)],
        compiler_params=_cparams("parallel", "arbitrary"),
        name="peer_dense",
    )(h2t, *sel, u, vt, x1, modrows, modrows, fg)


def _pack_rows_kernel(x_ref, o_ref):
    o_ref[...] = pltpu.bitcast(x_ref[...].astype(BF16), jnp.int32)


def _pack_rows(x):
    R, C = x.shape
    wpr = _words_per_row()
    br, bc = 512, 1024
    return pl.pallas_call(
        _pack_rows_kernel,
        grid=(R // br, C // bc),
        in_specs=[pl.BlockSpec((br, bc), lambda i, j: (i, j))],
        out_specs=pl.BlockSpec((br // wpr, bc), lambda i, j: (i, j)),
        out_shape=jax.ShapeDtypeStruct((R // wpr, C), jnp.int32),
        compiler_params=_cparams("parallel", "parallel"),
        name="pack_rows",
    )(x)


def _peer_tile(rows_per_sample):
    for tn in (1024, 768, 512, 256):
        if rows_per_sample % tn == 0:
            return tn
    raise ValueError(rows_per_sample)


_SWAP = np.concatenate([np.arange(16, 32), np.arange(0, 16), np.arange(48, 64), np.arange(32, 48)])


def _layout_w_in(w):
    D = w.shape[0]
    idx = np.cumsum(IN_SPLITS)[:-1]
    p, q, k, v, r, lrf, lrb, qdn, kvdn, krope, gz = jnp.split(w, idx, axis=1)
    zeros = lambda n: jnp.zeros((D, n), w.dtype)
    misc = jnp.concatenate([krope, lrf, lrb, zeros(LANES - MLA_ROPE - 2 * GLA_GATE_RANK)], axis=1)
    sw = jnp.concatenate([krope[:, _SWAP], zeros(LANES - MLA_ROPE)], axis=1)
    out = jnp.concatenate([p, q, k, v, r, gz, kvdn, misc, sw, qdn], axis=1)
    return jnp.concatenate([out, zeros(Z_COLS - out.shape[1])], axis=1).astype(BF16)


def _layout_gate_w(gate_w, row0):
    HK = GLA_HEADS * GLA_DK
    pad = jnp.zeros((LANES, HK), gate_w.dtype)
    return pad.at[row0:row0 + GLA_GATE_RANK].set(gate_w).astype(BF16)


def _layout_w_uq(w):
    H = MLA_HEADS
    w3 = w.reshape(MLA_Q_RANK, H, MLA_NOPE + MLA_ROPE)
    qn, qr = w3[..., :MLA_NOPE], w3[..., MLA_NOPE:]
    z = jnp.zeros((MLA_Q_RANK, H, LANES - MLA_ROPE), w.dtype)
    main = jnp.concatenate([qn, qr, z], axis=-1).reshape(MLA_Q_RANK, H * MQ)
    swp = jnp.concatenate([qr[..., _SWAP], z], axis=-1).reshape(MLA_Q_RANK, H * LANES)
    return jnp.concatenate([main, swp], axis=1).astype(BF16)


def _rope_tables(seq):
    half = MLA_ROPE // 2
    t = jnp.arange(seq)
    inv = ROPE_BASE ** (-jnp.arange(0, half, 2, dtype=F32) / half)
    ar = (t // GRID_W).astype(F32)[:, None] * inv
    ac = (t % GRID_W).astype(F32)[:, None] * inv
    cos = jnp.concatenate([jnp.cos(ar), jnp.cos(ar), jnp.cos(ac), jnp.cos(ac)], axis=1)
    sin = jnp.concatenate([-jnp.sin(ar), jnp.sin(ar), -jnp.sin(ac), jnp.sin(ac)], axis=1)
    cos = jnp.concatenate([jnp.ones((CTX_LEN, MLA_ROPE), F32), cos], axis=0)
    sin = jnp.concatenate([jnp.zeros((CTX_LEN, MLA_ROPE), F32), sin], axis=0)
    pad = jnp.zeros((CTX_LEN + seq, LANES - MLA_ROPE), F32)
    return jnp.concatenate([cos, pad], axis=1), jnp.concatenate([sin, pad], axis=1)


def kernel(x, c, ctx, c_ctx, ada_w, ada_b, norm1_g, norm2_g, w_in, pool_w, pool_scale, gla_gate_w, gla_gate_b, gla_norm_g, mla_q_norm_g, mla_kv_norm_g, mla_w_uq, mla_w_ukv, branch_w, w_out, peer_wq, peer_keys, peer_u, peer_v, final_norm_g):
    B, T, D = x.shape
    R = CTX_LEN + T
    assert D == D_MODEL and ctx.shape[1] == CTX_LEN == TM and T % TM == 0 and T % GRID_W == 0

    nrow = -(-(B + 1) // 8) * 8
    cc = jnp.concatenate([c, c_ctx[None], jnp.zeros((nrow - B - 1, D), F32)], axis=0)
    mod = _ada(cc, ada_w, ada_b)
    modrows = jnp.stack([jnp.broadcast_to(mod[:, B:B + 1], (DEPTH, B, 6 * D)), mod[:, :B]], axis=2)
    modrows = modrows.reshape(DEPTH, 2 * B, 1, 6 * D)

    cos, sin = _rope_tables(T)
    xall = jnp.concatenate([ctx, x], axis=1)
    row = lambda a: a.reshape(1, -1)

    for i in range(DEPTH):
        with_ctx = i < DEPTH - 1
        final = i == DEPTH - 1
        z = _in_proj(xall, modrows[i], row(norm1_g[i]), _layout_w_in(w_in[i]))

        pool_o = _pool(z, pool_w[i].astype(BF16), row(pool_scale[i]))

        o_b = _gla(z, _layout_gate_w(gla_gate_w[i, 1], MISC_LRB), row(gla_gate_b[i, 1]), fwd=False)
        gla_o = _gla(z, _layout_gate_w(gla_gate_w[i, 0], MISC_LRF), row(gla_gate_b[i, 0]), fwd=True,
                     ob=o_b, gn=row(gla_norm_g[i]))

        q, k, v = _mla_up(z, cos, sin, row(mla_q_norm_g[i]), row(mla_kv_norm_g[i]),
                          _layout_w_uq(mla_w_uq[i]), mla_w_ukv[i].astype(BF16))
        mla_o = _attn(q, k, v, with_ctx=with_ctx)

        x1, h2t = _merge(pool_o, gla_o, mla_o, z, xall, modrows[i], branch_w[i].astype(BF16),
                        w_out[i].astype(BF16), row(norm2_g[i]), with_ctx=with_ctx)

        rows = x1.shape[1]
        keys = peer_keys[i].reshape(2 * PEER_HEADS, PEER_N_KEYS, PEER_HALF).astype(BF16)
        sel = _peer_score(h2t, peer_wq[i].T.astype(BF16), keys)
        xall = _peer_dense(h2t, sel, _pack_rows(peer_u[i]), _pack_rows(peer_v[i].T), x1.reshape(B * rows, D),
                           modrows[i], row(final_norm_g), tn=_peer_tile(rows), rows_per_sample=rows,
                           ctx_rows=CTX_LEN if with_ctx else 0, final=final).reshape(B, rows, D)
    return xall
```

```python
import functools

import jax
import jax.numpy as jnp
import numpy as np
from jax import lax
from jax.experimental import pallas as pl
from jax.experimental.pallas import tpu as pltpu

F32 = jnp.float32
BF16 = jnp.bfloat16

D_MODEL = 1024
DEPTH = 2
CTX_LEN = 256
GRID_W = 64
NORM_EPS = 1e-6
POOL_WIDTH = 1024
POOL_WINDOWS = (2, 4, 8, 16)
POOL_GROUP = POOL_WIDTH // len(POOL_WINDOWS)
GLA_HEADS = 4
GLA_DK = 128
GLA_DV = 256
GLA_GATE_RANK = 16
GLA_TAU = 16.0
GLA_CHUNK = 64
MLA_HEADS = 8
MLA_Q_RANK = 384
MLA_KV_RANK = 256
MLA_NOPE = 128
MLA_ROPE = 64
MLA_V = 128
ROPE_BASE = 10000.0
N_BRANCH = 3
PEER_HEADS = 8
PEER_N_KEYS = 128
PEER_TOPK = 16
PEER_QDIM = 256
PEER_HALF = PEER_QDIM // 2
IN_SPLITS = (POOL_WIDTH, GLA_HEADS * GLA_DK, GLA_HEADS * GLA_DK, GLA_HEADS * GLA_DV, GLA_HEADS * GLA_DV,
             GLA_GATE_RANK, GLA_GATE_RANK, MLA_Q_RANK, MLA_KV_RANK, MLA_ROPE, N_BRANCH * D_MODEL)

LANES = 128
BF16_ROWS = 16
TM = 256
VMEM_LIMIT = 56 * 1024 * 1024

COL_P = 0
COL_Q = 1024
COL_K = 1536
COL_V = 2048
COL_R = 3072
COL_GZ = 4096
COL_KVDN = 7168
COL_MISC = 7424
COL_SW = 7552
COL_QDN = 7680
Z_COLS = 8192
IN_NBLK = 4096
MISC_LRF = MLA_ROPE
MISC_LRB = MLA_ROPE + GLA_GATE_RANK
MQ = MLA_NOPE + LANES


def _cparams(*sem):
    return pltpu.CompilerParams(dimension_semantics=sem, vmem_limit_bytes=VMEM_LIMIT)


def _rms(x, g):
    return x * lax.rsqrt(jnp.mean(x * x, axis=-1, keepdims=True) + NORM_EPS) * g


def _dot(a, b):
    return jnp.dot(a, b, preferred_element_type=F32)


def _dot_nt(a, b):
    return lax.dot_general(a, b, (((1,), (1,)), ((), ())), preferred_element_type=F32)


def _dot_tn(a, b):
    return lax.dot_general(a, b, (((0,), (0,)), ((), ())), preferred_element_type=F32)


def _ada_kernel(c_ref, w_ref, b_ref, o_ref):
    c = c_ref[...]
    a = (c * jax.nn.sigmoid(c)).astype(BF16)
    o_ref[0] = _dot(a, w_ref[0].astype(BF16)) + b_ref[0]


def _ada(cc, ada_w, ada_b):
    L, D, N = ada_w.shape
    rows = cc.shape[0]
    nb = N // D
    return pl.pallas_call(
        _ada_kernel,
        grid=(L, nb),
        in_specs=[pl.BlockSpec((rows, D), lambda l, j: (0, 0)),
                  pl.BlockSpec((1, D, D), lambda l, j: (l, 0, j)),
                  pl.BlockSpec((1, 1, D), lambda l, j: (l, 0, j))],
        out_specs=pl.BlockSpec((1, rows, D), lambda l, j: (l, 0, j)),
        out_shape=jax.ShapeDtypeStruct((L, rows, N), F32),
        compiler_params=_cparams("parallel", "parallel"),
        name="ada_mod",
    )(cc, ada_w, ada_b.reshape(L, 1, N))


def _in_proj_kernel(x_ref, mod_ref, g_ref, w_ref, z_ref):
    D = D_MODEL
    x = x_ref[0]
    mod = mod_ref[0]
    h = _rms(x, g_ref[...]) * (1.0 + mod[:, D:2 * D]) + mod[:, 0:D]
    z_ref[0] = _dot(h.astype(BF16), w_ref[...]).astype(BF16)


def _in_proj(xall, modrows, g, w):
    B, R, D = xall.shape
    nt = R // TM
    nn = Z_COLS // IN_NBLK
    return pl.pallas_call(
        _in_proj_kernel,
        grid=(nn, B, nt),
        in_specs=[pl.BlockSpec((1, TM, D), lambda n, b, t: (b, t, 0)),
                  pl.BlockSpec((1, 1, 6 * D), lambda n, b, t: (2 * b + jnp.minimum(t, 1), 0, 0)),
                  pl.BlockSpec((1, D), lambda n, b, t: (0, 0)),
                  pl.BlockSpec((D, IN_NBLK), lambda n, b, t: (0, n))],
        out_specs=pl.BlockSpec((1, TM, IN_NBLK), lambda n, b, t: (b, t, n)),
        out_shape=jax.ShapeDtypeStruct((B, R, Z_COLS), BF16),
        compiler_params=_cparams("parallel", "parallel", "parallel"),
        name="in_proj",
    )(xall, modrows, g, w)


def _pool_kernel(u_ref, w_ref, sc_ref, o_ref):
    R = u_ref.shape[1]
    row = lax.broadcasted_iota(jnp.int32, (R, POOL_GROUP), 0)
    seg_lo = jnp.where(row < CTX_LEN, 0, CTX_LEN)
    seg_hi = jnp.where(row < CTX_LEN, CTX_LEN, R)
    for gi, win in enumerate(POOL_WINDOWS):
        sl = slice(gi * POOL_GROUP, (gi + 1) * POOL_GROUP)
        u = u_ref[0, :, sl].astype(F32)
        lo_off, hi_off = win // 2, win - win // 2
        acc = jnp.zeros_like(u)
        for d in range(-lo_off, hi_off):
            shifted = u if d == 0 else pltpu.roll(u, (R - d) % R, axis=0)
            ok = (row + d >= seg_lo) & (row + d < seg_hi)
            acc = acc + jnp.where(ok, shifted, 0.0)
        cnt = (jnp.minimum(row + hi_off, seg_hi) - jnp.maximum(row - lo_off, seg_lo)).astype(F32)
        diff = (acc / cnt - u).astype(BF16)
        o_ref[0, :, sl] = (_dot(diff, w_ref[gi]) * sc_ref[:, sl]).astype(BF16)


def _pool(z, pool_w, pool_scale):
    B, R, _ = z.shape
    return pl.pallas_call(
        _pool_kernel,
        grid=(B,),
        in_specs=[pl.BlockSpec((1, R, POOL_WIDTH), lambda b: (b, 0, COL_P // POOL_WIDTH)),
                  pl.BlockSpec(pool_w.shape, lambda b: (0, 0, 0)),
                  pl.BlockSpec((1, POOL_WIDTH), lambda b: (0, 0))],
        out_specs=pl.BlockSpec((1, R, POOL_WIDTH), lambda b: (b, 0, 0)),
        out_shape=jax.ShapeDtypeStruct((B, R, POOL_WIDTH), BF16),
        compiler_params=_cparams("parallel"),
        name="pool",
    )(z, pool_w, pool_scale)


def _split3(x):
    a = x.astype(BF16)
    r = x - a.astype(F32)
    b = r.astype(BF16)
    c = (r - b.astype(F32)).astype(BF16)
    return a, b, c


def _gla_kernel(*refs, fwd):
    if fwd:
        q_ref, k_ref, v_ref, misc_ref, gw_ref, gb_ref, ob_ref, r_ref, gn_ref, o_ref, st_ref = refs
    else:
        q_ref, k_ref, v_ref, misc_ref, gw_ref, gb_ref, o_ref, st_ref = refs
    C = GLA_CHUNK
    nchunk = TM // C
    HK = GLA_HEADS * GLA_DK

    @pl.when(pl.program_id(1) == 0)
    def _():
        st_ref[...] = jnp.zeros_like(st_ref)

    zg = _dot(misc_ref[0], gw_ref[...]) + gb_ref[...]
    g = (jnp.minimum(zg, 0.0) - jnp.log1p(jnp.exp(-jnp.abs(zg)))) * (1.0 / GLA_TAU)

    row = lax.broadcasted_iota(jnp.int32, (TM, TM), 0)
    col = lax.broadcasted_iota(jnp.int32, (TM, TM), 1)
    shift = C.bit_length() - 1
    same = (row >> shift) == (col >> shift)
    tri = same & ((col <= row) if fwd else (col >= row))
    ones = jnp.concatenate([jnp.where(tri, 1.0, 0.0), jnp.where(same, 1.0, 0.0)], axis=0).astype(BF16)
    g1, g2, g3 = _split3(g)
    both = _dot(ones, g1) + _dot(ones, g2) + _dot(ones, g3)
    b = both[:TM]
    bt = both[TM:]

    qe = q_ref[0].astype(F32) * (GLA_DK ** -0.5) * jnp.exp(b)
    ke = (k_ref[0].astype(F32) * jnp.exp(-b)).astype(BF16)
    kd = (k_ref[0].astype(F32) * jnp.exp(bt - b)).astype(BF16)
    qe = qe.astype(BF16)
    dec = jnp.exp(bt)
    v = v_ref[0]

    order = range(nchunk) if fwd else range(nchunk - 1, -1, -1)
    for h in range(GLA_HEADS):
        ks = slice(h * GLA_DK, (h + 1) * GLA_DK)
        vs = slice(h * GLA_DV, (h + 1) * GLA_DV)
        att = jnp.where(tri, _dot_nt(qe[:, ks], ke[:, ks]), 0.0).astype(BF16)
        o_in = _dot(att, v[:, vs])
        st = st_ref[h]
        parts = [None] * nchunk
        for j in order:
            rs = slice(j * C, (j + 1) * C)
            parts[j] = o_in[rs] + _dot_nt(qe[rs, ks], st.astype(BF16))
            st = st * dec[j * C:j * C + 1, ks] + _dot_tn(v[rs, vs], kd[rs, ks])
        st_ref[h] = st
        o = jnp.concatenate(parts, axis=0)
        if fwd:
            o = o + ob_ref[0, :, vs].astype(F32)
            r = r_ref[0, :, vs].astype(F32)
            o = _rms(o, gn_ref[...]) * (r * jax.nn.sigmoid(r))
        o_ref[0, :, vs] = o.astype(BF16)


def _gla(z, gw, gb, *, fwd, ob=None, gn=None):
    B, R, _ = z.shape
    nt = R // TM
    HK, HV = GLA_HEADS * GLA_DK, GLA_HEADS * GLA_DV
    if fwd:
        tile = lambda c: c
    else:
        tile = lambda c: jnp.where(c == 0, 0, nt - c)
    in_specs = [pl.BlockSpec((1, TM, HK), lambda b, c: (b, tile(c), COL_Q // HK)),
                pl.BlockSpec((1, TM, HK), lambda b, c: (b, tile(c), COL_K // HK)),
                pl.BlockSpec((1, TM, HV), lambda b, c: (b, tile(c), COL_V // HV)),
                pl.BlockSpec((1, TM, LANES), lambda b, c: (b, tile(c), COL_MISC // LANES)),
                pl.BlockSpec((LANES, HK), lambda b, c: (0, 0)),
                pl.BlockSpec((1, HK), lambda b, c: (0, 0))]
    args = [z, z, z, z, gw, gb]
    if fwd:
        in_specs += [pl.BlockSpec((1, TM, HV), lambda b, c: (b, tile(c), 0)),
                     pl.BlockSpec((1, TM, HV), lambda b, c: (b, tile(c), COL_R // HV)),
                     pl.BlockSpec((1, GLA_DV), lambda b, c: (0, 0))]
        args += [ob, z, gn]
    return pl.pallas_call(
        functools.partial(_gla_kernel, fwd=fwd),
        grid=(B, nt),
        in_specs=in_specs,
        out_specs=pl.BlockSpec((1, TM, HV), lambda b, c: (b, tile(c), 0)),
        out_shape=jax.ShapeDtypeStruct((B, R, HV), BF16),
        scratch_shapes=[pltpu.VMEM((GLA_HEADS, GLA_DV, GLA_DK), F32)],
        compiler_params=_cparams("parallel", "arbitrary"),
        name="gla_fwd" if fwd else "gla_bwd",
    )(*args)


def _mla_up_kernel(qdn_ref, kvdn_ref, misc_ref, sw_ref, cos_ref, sin_ref, qg_ref, kvg_ref, wq_ref, wkv_ref,
                   q_ref, k_ref, v_ref):
    H = MLA_HEADS
    scale = (MLA_NOPE + MLA_ROPE) ** -0.5 * float(np.log2(np.e))
    cos = cos_ref[...]
    sin = sin_ref[...]
    qn = _rms(qdn_ref[0].astype(F32), qg_ref[...]).astype(BF16)
    qall = _dot(qn, wq_ref[...])
    for h in range(H):
        base = h * MQ
        q_ref[0, :, base:base + MLA_NOPE] = (qall[:, base:base + MLA_NOPE] * scale).astype(BF16)
        rot = (qall[:, base + MLA_NOPE:base + MQ] * cos
               + qall[:, H * MQ + h * LANES:H * MQ + (h + 1) * LANES] * sin)
        q_ref[0, :, base + MLA_NOPE:base + MQ] = (rot * scale).astype(BF16)
    kvn = _rms(kvdn_ref[0].astype(F32), kvg_ref[...]).astype(BF16)
    kvall = _dot(kvn, wkv_ref[...])
    kr = (misc_ref[0].astype(F32) * cos + sw_ref[0].astype(F32) * sin).astype(BF16)
    for h in range(H):
        src = h * (MLA_NOPE + MLA_V)
        k_ref[0, :, h * MQ:h * MQ + MLA_NOPE] = kvall[:, src:src + MLA_NOPE].astype(BF16)
        k_ref[0, :, h * MQ + MLA_NOPE:(h + 1) * MQ] = kr
        v_ref[0, :, h * MLA_V:(h + 1) * MLA_V] = kvall[:, src + MLA_NOPE:src + MLA_NOPE + MLA_V].astype(BF16)


def _mla_up(z, cos, sin, qg, kvg, wq, wkv):
    B, R, _ = z.shape
    nt = R // TM
    H = MLA_HEADS
    const = lambda b, t: (0, 0)
    return pl.pallas_call(
        _mla_up_kernel,
        grid=(B, nt),
        in_specs=[pl.BlockSpec((1, TM, MLA_Q_RANK), lambda b, t: (b, t, COL_QDN // MLA_Q_RANK)),
                  pl.BlockSpec((1, TM, MLA_KV_RANK), lambda b, t: (b, t, COL_KVDN // MLA_KV_RANK)),
                  pl.BlockSpec((1, TM, LANES), lambda b, t: (b, t, COL_MISC // LANES)),
                  pl.BlockSpec((1, TM, LANES), lambda b, t: (b, t, COL_SW // LANES)),
                  pl.BlockSpec((TM, LANES), lambda b, t: (t, 0)),
                  pl.BlockSpec((TM, LANES), lambda b, t: (t, 0)),
                  pl.BlockSpec(qg.shape, const),
                  pl.BlockSpec(kvg.shape, const),
                  pl.BlockSpec(wq.shape, const),
                  pl.BlockSpec(wkv.shape, const)],
        out_specs=[pl.BlockSpec((1, TM, H * MQ), lambda b, t: (b, t, 0)),
                   pl.BlockSpec((1, TM, H * MQ), lambda b, t: (b, t, 0)),
                   pl.BlockSpec((1, TM, H * MLA_V), lambda b, t: (b, t, 0))],
        out_shape=[jax.ShapeDtypeStruct((B, R, H * MQ), BF16),
                   jax.ShapeDtypeStruct((B, R, H * MQ), BF16),
                   jax.ShapeDtypeStruct((B, R, H * MLA_V), BF16)],
        compiler_params=_cparams("parallel", "parallel"),
        name="mla_up",
    )(z, z, z, z, cos, sin, qg, kvg, wq, wkv)


ATTN_TQ = 512


def _attn_kernel(q_ref, k_ref, v_ref, o_ref):
    s = _dot_nt(q_ref[0], k_ref[0])
    p = jnp.exp2(s - jnp.max(s, axis=-1, keepdims=True))
    l = jnp.sum(p, axis=-1, keepdims=True)
    o_ref[0] = (_dot(p.astype(BF16), v_ref[0]) / l).astype(BF16)


def _attn_call(q, k, v, tq, nk, name):
    B, T, _ = q.shape
    return pl.pallas_call(
        _attn_kernel,
        grid=(B, MLA_HEADS, T // tq),
        in_specs=[pl.BlockSpec((1, tq, MQ), lambda b, h, i: (b, i, h)),
                  pl.BlockSpec((1, nk, MQ), lambda b, h, i: (b, 0, h)),
                  pl.BlockSpec((1, nk, MLA_V), lambda b, h, i: (b, 0, h))],
        out_specs=pl.BlockSpec((1, tq, MLA_V), lambda b, h, i: (b, i, h)),
        out_shape=jax.ShapeDtypeStruct((B, T, MLA_HEADS * MLA_V), BF16),
        compiler_params=_cparams("parallel", "parallel", "arbitrary"),
        name=name,
    )(q, k, v)


def _attn(q, k, v, *, with_ctx):
    B, R, _ = q.shape
    T = R - CTX_LEN
    tq = ATTN_TQ if T % ATTN_TQ == 0 else TM
    o_lat = _attn_call(q[:, CTX_LEN:], k, v, tq, R, "mla_attn")
    if not with_ctx:
        return o_lat
    o_ctx = _attn_call(q[:, :CTX_LEN], k, v, CTX_LEN, CTX_LEN, "mla_attn_ctx")
    return jnp.concatenate([o_ctx, o_lat], axis=1)


def _merge_kernel(po_ref, go_ref, mo_ref, gz0_ref, gz1_ref, gz2_ref, x_ref, mod_ref, bw_ref, wo_ref, g_ref,
                  x1_ref, h2t_ref):
    D = D_MODEL
    m = None
    for i, (o_ref, gz_ref) in enumerate(((po_ref, gz0_ref), (go_ref, gz1_ref), (mo_ref, gz2_ref))):
        t = jax.nn.sigmoid(gz_ref[0].astype(F32)) * _dot(o_ref[0], bw_ref[i])
        m = t if m is None else m + t
    y = _dot(m.astype(BF16), wo_ref[...])
    mod = mod_ref[0]
    x1 = x_ref[0] + mod[:, 2 * D:3 * D] * y
    x1_ref[0] = x1
    h2 = _rms(x1, g_ref[...]) * (1.0 + mod[:, 4 * D:5 * D]) + mod[:, 3 * D:4 * D]
    h2t_ref[...] = pltpu.bitcast(h2.T.astype(BF16), jnp.int32)


def _merge(pool_o, gla_o, mla_o, z, xall, modrows, bw, wo, g, *, with_ctx):
    B, R, D = xall.shape
    t_off = 0 if with_ctx else CTX_LEN // TM
    nt = R // TM - t_off
    tok = lambda b, t: (b, t + t_off, 0)
    gzb = COL_GZ // D
    wpr = _words_per_row()
    mo_off = t_off if mla_o.shape[1] == R else 0
    return pl.pallas_call(
        _merge_kernel,
        grid=(B, nt),
        in_specs=[pl.BlockSpec((1, TM, D), tok),
                  pl.BlockSpec((1, TM, D), tok),
                  pl.BlockSpec((1, TM, D), lambda b, t: (b, t + mo_off, 0)),
                  pl.BlockSpec((1, TM, D), lambda b, t: (b, t + t_off, gzb)),
                  pl.BlockSpec((1, TM, D), lambda b, t: (b, t + t_off, gzb + 1)),
                  pl.BlockSpec((1, TM, D), lambda b, t: (b, t + t_off, gzb + 2)),
                  pl.BlockSpec((1, TM, D), tok),
                  pl.BlockSpec((1, 1, 6 * D), lambda b, t: (2 * b + jnp.minimum(t + t_off, 1), 0, 0)),
                  pl.BlockSpec(bw.shape, lambda b, t: (0, 0, 0)),
                  pl.BlockSpec(wo.shape, lambda b, t: (0, 0)),
                  pl.BlockSpec((1, D), lambda b, t: (0, 0))],
        out_specs=[pl.BlockSpec((1, TM, D), lambda b, t: (b, t, 0)),
                   pl.BlockSpec((D // wpr, TM), lambda b, t: (0, b * nt + t))],
        out_shape=[jax.ShapeDtypeStruct((B, nt * TM, D), F32),
                   jax.ShapeDtypeStruct((D // wpr, B * nt * TM), jnp.int32)],
        compiler_params=_cparams("parallel", "parallel"),
        name="merge",
    )(pool_o, gla_o, mla_o, z, z, z, xall, modrows, bw, wo, g)


def _top16(s):
    nk, n = s.shape
    iota = lax.broadcasted_iota(jnp.int32, (nk, n), 0).astype(F32)
    slot = lax.broadcasted_iota(jnp.int32, (PEER_TOPK, n), 0)
    rank = jnp.full((nk, n), float(PEER_TOPK), F32)
    vals = jnp.zeros((PEER_TOPK, n), F32)
    for j in range(PEER_TOPK):
        m = jnp.max(s, axis=0, keepdims=True)
        idx = jnp.min(jnp.where(s == m, iota, float(nk)), axis=0, keepdims=True)
        hit = iota == idx
        rank = jnp.where(hit, float(j), rank)
        s = jnp.where(hit, -jnp.inf, s)
        vals = jnp.where(slot == j, m, vals)
    return vals, rank


PAIR_SHORT = PEER_TOPK // 2
PAIR_ROWS = PEER_TOPK + (PEER_TOPK - 1) * PAIR_SHORT
PAIR_VALID = sum(PEER_TOPK // (i + 1) for i in range(PEER_TOPK))


def _pair_rows(t2):
    return jnp.concatenate([t2] + [t2[0:PAIR_SHORT]] * (PEER_TOPK - 1), axis=0)


def _pair_candidates(t1, t2):
    K = PEER_TOPK
    n = t1.shape[1]
    first = jnp.concatenate([jnp.broadcast_to(t1[0:1], (K, n))]
                            + [jnp.broadcast_to(t1[i:i + 1], (PAIR_SHORT, n)) for i in range(1, K)], axis=0)
    r = lax.broadcasted_iota(jnp.int32, (PAIR_ROWS, n), 0)
    sh = PAIR_SHORT.bit_length() - 1
    i_of = jnp.where(r < K, 0, ((r - K) >> sh) + 1)
    j_of = jnp.where(r < K, r, (r - K) & (PAIR_SHORT - 1))
    cand = jnp.where((i_of + 1) * (j_of + 1) <= K, first + _pair_rows(t2), -jnp.inf)
    return cand, (i_of * K + j_of).astype(F32)


def _top16_values(s, with_rank):
    n = s.shape[1]
    slot = lax.broadcasted_iota(jnp.int32, (PEER_TOPK, n), 0)
    vals = jnp.zeros((PEER_TOPK, n), F32)
    rank = jnp.full(s.shape, float(PEER_TOPK), F32) if with_rank else None
    for j in range(PEER_TOPK):
        m = jnp.max(s, axis=0, keepdims=True)
        hit = s == m
        if with_rank:
            rank = jnp.where(hit, float(j), rank)
        s = jnp.where(hit, -jnp.inf, s)
        vals = jnp.where(slot == j, m, vals)
    used = jnp.sum(jnp.where(s == -jnp.inf, 1.0, 0.0), axis=0, keepdims=True)
    return vals, rank, jnp.where(used == float(PEER_TOPK), 1.0, 0.0)


def _pair_counts(t1, t2):
    K = PEER_TOPK
    n = t1.shape[1]
    cand0, _ = _pair_candidates(t1, t2)
    cand = cand0
    zsum = jnp.zeros((1, n), F32)
    best = m = None
    for j in range(K):
        m = jnp.max(cand, axis=0, keepdims=True)
        cand = jnp.where(cand == m, -jnp.inf, cand)
        if j == 0:
            best = m
        zsum = zsum + jnp.exp(m - best)
    used = jnp.sum(jnp.where(cand == -jnp.inf, 1.0, 0.0), axis=0, keepdims=True)
    exact = jnp.where(used == float(PAIR_ROWS - PAIR_VALID + K), 1.0, 0.0)
    picked = jnp.where(cand0 >= m, 1.0, 0.0)
    slot = lax.broadcasted_iota(jnp.int32, (K, n), 0)
    cnt = jnp.zeros((K, n), F32)
    for i in range(K):
        blk = picked[0:K] if i == 0 else picked[K + (i - 1) * PAIR_SHORT:K + i * PAIR_SHORT]
        cnt = jnp.where(slot == i, jnp.sum(blk, axis=0, keepdims=True), cnt)
    return cnt, zsum, exact


def _pair_top16(t1, t2):
    K = PEER_TOPK
    n = t1.shape[1]
    cand, flat = _pair_candidates(t1, t2)
    irow = lax.broadcasted_iota(jnp.int32, (K, n), 0).astype(F32)
    cnt = jnp.zeros((K, n), F32)
    zsum = jnp.zeros((1, n), F32)
    best = None
    for j in range(K):
        m = jnp.max(cand, axis=0, keepdims=True)
        idx = jnp.min(jnp.where(cand == m, flat, float(K * K)), axis=0, keepdims=True)
        cand = jnp.where(flat == idx, -jnp.inf, cand)
        cnt = cnt + jnp.where(irow == jnp.floor(idx * (1.0 / K)), 1.0, 0.0)
        if j == 0:
            best = m
        zsum = zsum + jnp.exp(m - best)
    return cnt, zsum


def _words_per_row():
    return 4 // jnp.dtype(BF16).itemsize


def _splat_words(x):
    bits = lax.bitcast_convert_type(x.astype(BF16).astype(F32), jnp.int32)
    if _words_per_row() == 1:
        return bits
    return bits | lax.shift_right_logical(bits, 16)


def _peer_score_kernel(h2t_ref, wqt_ref, keys_ref, c2_ref, p2_ref, cnt_ref, ka_ref, qt_ref):
    K = PEER_TOPK
    qt_ref[...] = _dot(wqt_ref[...], pltpu.bitcast(h2t_ref[...], BF16)).astype(BF16)

    def store(h, cs, s1, s2, t1, rank1, t2, rank2, cnt, zsum):
        rowcnt = jnp.zeros_like(s1)
        for j in range(K):
            in_row = (s1 == t1[j:j + 1]) if rank1 is None else (rank1 == float(j))
            rowcnt = jnp.where(in_row, cnt[j:j + 1], rowcnt)
        c2_ref[h, :, cs] = pltpu.bitcast(rank2.astype(BF16), jnp.int32)
        p2_ref[h, :, cs] = pltpu.bitcast(jnp.exp(s2 - t2[0:1]).astype(BF16), jnp.int32)
        cnt_ref[h, :, cs] = _splat_words(rowcnt)
        ka_ref[h, :, cs] = _splat_words(jnp.exp(s1 - t1[0:1]) / zsum)

    def head(h, carry):
        base = pl.multiple_of(h * PEER_QDIM, PEER_QDIM)
        s1_all = _dot(keys_ref[h], qt_ref[pl.ds(base, PEER_HALF), :])
        s2_all = _dot(keys_ref[PEER_HEADS + h], qt_ref[pl.ds(base + PEER_HALF, PEER_HALF), :])
        chunks = []
        for c in range(s1_all.shape[1] // LANES):
            cs = slice(c * LANES, (c + 1) * LANES)
            s1, s2 = s1_all[:, cs], s2_all[:, cs]
            t1, _, ok1 = _top16_values(s1, with_rank=False)
            t2, rank2, ok2 = _top16_values(s2, with_rank=True)
            cnt, zsum, ok3 = _pair_counts(t1, t2)
            store(h, cs, s1, s2, t1, None, t2, rank2, cnt, zsum)
            chunks.append((cs, s1, s2, jnp.min(ok1 * ok2 * ok3)))

        for cs, s1, s2, exact in chunks:
            @pl.when(exact < 0.5)
            def _(cs=cs, s1=s1, s2=s2):
                t1x, rank1x = _top16(s1)
                t2x, rank2x = _top16(s2)
                cntx, zx = _pair_top16(t1x, t2x)
                store(h, cs, s1, s2, t1x, rank1x, t2x, rank2x, cntx, zx)

        return carry

    lax.fori_loop(0, PEER_HEADS, head, 0)


PEER_SCORE_TN = 2 * LANES


def _peer_score(h2t, wqt, keys):
    ntok = h2t.shape[1]
    tn = PEER_SCORE_TN
    nw = PEER_N_KEYS // _words_per_row()
    tbl = lambda rows: jax.ShapeDtypeStruct((PEER_HEADS, rows, ntok), jnp.int32)
    ospec = lambda rows: pl.BlockSpec((PEER_HEADS, rows, tn), lambda i: (0, 0, i))
    return pl.pallas_call(
        _peer_score_kernel,
        grid=(ntok // tn,),
        in_specs=[pl.BlockSpec((h2t.shape[0], tn), lambda i: (0, i)),
                  pl.BlockSpec(wqt.shape, lambda i: (0, 0)),
                  pl.BlockSpec(keys.shape, lambda i: (0, 0, 0))],
        out_specs=[ospec(nw), ospec(nw), ospec(PEER_N_KEYS), ospec(PEER_N_KEYS)],
        out_shape=[tbl(nw), tbl(nw), tbl(PEER_N_KEYS), tbl(PEER_N_KEYS)],
        scratch_shapes=[pltpu.VMEM((PEER_HEADS * PEER_QDIM, tn), BF16)],
        compiler_params=_cparams("parallel"),
        name="peer_score",
    )(h2t, wqt, keys)


PEER_KEYS_PER_STEP = 4
PEER_ETILE = PEER_KEYS_PER_STEP * PEER_N_KEYS
PEER_KEY_GROUP = 2


def _peer_dense_kernel(h2t_ref, c2_ref, p2_ref, cnt_ref, ka_ref, u_ref, vt_ref, x1_ref, modc_ref, modl_ref, fg_ref,
                       o_ref, acc_ref, pre_ref, hid_ref, *, ctx_rows, rows_per_sample, final):
    D = D_MODEL
    NK = PEER_N_KEYS
    tn = h2t_ref.shape[1]
    e = pl.program_id(1)

    @pl.when(e == 0)
    def _():
        acc_ref[...] = jnp.zeros_like(acc_ref)

    pre_ref[...] = _dot(pltpu.bitcast(u_ref[...], BF16), pltpu.bitcast(h2t_ref[...], BF16))

    row = lambda ref, i, h: ref[h, pl.ds(e * PEER_KEYS_PER_STEP + i, 1), :]
    cnt_rows = [[row(cnt_ref, i, h) for h in range(PEER_HEADS)] for i in range(PEER_KEYS_PER_STEP)]
    ka_rows = [[row(ka_ref, i, h) for h in range(PEER_HEADS)] for i in range(PEER_KEYS_PER_STEP)]
    wpr = _words_per_row()

    def key_rows(i, h, cs):
        rep = lambda words: pltpu.bitcast(jnp.broadcast_to(words[:, cs], (BF16_ROWS // wpr, LANES)), BF16)
        return rep(cnt_rows[i][h]), rep(ka_rows[i][h])

    for c in range(tn // LANES):
        cs = slice(c * LANES, (c + 1) * LANES)
        for g in range(0, PEER_KEYS_PER_STEP, PEER_KEY_GROUP):
            group = range(g, g + PEER_KEY_GROUP)
            w = {i: None for i in group}
            for h in range(PEER_HEADS):
                c2 = pltpu.bitcast(c2_ref[h, :, cs], BF16).reshape(NK // BF16_ROWS, BF16_ROWS, LANES)
                p2 = pltpu.bitcast(p2_ref[h, :, cs], BF16).reshape(NK // BF16_ROWS, BF16_ROWS, LANES)
                for i in group:
                    cnt, kaw = key_rows(i, h, cs)
                    t = jnp.where(c2 < cnt[None], p2, 0.0) * kaw[None]
                    w[i] = t if w[i] is None else w[i] + t
            for i in group:
                rows = slice(i * NK, (i + 1) * NK)
                x = pre_ref[rows, cs].astype(BF16)
                act = 0.5 * x * (1.0 + lax.erf(x * (2.0 ** -0.5)))
                hid_ref[rows, cs] = act * w[i].reshape(NK, LANES)

    acc_ref[...] += _dot(pltpu.bitcast(vt_ref[...], BF16), hid_ref[...])

    @pl.when(e == pl.num_programs(1) - 1)
    def _():
        y = acc_ref[...].T
        g_lat = modl_ref[0][:, 5 * D:6 * D]
        if ctx_rows:
            g_ctx = modc_ref[0][:, 5 * D:6 * D]
            tiles = rows_per_sample // tn
            r0 = (pl.program_id(0) % tiles) * tn
            row = r0 + lax.broadcasted_iota(jnp.int32, (tn, D), 0)
            gate = jnp.where(row < ctx_rows, g_ctx, g_lat)
        else:
            gate = g_lat
        x2 = x1_ref[...] + gate * y
        if final:
            x2 = _rms(x2, fg_ref[...])
        o_ref[...] = x2


def _peer_dense(h2t, sel, u, vt, x1, modrows, fg, *, tn, rows_per_sample, ctx_rows, final):
    ntok, D = x1.shape
    wpr = _words_per_row()
    ne = u.shape[0] * wpr // PEER_ETILE
    tiles = rows_per_sample // tn
    tspec = lambda t: pl.BlockSpec((PEER_HEADS, t.shape[1], tn), lambda i, s: (0, 0, i))
    return pl.pallas_call(
        functools.partial(_peer_dense_kernel, ctx_rows=ctx_rows, rows_per_sample=rows_per_sample, final=final),
        grid=(ntok // tn, ne),
        in_specs=[pl.BlockSpec((D // wpr, tn), lambda i, s: (0, i)),
                  tspec(sel[0]), tspec(sel[1]), tspec(sel[2]), tspec(sel[3]),
                  pl.BlockSpec((PEER_ETILE // wpr, D), lambda i, s: (s, 0)),
                  pl.BlockSpec((D // wpr, PEER_ETILE), lambda i, s: (0, s)),
                  pl.BlockSpec((tn, D), lambda i, s: (i, 0), pipeline_mode=pl.Buffered(1)),
                  pl.BlockSpec((1, 1, 6 * D), lambda i, s: (2 * (i // tiles), 0, 0)),
                  pl.BlockSpec((1, 1, 6 * D), lambda i, s: (2 * (i // tiles) + 1, 0, 0)),
                  pl.BlockSpec((1, D), lambda i, s: (0, 0))],
        out_specs=pl.BlockSpec((tn, D), lambda i, s: (i, 0)),
        out_shape=jax.ShapeDtypeStruct((ntok, D), F32),
        scratch_shapes=[pltpu.VMEM((D, tn), F32),
                        pltpu.VMEM((PEER_ETILE, tn), F32),
                        pltpu.VMEM((PEER_ETILE, tn), BF16)],
        compiler_params=_cparams("parallel", "arbitrary"),
        name="peer_dense",
    )(h2t, *sel, u, vt, x1, modrows, modrows, fg)


def _pack_rows_kernel(x_ref, o_ref):
    o_ref[...] = pltpu.bitcast(x_ref[...].astype(BF16), jnp.int32)


def _pack_rows(x):
    R, C = x.shape
    wpr = _words_per_row()
    br, bc = 512, 1024
    return pl.pallas_call(
        _pack_rows_kernel,
        grid=(R // br, C // bc),
        in_specs=[pl.BlockSpec((br, bc), lambda i, j: (i, j))],
        out_specs=pl.BlockSpec((br // wpr, bc), lambda i, j: (i, j)),
        out_shape=jax.ShapeDtypeStruct((R // wpr, C), jnp.int32),
        compiler_params=_cparams("parallel", "parallel"),
        name="pack_rows",
    )(x)


def _peer_tile(rows_per_sample):
    for tn in (1024, 768, 512, 256):
        if rows_per_sample % tn == 0:
            return tn
    raise ValueError(rows_per_sample)


_SWAP = np.concatenate([np.arange(16, 32), np.arange(0, 16), np.arange(48, 64), np.arange(32, 48)])


def _layout_w_in(w):
    D = w.shape[0]
    idx = np.cumsum(IN_SPLITS)[:-1]
    p, q, k, v, r, lrf, lrb, qdn, kvdn, krope, gz = jnp.split(w, idx, axis=1)
    zeros = lambda n: jnp.zeros((D, n), w.dtype)
    misc = jnp.concatenate([krope, lrf, lrb, zeros(LANES - MLA_ROPE - 2 * GLA_GATE_RANK)], axis=1)
    sw = jnp.concatenate([krope[:, _SWAP], zeros(LANES - MLA_ROPE)], axis=1)
    out = jnp.concatenate([p, q, k, v, r, gz, kvdn, misc, sw, qdn], axis=1)
    return jnp.concatenate([out, zeros(Z_COLS - out.shape[1])], axis=1).astype(BF16)


def _layout_gate_w(gate_w, row0):
    HK = GLA_HEADS * GLA_DK
    pad = jnp.zeros((LANES, HK), gate_w.dtype)
    return pad.at[row0:row0 + GLA_GATE_RANK].set(gate_w).astype(BF16)


def _layout_w_uq(w):
    H = MLA_HEADS
    w3 = w.reshape(MLA_Q_RANK, H, MLA_NOPE + MLA_ROPE)
    qn, qr = w3[..., :MLA_NOPE], w3[..., MLA_NOPE:]
    z = jnp.zeros((MLA_Q_RANK, H, LANES - MLA_ROPE), w.dtype)
    main = jnp.concatenate([qn, qr, z], axis=-1).reshape(MLA_Q_RANK, H * MQ)
    swp = jnp.concatenate([qr[..., _SWAP], z], axis=-1).reshape(MLA_Q_RANK, H * LANES)
    return jnp.concatenate([main, swp], axis=1).astype(BF16)


def _rope_tables(seq):
    half = MLA_ROPE // 2
    t = jnp.arange(seq)
    inv = ROPE_BASE ** (-jnp.arange(0, half, 2, dtype=F32) / half)
    ar = (t // GRID_W).astype(F32)[:, None] * inv
    ac = (t % GRID_W).astype(F32)[:, None] * inv
    cos = jnp.concatenate([jnp.cos(ar), jnp.cos(ar), jnp.cos(ac), jnp.cos(ac)], axis=1)
    sin = jnp.concatenate([-jnp.sin(ar), jnp.sin(ar), -jnp.sin(ac), jnp.sin(ac)], axis=1)
    cos = jnp.concatenate([jnp.ones((CTX_LEN, MLA_ROPE), F32), cos], axis=0)
    sin = jnp.concatenate([jnp.zeros((CTX_LEN, MLA_ROPE), F32), sin], axis=0)
    pad = jnp.zeros((CTX_LEN + seq, LANES - MLA_ROPE), F32)
    return jnp.concatenate([cos, pad], axis=1), jnp.concatenate([sin, pad], axis=1)


def kernel(x, c, ctx, c_ctx, ada_w, ada_b, norm1_g, norm2_g, w_in, pool_w, pool_scale, gla_gate_w, gla_gate_b, gla_norm_g, mla_q_norm_g, mla_kv_norm_g, mla_w_uq, mla_w_ukv, branch_w, w_out, peer_wq, peer_keys, peer_u, peer_v, final_norm_g):
    B, T, D = x.shape
    R = CTX_LEN + T
    assert D == D_MODEL and ctx.shape[1] == CTX_LEN == TM and T % TM == 0 and T % GRID_W == 0

    nrow = -(-(B + 1) // 8) * 8
    cc = jnp.concatenate([c, c_ctx[None], jnp.zeros((nrow - B - 1, D), F32)], axis=0)
    mod = _ada(cc, ada_w, ada_b)
    modrows = jnp.stack([jnp.broadcast_to(mod[:, B:B + 1], (DEPTH, B, 6 * D)), mod[:, :B]], axis=2)
    modrows = modrows.reshape(DEPTH, 2 * B, 1, 6 * D)

    cos, sin = _rope_tables(T)
    xall = jnp.concatenate([ctx, x], axis=1)
    row = lambda a: a.reshape(1, -1)

    for i in range(DEPTH):
        with_ctx = i < DEPTH - 1
        final = i == DEPTH - 1
        z = _in_proj(xall, modrows[i], row(norm1_g[i]), _layout_w_in(w_in[i]))

        pool_o = _pool(z, pool_w[i].astype(BF16), row(pool_scale[i]))

        o_b = _gla(z, _layout_gate_w(gla_gate_w[i, 1], MISC_LRB), row(gla_gate_b[i, 1]), fwd=False)
        gla_o = _gla(z, _layout_gate_w(gla_gate_w[i, 0], MISC_LRF), row(gla_gate_b[i, 0]), fwd=True,
                     ob=o_b, gn=row(gla_norm_g[i]))

        q, k, v = _mla_up(z, cos, sin, row(mla_q_norm_g[i]), row(mla_kv_norm_g[i]),
                          _layout_w_uq(mla_w_uq[i]), mla_w_ukv[i].astype(BF16))
        mla_o = _attn(q, k, v, with_ctx=with_ctx)

        x1, h2t = _merge(pool_o, gla_o, mla_o, z, xall, modrows[i], branch_w[i].astype(BF16),
                        w_out[i].astype(BF16), row(norm2_g[i]), with_ctx=with_ctx)

        rows = x1.shape[1]
        keys = peer_keys[i].reshape(2 * PEER_HEADS, PEER_N_KEYS, PEER_HALF).astype(BF16)
        sel = _peer_score(h2t, peer_wq[i].T.astype(BF16), keys)
        xall = _peer_dense(h2t, sel, _pack_rows(peer_u[i]), _pack_rows(peer_v[i].T), x1.reshape(B * rows, D),
                           modrows[i], row(final_norm_g), tn=_peer_tile(rows), rows_per_sample=rows,
                           ctx_rows=CTX_LEN if with_ctx else 0, final=final).reshape(B, rows, D)
    return xall
```

```python
import functools

import jax
import jax.numpy as jnp
import numpy as np
from jax import lax
from jax.experimental import pallas as pl
from jax.experimental.pallas import tpu as pltpu

F32 = jnp.float32
BF16 = jnp.bfloat16

D_MODEL = 1024
DEPTH = 2
CTX_LEN = 256
GRID_W = 64
NORM_EPS = 1e-6
POOL_WIDTH = 1024
POOL_WINDOWS = (2, 4, 8, 16)
POOL_GROUP = POOL_WIDTH // len(POOL_WINDOWS)
GLA_HEADS = 4
GLA_DK = 128
GLA_DV = 256
GLA_GATE_RANK = 16
GLA_TAU = 16.0
GLA_CHUNK = 64
MLA_HEADS = 8
MLA_Q_RANK = 384
MLA_KV_RANK = 256
MLA_NOPE = 128
MLA_ROPE = 64
MLA_V = 128
ROPE_BASE = 10000.0
N_BRANCH = 3
PEER_HEADS = 8
PEER_N_KEYS = 128
PEER_TOPK = 16
PEER_QDIM = 256
PEER_HALF = PEER_QDIM // 2
IN_SPLITS = (POOL_WIDTH, GLA_HEADS * GLA_DK, GLA_HEADS * GLA_DK, GLA_HEADS * GLA_DV, GLA_HEADS * GLA_DV,
             GLA_GATE_RANK, GLA_GATE_RANK, MLA_Q_RANK, MLA_KV_RANK, MLA_ROPE, N_BRANCH * D_MODEL)

LANES = 128
BF16_ROWS = 16
TM = 256
VMEM_LIMIT = 56 * 1024 * 1024

COL_P = 0
COL_Q = 1024
COL_K = 1536
COL_V = 2048
COL_R = 3072
COL_GZ = 4096
COL_KVDN = 7168
COL_MISC = 7424
COL_SW = 7552
COL_QDN = 7680
Z_COLS = 8192
IN_NBLK = 4096
MISC_LRF = MLA_ROPE
MISC_LRB = MLA_ROPE + GLA_GATE_RANK
MQ = MLA_NOPE + LANES


def _cparams(*sem):
    return pltpu.CompilerParams(dimension_semantics=sem, vmem_limit_bytes=VMEM_LIMIT)


def _rms(x, g):
    return x * lax.rsqrt(jnp.mean(x * x, axis=-1, keepdims=True) + NORM_EPS) * g


def _dot(a, b):
    return jnp.dot(a, b, preferred_element_type=F32)


def _dot_nt(a, b):
    return lax.dot_general(a, b, (((1,), (1,)), ((), ())), preferred_element_type=F32)


def _dot_tn(a, b):
    return lax.dot_general(a, b, (((0,), (0,)), ((), ())), preferred_element_type=F32)


def _ada_kernel(c_ref, w_ref, b_ref, o_ref):
    c = c_ref[...]
    a = (c * jax.nn.sigmoid(c)).astype(BF16)
    o_ref[0] = _dot(a, w_ref[0].astype(BF16)) + b_ref[0]


def _ada(cc, ada_w, ada_b):
    L, D, N = ada_w.shape
    rows = cc.shape[0]
    nb = N // D
    return pl.pallas_call(
        _ada_kernel,
        grid=(L, nb),
        in_specs=[pl.BlockSpec((rows, D), lambda l, j: (0, 0)),
                  pl.BlockSpec((1, D, D), lambda l, j: (l, 0, j)),
                  pl.BlockSpec((1, 1, D), lambda l, j: (l, 0, j))],
        out_specs=pl.BlockSpec((1, rows, D), lambda l, j: (l, 0, j)),
        out_shape=jax.ShapeDtypeStruct((L, rows, N), F32),
        compiler_params=_cparams("parallel", "parallel"),
        name="ada_mod",
    )(cc, ada_w, ada_b.reshape(L, 1, N))


def _in_proj_kernel(x_ref, mod_ref, g_ref, w_ref, z_ref):
    D = D_MODEL
    x = x_ref[0]
    mod = mod_ref[0]
    h = _rms(x, g_ref[...]) * (1.0 + mod[:, D:2 * D]) + mod[:, 0:D]
    z_ref[0] = _dot(h.astype(BF16), w_ref[...]).astype(BF16)


def _in_proj(xall, modrows, g, w):
    B, R, D = xall.shape
    nt = R // TM
    nn = Z_COLS // IN_NBLK
    return pl.pallas_call(
        _in_proj_kernel,
        grid=(nn, B, nt),
        in_specs=[pl.BlockSpec((1, TM, D), lambda n, b, t: (b, t, 0)),
                  pl.BlockSpec((1, 1, 6 * D), lambda n, b, t: (2 * b + jnp.minimum(t, 1), 0, 0)),
                  pl.BlockSpec((1, D), lambda n, b, t: (0, 0)),
                  pl.BlockSpec((D, IN_NBLK), lambda n, b, t: (0, n))],
        out_specs=pl.BlockSpec((1, TM, IN_NBLK), lambda n, b, t: (b, t, n)),
        out_shape=jax.ShapeDtypeStruct((B, R, Z_COLS), BF16),
        compiler_params=_cparams("parallel", "parallel", "parallel"),
        name="in_proj",
    )(xall, modrows, g, w)


def _pool_kernel(u_ref, w_ref, sc_ref, o_ref):
    R = u_ref.shape[1]
    row = lax.broadcasted_iota(jnp.int32, (R, POOL_GROUP), 0)
    seg_lo = jnp.where(row < CTX_LEN, 0, CTX_LEN)
    seg_hi = jnp.where(row < CTX_LEN, CTX_LEN, R)
    for gi, win in enumerate(POOL_WINDOWS):
        sl = slice(gi * POOL_GROUP, (gi + 1) * POOL_GROUP)
        u = u_ref[0, :, sl].astype(F32)
        lo_off, hi_off = win // 2, win - win // 2
        acc = jnp.zeros_like(u)
        for d in range(-lo_off, hi_off):
            shifted = u if d == 0 else pltpu.roll(u, (R - d) % R, axis=0)
            ok = (row + d >= seg_lo) & (row + d < seg_hi)
            acc = acc + jnp.where(ok, shifted, 0.0)
        cnt = (jnp.minimum(row + hi_off, seg_hi) - jnp.maximum(row - lo_off, seg_lo)).astype(F32)
        diff = (acc / cnt - u).astype(BF16)
        o_ref[0, :, sl] = (_dot(diff, w_ref[gi]) * sc_ref[:, sl]).astype(BF16)


def _pool(z, pool_w, pool_scale):
    B, R, _ = z.shape
    return pl.pallas_call(
        _pool_kernel,
        grid=(B,),
        in_specs=[pl.BlockSpec((1, R, POOL_WIDTH), lambda b: (b, 0, COL_P // POOL_WIDTH)),
                  pl.BlockSpec(pool_w.shape, lambda b: (0, 0, 0)),
                  pl.BlockSpec((1, POOL_WIDTH), lambda b: (0, 0))],
        out_specs=pl.BlockSpec((1, R, POOL_WIDTH), lambda b: (b, 0, 0)),
        out_shape=jax.ShapeDtypeStruct((B, R, POOL_WIDTH), BF16),
        compiler_params=_cparams("parallel"),
        name="pool",
    )(z, pool_w, pool_scale)


def _split3(x):
    a = x.astype(BF16)
    r = x - a.astype(F32)
    b = r.astype(BF16)
    c = (r - b.astype(F32)).astype(BF16)
    return a, b, c


def _gla_kernel(*refs, fwd):
    if fwd:
        q_ref, k_ref, v_ref, misc_ref, gw_ref, gb_ref, ob_ref, r_ref, gn_ref, o_ref, st_ref = refs
    else:
        q_ref, k_ref, v_ref, misc_ref, gw_ref, gb_ref, o_ref, st_ref = refs
    C = GLA_CHUNK
    nchunk = TM // C
    HK = GLA_HEADS * GLA_DK

    @pl.when(pl.program_id(1) == 0)
    def _():
        st_ref[...] = jnp.zeros_like(st_ref)

    zg = _dot(misc_ref[0], gw_ref[...]) + gb_ref[...]
    g = (jnp.minimum(zg, 0.0) - jnp.log1p(jnp.exp(-jnp.abs(zg)))) * (1.0 / GLA_TAU)

    row = lax.broadcasted_iota(jnp.int32, (TM, TM), 0)
    col = lax.broadcasted_iota(jnp.int32, (TM, TM), 1)
    shift = C.bit_length() - 1
    same = (row >> shift) == (col >> shift)
    tri = same & ((col <= row) if fwd else (col >= row))
    ones = jnp.concatenate([jnp.where(tri, 1.0, 0.0), jnp.where(same, 1.0, 0.0)], axis=0).astype(BF16)
    g1, g2, g3 = _split3(g)
    both = _dot(ones, g1) + _dot(ones, g2) + _dot(ones, g3)
    b = both[:TM]
    bt = both[TM:]

    qe = q_ref[0].astype(F32) * (GLA_DK ** -0.5) * jnp.exp(b)
    ke = (k_ref[0].astype(F32) * jnp.exp(-b)).astype(BF16)
    kd = (k_ref[0].astype(F32) * jnp.exp(bt - b)).astype(BF16)
    qe = qe.astype(BF16)
    dec = jnp.exp(bt)
    v = v_ref[0]

    order = range(nchunk) if fwd else range(nchunk - 1, -1, -1)
    for h in range(GLA_HEADS):
        ks = slice(h * GLA_DK, (h + 1) * GLA_DK)
        vs = slice(h * GLA_DV, (h + 1) * GLA_DV)
        att = jnp.where(tri, _dot_nt(qe[:, ks], ke[:, ks]), 0.0).astype(BF16)
        o_in = _dot(att, v[:, vs])
        st = st_ref[h]
        parts = [None] * nchunk
        for j in order:
            rs = slice(j * C, (j + 1) * C)
            parts[j] = o_in[rs] + _dot_nt(qe[rs, ks], st.astype(BF16))
            st = st * dec[j * C:j * C + 1, ks] + _dot_tn(v[rs, vs], kd[rs, ks])
        st_ref[h] = st
        o = jnp.concatenate(parts, axis=0)
        if fwd:
            o = o + ob_ref[0, :, vs].astype(F32)
            r = r_ref[0, :, vs].astype(F32)
            o = _rms(o, gn_ref[...]) * (r * jax.nn.sigmoid(r))
        o_ref[0, :, vs] = o.astype(BF16)


def _gla(z, gw, gb, *, fwd, ob=None, gn=None):
    B, R, _ = z.shape
    nt = R // TM
    HK, HV = GLA_HEADS * GLA_DK, GLA_HEADS * GLA_DV
    if fwd:
        tile = lambda c: c
    else:
        tile = lambda c: jnp.where(c == 0, 0, nt - c)
    in_specs = [pl.BlockSpec((1, TM, HK), lambda b, c: (b, tile(c), COL_Q // HK)),
                pl.BlockSpec((1, TM, HK), lambda b, c: (b, tile(c), COL_K // HK)),
                pl.BlockSpec((1, TM, HV), lambda b, c: (b, tile(c), COL_V // HV)),
                pl.BlockSpec((1, TM, LANES), lambda b, c: (b, tile(c), COL_MISC // LANES)),
                pl.BlockSpec((LANES, HK), lambda b, c: (0, 0)),
                pl.BlockSpec((1, HK), lambda b, c: (0, 0))]
    args = [z, z, z, z, gw, gb]
    if fwd:
        in_specs += [pl.BlockSpec((1, TM, HV), lambda b, c: (b, tile(c), 0)),
                     pl.BlockSpec((1, TM, HV), lambda b, c: (b, tile(c), COL_R // HV)),
                     pl.BlockSpec((1, GLA_DV), lambda b, c: (0, 0))]
        args += [ob, z, gn]
    return pl.pallas_call(
        functools.partial(_gla_kernel, fwd=fwd),
        grid=(B, nt),
        in_specs=in_specs,
        out_specs=pl.BlockSpec((1, TM, HV), lambda b, c: (b, tile(c), 0)),
        out_shape=jax.ShapeDtypeStruct((B, R, HV), BF16),
        scratch_shapes=[pltpu.VMEM((GLA_HEADS, GLA_DV, GLA_DK), F32)],
        compiler_params=_cparams("parallel", "arbitrary"),
        name="gla_fwd" if fwd else "gla_bwd",
    )(*args)


def _mla_up_kernel(qdn_ref, kvdn_ref, misc_ref, sw_ref, cos_ref, sin_ref, qg_ref, kvg_ref, wq_ref, wkv_ref,
                   q_ref, k_ref, v_ref):
    H = MLA_HEADS
    scale = (MLA_NOPE + MLA_ROPE) ** -0.5 * float(np.log2(np.e))
    cos = cos_ref[...]
    sin = sin_ref[...]
    qn = _rms(qdn_ref[0].astype(F32), qg_ref[...]).astype(BF16)
    qall = _dot(qn, wq_ref[...])
    for h in range(H):
        base = h * MQ
        q_ref[0, :, base:base + MLA_NOPE] = (qall[:, base:base + MLA_NOPE] * scale).astype(BF16)
        rot = (qall[:, base + MLA_NOPE:base + MQ] * cos
               + qall[:, H * MQ + h * LANES:H * MQ + (h + 1) * LANES] * sin)
        q_ref[0, :, base + MLA_NOPE:base + MQ] = (rot * scale).astype(BF16)
    kvn = _rms(kvdn_ref[0].astype(F32), kvg_ref[...]).astype(BF16)
    kvall = _dot(kvn, wkv_ref[...])
    kr = (misc_ref[0].astype(F32) * cos + sw_ref[0].astype(F32) * sin).astype(BF16)
    for h in range(H):
        src = h * (MLA_NOPE + MLA_V)
        k_ref[0, :, h * MQ:h * MQ + MLA_NOPE] = kvall[:, src:src + MLA_NOPE].astype(BF16)
        k_ref[0, :, h * MQ + MLA_NOPE:(h + 1) * MQ] = kr
        v_ref[0, :, h * MLA_V:(h + 1) * MLA_V] = kvall[:, src + MLA_NOPE:src + MLA_NOPE + MLA_V].astype(BF16)


def _mla_up(z, cos, sin, qg, kvg, wq, wkv):
    B, R, _ = z.shape
    nt = R // TM
    H = MLA_HEADS
    const = lambda b, t: (0, 0)
    return pl.pallas_call(
        _mla_up_kernel,
        grid=(B, nt),
        in_specs=[pl.BlockSpec((1, TM, MLA_Q_RANK), lambda b, t: (b, t, COL_QDN // MLA_Q_RANK)),
                  pl.BlockSpec((1, TM, MLA_KV_RANK), lambda b, t: (b, t, COL_KVDN // MLA_KV_RANK)),
                  pl.BlockSpec((1, TM, LANES), lambda b, t: (b, t, COL_MISC // LANES)),
                  pl.BlockSpec((1, TM, LANES), lambda b, t: (b, t, COL_SW // LANES)),
                  pl.BlockSpec((TM, LANES), lambda b, t: (t, 0)),
                  pl.BlockSpec((TM, LANES), lambda b, t: (t, 0)),
                  pl.BlockSpec(qg.shape, const),
                  pl.BlockSpec(kvg.shape, const),
                  pl.BlockSpec(wq.shape, const),
                  pl.BlockSpec(wkv.shape, const)],
        out_specs=[pl.BlockSpec((1, TM, H * MQ), lambda b, t: (b, t, 0)),
                   pl.BlockSpec((1, TM, H * MQ), lambda b, t: (b, t, 0)),
                   pl.BlockSpec((1, TM, H * MLA_V), lambda b, t: (b, t, 0))],
        out_shape=[jax.ShapeDtypeStruct((B, R, H * MQ), BF16),
                   jax.ShapeDtypeStruct((B, R, H * MQ), BF16),
                   jax.ShapeDtypeStruct((B, R, H * MLA_V), BF16)],
        compiler_params=_cparams("parallel", "parallel"),
        name="mla_up",
    )(z, z, z, z, cos, sin, qg, kvg, wq, wkv)


ATTN_HEADS = 4


def _attn_kernel(q_ref, k_ref, v_ref, o_ref, *, q_off):
    R = k_ref.shape[1]

    def attend(nk):
        for j in range(ATTN_HEADS):
            s = _dot_nt(q_ref[0, :, j * MQ:(j + 1) * MQ], k_ref[0, :nk, j * MQ:(j + 1) * MQ])
            p = jnp.exp2(s - jnp.max(s, axis=-1, keepdims=True))
            l = jnp.sum(p, axis=-1, keepdims=True)
            o = _dot(p.astype(BF16), v_ref[0, :nk, j * MLA_V:(j + 1) * MLA_V]) / l
            o_ref[0, :, j * MLA_V:(j + 1) * MLA_V] = o.astype(BF16)

    if q_off == 0:
        qi = pl.program_id(2)

        @pl.when(qi == 0)
        def _():
            attend(CTX_LEN)

        @pl.when(qi > 0)
        def _():
            attend(R)
    else:
        attend(R)


def _attn(q, k, v, *, with_ctx):
    B, R, _ = q.shape
    q_off = 0 if with_ctx else CTX_LEN // TM
    nq = R // TM - q_off
    G = ATTN_HEADS
    return pl.pallas_call(
        functools.partial(_attn_kernel, q_off=q_off),
        grid=(B, MLA_HEADS // G, nq),
        in_specs=[pl.BlockSpec((1, TM, G * MQ), lambda b, h, i: (b, i + q_off, h)),
                  pl.BlockSpec((1, R, G * MQ), lambda b, h, i: (b, 0, h)),
                  pl.BlockSpec((1, R, G * MLA_V), lambda b, h, i: (b, 0, h))],
        out_specs=pl.BlockSpec((1, TM, G * MLA_V), lambda b, h, i: (b, i + q_off, h)),
        out_shape=jax.ShapeDtypeStruct((B, R, MLA_HEADS * MLA_V), BF16),
        compiler_params=_cparams("parallel", "parallel", "arbitrary"),
        name="mla_attn",
    )(q, k, v)


def _merge_kernel(po_ref, go_ref, mo_ref, gz0_ref, gz1_ref, gz2_ref, x_ref, mod_ref, bw_ref, wo_ref, g_ref,
                  x1_ref, h2t_ref):
    D = D_MODEL
    m = None
    for i, (o_ref, gz_ref) in enumerate(((po_ref, gz0_ref), (go_ref, gz1_ref), (mo_ref, gz2_ref))):
        t = jax.nn.sigmoid(gz_ref[0].astype(F32)) * _dot(o_ref[0], bw_ref[i])
        m = t if m is None else m + t
    y = _dot(m.astype(BF16), wo_ref[...])
    mod = mod_ref[0]
    x1 = x_ref[0] + mod[:, 2 * D:3 * D] * y
    x1_ref[0] = x1
    h2 = _rms(x1, g_ref[...]) * (1.0 + mod[:, 4 * D:5 * D]) + mod[:, 3 * D:4 * D]
    h2t_ref[...] = pltpu.bitcast(h2.T.astype(BF16), jnp.int32)


def _merge(pool_o, gla_o, mla_o, z, xall, modrows, bw, wo, g, *, with_ctx):
    B, R, D = xall.shape
    t_off = 0 if with_ctx else CTX_LEN // TM
    nt = R // TM - t_off
    tok = lambda b, t: (b, t + t_off, 0)
    gzb = COL_GZ // D
    wpr = _words_per_row()
    mo_off = t_off if mla_o.shape[1] == R else 0
    return pl.pallas_call(
        _merge_kernel,
        grid=(B, nt),
        in_specs=[pl.BlockSpec((1, TM, D), tok),
                  pl.BlockSpec((1, TM, D), tok),
                  pl.BlockSpec((1, TM, D), lambda b, t: (b, t + mo_off, 0)),
                  pl.BlockSpec((1, TM, D), lambda b, t: (b, t + t_off, gzb)),
                  pl.BlockSpec((1, TM, D), lambda b, t: (b, t + t_off, gzb + 1)),
                  pl.BlockSpec((1, TM, D), lambda b, t: (b, t + t_off, gzb + 2)),
                  pl.BlockSpec((1, TM, D), tok),
                  pl.BlockSpec((1, 1, 6 * D), lambda b, t: (2 * b + jnp.minimum(t + t_off, 1), 0, 0)),
                  pl.BlockSpec(bw.shape, lambda b, t: (0, 0, 0)),
                  pl.BlockSpec(wo.shape, lambda b, t: (0, 0)),
                  pl.BlockSpec((1, D), lambda b, t: (0, 0))],
        out_specs=[pl.BlockSpec((1, TM, D), lambda b, t: (b, t, 0)),
                   pl.BlockSpec((D // wpr, TM), lambda b, t: (0, b * nt + t))],
        out_shape=[jax.ShapeDtypeStruct((B, nt * TM, D), F32),
                   jax.ShapeDtypeStruct((D // wpr, B * nt * TM), jnp.int32)],
        compiler_params=_cparams("parallel", "parallel"),
        name="merge",
    )(pool_o, gla_o, mla_o, z, z, z, xall, modrows, bw, wo, g)


def _top16(s):
    nk, n = s.shape
    iota = lax.broadcasted_iota(jnp.int32, (nk, n), 0).astype(F32)
    slot = lax.broadcasted_iota(jnp.int32, (PEER_TOPK, n), 0)
    rank = jnp.full((nk, n), float(PEER_TOPK), F32)
    vals = jnp.zeros((PEER_TOPK, n), F32)
    for j in range(PEER_TOPK):
        m = jnp.max(s, axis=0, keepdims=True)
        idx = jnp.min(jnp.where(s == m, iota, float(nk)), axis=0, keepdims=True)
        hit = iota == idx
        rank = jnp.where(hit, float(j), rank)
        s = jnp.where(hit, -jnp.inf, s)
        vals = jnp.where(slot == j, m, vals)
    return vals, rank


PAIR_SHORT = PEER_TOPK // 2
PAIR_ROWS = PEER_TOPK + (PEER_TOPK - 1) * PAIR_SHORT
PAIR_VALID = sum(PEER_TOPK // (i + 1) for i in range(PEER_TOPK))


def _pair_rows(t2):
    return jnp.concatenate([t2] + [t2[0:PAIR_SHORT]] * (PEER_TOPK - 1), axis=0)


def _pair_candidates(t1, t2):
    K = PEER_TOPK
    n = t1.shape[1]
    first = jnp.concatenate([jnp.broadcast_to(t1[0:1], (K, n))]
                            + [jnp.broadcast_to(t1[i:i + 1], (PAIR_SHORT, n)) for i in range(1, K)], axis=0)
    r = lax.broadcasted_iota(jnp.int32, (PAIR_ROWS, n), 0)
    sh = PAIR_SHORT.bit_length() - 1
    i_of = jnp.where(r < K, 0, ((r - K) >> sh) + 1)
    j_of = jnp.where(r < K, r, (r - K) & (PAIR_SHORT - 1))
    cand = jnp.where((i_of + 1) * (j_of + 1) <= K, first + _pair_rows(t2), -jnp.inf)
    return cand, (i_of * K + j_of).astype(F32)


def _top16_values(s, with_rank):
    n = s.shape[1]
    slot = lax.broadcasted_iota(jnp.int32, (PEER_TOPK, n), 0)
    vals = jnp.zeros((PEER_TOPK, n), F32)
    rank = jnp.full(s.shape, float(PEER_TOPK), F32) if with_rank else None
    for j in range(PEER_TOPK):
        m = jnp.max(s, axis=0, keepdims=True)
        hit = s == m
        if with_rank:
            rank = jnp.where(hit, float(j), rank)
        s = jnp.where(hit, -jnp.inf, s)
        vals = jnp.where(slot == j, m, vals)
    used = jnp.sum(jnp.where(s == -jnp.inf, 1.0, 0.0), axis=0, keepdims=True)
    return vals, rank, jnp.where(used == float(PEER_TOPK), 1.0, 0.0)


def _pair_counts(t1, t2):
    K = PEER_TOPK
    n = t1.shape[1]
    cand0, _ = _pair_candidates(t1, t2)
    cand = cand0
    zsum = jnp.zeros((1, n), F32)
    best = m = None
    for j in range(K):
        m = jnp.max(cand, axis=0, keepdims=True)
        cand = jnp.where(cand == m, -jnp.inf, cand)
        if j == 0:
            best = m
        zsum = zsum + jnp.exp(m - best)
    used = jnp.sum(jnp.where(cand == -jnp.inf, 1.0, 0.0), axis=0, keepdims=True)
    exact = jnp.where(used == float(PAIR_ROWS - PAIR_VALID + K), 1.0, 0.0)
    picked = jnp.where(cand0 >= m, 1.0, 0.0)
    slot = lax.broadcasted_iota(jnp.int32, (K, n), 0)
    cnt = jnp.zeros((K, n), F32)
    for i in range(K):
        blk = picked[0:K] if i == 0 else picked[K + (i - 1) * PAIR_SHORT:K + i * PAIR_SHORT]
        cnt = jnp.where(slot == i, jnp.sum(blk, axis=0, keepdims=True), cnt)
    return cnt, zsum, exact


def _pair_top16(t1, t2):
    K = PEER_TOPK
    n = t1.shape[1]
    cand, flat = _pair_candidates(t1, t2)
    irow = lax.broadcasted_iota(jnp.int32, (K, n), 0).astype(F32)
    cnt = jnp.zeros((K, n), F32)
    zsum = jnp.zeros((1, n), F32)
    best = None
    for j in range(K):
        m = jnp.max(cand, axis=0, keepdims=True)
        idx = jnp.min(jnp.where(cand == m, flat, float(K * K)), axis=0, keepdims=True)
        cand = jnp.where(flat == idx, -jnp.inf, cand)
        cnt = cnt + jnp.where(irow == jnp.floor(idx * (1.0 / K)), 1.0, 0.0)
        if j == 0:
            best = m
        zsum = zsum + jnp.exp(m - best)
    return cnt, zsum


def _words_per_row():
    return 4 // jnp.dtype(BF16).itemsize


def _splat_words(x):
    bits = lax.bitcast_convert_type(x.astype(BF16).astype(F32), jnp.int32)
    if _words_per_row() == 1:
        return bits
    return bits | lax.shift_right_logical(bits, 16)


def _peer_score_kernel(h2t_ref, wqt_ref, keys_ref, c2_ref, p2_ref, cnt_ref, ka_ref, qt_ref):
    K = PEER_TOPK
    qt_ref[...] = _dot(wqt_ref[...], pltpu.bitcast(h2t_ref[...], BF16)).astype(BF16)

    def store(h, cs, s1, s2, t1, rank1, t2, rank2, cnt, zsum):
        rowcnt = jnp.zeros_like(s1)
        for j in range(K):
            in_row = (s1 == t1[j:j + 1]) if rank1 is None else (rank1 == float(j))
            rowcnt = jnp.where(in_row, cnt[j:j + 1], rowcnt)
        c2_ref[h, :, cs] = pltpu.bitcast(rank2.astype(BF16), jnp.int32)
        p2_ref[h, :, cs] = pltpu.bitcast(jnp.exp(s2 - t2[0:1]).astype(BF16), jnp.int32)
        cnt_ref[h, :, cs] = _splat_words(rowcnt)
        ka_ref[h, :, cs] = _splat_words(jnp.exp(s1 - t1[0:1]) / zsum)

    def head(h, carry):
        base = pl.multiple_of(h * PEER_QDIM, PEER_QDIM)
        s1_all = _dot(keys_ref[h], qt_ref[pl.ds(base, PEER_HALF), :])
        s2_all = _dot(keys_ref[PEER_HEADS + h], qt_ref[pl.ds(base + PEER_HALF, PEER_HALF), :])
        chunks = []
        for c in range(s1_all.shape[1] // LANES):
            cs = slice(c * LANES, (c + 1) * LANES)
            s1, s2 = s1_all[:, cs], s2_all[:, cs]
            t1, _, ok1 = _top16_values(s1, with_rank=False)
            t2, rank2, ok2 = _top16_values(s2, with_rank=True)
            cnt, zsum, ok3 = _pair_counts(t1, t2)
            store(h, cs, s1, s2, t1, None, t2, rank2, cnt, zsum)
            chunks.append((cs, s1, s2, jnp.min(ok1 * ok2 * ok3)))

        for cs, s1, s2, exact in chunks:
            @pl.when(exact < 0.5)
            def _(cs=cs, s1=s1, s2=s2):
                t1x, rank1x = _top16(s1)
                t2x, rank2x = _top16(s2)
                cntx, zx = _pair_top16(t1x, t2x)
                store(h, cs, s1, s2, t1x, rank1x, t2x, rank2x, cntx, zx)

        return carry

    lax.fori_loop(0, PEER_HEADS, head, 0)


PEER_SCORE_TN = 2 * LANES


def _peer_score(h2t, wqt, keys):
    ntok = h2t.shape[1]
    tn = PEER_SCORE_TN
    nw = PEER_N_KEYS // _words_per_row()
    tbl = lambda rows: jax.ShapeDtypeStruct((PEER_HEADS, rows, ntok), jnp.int32)
    ospec = lambda rows: pl.BlockSpec((PEER_HEADS, rows, tn), lambda i: (0, 0, i))
    return pl.pallas_call(
        _peer_score_kernel,
        grid=(ntok // tn,),
        in_specs=[pl.BlockSpec((h2t.shape[0], tn), lambda i: (0, i)),
                  pl.BlockSpec(wqt.shape, lambda i: (0, 0)),
                  pl.BlockSpec(keys.shape, lambda i: (0, 0, 0))],
        out_specs=[ospec(nw), ospec(nw), ospec(PEER_N_KEYS), ospec(PEER_N_KEYS)],
        out_shape=[tbl(nw), tbl(nw), tbl(PEER_N_KEYS), tbl(PEER_N_KEYS)],
        scratch_shapes=[pltpu.VMEM((PEER_HEADS * PEER_QDIM, tn), BF16)],
        compiler_params=_cparams("parallel"),
        name="peer_score",
    )(h2t, wqt, keys)


PEER_KEYS_PER_STEP = 4
PEER_ETILE = PEER_KEYS_PER_STEP * PEER_N_KEYS
PEER_KEY_GROUP = 2


def _peer_dense_kernel(h2t_ref, c2_ref, p2_ref, cnt_ref, ka_ref, u_ref, vt_ref, x1_ref, modc_ref, modl_ref, fg_ref,
                       o_ref, acc_ref, pre_ref, hid_ref, *, ctx_rows, rows_per_sample, final):
    D = D_MODEL
    NK = PEER_N_KEYS
    tn = h2t_ref.shape[1]
    e = pl.program_id(1)

    @pl.when(e == 0)
    def _():
        acc_ref[...] = jnp.zeros_like(acc_ref)

    pre_ref[...] = _dot(pltpu.bitcast(u_ref[...], BF16), pltpu.bitcast(h2t_ref[...], BF16))

    row = lambda ref, i, h: ref[h, pl.ds(e * PEER_KEYS_PER_STEP + i, 1), :]
    cnt_rows = [[row(cnt_ref, i, h) for h in range(PEER_HEADS)] for i in range(PEER_KEYS_PER_STEP)]
    ka_rows = [[row(ka_ref, i, h) for h in range(PEER_HEADS)] for i in range(PEER_KEYS_PER_STEP)]
    wpr = _words_per_row()

    def key_rows(i, h, cs):
        rep = lambda words: pltpu.bitcast(jnp.broadcast_to(words[:, cs], (BF16_ROWS // wpr, LANES)), BF16)
        return rep(cnt_rows[i][h]), rep(ka_rows[i][h])

    for c in range(tn // LANES):
        cs = slice(c * LANES, (c + 1) * LANES)
        for g in range(0, PEER_KEYS_PER_STEP, PEER_KEY_GROUP):
            group = range(g, g + PEER_KEY_GROUP)
            w = {i: None for i in group}
            for h in range(PEER_HEADS):
                c2 = pltpu.bitcast(c2_ref[h, :, cs], BF16).reshape(NK // BF16_ROWS, BF16_ROWS, LANES)
                p2 = pltpu.bitcast(p2_ref[h, :, cs], BF16).reshape(NK // BF16_ROWS, BF16_ROWS, LANES)
                for i in group:
                    cnt, kaw = key_rows(i, h, cs)
                    t = jnp.where(c2 < cnt[None], p2, 0.0) * kaw[None]
                    w[i] = t if w[i] is None else w[i] + t
            for i in group:
                rows = slice(i * NK, (i + 1) * NK)
                x = pre_ref[rows, cs].astype(BF16)
                act = 0.5 * x * (1.0 + lax.erf(x * (2.0 ** -0.5)))
                hid_ref[rows, cs] = act * w[i].reshape(NK, LANES)

    acc_ref[...] += _dot(pltpu.bitcast(vt_ref[...], BF16), hid_ref[...])

    @pl.when(e == pl.num_programs(1) - 1)
    def _():
        y = acc_ref[...].T
        g_lat = modl_ref[0][:, 5 * D:6 * D]
        if ctx_rows:
            g_ctx = modc_ref[0][:, 5 * D:6 * D]
            tiles = rows_per_sample // tn
            r0 = (pl.program_id(0) % tiles) * tn
            row = r0 + lax.broadcasted_iota(jnp.int32, (tn, D), 0)
            gate = jnp.where(row < ctx_rows, g_ctx, g_lat)
        else:
            gate = g_lat
        x2 = x1_ref[...] + gate * y
        if final:
            x2 = _rms(x2, fg_ref[...])
        o_ref[...] = x2


def _peer_dense(h2t, sel, u, vt, x1, modrows, fg, *, tn, rows_per_sample, ctx_rows, final):
    ntok, D = x1.shape
    wpr = _words_per_row()
    ne = u.shape[0] * wpr // PEER_ETILE
    tiles = rows_per_sample // tn
    tspec = lambda t: pl.BlockSpec((PEER_HEADS, t.shape[1], tn), lambda i, s: (0, 0, i))
    return pl.pallas_call(
        functools.partial(_peer_dense_kernel, ctx_rows=ctx_rows, rows_per_sample=rows_per_sample, final=final),
        grid=(ntok // tn, ne),
        in_specs=[pl.BlockSpec((D // wpr, tn), lambda i, s: (0, i)),
                  tspec(sel[0]), tspec(sel[1]), tspec(sel[2]), tspec(sel[3]),
                  pl.BlockSpec((PEER_ETILE // wpr, D), lambda i, s: (s, 0)),
                  pl.BlockSpec((D // wpr, PEER_ETILE), lambda i, s: (0, s)),
                  pl.BlockSpec((tn, D), lambda i, s: (i, 0), pipeline_mode=pl.Buffered(1)),
                  pl.BlockSpec((1, 1, 6 * D), lambda i, s: (2 * (i // tiles), 0, 0)),
                  pl.BlockSpec((1, 1, 6 * D), lambda i, s: (2 * (i // tiles) + 1, 0, 0)),
                  pl.BlockSpec((1, D), lambda i, s: (0, 0))],
        out_specs=pl.BlockSpec((tn, D), lambda i, s: (i, 0)),
        out_shape=jax.ShapeDtypeStruct((ntok, D), F32),
        scratch_shapes=[pltpu.VMEM((D, tn), F32),
                        pltpu.VMEM((PEER_ETILE, tn), F32),
                        pltpu.VMEM((PEER_ETILE, tn), BF16)],
        compiler_params=_cparams("parallel", "arbitrary"),
        name="peer_dense",
    )(h2t, *sel, u, vt, x1, modrows, modrows, fg)


def _pack_rows_kernel(x_ref, o_ref):
    o_ref[...] = pltpu.bitcast(x_ref[...].astype(BF16), jnp.int32)


def _pack_rows(x):
    R, C = x.shape
    wpr = _words_per_row()
    br, bc = 512, 1024
    return pl.pallas_call(
        _pack_rows_kernel,
        grid=(R // br, C // bc),
        in_specs=[pl.BlockSpec((br, bc), lambda i, j: (i, j))],
        out_specs=pl.BlockSpec((br // wpr, bc), lambda i, j: (i, j)),
        out_shape=jax.ShapeDtypeStruct((R // wpr, C), jnp.int32),
        compiler_params=_cparams("parallel", "parallel"),
        name="pack_rows",
    )(x)


def _peer_tile(rows_per_sample):
    for tn in (1024, 768, 512, 256):
        if rows_per_sample % tn == 0:
            return tn
    raise ValueError(rows_per_sample)


_SWAP = np.concatenate([np.arange(16, 32), np.arange(0, 16), np.arange(48, 64), np.arange(32, 48)])


def _layout_w_in(w):
    D = w.shape[0]
    idx = np.cumsum(IN_SPLITS)[:-1]
    p, q, k, v, r, lrf, lrb, qdn, kvdn, krope, gz = jnp.split(w, idx, axis=1)
    zeros = lambda n: jnp.zeros((D, n), w.dtype)
    misc = jnp.concatenate([krope, lrf, lrb, zeros(LANES - MLA_ROPE - 2 * GLA_GATE_RANK)], axis=1)
    sw = jnp.concatenate([krope[:, _SWAP], zeros(LANES - MLA_ROPE)], axis=1)
    out = jnp.concatenate([p, q, k, v, r, gz, kvdn, misc, sw, qdn], axis=1)
    return jnp.concatenate([out, zeros(Z_COLS - out.shape[1])], axis=1).astype(BF16)


def _layout_gate_w(gate_w, row0):
    HK = GLA_HEADS * GLA_DK
    pad = jnp.zeros((LANES, HK), gate_w.dtype)
    return pad.at[row0:row0 + GLA_GATE_RANK].set(gate_w).astype(BF16)


def _layout_w_uq(w):
    H = MLA_HEADS
    w3 = w.reshape(MLA_Q_RANK, H, MLA_NOPE + MLA_ROPE)
    qn, qr = w3[..., :MLA_NOPE], w3[..., MLA_NOPE:]
    z = jnp.zeros((MLA_Q_RANK, H, LANES - MLA_ROPE), w.dtype)
    main = jnp.concatenate([qn, qr, z], axis=-1).reshape(MLA_Q_RANK, H * MQ)
    swp = jnp.concatenate([qr[..., _SWAP], z], axis=-1).reshape(MLA_Q_RANK, H * LANES)
    return jnp.concatenate([main, swp], axis=1).astype(BF16)


def _rope_tables(seq):
    half = MLA_ROPE // 2
    t = jnp.arange(seq)
    inv = ROPE_BASE ** (-jnp.arange(0, half, 2, dtype=F32) / half)
    ar = (t // GRID_W).astype(F32)[:, None] * inv
    ac = (t % GRID_W).astype(F32)[:, None] * inv
    cos = jnp.concatenate([jnp.cos(ar), jnp.cos(ar), jnp.cos(ac), jnp.cos(ac)], axis=1)
    sin = jnp.concatenate([-jnp.sin(ar), jnp.sin(ar), -jnp.sin(ac), jnp.sin(ac)], axis=1)
    cos = jnp.concatenate([jnp.ones((CTX_LEN, MLA_ROPE), F32), cos], axis=0)
    sin = jnp.concatenate([jnp.zeros((CTX_LEN, MLA_ROPE), F32), sin], axis=0)
    pad = jnp.zeros((CTX_LEN + seq, LANES - MLA_ROPE), F32)
    return jnp.concatenate([cos, pad], axis=1), jnp.concatenate([sin, pad], axis=1)


def kernel(x, c, ctx, c_ctx, ada_w, ada_b, norm1_g, norm2_g, w_in, pool_w, pool_scale, gla_gate_w, gla_gate_b, gla_norm_g, mla_q_norm_g, mla_kv_norm_g, mla_w_uq, mla_w_ukv, branch_w, w_out, peer_wq, peer_keys, peer_u, peer_v, final_norm_g):
    B, T, D = x.shape
    R = CTX_LEN + T
    assert D == D_MODEL and ctx.shape[1] == CTX_LEN == TM and T % TM == 0 and T % GRID_W == 0

    nrow = -(-(B + 1) // 8) * 8
    cc = jnp.concatenate([c, c_ctx[None], jnp.zeros((nrow - B - 1, D), F32)], axis=0)
    mod = _ada(cc, ada_w, ada_b)
    modrows = jnp.stack([jnp.broadcast_to(mod[:, B:B + 1], (DEPTH, B, 6 * D)), mod[:, :B]], axis=2)
    modrows = modrows.reshape(DEPTH, 2 * B, 1, 6 * D)

    cos, sin = _rope_tables(T)
    xall = jnp.concatenate([ctx, x], axis=1)
    row = lambda a: a.reshape(1, -1)

    for i in range(DEPTH):
        with_ctx = i < DEPTH - 1
        final = i == DEPTH - 1
        z = _in_proj(xall, modrows[i], row(norm1_g[i]), _layout_w_in(w_in[i]))

        pool_o = _pool(z, pool_w[i].astype(BF16), row(pool_scale[i]))

        o_b = _gla(z, _layout_gate_w(gla_gate_w[i, 1], MISC_LRB), row(gla_gate_b[i, 1]), fwd=False)
        gla_o = _gla(z, _layout_gate_w(gla_gate_w[i, 0], MISC_LRF), row(gla_gate_b[i, 0]), fwd=True,
                     ob=o_b, gn=row(gla_norm_g[i]))

        q, k, v = _mla_up(z, cos, sin, row(mla_q_norm_g[i]), row(mla_kv_norm_g[i]),
                          _layout_w_uq(mla_w_uq[i]), mla_w_ukv[i].astype(BF16))
        mla_o = _attn(q, k, v, with_ctx=with_ctx)

        x1, h2t = _merge(pool_o, gla_o, mla_o, z, xall, modrows[i], branch_w[i].astype(BF16),
                        w_out[i].astype(BF16), row(norm2_g[i]), with_ctx=with_ctx)

        rows = x1.shape[1]
        keys = peer_keys[i].reshape(2 * PEER_HEADS, PEER_N_KEYS, PEER_HALF).astype(BF16)
        sel = _peer_score(h2t, peer_wq[i].T.astype(BF16), keys)
        xall = _peer_dense(h2t, sel, _pack_rows(peer_u[i]), _pack_rows(peer_v[i].T), x1.reshape(B * rows, D),
                           modrows[i], row(final_norm_g), tn=_peer_tile(rows), rows_per_sample=rows,
                           ctx_rows=CTX_LEN if with_ctx else 0, final=final).reshape(B, rows, D)
    return xall
```

```python
import functools

import jax
import jax.numpy as jnp
import numpy as np
from jax import lax
from jax.experimental import pallas as pl
from jax.experimental.pallas import tpu as pltpu

F32 = jnp.float32
BF16 = jnp.bfloat16

D_MODEL = 1024
DEPTH = 2
CTX_LEN = 256
GRID_W = 64
NORM_EPS = 1e-6
POOL_WIDTH = 1024
POOL_WINDOWS = (2, 4, 8, 16)
POOL_GROUP = POOL_WIDTH // len(POOL_WINDOWS)
GLA_HEADS = 4
GLA_DK = 128
GLA_DV = 256
GLA_GATE_RANK = 16
GLA_TAU = 16.0
GLA_CHUNK = 64
MLA_HEADS = 8
MLA_Q_RANK = 384
MLA_KV_RANK = 256
MLA_NOPE = 128
MLA_ROPE = 64
MLA_V = 128
ROPE_BASE = 10000.0
N_BRANCH = 3
PEER_HEADS = 8
PEER_N_KEYS = 128
PEER_TOPK = 16
PEER_QDIM = 256
PEER_HALF = PEER_QDIM // 2
IN_SPLITS = (POOL_WIDTH, GLA_HEADS * GLA_DK, GLA_HEADS * GLA_DK, GLA_HEADS * GLA_DV, GLA_HEADS * GLA_DV,
             GLA_GATE_RANK, GLA_GATE_RANK, MLA_Q_RANK, MLA_KV_RANK, MLA_ROPE, N_BRANCH * D_MODEL)

LANES = 128
BF16_ROWS = 16
TM = 256
VMEM_LIMIT = 56 * 1024 * 1024

COL_P = 0
COL_Q = 1024
COL_K = 1536
COL_V = 2048
COL_R = 3072
COL_GZ = 4096
COL_KVDN = 7168
COL_MISC = 7424
COL_SW = 7552
COL_QDN = 7680
Z_COLS = 8192
IN_NBLK = 4096
MISC_LRF = MLA_ROPE
MISC_LRB = MLA_ROPE + GLA_GATE_RANK
MQ = MLA_NOPE + LANES


def _cparams(*sem):
    return pltpu.CompilerParams(dimension_semantics=sem, vmem_limit_bytes=VMEM_LIMIT)


def _rms(x, g):
    return x * lax.rsqrt(jnp.mean(x * x, axis=-1, keepdims=True) + NORM_EPS) * g


def _dot(a, b):
    return jnp.dot(a, b, preferred_element_type=F32)


def _dot_nt(a, b):
    return lax.dot_general(a, b, (((1,), (1,)), ((), ())), preferred_element_type=F32)


def _dot_tn(a, b):
    return lax.dot_general(a, b, (((0,), (0,)), ((), ())), preferred_element_type=F32)


def _ada_kernel(c_ref, w_ref, b_ref, o_ref):
    c = c_ref[...]
    a = (c * jax.nn.sigmoid(c)).astype(BF16)
    o_ref[0] = _dot(a, w_ref[0].astype(BF16)) + b_ref[0]


def _ada(cc, ada_w, ada_b):
    L, D, N = ada_w.shape
    rows = cc.shape[0]
    nb = N // D
    return pl.pallas_call(
        _ada_kernel,
        grid=(L, nb),
        in_specs=[pl.BlockSpec((rows, D), lambda l, j: (0, 0)),
                  pl.BlockSpec((1, D, D), lambda l, j: (l, 0, j)),
                  pl.BlockSpec((1, 1, D), lambda l, j: (l, 0, j))],
        out_specs=pl.BlockSpec((1, rows, D), lambda l, j: (l, 0, j)),
        out_shape=jax.ShapeDtypeStruct((L, rows, N), F32),
        compiler_params=_cparams("parallel", "parallel"),
        name="ada_mod",
    )(cc, ada_w, ada_b.reshape(L, 1, N))


def _in_proj_kernel(x_ref, mod_ref, g_ref, w_ref, z_ref):
    D = D_MODEL
    x = x_ref[0]
    mod = mod_ref[0]
    h = _rms(x, g_ref[...]) * (1.0 + mod[:, D:2 * D]) + mod[:, 0:D]
    z_ref[0] = _dot(h.astype(BF16), w_ref[...]).astype(BF16)


def _in_proj(xall, modrows, g, w):
    B, R, D = xall.shape
    nt = R // TM
    nn = Z_COLS // IN_NBLK
    return pl.pallas_call(
        _in_proj_kernel,
        grid=(nn, B, nt),
        in_specs=[pl.BlockSpec((1, TM, D), lambda n, b, t: (b, t, 0)),
                  pl.BlockSpec((1, 1, 6 * D), lambda n, b, t: (2 * b + jnp.minimum(t, 1), 0, 0)),
                  pl.BlockSpec((1, D), lambda n, b, t: (0, 0)),
                  pl.BlockSpec((D, IN_NBLK), lambda n, b, t: (0, n))],
        out_specs=pl.BlockSpec((1, TM, IN_NBLK), lambda n, b, t: (b, t, n)),
        out_shape=jax.ShapeDtypeStruct((B, R, Z_COLS), BF16),
        compiler_params=_cparams("parallel", "parallel", "parallel"),
        name="in_proj",
    )(xall, modrows, g, w)


def _pool_kernel(u_ref, w_ref, sc_ref, o_ref):
    R = u_ref.shape[1]
    row = lax.broadcasted_iota(jnp.int32, (R, POOL_GROUP), 0)
    seg_lo = jnp.where(row < CTX_LEN, 0, CTX_LEN)
    seg_hi = jnp.where(row < CTX_LEN, CTX_LEN, R)
    for gi, win in enumerate(POOL_WINDOWS):
        sl = slice(gi * POOL_GROUP, (gi + 1) * POOL_GROUP)
        u = u_ref[0, :, sl].astype(F32)
        lo_off, hi_off = win // 2, win - win // 2
        acc = jnp.zeros_like(u)
        for d in range(-lo_off, hi_off):
            shifted = u if d == 0 else pltpu.roll(u, (R - d) % R, axis=0)
            ok = (row + d >= seg_lo) & (row + d < seg_hi)
            acc = acc + jnp.where(ok, shifted, 0.0)
        cnt = (jnp.minimum(row + hi_off, seg_hi) - jnp.maximum(row - lo_off, seg_lo)).astype(F32)
        diff = (acc / cnt - u).astype(BF16)
        o_ref[0, :, sl] = (_dot(diff, w_ref[gi]) * sc_ref[:, sl]).astype(BF16)


def _pool(z, pool_w, pool_scale):
    B, R, _ = z.shape
    return pl.pallas_call(
        _pool_kernel,
        grid=(B,),
        in_specs=[pl.BlockSpec((1, R, POOL_WIDTH), lambda b: (b, 0, COL_P // POOL_WIDTH)),
                  pl.BlockSpec(pool_w.shape, lambda b: (0, 0, 0)),
                  pl.BlockSpec((1, POOL_WIDTH), lambda b: (0, 0))],
        out_specs=pl.BlockSpec((1, R, POOL_WIDTH), lambda b: (b, 0, 0)),
        out_shape=jax.ShapeDtypeStruct((B, R, POOL_WIDTH), BF16),
        compiler_params=_cparams("parallel"),
        name="pool",
    )(z, pool_w, pool_scale)


def _split3(x):
    a = x.astype(BF16)
    r = x - a.astype(F32)
    b = r.astype(BF16)
    c = (r - b.astype(F32)).astype(BF16)
    return a, b, c


def _gla_kernel(*refs, fwd):
    if fwd:
        q_ref, k_ref, v_ref, misc_ref, gw_ref, gb_ref, ob_ref, r_ref, gn_ref, o_ref, st_ref = refs
    else:
        q_ref, k_ref, v_ref, misc_ref, gw_ref, gb_ref, o_ref, st_ref = refs
    C = GLA_CHUNK
    nchunk = TM // C
    HK = GLA_HEADS * GLA_DK

    @pl.when(pl.program_id(1) == 0)
    def _():
        st_ref[...] = jnp.zeros_like(st_ref)

    zg = _dot(misc_ref[0], gw_ref[...]) + gb_ref[...]
    g = (jnp.minimum(zg, 0.0) - jnp.log1p(jnp.exp(-jnp.abs(zg)))) * (1.0 / GLA_TAU)

    row = lax.broadcasted_iota(jnp.int32, (TM, TM), 0)
    col = lax.broadcasted_iota(jnp.int32, (TM, TM), 1)
    shift = C.bit_length() - 1
    same = (row >> shift) == (col >> shift)
    tri = same & ((col <= row) if fwd else (col >= row))
    ones = jnp.concatenate([jnp.where(tri, 1.0, 0.0), jnp.where(same, 1.0, 0.0)], axis=0).astype(BF16)
    g1, g2, g3 = _split3(g)
    both = _dot(ones, g1) + _dot(ones, g2) + _dot(ones, g3)
    b = both[:TM]
    bt = both[TM:]

    qe = q_ref[0].astype(F32) * (GLA_DK ** -0.5) * jnp.exp(b)
    ke = (k_ref[0].astype(F32) * jnp.exp(-b)).astype(BF16)
    kd = (k_ref[0].astype(F32) * jnp.exp(bt - b)).astype(BF16)
    qe = qe.astype(BF16)
    dec = jnp.exp(bt)
    v = v_ref[0]

    order = range(nchunk) if fwd else range(nchunk - 1, -1, -1)
    for h in range(GLA_HEADS):
        ks = slice(h * GLA_DK, (h + 1) * GLA_DK)
        vs = slice(h * GLA_DV, (h + 1) * GLA_DV)
        att = jnp.where(tri, _dot_nt(qe[:, ks], ke[:, ks]), 0.0).astype(BF16)
        o_in = _dot(att, v[:, vs])
        st = st_ref[h]
        parts = [None] * nchunk
        for j in order:
            rs = slice(j * C, (j + 1) * C)
            parts[j] = o_in[rs] + _dot_nt(qe[rs, ks], st.astype(BF16))
            st = st * dec[j * C:j * C + 1, ks] + _dot_tn(v[rs, vs], kd[rs, ks])
        st_ref[h] = st
        o = jnp.concatenate(parts, axis=0)
        if fwd:
            o = o + ob_ref[0, :, vs].astype(F32)
            r = r_ref[0, :, vs].astype(F32)
            o = _rms(o, gn_ref[...]) * (r * jax.nn.sigmoid(r))
        o_ref[0, :, vs] = o.astype(BF16)


def _gla(z, gw, gb, *, fwd, ob=None, gn=None):
    B, R, _ = z.shape
    nt = R // TM
    HK, HV = GLA_HEADS * GLA_DK, GLA_HEADS * GLA_DV
    if fwd:
        tile = lambda c: c
    else:
        tile = lambda c: jnp.where(c == 0, 0, nt - c)
    in_specs = [pl.BlockSpec((1, TM, HK), lambda b, c: (b, tile(c), COL_Q // HK)),
                pl.BlockSpec((1, TM, HK), lambda b, c: (b, tile(c), COL_K // HK)),
                pl.BlockSpec((1, TM, HV), lambda b, c: (b, tile(c), COL_V // HV)),
                pl.BlockSpec((1, TM, LANES), lambda b, c: (b, tile(c), COL_MISC // LANES)),
                pl.BlockSpec((LANES, HK), lambda b, c: (0, 0)),
                pl.BlockSpec((1, HK), lambda b, c: (0, 0))]
    args = [z, z, z, z, gw, gb]
    if fwd:
        in_specs += [pl.BlockSpec((1, TM, HV), lambda b, c: (b, tile(c), 0)),
                     pl.BlockSpec((1, TM, HV), lambda b, c: (b, tile(c), COL_R // HV)),
                     pl.BlockSpec((1, GLA_DV), lambda b, c: (0, 0))]
        args += [ob, z, gn]
    return pl.pallas_call(
        functools.partial(_gla_kernel, fwd=fwd),
        grid=(B, nt),
        in_specs=in_specs,
        out_specs=pl.BlockSpec((1, TM, HV), lambda b, c: (b, tile(c), 0)),
        out_shape=jax.ShapeDtypeStruct((B, R, HV), BF16),
        scratch_shapes=[pltpu.VMEM((GLA_HEADS, GLA_DV, GLA_DK), F32)],
        compiler_params=_cparams("parallel", "arbitrary"),
        name="gla_fwd" if fwd else "gla_bwd",
    )(*args)


def _mla_up_kernel(qdn_ref, kvdn_ref, misc_ref, sw_ref, cos_ref, sin_ref, qg_ref, kvg_ref, wq_ref, wkv_ref,
                   q_ref, k_ref, v_ref):
    H = MLA_HEADS
    scale = (MLA_NOPE + MLA_ROPE) ** -0.5 * float(np.log2(np.e))
    cos = cos_ref[...]
    sin = sin_ref[...]
    qn = _rms(qdn_ref[0].astype(F32), qg_ref[...]).astype(BF16)
    qall = _dot(qn, wq_ref[...])
    for h in range(H):
        base = h * MQ
        q_ref[0, :, base:base + MLA_NOPE] = (qall[:, base:base + MLA_NOPE] * scale).astype(BF16)
        rot = (qall[:, base + MLA_NOPE:base + MQ] * cos
               + qall[:, H * MQ + h * LANES:H * MQ + (h + 1) * LANES] * sin)
        q_ref[0, :, base + MLA_NOPE:base + MQ] = (rot * scale).astype(BF16)
    kvn = _rms(kvdn_ref[0].astype(F32), kvg_ref[...]).astype(BF16)
    kvall = _dot(kvn, wkv_ref[...])
    kr = (misc_ref[0].astype(F32) * cos + sw_ref[0].astype(F32) * sin).astype(BF16)
    for h in range(H):
        src = h * (MLA_NOPE + MLA_V)
        k_ref[0, :, h * MQ:h * MQ + MLA_NOPE] = kvall[:, src:src + MLA_NOPE].astype(BF16)
        k_ref[0, :, h * MQ + MLA_NOPE:(h + 1) * MQ] = kr
        v_ref[0, :, h * MLA_V:(h + 1) * MLA_V] = kvall[:, src + MLA_NOPE:src + MLA_NOPE + MLA_V].astype(BF16)


def _mla_up(z, cos, sin, qg, kvg, wq, wkv):
    B, R, _ = z.shape
    nt = R // TM
    H = MLA_HEADS
    const = lambda b, t: (0, 0)
    return pl.pallas_call(
        _mla_up_kernel,
        grid=(B, nt),
        in_specs=[pl.BlockSpec((1, TM, MLA_Q_RANK), lambda b, t: (b, t, COL_QDN // MLA_Q_RANK)),
                  pl.BlockSpec((1, TM, MLA_KV_RANK), lambda b, t: (b, t, COL_KVDN // MLA_KV_RANK)),
                  pl.BlockSpec((1, TM, LANES), lambda b, t: (b, t, COL_MISC // LANES)),
                  pl.BlockSpec((1, TM, LANES), lambda b, t: (b, t, COL_SW // LANES)),
                  pl.BlockSpec((TM, LANES), lambda b, t: (t, 0)),
                  pl.BlockSpec((TM, LANES), lambda b, t: (t, 0)),
                  pl.BlockSpec(qg.shape, const),
                  pl.BlockSpec(kvg.shape, const),
                  pl.BlockSpec(wq.shape, const),
                  pl.BlockSpec(wkv.shape, const)],
        out_specs=[pl.BlockSpec((1, TM, H * MQ), lambda b, t: (b, t, 0)),
                   pl.BlockSpec((1, TM, H * MQ), lambda b, t: (b, t, 0)),
                   pl.BlockSpec((1, TM, H * MLA_V), lambda b, t: (b, t, 0))],
        out_shape=[jax.ShapeDtypeStruct((B, R, H * MQ), BF16),
                   jax.ShapeDtypeStruct((B, R, H * MQ), BF16),
                   jax.ShapeDtypeStruct((B, R, H * MLA_V), BF16)],
        compiler_params=_cparams("parallel", "parallel"),
        name="mla_up",
    )(z, z, z, z, cos, sin, qg, kvg, wq, wkv)


ATTN_HEADS = 8


def _attn_kernel(q_ref, k_ref, v_ref, o_ref, *, q_off):
    R = k_ref.shape[1]

    def attend(nk):
        for j in range(ATTN_HEADS):
            s = _dot_nt(q_ref[0, :, j * MQ:(j + 1) * MQ], k_ref[0, :nk, j * MQ:(j + 1) * MQ])
            p = jnp.exp2(s - jnp.max(s, axis=-1, keepdims=True))
            l = jnp.sum(p, axis=-1, keepdims=True)
            o = _dot(p.astype(BF16), v_ref[0, :nk, j * MLA_V:(j + 1) * MLA_V]) / l
            o_ref[0, :, j * MLA_V:(j + 1) * MLA_V] = o.astype(BF16)

    if q_off == 0:
        qi = pl.program_id(2)

        @pl.when(qi == 0)
        def _():
            attend(CTX_LEN)

        @pl.when(qi > 0)
        def _():
            attend(R)
    else:
        attend(R)


def _attn(q, k, v, *, with_ctx):
    B, R, _ = q.shape
    q_off = 0 if with_ctx else CTX_LEN // TM
    nq = R // TM - q_off
    G = ATTN_HEADS
    return pl.pallas_call(
        functools.partial(_attn_kernel, q_off=q_off),
        grid=(B, MLA_HEADS // G, nq),
        in_specs=[pl.BlockSpec((1, TM, G * MQ), lambda b, h, i: (b, i + q_off, h)),
                  pl.BlockSpec((1, R, G * MQ), lambda b, h, i: (b, 0, h)),
                  pl.BlockSpec((1, R, G * MLA_V), lambda b, h, i: (b, 0, h))],
        out_specs=pl.BlockSpec((1, TM, G * MLA_V), lambda b, h, i: (b, i + q_off, h)),
        out_shape=jax.ShapeDtypeStruct((B, R, MLA_HEADS * MLA_V), BF16),
        compiler_params=_cparams("parallel", "parallel", "arbitrary"),
        name="mla_attn",
    )(q, k, v)


def _merge_kernel(po_ref, go_ref, mo_ref, gz0_ref, gz1_ref, gz2_ref, x_ref, mod_ref, bw_ref, wo_ref, g_ref,
                  x1_ref, h2t_ref):
    D = D_MODEL
    m = None
    for i, (o_ref, gz_ref) in enumerate(((po_ref, gz0_ref), (go_ref, gz1_ref), (mo_ref, gz2_ref))):
        t = jax.nn.sigmoid(gz_ref[0].astype(F32)) * _dot(o_ref[0], bw_ref[i])
        m = t if m is None else m + t
    y = _dot(m.astype(BF16), wo_ref[...])
    mod = mod_ref[0]
    x1 = x_ref[0] + mod[:, 2 * D:3 * D] * y
    x1_ref[0] = x1
    h2 = _rms(x1, g_ref[...]) * (1.0 + mod[:, 4 * D:5 * D]) + mod[:, 3 * D:4 * D]
    h2t_ref[...] = pltpu.bitcast(h2.T.astype(BF16), jnp.int32)


def _merge(pool_o, gla_o, mla_o, z, xall, modrows, bw, wo, g, *, with_ctx):
    B, R, D = xall.shape
    t_off = 0 if with_ctx else CTX_LEN // TM
    nt = R // TM - t_off
    tok = lambda b, t: (b, t + t_off, 0)
    gzb = COL_GZ // D
    wpr = _words_per_row()
    mo_off = t_off if mla_o.shape[1] == R else 0
    return pl.pallas_call(
        _merge_kernel,
        grid=(B, nt),
        in_specs=[pl.BlockSpec((1, TM, D), tok),
                  pl.BlockSpec((1, TM, D), tok),
                  pl.BlockSpec((1, TM, D), lambda b, t: (b, t + mo_off, 0)),
                  pl.BlockSpec((1, TM, D), lambda b, t: (b, t + t_off, gzb)),
                  pl.BlockSpec((1, TM, D), lambda b, t: (b, t + t_off, gzb + 1)),
                  pl.BlockSpec((1, TM, D), lambda b, t: (b, t + t_off, gzb + 2)),
                  pl.BlockSpec((1, TM, D), tok),
                  pl.BlockSpec((1, 1, 6 * D), lambda b, t: (2 * b + jnp.minimum(t + t_off, 1), 0, 0)),
                  pl.BlockSpec(bw.shape, lambda b, t: (0, 0, 0)),
                  pl.BlockSpec(wo.shape, lambda b, t: (0, 0)),
                  pl.BlockSpec((1, D), lambda b, t: (0, 0))],
        out_specs=[pl.BlockSpec((1, TM, D), lambda b, t: (b, t, 0)),
                   pl.BlockSpec((D // wpr, TM), lambda b, t: (0, b * nt + t))],
        out_shape=[jax.ShapeDtypeStruct((B, nt * TM, D), F32),
                   jax.ShapeDtypeStruct((D // wpr, B * nt * TM), jnp.int32)],
        compiler_params=_cparams("parallel", "parallel"),
        name="merge",
    )(pool_o, gla_o, mla_o, z, z, z, xall, modrows, bw, wo, g)


def _top16(s):
    nk, n = s.shape
    iota = lax.broadcasted_iota(jnp.int32, (nk, n), 0).astype(F32)
    slot = lax.broadcasted_iota(jnp.int32, (PEER_TOPK, n), 0)
    rank = jnp.full((nk, n), float(PEER_TOPK), F32)
    vals = jnp.zeros((PEER_TOPK, n), F32)
    for j in range(PEER_TOPK):
        m = jnp.max(s, axis=0, keepdims=True)
        idx = jnp.min(jnp.where(s == m, iota, float(nk)), axis=0, keepdims=True)
        hit = iota == idx
        rank = jnp.where(hit, float(j), rank)
        s = jnp.where(hit, -jnp.inf, s)
        vals = jnp.where(slot == j, m, vals)
    return vals, rank


PAIR_SHORT = PEER_TOPK // 2
PAIR_ROWS = PEER_TOPK + (PEER_TOPK - 1) * PAIR_SHORT
PAIR_VALID = sum(PEER_TOPK // (i + 1) for i in range(PEER_TOPK))


def _pair_rows(t2):
    return jnp.concatenate([t2] + [t2[0:PAIR_SHORT]] * (PEER_TOPK - 1), axis=0)


def _pair_candidates(t1, t2):
    K = PEER_TOPK
    n = t1.shape[1]
    first = jnp.concatenate([jnp.broadcast_to(t1[0:1], (K, n))]
                            + [jnp.broadcast_to(t1[i:i + 1], (PAIR_SHORT, n)) for i in range(1, K)], axis=0)
    r = lax.broadcasted_iota(jnp.int32, (PAIR_ROWS, n), 0)
    sh = PAIR_SHORT.bit_length() - 1
    i_of = jnp.where(r < K, 0, ((r - K) >> sh) + 1)
    j_of = jnp.where(r < K, r, (r - K) & (PAIR_SHORT - 1))
    cand = jnp.where((i_of + 1) * (j_of + 1) <= K, first + _pair_rows(t2), -jnp.inf)
    return cand, (i_of * K + j_of).astype(F32)


def _top16_values(s, with_rank):
    n = s.shape[1]
    slot = lax.broadcasted_iota(jnp.int32, (PEER_TOPK, n), 0)
    vals = jnp.zeros((PEER_TOPK, n), F32)
    rank = jnp.full(s.shape, float(PEER_TOPK), F32) if with_rank else None
    for j in range(PEER_TOPK):
        m = jnp.max(s, axis=0, keepdims=True)
        hit = s == m
        if with_rank:
            rank = jnp.where(hit, float(j), rank)
        s = jnp.where(hit, -jnp.inf, s)
        vals = jnp.where(slot == j, m, vals)
    used = jnp.sum(jnp.where(s == -jnp.inf, 1.0, 0.0), axis=0, keepdims=True)
    return vals, rank, jnp.where(used == float(PEER_TOPK), 1.0, 0.0)


def _pair_counts(t1, t2):
    K = PEER_TOPK
    n = t1.shape[1]
    cand0, _ = _pair_candidates(t1, t2)
    cand = cand0
    zsum = jnp.zeros((1, n), F32)
    best = m = None
    for j in range(K):
        m = jnp.max(cand, axis=0, keepdims=True)
        cand = jnp.where(cand == m, -jnp.inf, cand)
        if j == 0:
            best = m
        zsum = zsum + jnp.exp(m - best)
    used = jnp.sum(jnp.where(cand == -jnp.inf, 1.0, 0.0), axis=0, keepdims=True)
    exact = jnp.where(used == float(PAIR_ROWS - PAIR_VALID + K), 1.0, 0.0)
    picked = jnp.where(cand0 >= m, 1.0, 0.0)
    slot = lax.broadcasted_iota(jnp.int32, (K, n), 0)
    cnt = jnp.zeros((K, n), F32)
    for i in range(K):
        blk = picked[0:K] if i == 0 else picked[K + (i - 1) * PAIR_SHORT:K + i * PAIR_SHORT]
        cnt = jnp.where(slot == i, jnp.sum(blk, axis=0, keepdims=True), cnt)
    return cnt, zsum, exact


def _pair_top16(t1, t2):
    K = PEER_TOPK
    n = t1.shape[1]
    cand, flat = _pair_candidates(t1, t2)
    irow = lax.broadcasted_iota(jnp.int32, (K, n), 0).astype(F32)
    cnt = jnp.zeros((K, n), F32)
    zsum = jnp.zeros((1, n), F32)
    best = None
    for j in range(K):
        m = jnp.max(cand, axis=0, keepdims=True)
        idx = jnp.min(jnp.where(cand == m, flat, float(K * K)), axis=0, keepdims=True)
        cand = jnp.where(flat == idx, -jnp.inf, cand)
        cnt = cnt + jnp.where(irow == jnp.floor(idx * (1.0 / K)), 1.0, 0.0)
        if j == 0:
            best = m
        zsum = zsum + jnp.exp(m - best)
    return cnt, zsum


def _words_per_row():
    return 4 // jnp.dtype(BF16).itemsize


def _splat_words(x):
    bits = lax.bitcast_convert_type(x.astype(BF16).astype(F32), jnp.int32)
    if _words_per_row() == 1:
        return bits
    return bits | lax.shift_right_logical(bits, 16)


def _peer_score_kernel(h2t_ref, wqt_ref, keys_ref, c2_ref, p2_ref, cnt_ref, ka_ref, qt_ref):
    K = PEER_TOPK
    qt_ref[...] = _dot(wqt_ref[...], pltpu.bitcast(h2t_ref[...], BF16)).astype(BF16)

    def store(h, cs, s1, s2, t1, rank1, t2, rank2, cnt, zsum):
        rowcnt = jnp.zeros_like(s1)
        for j in range(K):
            in_row = (s1 == t1[j:j + 1]) if rank1 is None else (rank1 == float(j))
            rowcnt = jnp.where(in_row, cnt[j:j + 1], rowcnt)
        c2_ref[h, :, cs] = pltpu.bitcast(rank2.astype(BF16), jnp.int32)
        p2_ref[h, :, cs] = pltpu.bitcast(jnp.exp(s2 - t2[0:1]).astype(BF16), jnp.int32)
        cnt_ref[h, :, cs] = _splat_words(rowcnt)
        ka_ref[h, :, cs] = _splat_words(jnp.exp(s1 - t1[0:1]) / zsum)

    def head(h, carry):
        base = pl.multiple_of(h * PEER_QDIM, PEER_QDIM)
        s1_all = _dot(keys_ref[h], qt_ref[pl.ds(base, PEER_HALF), :])
        s2_all = _dot(keys_ref[PEER_HEADS + h], qt_ref[pl.ds(base + PEER_HALF, PEER_HALF), :])
        chunks = []
        for c in range(s1_all.shape[1] // LANES):
            cs = slice(c * LANES, (c + 1) * LANES)
            s1, s2 = s1_all[:, cs], s2_all[:, cs]
            t1, _, ok1 = _top16_values(s1, with_rank=False)
            t2, rank2, ok2 = _top16_values(s2, with_rank=True)
            cnt, zsum, ok3 = _pair_counts(t1, t2)
            store(h, cs, s1, s2, t1, None, t2, rank2, cnt, zsum)
            chunks.append((cs, s1, s2, jnp.min(ok1 * ok2 * ok3)))

        for cs, s1, s2, exact in chunks:
            @pl.when(exact < 0.5)
            def _(cs=cs, s1=s1, s2=s2):
                t1x, rank1x = _top16(s1)
                t2x, rank2x = _top16(s2)
                cntx, zx = _pair_top16(t1x, t2x)
                store(h, cs, s1, s2, t1x, rank1x, t2x, rank2x, cntx, zx)

        return carry

    lax.fori_loop(0, PEER_HEADS, head, 0)


PEER_SCORE_TN = 4 * LANES


def _peer_score(h2t, wqt, keys):
    ntok = h2t.shape[1]
    tn = PEER_SCORE_TN
    nw = PEER_N_KEYS // _words_per_row()
    tbl = lambda rows: jax.ShapeDtypeStruct((PEER_HEADS, rows, ntok), jnp.int32)
    ospec = lambda rows: pl.BlockSpec((PEER_HEADS, rows, tn), lambda i: (0, 0, i))
    return pl.pallas_call(
        _peer_score_kernel,
        grid=(ntok // tn,),
        in_specs=[pl.BlockSpec((h2t.shape[0], tn), lambda i: (0, i)),
                  pl.BlockSpec(wqt.shape, lambda i: (0, 0)),
                  pl.BlockSpec(keys.shape, lambda i: (0, 0, 0))],
        out_specs=[ospec(nw), ospec(nw), ospec(PEER_N_KEYS), ospec(PEER_N_KEYS)],
        out_shape=[tbl(nw), tbl(nw), tbl(PEER_N_KEYS), tbl(PEER_N_KEYS)],
        scratch_shapes=[pltpu.VMEM((PEER_HEADS * PEER_QDIM, tn), BF16)],
        compiler_params=_cparams("parallel"),
        name="peer_score",
    )(h2t, wqt, keys)


PEER_KEYS_PER_STEP = 4
PEER_ETILE = PEER_KEYS_PER_STEP * PEER_N_KEYS
PEER_KEY_GROUP = 2


def _peer_dense_kernel(h2t_ref, c2_ref, p2_ref, cnt_ref, ka_ref, u_ref, vt_ref, x1_ref, modc_ref, modl_ref, fg_ref,
                       o_ref, acc_ref, pre_ref, hid_ref, *, ctx_rows, rows_per_sample, final):
    D = D_MODEL
    NK = PEER_N_KEYS
    tn = h2t_ref.shape[1]
    e = pl.program_id(1)

    @pl.when(e == 0)
    def _():
        acc_ref[...] = jnp.zeros_like(acc_ref)

    pre_ref[...] = _dot(pltpu.bitcast(u_ref[...], BF16), pltpu.bitcast(h2t_ref[...], BF16))

    row = lambda ref, i, h: ref[h, pl.ds(e * PEER_KEYS_PER_STEP + i, 1), :]
    cnt_rows = [[row(cnt_ref, i, h) for h in range(PEER_HEADS)] for i in range(PEER_KEYS_PER_STEP)]
    ka_rows = [[row(ka_ref, i, h) for h in range(PEER_HEADS)] for i in range(PEER_KEYS_PER_STEP)]
    wpr = _words_per_row()

    def key_rows(i, h, cs):
        rep = lambda words: pltpu.bitcast(jnp.broadcast_to(words[:, cs], (BF16_ROWS // wpr, LANES)), BF16)
        return rep(cnt_rows[i][h]), rep(ka_rows[i][h])

    for c in range(tn // LANES):
        cs = slice(c * LANES, (c + 1) * LANES)
        for g in range(0, PEER_KEYS_PER_STEP, PEER_KEY_GROUP):
            group = range(g, g + PEER_KEY_GROUP)
            w = {i: None for i in group}
            for h in range(PEER_HEADS):
                c2 = pltpu.bitcast(c2_ref[h, :, cs], BF16).reshape(NK // BF16_ROWS, BF16_ROWS, LANES)
                p2 = pltpu.bitcast(p2_ref[h, :, cs], BF16).reshape(NK // BF16_ROWS, BF16_ROWS, LANES)
                for i in group:
                    cnt, kaw = key_rows(i, h, cs)
                    t = jnp.where(c2 < cnt[None], p2, 0.0) * kaw[None]
                    w[i] = t if w[i] is None else w[i] + t
            for i in group:
                rows = slice(i * NK, (i + 1) * NK)
                x = pre_ref[rows, cs].astype(BF16)
                act = 0.5 * x * (1.0 + lax.erf(x * (2.0 ** -0.5)))
                hid_ref[rows, cs] = act * w[i].reshape(NK, LANES)

    acc_ref[...] += _dot(pltpu.bitcast(vt_ref[...], BF16), hid_ref[...])

    @pl.when(e == pl.num_programs(1) - 1)
    def _():
        y = acc_ref[...].T
        g_lat = modl_ref[0][:, 5 * D:6 * D]
        if ctx_rows:
            g_ctx = modc_ref[0][:, 5 * D:6 * D]
            tiles = rows_per_sample // tn
            r0 = (pl.program_id(0) % tiles) * tn
            row = r0 + lax.broadcasted_iota(jnp.int32, (tn, D), 0)
            gate = jnp.where(row < ctx_rows, g_ctx, g_lat)
        else:
            gate = g_lat
        x2 = x1_ref[...] + gate * y
        if final:
            x2 = _rms(x2, fg_ref[...])
        o_ref[...] = x2


def _peer_dense(h2t, sel, u, vt, x1, modrows, fg, *, tn, rows_per_sample, ctx_rows, final):
    ntok, D = x1.shape
    wpr = _words_per_row()
    ne = u.shape[0] * wpr // PEER_ETILE
    tiles = rows_per_sample // tn
    tspec = lambda t: pl.BlockSpec((PEER_HEADS, t.shape[1], tn), lambda i, s: (0, 0, i))
    return pl.pallas_call(
        functools.partial(_peer_dense_kernel, ctx_rows=ctx_rows, rows_per_sample=rows_per_sample, final=final),
        grid=(ntok // tn, ne),
        in_specs=[pl.BlockSpec((D // wpr, tn), lambda i, s: (0, i)),
                  tspec(sel[0]), tspec(sel[1]), tspec(sel[2]), tspec(sel[3]),
                  pl.BlockSpec((PEER_ETILE // wpr, D), lambda i, s: (s, 0)),
                  pl.BlockSpec((D // wpr, PEER_ETILE), lambda i, s: (0, s)),
                  pl.BlockSpec((tn, D), lambda i, s: (i, 0), pipeline_mode=pl.Buffered(1)),
                  pl.BlockSpec((1, 1, 6 * D), lambda i, s: (2 * (i // tiles), 0, 0)),
                  pl.BlockSpec((1, 1, 6 * D), lambda i, s: (2 * (i // tiles) + 1, 0, 0)),
                  pl.BlockSpec((1, D), lambda i, s: (0, 0))],
        out_specs=pl.BlockSpec((tn, D), lambda i, s: (i, 0)),
        out_shape=jax.ShapeDtypeStruct((ntok, D), F32),
        scratch_shapes=[pltpu.VMEM((D, tn), F32),
                        pltpu.VMEM((PEER_ETILE, tn), F32),
                        pltpu.VMEM((PEER_ETILE, tn), BF16)],
        compiler_params=_cparams("parallel", "arbitrary"),
        name="peer_dense",
    )(h2t, *sel, u, vt, x1, modrows, modrows, fg)


def _pack_rows_kernel(x_ref, o_ref):
    o_ref[...] = pltpu.bitcast(x_ref[...].astype(BF16), jnp.int32)


def _pack_rows(x):
    R, C = x.shape
    wpr = _words_per_row()
    br, bc = 512, 1024
    return pl.pallas_call(
        _pack_rows_kernel,
        grid=(R // br, C // bc),
        in_specs=[pl.BlockSpec((br, bc), lambda i, j: (i, j))],
        out_specs=pl.BlockSpec((br // wpr, bc), lambda i, j: (i, j)),
        out_shape=jax.ShapeDtypeStruct((R // wpr, C), jnp.int32),
        compiler_params=_cparams("parallel", "parallel"),
        name="pack_rows",
    )(x)


def _peer_tile(rows_per_sample):
    for tn in (1024, 768, 512, 256):
        if rows_per_sample % tn == 0:
            return tn
    raise ValueError(rows_per_sample)


_SWAP = np.concatenate([np.arange(16, 32), np.arange(0, 16), np.arange(48, 64), np.arange(32, 48)])


def _layout_w_in(w):
    D = w.shape[0]
    idx = np.cumsum(IN_SPLITS)[:-1]
    p, q, k, v, r, lrf, lrb, qdn, kvdn, krope, gz = jnp.split(w, idx, axis=1)
    zeros = lambda n: jnp.zeros((D, n), w.dtype)
    misc = jnp.concatenate([krope, lrf, lrb, zeros(LANES - MLA_ROPE - 2 * GLA_GATE_RANK)], axis=1)
    sw = jnp.concatenate([krope[:, _SWAP], zeros(LANES - MLA_ROPE)], axis=1)
    out = jnp.concatenate([p, q, k, v, r, gz, kvdn, misc, sw, qdn], axis=1)
    return jnp.concatenate([out, zeros(Z_COLS - out.shape[1])], axis=1).astype(BF16)


def _layout_gate_w(gate_w, row0):
    HK = GLA_HEADS * GLA_DK
    pad = jnp.zeros((LANES, HK), gate_w.dtype)
    return pad.at[row0:row0 + GLA_GATE_RANK].set(gate_w).astype(BF16)


def _layout_w_uq(w):
    H = MLA_HEADS
    w3 = w.reshape(MLA_Q_RANK, H, MLA_NOPE + MLA_ROPE)
    qn, qr = w3[..., :MLA_NOPE], w3[..., MLA_NOPE:]
    z = jnp.zeros((MLA_Q_RANK, H, LANES - MLA_ROPE), w.dtype)
    main = jnp.concatenate([qn, qr, z], axis=-1).reshape(MLA_Q_RANK, H * MQ)
    swp = jnp.concatenate([qr[..., _SWAP], z], axis=-1).reshape(MLA_Q_RANK, H * LANES)
    return jnp.concatenate([main, swp], axis=1).astype(BF16)


def _rope_tables(seq):
    half = MLA_ROPE // 2
    t = jnp.arange(seq)
    inv = ROPE_BASE ** (-jnp.arange(0, half, 2, dtype=F32) / half)
    ar = (t // GRID_W).astype(F32)[:, None] * inv
    ac = (t % GRID_W).astype(F32)[:, None] * inv
    cos = jnp.concatenate([jnp.cos(ar), jnp.cos(ar), jnp.cos(ac), jnp.cos(ac)], axis=1)
    sin = jnp.concatenate([-jnp.sin(ar), jnp.sin(ar), -jnp.sin(ac), jnp.sin(ac)], axis=1)
    cos = jnp.concatenate([jnp.ones((CTX_LEN, MLA_ROPE), F32), cos], axis=0)
    sin = jnp.concatenate([jnp.zeros((CTX_LEN, MLA_ROPE), F32), sin], axis=0)
    pad = jnp.zeros((CTX_LEN + seq, LANES - MLA_ROPE), F32)
    return jnp.concatenate([cos, pad], axis=1), jnp.concatenate([sin, pad], axis=1)


def kernel(x, c, ctx, c_ctx, ada_w, ada_b, norm1_g, norm2_g, w_in, pool_w, pool_scale, gla_gate_w, gla_gate_b, gla_norm_g, mla_q_norm_g, mla_kv_norm_g, mla_w_uq, mla_w_ukv, branch_w, w_out, peer_wq, peer_keys, peer_u, peer_v, final_norm_g):
    B, T, D = x.shape
    R = CTX_LEN + T
    assert D == D_MODEL and ctx.shape[1] == CTX_LEN == TM and T % TM == 0 and T % GRID_W == 0

    nrow = -(-(B + 1) // 8) * 8
    cc = jnp.concatenate([c, c_ctx[None], jnp.zeros((nrow - B - 1, D), F32)], axis=0)
    mod = _ada(cc, ada_w, ada_b)
    modrows = jnp.stack([jnp.broadcast_to(mod[:, B:B + 1], (DEPTH, B, 6 * D)), mod[:, :B]], axis=2)
    modrows = modrows.reshape(DEPTH, 2 * B, 1, 6 * D)

    cos, sin = _rope_tables(T)
    xall = jnp.concatenate([ctx, x], axis=1)
    row = lambda a: a.reshape(1, -1)

    for i in range(DEPTH):
        with_ctx = i < DEPTH - 1
        final = i == DEPTH - 1
        z = _in_proj(xall, modrows[i], row(norm1_g[i]), _layout_w_in(w_in[i]))

        pool_o = _pool(z, pool_w[i].astype(BF16), row(pool_scale[i]))

        o_b = _gla(z, _layout_gate_w(gla_gate_w[i, 1], MISC_LRB), row(gla_gate_b[i, 1]), fwd=False)
        gla_o = _gla(z, _layout_gate_w(gla_gate_w[i, 0], MISC_LRF), row(gla_gate_b[i, 0]), fwd=True,
                     ob=o_b, gn=row(gla_norm_g[i]))

        q, k, v = _mla_up(z, cos, sin, row(mla_q_norm_g[i]), row(mla_kv_norm_g[i]),
                          _layout_w_uq(mla_w_uq[i]), mla_w_ukv[i].astype(BF16))
        mla_o = _attn(q, k, v, with_ctx=with_ctx)

        x1, h2t = _merge(pool_o, gla_o, mla_o, z, xall, modrows[i], branch_w[i].astype(BF16),
                        w_out[i].astype(BF16), row(norm2_g[i]), with_ctx=with_ctx)

        rows = x1.shape[1]
        keys = peer_keys[i].reshape(2 * PEER_HEADS, PEER_N_KEYS, PEER_HALF).astype(BF16)
        sel = _peer_score(h2t, peer_wq[i].T.astype(BF16), keys)
        xall = _peer_dense(h2t, sel, _pack_rows(peer_u[i]), _pack_rows(peer_v[i].T), x1.reshape(B * rows, D),
                           modrows[i], row(final_norm_g), tn=_peer_tile(rows), rows_per_sample=rows,
                           ctx_rows=CTX_LEN if with_ctx else 0, final=final).reshape(B, rows, D)
    return xall
```

```python
import functools

import jax
import jax.numpy as jnp
import numpy as np
from jax import lax
from jax.experimental import pallas as pl
from jax.experimental.pallas import tpu as pltpu

F32 = jnp.float32
BF16 = jnp.bfloat16

D_MODEL = 1024
DEPTH = 2
CTX_LEN = 256
GRID_W = 64
NORM_EPS = 1e-6
POOL_WIDTH = 1024
POOL_WINDOWS = (2, 4, 8, 16)
POOL_GROUP = POOL_WIDTH // len(POOL_WINDOWS)
GLA_HEADS = 4
GLA_DK = 128
GLA_DV = 256
GLA_GATE_RANK = 16
GLA_TAU = 16.0
GLA_CHUNK = 64
MLA_HEADS = 8
MLA_Q_RANK = 384
MLA_KV_RANK = 256
MLA_NOPE = 128
MLA_ROPE = 64
MLA_V = 128
ROPE_BASE = 10000.0
N_BRANCH = 3
PEER_HEADS = 8
PEER_N_KEYS = 128
PEER_TOPK = 16
PEER_QDIM = 256
PEER_HALF = PEER_QDIM // 2
IN_SPLITS = (POOL_WIDTH, GLA_HEADS * GLA_DK, GLA_HEADS * GLA_DK, GLA_HEADS * GLA_DV, GLA_HEADS * GLA_DV,
             GLA_GATE_RANK, GLA_GATE_RANK, MLA_Q_RANK, MLA_KV_RANK, MLA_ROPE, N_BRANCH * D_MODEL)

LANES = 128
BF16_ROWS = 16
TM = 256
VMEM_LIMIT = 56 * 1024 * 1024

COL_P = 0
COL_Q = 1024
COL_K = 1536
COL_V = 2048
COL_R = 3072
COL_GZ = 4096
COL_KVDN = 7168
COL_MISC = 7424
COL_SW = 7552
COL_QDN = 7680
Z_COLS = 8192
IN_NBLK = 4096
MISC_LRF = MLA_ROPE
MISC_LRB = MLA_ROPE + GLA_GATE_RANK
MQ = MLA_NOPE + LANES


def _cparams(*sem):
    return pltpu.CompilerParams(dimension_semantics=sem, vmem_limit_bytes=VMEM_LIMIT)


def _rms(x, g):
    return x * lax.rsqrt(jnp.mean(x * x, axis=-1, keepdims=True) + NORM_EPS) * g


def _dot(a, b):
    return jnp.dot(a, b, preferred_element_type=F32)


def _dot_nt(a, b):
    return lax.dot_general(a, b, (((1,), (1,)), ((), ())), preferred_element_type=F32)


def _dot_tn(a, b):
    return lax.dot_general(a, b, (((0,), (0,)), ((), ())), preferred_element_type=F32)


def _ada_kernel(c_ref, w_ref, b_ref, o_ref):
    c = c_ref[...]
    a = (c * jax.nn.sigmoid(c)).astype(BF16)
    o_ref[0] = _dot(a, w_ref[0].astype(BF16)) + b_ref[0]


def _ada(cc, ada_w, ada_b):
    L, D, N = ada_w.shape
    rows = cc.shape[0]
    nb = N // D
    return pl.pallas_call(
        _ada_kernel,
        grid=(L, nb),
        in_specs=[pl.BlockSpec((rows, D), lambda l, j: (0, 0)),
                  pl.BlockSpec((1, D, D), lambda l, j: (l, 0, j)),
                  pl.BlockSpec((1, 1, D), lambda l, j: (l, 0, j))],
        out_specs=pl.BlockSpec((1, rows, D), lambda l, j: (l, 0, j)),
        out_shape=jax.ShapeDtypeStruct((L, rows, N), F32),
        compiler_params=_cparams("parallel", "parallel"),
        name="ada_mod",
    )(cc, ada_w, ada_b.reshape(L, 1, N))


def _in_proj_kernel(x_ref, mod_ref, g_ref, w_ref, z_ref):
    D = D_MODEL
    x = x_ref[0]
    mod = mod_ref[0]
    h = _rms(x, g_ref[...]) * (1.0 + mod[:, D:2 * D]) + mod[:, 0:D]
    z_ref[0] = _dot(h.astype(BF16), w_ref[...]).astype(BF16)


def _in_proj(xall, modrows, g, w):
    B, R, D = xall.shape
    nt = R // TM
    nn = Z_COLS // IN_NBLK
    return pl.pallas_call(
        _in_proj_kernel,
        grid=(nn, B, nt),
        in_specs=[pl.BlockSpec((1, TM, D), lambda n, b, t: (b, t, 0)),
                  pl.BlockSpec((1, 1, 6 * D), lambda n, b, t: (2 * b + jnp.minimum(t, 1), 0, 0)),
                  pl.BlockSpec((1, D), lambda n, b, t: (0, 0)),
                  pl.BlockSpec((D, IN_NBLK), lambda n, b, t: (0, n))],
        out_specs=pl.BlockSpec((1, TM, IN_NBLK), lambda n, b, t: (b, t, n)),
        out_shape=jax.ShapeDtypeStruct((B, R, Z_COLS), BF16),
        compiler_params=_cparams("parallel", "parallel", "parallel"),
        name="in_proj",
    )(xall, modrows, g, w)


def _pool_kernel(u_ref, w_ref, sc_ref, o_ref):
    R = u_ref.shape[1]
    row = lax.broadcasted_iota(jnp.int32, (R, POOL_GROUP), 0)
    seg_lo = jnp.where(row < CTX_LEN, 0, CTX_LEN)
    seg_hi = jnp.where(row < CTX_LEN, CTX_LEN, R)
    for gi, win in enumerate(POOL_WINDOWS):
        sl = slice(gi * POOL_GROUP, (gi + 1) * POOL_GROUP)
        u = u_ref[0, :, sl].astype(F32)
        lo_off, hi_off = win // 2, win - win // 2
        acc = jnp.zeros_like(u)
        for d in range(-lo_off, hi_off):
            shifted = u if d == 0 else pltpu.roll(u, (R - d) % R, axis=0)
            ok = (row + d >= seg_lo) & (row + d < seg_hi)
            acc = acc + jnp.where(ok, shifted, 0.0)
        cnt = (jnp.minimum(row + hi_off, seg_hi) - jnp.maximum(row - lo_off, seg_lo)).astype(F32)
        diff = (acc / cnt - u).astype(BF16)
        o_ref[0, :, sl] = (_dot(diff, w_ref[gi]) * sc_ref[:, sl]).astype(BF16)


def _pool(z, pool_w, pool_scale):
    B, R, _ = z.shape
    return pl.pallas_call(
        _pool_kernel,
        grid=(B,),
        in_specs=[pl.BlockSpec((1, R, POOL_WIDTH), lambda b: (b, 0, COL_P // POOL_WIDTH)),
                  pl.BlockSpec(pool_w.shape, lambda b: (0, 0, 0)),
                  pl.BlockSpec((1, POOL_WIDTH), lambda b: (0, 0))],
        out_specs=pl.BlockSpec((1, R, POOL_WIDTH), lambda b: (b, 0, 0)),
        out_shape=jax.ShapeDtypeStruct((B, R, POOL_WIDTH), BF16),
        compiler_params=_cparams("parallel"),
        name="pool",
    )(z, pool_w, pool_scale)


def _split3(x):
    a = x.astype(BF16)
    r = x - a.astype(F32)
    b = r.astype(BF16)
    c = (r - b.astype(F32)).astype(BF16)
    return a, b, c


def _gla_kernel(*refs, fwd):
    if fwd:
        q_ref, k_ref, v_ref, misc_ref, gw_ref, gb_ref, ob_ref, r_ref, gn_ref, o_ref, st_ref = refs
    else:
        q_ref, k_ref, v_ref, misc_ref, gw_ref, gb_ref, o_ref, st_ref = refs
    C = GLA_CHUNK
    nchunk = TM // C
    HK = GLA_HEADS * GLA_DK

    @pl.when(pl.program_id(1) == 0)
    def _():
        st_ref[...] = jnp.zeros_like(st_ref)

    zg = _dot(misc_ref[0], gw_ref[...]) + gb_ref[...]
    g = (jnp.minimum(zg, 0.0) - jnp.log1p(jnp.exp(-jnp.abs(zg)))) * (1.0 / GLA_TAU)

    row = lax.broadcasted_iota(jnp.int32, (TM, TM), 0)
    col = lax.broadcasted_iota(jnp.int32, (TM, TM), 1)
    shift = C.bit_length() - 1
    same = (row >> shift) == (col >> shift)
    tri = same & ((col <= row) if fwd else (col >= row))
    ones = jnp.concatenate([jnp.where(tri, 1.0, 0.0), jnp.where(same, 1.0, 0.0)], axis=0).astype(BF16)
    g1, g2, g3 = _split3(g)
    both = _dot(ones, g1) + _dot(ones, g2) + _dot(ones, g3)
    b = both[:TM]
    bt = both[TM:]

    qe = q_ref[0].astype(F32) * (GLA_DK ** -0.5) * jnp.exp(b)
    ke = (k_ref[0].astype(F32) * jnp.exp(-b)).astype(BF16)
    kd = (k_ref[0].astype(F32) * jnp.exp(bt - b)).astype(BF16)
    qe = qe.astype(BF16)
    dec = jnp.exp(bt)
    v = v_ref[0]

    order = range(nchunk) if fwd else range(nchunk - 1, -1, -1)
    for h in range(GLA_HEADS):
        ks = slice(h * GLA_DK, (h + 1) * GLA_DK)
        vs = slice(h * GLA_DV, (h + 1) * GLA_DV)
        att = jnp.where(tri, _dot_nt(qe[:, ks], ke[:, ks]), 0.0).astype(BF16)
        o_in = _dot(att, v[:, vs])
        st = st_ref[h]
        parts = [None] * nchunk
        for j in order:
            rs = slice(j * C, (j + 1) * C)
            parts[j] = o_in[rs] + _dot_nt(qe[rs, ks], st.astype(BF16))
            st = st * dec[j * C:j * C + 1, ks] + _dot_tn(v[rs, vs], kd[rs, ks])
        st_ref[h] = st
        o = jnp.concatenate(parts, axis=0)
        if fwd:
            o = o + ob_ref[0, :, vs].astype(F32)
            r = r_ref[0, :, vs].astype(F32)
            o = _rms(o, gn_ref[...]) * (r * jax.nn.sigmoid(r))
        o_ref[0, :, vs] = o.astype(BF16)


def _gla(z, gw, gb, *, fwd, ob=None, gn=None):
    B, R, _ = z.shape
    nt = R // TM
    HK, HV = GLA_HEADS * GLA_DK, GLA_HEADS * GLA_DV
    if fwd:
        tile = lambda c: c
    else:
        tile = lambda c: jnp.where(c == 0, 0, nt - c)
    in_specs = [pl.BlockSpec((1, TM, HK), lambda b, c: (b, tile(c), COL_Q // HK)),
                pl.BlockSpec((1, TM, HK), lambda b, c: (b, tile(c), COL_K // HK)),
                pl.BlockSpec((1, TM, HV), lambda b, c: (b, tile(c), COL_V // HV)),
                pl.BlockSpec((1, TM, LANES), lambda b, c: (b, tile(c), COL_MISC // LANES)),
                pl.BlockSpec((LANES, HK), lambda b, c: (0, 0)),
                pl.BlockSpec((1, HK), lambda b, c: (0, 0))]
    args = [z, z, z, z, gw, gb]
    if fwd:
        in_specs += [pl.BlockSpec((1, TM, HV), lambda b, c: (b, tile(c), 0)),
                     pl.BlockSpec((1, TM, HV), lambda b, c: (b, tile(c), COL_R // HV)),
                     pl.BlockSpec((1, GLA_DV), lambda b, c: (0, 0))]
        args += [ob, z, gn]
    return pl.pallas_call(
        functools.partial(_gla_kernel, fwd=fwd),
        grid=(B, nt),
        in_specs=in_specs,
        out_specs=pl.BlockSpec((1, TM, HV), lambda b, c: (b, tile(c), 0)),
        out_shape=jax.ShapeDtypeStruct((B, R, HV), BF16),
        scratch_shapes=[pltpu.VMEM((GLA_HEADS, GLA_DV, GLA_DK), F32)],
        compiler_params=_cparams("parallel", "arbitrary"),
        name="gla_fwd" if fwd else "gla_bwd",
    )(*args)


def _mla_up_kernel(qdn_ref, kvdn_ref, misc_ref, sw_ref, cos_ref, sin_ref, qg_ref, kvg_ref, wq_ref, wkv_ref,
                   q_ref, k_ref, v_ref):
    H = MLA_HEADS
    scale = (MLA_NOPE + MLA_ROPE) ** -0.5 * float(np.log2(np.e))
    cos = cos_ref[...]
    sin = sin_ref[...]
    qn = _rms(qdn_ref[0].astype(F32), qg_ref[...]).astype(BF16)
    qall = _dot(qn, wq_ref[...])
    for h in range(H):
        base = h * MQ
        q_ref[0, :, base:base + MLA_NOPE] = (qall[:, base:base + MLA_NOPE] * scale).astype(BF16)
        rot = (qall[:, base + MLA_NOPE:base + MQ] * cos
               + qall[:, H * MQ + h * LANES:H * MQ + (h + 1) * LANES] * sin)
        q_ref[0, :, base + MLA_NOPE:base + MQ] = (rot * scale).astype(BF16)
    kvn = _rms(kvdn_ref[0].astype(F32), kvg_ref[...]).astype(BF16)
    kvall = _dot(kvn, wkv_ref[...])
    kr = (misc_ref[0].astype(F32) * cos + sw_ref[0].astype(F32) * sin).astype(BF16)
    for h in range(H):
        src = h * (MLA_NOPE + MLA_V)
        k_ref[0, :, h * MQ:h * MQ + MLA_NOPE] = kvall[:, src:src + MLA_NOPE].astype(BF16)
        k_ref[0, :, h * MQ + MLA_NOPE:(h + 1) * MQ] = kr
        v_ref[0, :, h * MLA_V:(h + 1) * MLA_V] = kvall[:, src + MLA_NOPE:src + MLA_NOPE + MLA_V].astype(BF16)


def _mla_up(z, cos, sin, qg, kvg, wq, wkv):
    B, R, _ = z.shape
    nt = R // TM
    H = MLA_HEADS
    const = lambda b, t: (0, 0)
    return pl.pallas_call(
        _mla_up_kernel,
        grid=(B, nt),
        in_specs=[pl.BlockSpec((1, TM, MLA_Q_RANK), lambda b, t: (b, t, COL_QDN // MLA_Q_RANK)),
                  pl.BlockSpec((1, TM, MLA_KV_RANK), lambda b, t: (b, t, COL_KVDN // MLA_KV_RANK)),
                  pl.BlockSpec((1, TM, LANES), lambda b, t: (b, t, COL_MISC // LANES)),
                  pl.BlockSpec((1, TM, LANES), lambda b, t: (b, t, COL_SW // LANES)),
                  pl.BlockSpec((TM, LANES), lambda b, t: (t, 0)),
                  pl.BlockSpec((TM, LANES), lambda b, t: (t, 0)),
                  pl.BlockSpec(qg.shape, const),
                  pl.BlockSpec(kvg.shape, const),
                  pl.BlockSpec(wq.shape, const),
                  pl.BlockSpec(wkv.shape, const)],
        out_specs=[pl.BlockSpec((1, TM, H * MQ), lambda b, t: (b, t, 0)),
                   pl.BlockSpec((1, TM, H * MQ), lambda b, t: (b, t, 0)),
                   pl.BlockSpec((1, TM, H * MLA_V), lambda b, t: (b, t, 0))],
        out_shape=[jax.ShapeDtypeStruct((B, R, H * MQ), BF16),
                   jax.ShapeDtypeStruct((B, R, H * MQ), BF16),
                   jax.ShapeDtypeStruct((B, R, H * MLA_V), BF16)],
        compiler_params=_cparams("parallel", "parallel"),
        name="mla_up",
    )(z, z, z, z, cos, sin, qg, kvg, wq, wkv)


ATTN_HEADS = 8


def _attn_kernel(q_ref, k_ref, v_ref, o_ref, *, q_off):
    R = k_ref.shape[1]

    def attend(nk):
        for j in range(ATTN_HEADS):
            s = _dot_nt(q_ref[0, :, j * MQ:(j + 1) * MQ], k_ref[0, :nk, j * MQ:(j + 1) * MQ])
            p = jnp.exp2(s - jnp.max(s, axis=-1, keepdims=True))
            l = jnp.sum(p, axis=-1, keepdims=True)
            o = _dot(p.astype(BF16), v_ref[0, :nk, j * MLA_V:(j + 1) * MLA_V]) / l
            o_ref[0, :, j * MLA_V:(j + 1) * MLA_V] = o.astype(BF16)

    if q_off == 0:
        qi = pl.program_id(2)

        @pl.when(qi == 0)
        def _():
            attend(CTX_LEN)

        @pl.when(qi > 0)
        def _():
            attend(R)
    else:
        attend(R)


def _attn(q, k, v, *, with_ctx):
    B, R, _ = q.shape
    q_off = 0 if with_ctx else CTX_LEN // TM
    nq = R // TM - q_off
    G = ATTN_HEADS
    return pl.pallas_call(
        functools.partial(_attn_kernel, q_off=q_off),
        grid=(B, MLA_HEADS // G, nq),
        in_specs=[pl.BlockSpec((1, TM, G * MQ), lambda b, h, i: (b, i + q_off, h)),
                  pl.BlockSpec((1, R, G * MQ), lambda b, h, i: (b, 0, h)),
                  pl.BlockSpec((1, R, G * MLA_V), lambda b, h, i: (b, 0, h))],
        out_specs=pl.BlockSpec((1, TM, G * MLA_V), lambda b, h, i: (b, i + q_off, h)),
        out_shape=jax.ShapeDtypeStruct((B, R, MLA_HEADS * MLA_V), BF16),
        compiler_params=_cparams("parallel", "parallel", "arbitrary"),
        name="mla_attn",
    )(q, k, v)


def _merge_kernel(po_ref, go_ref, mo_ref, gz0_ref, gz1_ref, gz2_ref, x_ref, mod_ref, bw_ref, wo_ref, g_ref,
                  x1_ref, h2t_ref):
    D = D_MODEL
    m = None
    for i, (o_ref, gz_ref) in enumerate(((po_ref, gz0_ref), (go_ref, gz1_ref), (mo_ref, gz2_ref))):
        t = jax.nn.sigmoid(gz_ref[0].astype(F32)) * _dot(o_ref[0], bw_ref[i])
        m = t if m is None else m + t
    y = _dot(m.astype(BF16), wo_ref[...])
    mod = mod_ref[0]
    x1 = x_ref[0] + mod[:, 2 * D:3 * D] * y
    x1_ref[0] = x1
    h2 = _rms(x1, g_ref[...]) * (1.0 + mod[:, 4 * D:5 * D]) + mod[:, 3 * D:4 * D]
    h2t_ref[...] = pltpu.bitcast(h2.T.astype(BF16), jnp.int32)


def _merge(pool_o, gla_o, mla_o, z, xall, modrows, bw, wo, g, *, with_ctx):
    B, R, D = xall.shape
    t_off = 0 if with_ctx else CTX_LEN // TM
    nt = R // TM - t_off
    tok = lambda b, t: (b, t + t_off, 0)
    gzb = COL_GZ // D
    wpr = _words_per_row()
    mo_off = t_off if mla_o.shape[1] == R else 0
    return pl.pallas_call(
        _merge_kernel,
        grid=(B, nt),
        in_specs=[pl.BlockSpec((1, TM, D), tok),
                  pl.BlockSpec((1, TM, D), tok),
                  pl.BlockSpec((1, TM, D), lambda b, t: (b, t + mo_off, 0)),
                  pl.BlockSpec((1, TM, D), lambda b, t: (b, t + t_off, gzb)),
                  pl.BlockSpec((1, TM, D), lambda b, t: (b, t + t_off, gzb + 1)),
                  pl.BlockSpec((1, TM, D), lambda b, t: (b, t + t_off, gzb + 2)),
                  pl.BlockSpec((1, TM, D), tok),
                  pl.BlockSpec((1, 1, 6 * D), lambda b, t: (2 * b + jnp.minimum(t + t_off, 1), 0, 0)),
                  pl.BlockSpec(bw.shape, lambda b, t: (0, 0, 0)),
                  pl.BlockSpec(wo.shape, lambda b, t: (0, 0)),
                  pl.BlockSpec((1, D), lambda b, t: (0, 0))],
        out_specs=[pl.BlockSpec((1, TM, D), lambda b, t: (b, t, 0)),
                   pl.BlockSpec((D // wpr, TM), lambda b, t: (0, b * nt + t))],
        out_shape=[jax.ShapeDtypeStruct((B, nt * TM, D), F32),
                   jax.ShapeDtypeStruct((D // wpr, B * nt * TM), jnp.int32)],
        compiler_params=_cparams("parallel", "parallel"),
        name="merge",
    )(pool_o, gla_o, mla_o, z, z, z, xall, modrows, bw, wo, g)


def _top16(s):
    nk, n = s.shape
    iota = lax.broadcasted_iota(jnp.int32, (nk, n), 0).astype(F32)
    slot = lax.broadcasted_iota(jnp.int32, (PEER_TOPK, n), 0)
    rank = jnp.full((nk, n), float(PEER_TOPK), F32)
    vals = jnp.zeros((PEER_TOPK, n), F32)
    for j in range(PEER_TOPK):
        m = jnp.max(s, axis=0, keepdims=True)
        idx = jnp.min(jnp.where(s == m, iota, float(nk)), axis=0, keepdims=True)
        hit = iota == idx
        rank = jnp.where(hit, float(j), rank)
        s = jnp.where(hit, -jnp.inf, s)
        vals = jnp.where(slot == j, m, vals)
    return vals, rank


PAIR_SHORT = PEER_TOPK // 2
PAIR_ROWS = PEER_TOPK + (PEER_TOPK - 1) * PAIR_SHORT
PAIR_VALID = sum(PEER_TOPK // (i + 1) for i in range(PEER_TOPK))


def _pair_rows(t2):
    return jnp.concatenate([t2] + [t2[0:PAIR_SHORT]] * (PEER_TOPK - 1), axis=0)


def _pair_candidates(t1, t2):
    K = PEER_TOPK
    n = t1.shape[1]
    first = jnp.concatenate([jnp.broadcast_to(t1[0:1], (K, n))]
                            + [jnp.broadcast_to(t1[i:i + 1], (PAIR_SHORT, n)) for i in range(1, K)], axis=0)
    r = lax.broadcasted_iota(jnp.int32, (PAIR_ROWS, n), 0)
    sh = PAIR_SHORT.bit_length() - 1
    i_of = jnp.where(r < K, 0, ((r - K) >> sh) + 1)
    j_of = jnp.where(r < K, r, (r - K) & (PAIR_SHORT - 1))
    cand = jnp.where((i_of + 1) * (j_of + 1) <= K, first + _pair_rows(t2), -jnp.inf)
    return cand, (i_of * K + j_of).astype(F32)


def _top16_values(s, with_rank):
    n = s.shape[1]
    slot = lax.broadcasted_iota(jnp.int32, (PEER_TOPK, n), 0)
    vals = jnp.zeros((PEER_TOPK, n), F32)
    rank = jnp.full(s.shape, float(PEER_TOPK), F32) if with_rank else None
    for j in range(PEER_TOPK):
        m = jnp.max(s, axis=0, keepdims=True)
        hit = s == m
        if with_rank:
            rank = jnp.where(hit, float(j), rank)
        s = jnp.where(hit, -jnp.inf, s)
        vals = jnp.where(slot == j, m, vals)
    used = jnp.sum(jnp.where(s == -jnp.inf, 1.0, 0.0), axis=0, keepdims=True)
    return vals, rank, jnp.where(used == float(PEER_TOPK), 1.0, 0.0)


def _pair_counts(t1, t2):
    K = PEER_TOPK
    n = t1.shape[1]
    cand0, _ = _pair_candidates(t1, t2)
    cand = cand0
    zsum = jnp.zeros((1, n), F32)
    best = m = None
    for j in range(K):
        m = jnp.max(cand, axis=0, keepdims=True)
        cand = jnp.where(cand == m, -jnp.inf, cand)
        if j == 0:
            best = m
        zsum = zsum + jnp.exp(m - best)
    used = jnp.sum(jnp.where(cand == -jnp.inf, 1.0, 0.0), axis=0, keepdims=True)
    exact = jnp.where(used == float(PAIR_ROWS - PAIR_VALID + K), 1.0, 0.0)
    picked = jnp.where(cand0 >= m, 1.0, 0.0)
    slot = lax.broadcasted_iota(jnp.int32, (K, n), 0)
    cnt = jnp.zeros((K, n), F32)
    for i in range(K):
        blk = picked[0:K] if i == 0 else picked[K + (i - 1) * PAIR_SHORT:K + i * PAIR_SHORT]
        cnt = jnp.where(slot == i, jnp.sum(blk, axis=0, keepdims=True), cnt)
    return cnt, zsum, exact


def _pair_top16(t1, t2):
    K = PEER_TOPK
    n = t1.shape[1]
    cand, flat = _pair_candidates(t1, t2)
    irow = lax.broadcasted_iota(jnp.int32, (K, n), 0).astype(F32)
    cnt = jnp.zeros((K, n), F32)
    zsum = jnp.zeros((1, n), F32)
    best = None
    for j in range(K):
        m = jnp.max(cand, axis=0, keepdims=True)
        idx = jnp.min(jnp.where(cand == m, flat, float(K * K)), axis=0, keepdims=True)
        cand = jnp.where(flat == idx, -jnp.inf, cand)
        cnt = cnt + jnp.where(irow == jnp.floor(idx * (1.0 / K)), 1.0, 0.0)
        if j == 0:
            best = m
        zsum = zsum + jnp.exp(m - best)
    return cnt, zsum


def _words_per_row():
    return 4 // jnp.dtype(BF16).itemsize


def _splat_words(x):
    bits = lax.bitcast_convert_type(x.astype(BF16).astype(F32), jnp.int32)
    if _words_per_row() == 1:
        return bits
    return bits | lax.shift_right_logical(bits, 16)


def _peer_score_kernel(h2t_ref, wqt_ref, keys_ref, c2_ref, p2_ref, cnt_ref, ka_ref, qt_ref):
    K = PEER_TOPK
    qt_ref[...] = _dot(wqt_ref[...], pltpu.bitcast(h2t_ref[...], BF16)).astype(BF16)

    def store(h, cs, s1, s2, t1, rank1, t2, rank2, cnt, zsum):
        rowcnt = jnp.zeros_like(s1)
        for j in range(K):
            in_row = (s1 == t1[j:j + 1]) if rank1 is None else (rank1 == float(j))
            rowcnt = jnp.where(in_row, cnt[j:j + 1], rowcnt)
        c2_ref[h, :, cs] = pltpu.bitcast(rank2.astype(BF16), jnp.int32)
        p2_ref[h, :, cs] = pltpu.bitcast(jnp.exp(s2 - t2[0:1]).astype(BF16), jnp.int32)
        cnt_ref[h, :, cs] = _splat_words(rowcnt)
        ka_ref[h, :, cs] = _splat_words(jnp.exp(s1 - t1[0:1]) / zsum)

    def head(h, carry):
        base = pl.multiple_of(h * PEER_QDIM, PEER_QDIM)
        s1_all = _dot(keys_ref[h], qt_ref[pl.ds(base, PEER_HALF), :])
        s2_all = _dot(keys_ref[PEER_HEADS + h], qt_ref[pl.ds(base + PEER_HALF, PEER_HALF), :])
        chunks = []
        for c in range(s1_all.shape[1] // LANES):
            cs = slice(c * LANES, (c + 1) * LANES)
            s1, s2 = s1_all[:, cs], s2_all[:, cs]
            t1, _, ok1 = _top16_values(s1, with_rank=False)
            t2, rank2, ok2 = _top16_values(s2, with_rank=True)
            cnt, zsum, ok3 = _pair_counts(t1, t2)
            store(h, cs, s1, s2, t1, None, t2, rank2, cnt, zsum)
            chunks.append((cs, s1, s2, jnp.min(ok1 * ok2 * ok3)))

        for cs, s1, s2, exact in chunks:
            @pl.when(exact < 0.5)
            def _(cs=cs, s1=s1, s2=s2):
                t1x, rank1x = _top16(s1)
                t2x, rank2x = _top16(s2)
                cntx, zx = _pair_top16(t1x, t2x)
                store(h, cs, s1, s2, t1x, rank1x, t2x, rank2x, cntx, zx)

        return carry

    lax.fori_loop(0, PEER_HEADS, head, 0)


PEER_SCORE_TN = 4 * LANES


def _peer_score(h2t, wqt, keys):
    ntok = h2t.shape[1]
    tn = PEER_SCORE_TN
    nw = PEER_N_KEYS // _words_per_row()
    tbl = lambda rows: jax.ShapeDtypeStruct((PEER_HEADS, rows, ntok), jnp.int32)
    ospec = lambda rows: pl.BlockSpec((PEER_HEADS, rows, tn), lambda i: (0, 0, i))
    return pl.pallas_call(
        _peer_score_kernel,
        grid=(ntok // tn,),
        in_specs=[pl.BlockSpec((h2t.shape[0], tn), lambda i: (0, i)),
                  pl.BlockSpec(wqt.shape, lambda i: (0, 0)),
                  pl.BlockSpec(keys.shape, lambda i: (0, 0, 0))],
        out_specs=[ospec(nw), ospec(nw), ospec(PEER_N_KEYS), ospec(PEER_N_KEYS)],
        out_shape=[tbl(nw), tbl(nw), tbl(PEER_N_KEYS), tbl(PEER_N_KEYS)],
        scratch_shapes=[pltpu.VMEM((PEER_HEADS * PEER_QDIM, tn), BF16)],
        compiler_params=_cparams("parallel"),
        name="peer_score",
    )(h2t, wqt, keys)


PEER_MAX_STEP_CELLS = 1024 * 768


def _peer_keys_per_step(tn):
    return 8 if 8 * PEER_N_KEYS * tn <= PEER_MAX_STEP_CELLS else 4


PEER_KEY_GROUP = 2


def _peer_dense_kernel(h2t_ref, c2_ref, p2_ref, cnt_ref, ka_ref, u_ref, vt_ref, x1_ref, modc_ref, modl_ref, fg_ref,
                       o_ref, acc_ref, pre_ref, hid_ref, *, ctx_rows, rows_per_sample, final):
    D = D_MODEL
    NK = PEER_N_KEYS
    tn = h2t_ref.shape[1]
    e = pl.program_id(1)

    @pl.when(e == 0)
    def _():
        acc_ref[...] = jnp.zeros_like(acc_ref)

    pre_ref[...] = _dot(pltpu.bitcast(u_ref[...], BF16), pltpu.bitcast(h2t_ref[...], BF16))

    kps = _peer_keys_per_step(tn)
    row = lambda ref, i, h: ref[h, pl.ds(e * kps + i, 1), :]
    cnt_rows = [[row(cnt_ref, i, h) for h in range(PEER_HEADS)] for i in range(kps)]
    ka_rows = [[row(ka_ref, i, h) for h in range(PEER_HEADS)] for i in range(kps)]
    wpr = _words_per_row()

    def key_rows(i, h, cs):
        rep = lambda words: pltpu.bitcast(jnp.broadcast_to(words[:, cs], (BF16_ROWS // wpr, LANES)), BF16)
        return rep(cnt_rows[i][h]), rep(ka_rows[i][h])

    for c in range(tn // LANES):
        cs = slice(c * LANES, (c + 1) * LANES)
        for g in range(0, kps, PEER_KEY_GROUP):
            group = range(g, g + PEER_KEY_GROUP)
            w = {i: None for i in group}
            for h in range(PEER_HEADS):
                c2 = pltpu.bitcast(c2_ref[h, :, cs], BF16).reshape(NK // BF16_ROWS, BF16_ROWS, LANES)
                p2 = pltpu.bitcast(p2_ref[h, :, cs], BF16).reshape(NK // BF16_ROWS, BF16_ROWS, LANES)
                for i in group:
                    cnt, kaw = key_rows(i, h, cs)
                    t = jnp.where(c2 < cnt[None], p2, 0.0) * kaw[None]
                    w[i] = t if w[i] is None else w[i] + t
            for i in group:
                rows = slice(i * NK, (i + 1) * NK)
                x = pre_ref[rows, cs].astype(BF16)
                act = 0.5 * x * (1.0 + lax.erf(x * (2.0 ** -0.5)))
                hid_ref[rows, cs] = act * w[i].reshape(NK, LANES)

    acc_ref[...] += _dot(pltpu.bitcast(vt_ref[...], BF16), hid_ref[...])

    @pl.when(e == pl.num_programs(1) - 1)
    def _():
        y = acc_ref[...].T
        g_lat = modl_ref[0][:, 5 * D:6 * D]
        if ctx_rows:
            g_ctx = modc_ref[0][:, 5 * D:6 * D]
            tiles = rows_per_sample // tn
            r0 = (pl.program_id(0) % tiles) * tn
            row = r0 + lax.broadcasted_iota(jnp.int32, (tn, D), 0)
            gate = jnp.where(row < ctx_rows, g_ctx, g_lat)
        else:
            gate = g_lat
        x2 = x1_ref[...] + gate * y
        if final:
            x2 = _rms(x2, fg_ref[...])
        o_ref[...] = x2


def _peer_dense(h2t, sel, u, vt, x1, modrows, fg, *, tn, rows_per_sample, ctx_rows, final):
    ntok, D = x1.shape
    wpr = _words_per_row()
    etile = _peer_keys_per_step(tn) * PEER_N_KEYS
    ne = u.shape[0] * wpr // etile
    tiles = rows_per_sample // tn
    tspec = lambda t: pl.BlockSpec((PEER_HEADS, t.shape[1], tn), lambda i, s: (0, 0, i))
    return pl.pallas_call(
        functools.partial(_peer_dense_kernel, ctx_rows=ctx_rows, rows_per_sample=rows_per_sample, final=final),
        grid=(ntok // tn, ne),
        in_specs=[pl.BlockSpec((D // wpr, tn), lambda i, s: (0, i)),
                  tspec(sel[0]), tspec(sel[1]), tspec(sel[2]), tspec(sel[3]),
                  pl.BlockSpec((etile // wpr, D), lambda i, s: (s, 0)),
                  pl.BlockSpec((D // wpr, etile), lambda i, s: (0, s)),
                  pl.BlockSpec((tn, D), lambda i, s: (i, 0), pipeline_mode=pl.Buffered(1)),
                  pl.BlockSpec((1, 1, 6 * D), lambda i, s: (2 * (i // tiles), 0, 0)),
                  pl.BlockSpec((1, 1, 6 * D), lambda i, s: (2 * (i // tiles) + 1, 0, 0)),
                  pl.BlockSpec((1, D), lambda i, s: (0, 0))],
        out_specs=pl.BlockSpec((tn, D), lambda i, s: (i, 0)),
        out_shape=jax.ShapeDtypeStruct((ntok, D), F32),
        scratch_shapes=[pltpu.VMEM((D, tn), F32),
                        pltpu.VMEM((etile, tn), F32),
                        pltpu.VMEM((etile, tn), BF16)],
        compiler_params=_cparams("parallel", "arbitrary"),
        name="peer_dense",
    )(h2t, *sel, u, vt, x1, modrows, modrows, fg)


def _pack_rows_kernel(x_ref, o_ref):
    o_ref[...] = pltpu.bitcast(x_ref[...].astype(BF16), jnp.int32)


def _pack_rows(x):
    R, C = x.shape
    wpr = _words_per_row()
    br, bc = 512, 1024
    return pl.pallas_call(
        _pack_rows_kernel,
        grid=(R // br, C // bc),
        in_specs=[pl.BlockSpec((br, bc), lambda i, j: (i, j))],
        out_specs=pl.BlockSpec((br // wpr, bc), lambda i, j: (i, j)),
        out_shape=jax.ShapeDtypeStruct((R // wpr, C), jnp.int32),
        compiler_params=_cparams("parallel", "parallel"),
        name="pack_rows",
    )(x)


def _peer_tile(rows_per_sample):
    for tn in (1024, 768, 512, 256):
        if rows_per_sample % tn == 0:
            return tn
    raise ValueError(rows_per_sample)


_SWAP = np.concatenate([np.arange(16, 32), np.arange(0, 16), np.arange(48, 64), np.arange(32, 48)])


def _layout_w_in(w):
    D = w.shape[0]
    idx = np.cumsum(IN_SPLITS)[:-1]
    p, q, k, v, r, lrf, lrb, qdn, kvdn, krope, gz = jnp.split(w, idx, axis=1)
    zeros = lambda n: jnp.zeros((D, n), w.dtype)
    misc = jnp.concatenate([krope, lrf, lrb, zeros(LANES - MLA_ROPE - 2 * GLA_GATE_RANK)], axis=1)
    sw = jnp.concatenate([krope[:, _SWAP], zeros(LANES - MLA_ROPE)], axis=1)
    out = jnp.concatenate([p, q, k, v, r, gz, kvdn, misc, sw, qdn], axis=1)
    return jnp.concatenate([out, zeros(Z_COLS - out.shape[1])], axis=1).astype(BF16)


def _layout_gate_w(gate_w, row0):
    HK = GLA_HEADS * GLA_DK
    pad = jnp.zeros((LANES, HK), gate_w.dtype)
    return pad.at[row0:row0 + GLA_GATE_RANK].set(gate_w).astype(BF16)


def _layout_w_uq(w):
    H = MLA_HEADS
    w3 = w.reshape(MLA_Q_RANK, H, MLA_NOPE + MLA_ROPE)
    qn, qr = w3[..., :MLA_NOPE], w3[..., MLA_NOPE:]
    z = jnp.zeros((MLA_Q_RANK, H, LANES - MLA_ROPE), w.dtype)
    main = jnp.concatenate([qn, qr, z], axis=-1).reshape(MLA_Q_RANK, H * MQ)
    swp = jnp.concatenate([qr[..., _SWAP], z], axis=-1).reshape(MLA_Q_RANK, H * LANES)
    return jnp.concatenate([main, swp], axis=1).astype(BF16)


def _rope_tables(seq):
    half = MLA_ROPE // 2
    t = jnp.arange(seq)
    inv = ROPE_BASE ** (-jnp.arange(0, half, 2, dtype=F32) / half)
    ar = (t // GRID_W).astype(F32)[:, None] * inv
    ac = (t % GRID_W).astype(F32)[:, None] * inv
    cos = jnp.concatenate([jnp.cos(ar), jnp.cos(ar), jnp.cos(ac), jnp.cos(ac)], axis=1)
    sin = jnp.concatenate([-jnp.sin(ar), jnp.sin(ar), -jnp.sin(ac), jnp.sin(ac)], axis=1)
    cos = jnp.concatenate([jnp.ones((CTX_LEN, MLA_ROPE), F32), cos], axis=0)
    sin = jnp.concatenate([jnp.zeros((CTX_LEN, MLA_ROPE), F32), sin], axis=0)
    pad = jnp.zeros((CTX_LEN + seq, LANES - MLA_ROPE), F32)
    return jnp.concatenate([cos, pad], axis=1), jnp.concatenate([sin, pad], axis=1)


def kernel(x, c, ctx, c_ctx, ada_w, ada_b, norm1_g, norm2_g, w_in, pool_w, pool_scale, gla_gate_w, gla_gate_b, gla_norm_g, mla_q_norm_g, mla_kv_norm_g, mla_w_uq, mla_w_ukv, branch_w, w_out, peer_wq, peer_keys, peer_u, peer_v, final_norm_g):
    B, T, D = x.shape
    R = CTX_LEN + T
    assert D == D_MODEL and ctx.shape[1] == CTX_LEN == TM and T % TM == 0 and T % GRID_W == 0

    nrow = -(-(B + 1) // 8) * 8
    cc = jnp.concatenate([c, c_ctx[None], jnp.zeros((nrow - B - 1, D), F32)], axis=0)
    mod = _ada(cc, ada_w, ada_b)
    modrows = jnp.stack([jnp.broadcast_to(mod[:, B:B + 1], (DEPTH, B, 6 * D)), mod[:, :B]], axis=2)
    modrows = modrows.reshape(DEPTH, 2 * B, 1, 6 * D)

    cos, sin = _rope_tables(T)
    xall = jnp.concatenate([ctx, x], axis=1)
    row = lambda a: a.reshape(1, -1)

    for i in range(DEPTH):
        with_ctx = i < DEPTH - 1
        final = i == DEPTH - 1
        z = _in_proj(xall, modrows[i], row(norm1_g[i]), _layout_w_in(w_in[i]))

        pool_o = _pool(z, pool_w[i].astype(BF16), row(pool_scale[i]))

        o_b = _gla(z, _layout_gate_w(gla_gate_w[i, 1], MISC_LRB), row(gla_gate_b[i, 1]), fwd=False)
        gla_o = _gla(z, _layout_gate_w(gla_gate_w[i, 0], MISC_LRF), row(gla_gate_b[i, 0]), fwd=True,
                     ob=o_b, gn=row(gla_norm_g[i]))

        q, k, v = _mla_up(z, cos, sin, row(mla_q_norm_g[i]), row(mla_kv_norm_g[i]),
                          _layout_w_uq(mla_w_uq[i]), mla_w_ukv[i].astype(BF16))
        mla_o = _attn(q, k, v, with_ctx=with_ctx)

        x1, h2t = _merge(pool_o, gla_o, mla_o, z, xall, modrows[i], branch_w[i].astype(BF16),
                        w_out[i].astype(BF16), row(norm2_g[i]), with_ctx=with_ctx)

        rows = x1.shape[1]
        keys = peer_keys[i].reshape(2 * PEER_HEADS, PEER_N_KEYS, PEER_HALF).astype(BF16)
        sel = _peer_score(h2t, peer_wq[i].T.astype(BF16), keys)
        xall = _peer_dense(h2t, sel, _pack_rows(peer_u[i]), _pack_rows(peer_v[i].T), x1.reshape(B * rows, D),
                           modrows[i], row(final_norm_g), tn=_peer_tile(rows), rows_per_sample=rows,
                           ctx_rows=CTX_LEN if with_ctx else 0, final=final).reshape(B, rows, D)
    return xall
```

```python
import functools

import jax
import jax.numpy as jnp
import numpy as np
from jax import lax
from jax.experimental import pallas as pl
from jax.experimental.pallas import tpu as pltpu

F32 = jnp.float32
BF16 = jnp.bfloat16

D_MODEL = 1024
DEPTH = 2
CTX_LEN = 256
GRID_W = 64
NORM_EPS = 1e-6
POOL_WIDTH = 1024
POOL_WINDOWS = (2, 4, 8, 16)
POOL_GROUP = POOL_WIDTH // len(POOL_WINDOWS)
GLA_HEADS = 4
GLA_DK = 128
GLA_DV = 256
GLA_GATE_RANK = 16
GLA_TAU = 16.0
GLA_CHUNK = 64
MLA_HEADS = 8
MLA_Q_RANK = 384
MLA_KV_RANK = 256
MLA_NOPE = 128
MLA_ROPE = 64
MLA_V = 128
ROPE_BASE = 10000.0
N_BRANCH = 3
PEER_HEADS = 8
PEER_N_KEYS = 128
PEER_TOPK = 16
PEER_QDIM = 256
PEER_HALF = PEER_QDIM // 2
IN_SPLITS = (POOL_WIDTH, GLA_HEADS * GLA_DK, GLA_HEADS * GLA_DK, GLA_HEADS * GLA_DV, GLA_HEADS * GLA_DV,
             GLA_GATE_RANK, GLA_GATE_RANK, MLA_Q_RANK, MLA_KV_RANK, MLA_ROPE, N_BRANCH * D_MODEL)

LANES = 128
BF16_ROWS = 16
TM = 256
VMEM_LIMIT = 56 * 1024 * 1024

COL_P = 0
COL_Q = 1024
COL_K = 1536
COL_V = 2048
COL_R = 3072
COL_GZ = 4096
COL_KVDN = 7168
COL_MISC = 7424
COL_SW = 7552
COL_QDN = 7680
Z_COLS = 8192
IN_NBLK = 4096
MISC_LRF = MLA_ROPE
MISC_LRB = MLA_ROPE + GLA_GATE_RANK
MQ = MLA_NOPE + LANES


def _cparams(*sem):
    return pltpu.CompilerParams(dimension_semantics=sem, vmem_limit_bytes=VMEM_LIMIT)


def _rms(x, g):
    return x * lax.rsqrt(jnp.mean(x * x, axis=-1, keepdims=True) + NORM_EPS) * g


def _dot(a, b):
    return jnp.dot(a, b, preferred_element_type=F32)


def _dot_nt(a, b):
    return lax.dot_general(a, b, (((1,), (1,)), ((), ())), preferred_element_type=F32)


def _dot_tn(a, b):
    return lax.dot_general(a, b, (((0,), (0,)), ((), ())), preferred_element_type=F32)


def _ada_kernel(c_ref, w_ref, b_ref, o_ref):
    c = c_ref[...]
    a = (c * jax.nn.sigmoid(c)).astype(BF16)
    o_ref[0] = _dot(a, w_ref[0].astype(BF16)) + b_ref[0]


def _ada(cc, ada_w, ada_b):
    L, D, N = ada_w.shape
    rows = cc.shape[0]
    nb = N // D
    return pl.pallas_call(
        _ada_kernel,
        grid=(L, nb),
        in_specs=[pl.BlockSpec((rows, D), lambda l, j: (0, 0)),
                  pl.BlockSpec((1, D, D), lambda l, j: (l, 0, j)),
                  pl.BlockSpec((1, 1, D), lambda l, j: (l, 0, j))],
        out_specs=pl.BlockSpec((1, rows, D), lambda l, j: (l, 0, j)),
        out_shape=jax.ShapeDtypeStruct((L, rows, N), F32),
        compiler_params=_cparams("parallel", "parallel"),
        name="ada_mod",
    )(cc, ada_w, ada_b.reshape(L, 1, N))


def _in_proj_kernel(x_ref, mod_ref, g_ref, w_ref, z_ref):
    D = D_MODEL
    x = x_ref[0]
    mod = mod_ref[0]
    h = _rms(x, g_ref[...]) * (1.0 + mod[:, D:2 * D]) + mod[:, 0:D]
    z_ref[0] = _dot(h.astype(BF16), w_ref[...]).astype(BF16)


def _in_proj(xall, modrows, g, w):
    B, R, D = xall.shape
    nt = R // TM
    nn = Z_COLS // IN_NBLK
    return pl.pallas_call(
        _in_proj_kernel,
        grid=(nn, B, nt),
        in_specs=[pl.BlockSpec((1, TM, D), lambda n, b, t: (b, t, 0)),
                  pl.BlockSpec((1, 1, 6 * D), lambda n, b, t: (2 * b + jnp.minimum(t, 1), 0, 0)),
                  pl.BlockSpec((1, D), lambda n, b, t: (0, 0)),
                  pl.BlockSpec((D, IN_NBLK), lambda n, b, t: (0, n))],
        out_specs=pl.BlockSpec((1, TM, IN_NBLK), lambda n, b, t: (b, t, n)),
        out_shape=jax.ShapeDtypeStruct((B, R, Z_COLS), BF16),
        compiler_params=_cparams("parallel", "parallel", "parallel"),
        name="in_proj",
    )(xall, modrows, g, w)


def _pool_kernel(u_ref, w_ref, sc_ref, o_ref):
    R = u_ref.shape[1]
    row = lax.broadcasted_iota(jnp.int32, (R, POOL_GROUP), 0)
    seg_lo = jnp.where(row < CTX_LEN, 0, CTX_LEN)
    seg_hi = jnp.where(row < CTX_LEN, CTX_LEN, R)
    for gi, win in enumerate(POOL_WINDOWS):
        sl = slice(gi * POOL_GROUP, (gi + 1) * POOL_GROUP)
        u = u_ref[0, :, sl].astype(F32)
        lo_off, hi_off = win // 2, win - win // 2
        acc = jnp.zeros_like(u)
        for d in range(-lo_off, hi_off):
            shifted = u if d == 0 else pltpu.roll(u, (R - d) % R, axis=0)
            ok = (row + d >= seg_lo) & (row + d < seg_hi)
            acc = acc + jnp.where(ok, shifted, 0.0)
        cnt = (jnp.minimum(row + hi_off, seg_hi) - jnp.maximum(row - lo_off, seg_lo)).astype(F32)
        diff = (acc / cnt - u).astype(BF16)
        o_ref[0, :, sl] = (_dot(diff, w_ref[gi]) * sc_ref[:, sl]).astype(BF16)


def _pool(z, pool_w, pool_scale):
    B, R, _ = z.shape
    return pl.pallas_call(
        _pool_kernel,
        grid=(B,),
        in_specs=[pl.BlockSpec((1, R, POOL_WIDTH), lambda b: (b, 0, COL_P // POOL_WIDTH)),
                  pl.BlockSpec(pool_w.shape, lambda b: (0, 0, 0)),
                  pl.BlockSpec((1, POOL_WIDTH), lambda b: (0, 0))],
        out_specs=pl.BlockSpec((1, R, POOL_WIDTH), lambda b: (b, 0, 0)),
        out_shape=jax.ShapeDtypeStruct((B, R, POOL_WIDTH), BF16),
        compiler_params=_cparams("parallel"),
        name="pool",
    )(z, pool_w, pool_scale)


def _split3(x):
    a = x.astype(BF16)
    r = x - a.astype(F32)
    b = r.astype(BF16)
    c = (r - b.astype(F32)).astype(BF16)
    return a, b, c


def _gla_kernel(*refs, fwd):
    if fwd:
        q_ref, k_ref, v_ref, misc_ref, gw_ref, gb_ref, ob_ref, r_ref, gn_ref, o_ref, st_ref = refs
    else:
        q_ref, k_ref, v_ref, misc_ref, gw_ref, gb_ref, o_ref, st_ref = refs
    C = GLA_CHUNK
    nchunk = TM // C
    HK = GLA_HEADS * GLA_DK

    @pl.when(pl.program_id(1) == 0)
    def _():
        st_ref[...] = jnp.zeros_like(st_ref)

    zg = _dot(misc_ref[0], gw_ref[...]) + gb_ref[...]
    g = (jnp.minimum(zg, 0.0) - jnp.log1p(jnp.exp(-jnp.abs(zg)))) * (1.0 / GLA_TAU)

    row = lax.broadcasted_iota(jnp.int32, (TM, TM), 0)
    col = lax.broadcasted_iota(jnp.int32, (TM, TM), 1)
    shift = C.bit_length() - 1
    same = (row >> shift) == (col >> shift)
    tri = same & ((col <= row) if fwd else (col >= row))
    ones = jnp.concatenate([jnp.where(tri, 1.0, 0.0), jnp.where(same, 1.0, 0.0)], axis=0).astype(BF16)
    g1, g2, g3 = _split3(g)
    both = _dot(ones, g1) + _dot(ones, g2) + _dot(ones, g3)
    b = both[:TM]
    bt = both[TM:]

    qe = q_ref[0].astype(F32) * (GLA_DK ** -0.5) * jnp.exp(b)
    ke = (k_ref[0].astype(F32) * jnp.exp(-b)).astype(BF16)
    kd = (k_ref[0].astype(F32) * jnp.exp(bt - b)).astype(BF16)
    qe = qe.astype(BF16)
    dec = jnp.exp(bt)
    v = v_ref[0]

    order = range(nchunk) if fwd else range(nchunk - 1, -1, -1)
    for h in range(GLA_HEADS):
        ks = slice(h * GLA_DK, (h + 1) * GLA_DK)
        vs = slice(h * GLA_DV, (h + 1) * GLA_DV)
        att = jnp.where(tri, _dot_nt(qe[:, ks], ke[:, ks]), 0.0).astype(BF16)
        o_in = _dot(att, v[:, vs])
        st = st_ref[h]
        parts = [None] * nchunk
        for j in order:
            rs = slice(j * C, (j + 1) * C)
            parts[j] = o_in[rs] + _dot_nt(qe[rs, ks], st.astype(BF16))
            st = st * dec[j * C:j * C + 1, ks] + _dot_tn(v[rs, vs], kd[rs, ks])
        st_ref[h] = st
        o = jnp.concatenate(parts, axis=0)
        if fwd:
            o = o + ob_ref[0, :, vs].astype(F32)
            r = r_ref[0, :, vs].astype(F32)
            o = _rms(o, gn_ref[...]) * (r * jax.nn.sigmoid(r))
        o_ref[0, :, vs] = o.astype(BF16)


def _gla(z, gw, gb, *, fwd, ob=None, gn=None):
    B, R, _ = z.shape
    nt = R // TM
    HK, HV = GLA_HEADS * GLA_DK, GLA_HEADS * GLA_DV
    if fwd:
        tile = lambda c: c
    else:
        tile = lambda c: jnp.where(c == 0, 0, nt - c)
    in_specs = [pl.BlockSpec((1, TM, HK), lambda b, c: (b, tile(c), COL_Q // HK)),
                pl.BlockSpec((1, TM, HK), lambda b, c: (b, tile(c), COL_K // HK)),
                pl.BlockSpec((1, TM, HV), lambda b, c: (b, tile(c), COL_V // HV)),
                pl.BlockSpec((1, TM, LANES), lambda b, c: (b, tile(c), COL_MISC // LANES)),
                pl.BlockSpec((LANES, HK), lambda b, c: (0, 0)),
                pl.BlockSpec((1, HK), lambda b, c: (0, 0))]
    args = [z, z, z, z, gw, gb]
    if fwd:
        in_specs += [pl.BlockSpec((1, TM, HV), lambda b, c: (b, tile(c), 0)),
                     pl.BlockSpec((1, TM, HV), lambda b, c: (b, tile(c), COL_R // HV)),
                     pl.BlockSpec((1, GLA_DV), lambda b, c: (0, 0))]
        args += [ob, z, gn]
    return pl.pallas_call(
        functools.partial(_gla_kernel, fwd=fwd),
        grid=(B, nt),
        in_specs=in_specs,
        out_specs=pl.BlockSpec((1, TM, HV), lambda b, c: (b, tile(c), 0)),
        out_shape=jax.ShapeDtypeStruct((B, R, HV), BF16),
        scratch_shapes=[pltpu.VMEM((GLA_HEADS, GLA_DV, GLA_DK), F32)],
        compiler_params=_cparams("parallel", "arbitrary"),
        name="gla_fwd" if fwd else "gla_bwd",
    )(*args)


def _mla_up_kernel(qdn_ref, kvdn_ref, misc_ref, sw_ref, cos_ref, sin_ref, qg_ref, kvg_ref, wq_ref, wkv_ref,
                   q_ref, k_ref, v_ref):
    H = MLA_HEADS
    scale = (MLA_NOPE + MLA_ROPE) ** -0.5 * float(np.log2(np.e))
    cos = cos_ref[...]
    sin = sin_ref[...]
    qn = _rms(qdn_ref[0].astype(F32), qg_ref[...]).astype(BF16)
    qall = _dot(qn, wq_ref[...])
    for h in range(H):
        base = h * MQ
        q_ref[0, :, base:base + MLA_NOPE] = (qall[:, base:base + MLA_NOPE] * scale).astype(BF16)
        rot = (qall[:, base + MLA_NOPE:base + MQ] * cos
               + qall[:, H * MQ + h * LANES:H * MQ + (h + 1) * LANES] * sin)
        q_ref[0, :, base + MLA_NOPE:base + MQ] = (rot * scale).astype(BF16)
    kvn = _rms(kvdn_ref[0].astype(F32), kvg_ref[...]).astype(BF16)
    kvall = _dot(kvn, wkv_ref[...])
    kr = (misc_ref[0].astype(F32) * cos + sw_ref[0].astype(F32) * sin).astype(BF16)
    for h in range(H):
        src = h * (MLA_NOPE + MLA_V)
        k_ref[0, :, h * MQ:h * MQ + MLA_NOPE] = kvall[:, src:src + MLA_NOPE].astype(BF16)
        k_ref[0, :, h * MQ + MLA_NOPE:(h + 1) * MQ] = kr
        v_ref[0, :, h * MLA_V:(h + 1) * MLA_V] = kvall[:, src + MLA_NOPE:src + MLA_NOPE + MLA_V].astype(BF16)


def _mla_up(z, cos, sin, qg, kvg, wq, wkv):
    B, R, _ = z.shape
    nt = R // TM
    H = MLA_HEADS
    const = lambda b, t: (0, 0)
    return pl.pallas_call(
        _mla_up_kernel,
        grid=(B, nt),
        in_specs=[pl.BlockSpec((1, TM, MLA_Q_RANK), lambda b, t: (b, t, COL_QDN // MLA_Q_RANK)),
                  pl.BlockSpec((1, TM, MLA_KV_RANK), lambda b, t: (b, t, COL_KVDN // MLA_KV_RANK)),
                  pl.BlockSpec((1, TM, LANES), lambda b, t: (b, t, COL_MISC // LANES)),
                  pl.BlockSpec((1, TM, LANES), lambda b, t: (b, t, COL_SW // LANES)),
                  pl.BlockSpec((TM, LANES), lambda b, t: (t, 0)),
                  pl.BlockSpec((TM, LANES), lambda b, t: (t, 0)),
                  pl.BlockSpec(qg.shape, const),
                  pl.BlockSpec(kvg.shape, const),
                  pl.BlockSpec(wq.shape, const),
                  pl.BlockSpec(wkv.shape, const)],
        out_specs=[pl.BlockSpec((1, TM, H * MQ), lambda b, t: (b, t, 0)),
                   pl.BlockSpec((1, TM, H * MQ), lambda b, t: (b, t, 0)),
                   pl.BlockSpec((1, TM, H * MLA_V), lambda b, t: (b, t, 0))],
        out_shape=[jax.ShapeDtypeStruct((B, R, H * MQ), BF16),
                   jax.ShapeDtypeStruct((B, R, H * MQ), BF16),
                   jax.ShapeDtypeStruct((B, R, H * MLA_V), BF16)],
        compiler_params=_cparams("parallel", "parallel"),
        name="mla_up",
    )(z, z, z, z, cos, sin, qg, kvg, wq, wkv)


ATTN_HEADS = 8


def _attn_kernel(q_ref, k_ref, v_ref, o_ref, *, q_off):
    R = k_ref.shape[1]

    def attend(nk):
        for j in range(ATTN_HEADS):
            s = _dot_nt(q_ref[0, :, j * MQ:(j + 1) * MQ], k_ref[0, :nk, j * MQ:(j + 1) * MQ])
            p = jnp.exp2(s - jnp.max(s, axis=-1, keepdims=True))
            l = jnp.sum(p, axis=-1, keepdims=True)
            o = _dot(p.astype(BF16), v_ref[0, :nk, j * MLA_V:(j + 1) * MLA_V]) / l
            o_ref[0, :, j * MLA_V:(j + 1) * MLA_V] = o.astype(BF16)

    if q_off == 0:
        qi = pl.program_id(2)

        @pl.when(qi == 0)
        def _():
            attend(CTX_LEN)

        @pl.when(qi > 0)
        def _():
            attend(R)
    else:
        attend(R)


def _attn(q, k, v, *, with_ctx):
    B, R, _ = q.shape
    q_off = 0 if with_ctx else CTX_LEN // TM
    nq = R // TM - q_off
    G = ATTN_HEADS
    return pl.pallas_call(
        functools.partial(_attn_kernel, q_off=q_off),
        grid=(B, MLA_HEADS // G, nq),
        in_specs=[pl.BlockSpec((1, TM, G * MQ), lambda b, h, i: (b, i + q_off, h)),
                  pl.BlockSpec((1, R, G * MQ), lambda b, h, i: (b, 0, h)),
                  pl.BlockSpec((1, R, G * MLA_V), lambda b, h, i: (b, 0, h))],
        out_specs=pl.BlockSpec((1, TM, G * MLA_V), lambda b, h, i: (b, i + q_off, h)),
        out_shape=jax.ShapeDtypeStruct((B, R, MLA_HEADS * MLA_V), BF16),
        compiler_params=_cparams("parallel", "parallel", "arbitrary"),
        name="mla_attn",
    )(q, k, v)


def _merge_kernel(po_ref, go_ref, mo_ref, gz0_ref, gz1_ref, gz2_ref, x_ref, mod_ref, bw_ref, wo_ref, g_ref,
                  x1_ref, h2t_ref):
    D = D_MODEL
    m = None
    for i, (o_ref, gz_ref) in enumerate(((po_ref, gz0_ref), (go_ref, gz1_ref), (mo_ref, gz2_ref))):
        t = jax.nn.sigmoid(gz_ref[0].astype(F32)) * _dot(o_ref[0], bw_ref[i])
        m = t if m is None else m + t
    y = _dot(m.astype(BF16), wo_ref[...])
    mod = mod_ref[0]
    x1 = x_ref[0] + mod[:, 2 * D:3 * D] * y
    x1_ref[0] = x1
    h2 = _rms(x1, g_ref[...]) * (1.0 + mod[:, 4 * D:5 * D]) + mod[:, 3 * D:4 * D]
    h2t_ref[...] = pltpu.bitcast(h2.T.astype(BF16), jnp.int32)


def _merge(pool_o, gla_o, mla_o, z, xall, modrows, bw, wo, g, *, with_ctx):
    B, R, D = xall.shape
    t_off = 0 if with_ctx else CTX_LEN // TM
    nt = R // TM - t_off
    tok = lambda b, t: (b, t + t_off, 0)
    gzb = COL_GZ // D
    wpr = _words_per_row()
    mo_off = t_off if mla_o.shape[1] == R else 0
    return pl.pallas_call(
        _merge_kernel,
        grid=(B, nt),
        in_specs=[pl.BlockSpec((1, TM, D), tok),
                  pl.BlockSpec((1, TM, D), tok),
                  pl.BlockSpec((1, TM, D), lambda b, t: (b, t + mo_off, 0)),
                  pl.BlockSpec((1, TM, D), lambda b, t: (b, t + t_off, gzb)),
                  pl.BlockSpec((1, TM, D), lambda b, t: (b, t + t_off, gzb + 1)),
                  pl.BlockSpec((1, TM, D), lambda b, t: (b, t + t_off, gzb + 2)),
                  pl.BlockSpec((1, TM, D), tok),
                  pl.BlockSpec((1, 1, 6 * D), lambda b, t: (2 * b + jnp.minimum(t + t_off, 1), 0, 0)),
                  pl.BlockSpec(bw.shape, lambda b, t: (0, 0, 0)),
                  pl.BlockSpec(wo.shape, lambda b, t: (0, 0)),
                  pl.BlockSpec((1, D), lambda b, t: (0, 0))],
        out_specs=[pl.BlockSpec((1, TM, D), lambda b, t: (b, t, 0)),
                   pl.BlockSpec((D // wpr, TM), lambda b, t: (0, b * nt + t))],
        out_shape=[jax.ShapeDtypeStruct((B, nt * TM, D), F32),
                   jax.ShapeDtypeStruct((D // wpr, B * nt * TM), jnp.int32)],
        compiler_params=_cparams("parallel", "parallel"),
        name="merge",
    )(pool_o, gla_o, mla_o, z, z, z, xall, modrows, bw, wo, g)


def _top16(s):
    nk, n = s.shape
    iota = lax.broadcasted_iota(jnp.int32, (nk, n), 0).astype(F32)
    slot = lax.broadcasted_iota(jnp.int32, (PEER_TOPK, n), 0)
    rank = jnp.full((nk, n), float(PEER_TOPK), F32)
    vals = jnp.zeros((PEER_TOPK, n), F32)
    for j in range(PEER_TOPK):
        m = jnp.max(s, axis=0, keepdims=True)
        idx = jnp.min(jnp.where(s == m, iota, float(nk)), axis=0, keepdims=True)
        hit = iota == idx
        rank = jnp.where(hit, float(j), rank)
        s = jnp.where(hit, -jnp.inf, s)
        vals = jnp.where(slot == j, m, vals)
    return vals, rank


PAIR_SHORT = PEER_TOPK // 2
PAIR_ROWS = PEER_TOPK + (PEER_TOPK - 1) * PAIR_SHORT
PAIR_VALID = sum(PEER_TOPK // (i + 1) for i in range(PEER_TOPK))


def _pair_rows(t2):
    return jnp.concatenate([t2] + [t2[0:PAIR_SHORT]] * (PEER_TOPK - 1), axis=0)


def _pair_candidates(t1, t2):
    K = PEER_TOPK
    n = t1.shape[1]
    first = jnp.concatenate([jnp.broadcast_to(t1[0:1], (K, n))]
                            + [jnp.broadcast_to(t1[i:i + 1], (PAIR_SHORT, n)) for i in range(1, K)], axis=0)
    r = lax.broadcasted_iota(jnp.int32, (PAIR_ROWS, n), 0)
    sh = PAIR_SHORT.bit_length() - 1
    i_of = jnp.where(r < K, 0, ((r - K) >> sh) + 1)
    j_of = jnp.where(r < K, r, (r - K) & (PAIR_SHORT - 1))
    cand = jnp.where((i_of + 1) * (j_of + 1) <= K, first + _pair_rows(t2), -jnp.inf)
    return cand, (i_of * K + j_of).astype(F32)


def _top16_values(s, with_rank):
    n = s.shape[1]
    slot = lax.broadcasted_iota(jnp.int32, (PEER_TOPK, n), 0)
    vals = jnp.zeros((PEER_TOPK, n), F32)
    rank = jnp.full(s.shape, float(PEER_TOPK), F32) if with_rank else None
    for j in range(PEER_TOPK):
        m = jnp.max(s, axis=0, keepdims=True)
        hit = s == m
        if with_rank:
            rank = jnp.where(hit, float(j), rank)
        s = jnp.where(hit, -jnp.inf, s)
        vals = jnp.where(slot == j, m, vals)
    used = jnp.sum(jnp.where(s == -jnp.inf, 1.0, 0.0), axis=0, keepdims=True)
    return vals, rank, jnp.where(used == float(PEER_TOPK), 1.0, 0.0)


def _pair_counts(t1, t2):
    K = PEER_TOPK
    n = t1.shape[1]
    cand0, _ = _pair_candidates(t1, t2)
    cand = cand0
    zsum = jnp.zeros((1, n), F32)
    best = m = None
    for j in range(K):
        m = jnp.max(cand, axis=0, keepdims=True)
        cand = jnp.where(cand == m, -jnp.inf, cand)
        if j == 0:
            best = m
        zsum = zsum + jnp.exp(m - best)
    used = jnp.sum(jnp.where(cand == -jnp.inf, 1.0, 0.0), axis=0, keepdims=True)
    exact = jnp.where(used == float(PAIR_ROWS - PAIR_VALID + K), 1.0, 0.0)
    picked = jnp.where(cand0 >= m, 1.0, 0.0)
    slot = lax.broadcasted_iota(jnp.int32, (K, n), 0)
    cnt = jnp.zeros((K, n), F32)
    for i in range(K):
        blk = picked[0:K] if i == 0 else picked[K + (i - 1) * PAIR_SHORT:K + i * PAIR_SHORT]
        cnt = jnp.where(slot == i, jnp.sum(blk, axis=0, keepdims=True), cnt)
    return cnt, zsum, exact


def _pair_top16(t1, t2):
    K = PEER_TOPK
    n = t1.shape[1]
    cand, flat = _pair_candidates(t1, t2)
    irow = lax.broadcasted_iota(jnp.int32, (K, n), 0).astype(F32)
    cnt = jnp.zeros((K, n), F32)
    zsum = jnp.zeros((1, n), F32)
    best = None
    for j in range(K):
        m = jnp.max(cand, axis=0, keepdims=True)
        idx = jnp.min(jnp.where(cand == m, flat, float(K * K)), axis=0, keepdims=True)
        cand = jnp.where(flat == idx, -jnp.inf, cand)
        cnt = cnt + jnp.where(irow == jnp.floor(idx * (1.0 / K)), 1.0, 0.0)
        if j == 0:
            best = m
        zsum = zsum + jnp.exp(m - best)
    return cnt, zsum


def _words_per_row():
    return 4 // jnp.dtype(BF16).itemsize


def _splat_words(x):
    bits = lax.bitcast_convert_type(x.astype(BF16).astype(F32), jnp.int32)
    if _words_per_row() == 1:
        return bits
    return bits | lax.shift_right_logical(bits, 16)


def _peer_score_kernel(h2t_ref, wqt_ref, keys_ref, c2_ref, p2_ref, cnt_ref, ka_ref, qt_ref):
    K = PEER_TOPK
    qt_ref[...] = _dot(wqt_ref[...], pltpu.bitcast(h2t_ref[...], BF16)).astype(BF16)

    def store(h, cs, s1, s2, t1, rank1, t2, rank2, cnt, zsum):
        rowcnt = jnp.zeros_like(s1)
        for j in range(K):
            in_row = (s1 == t1[j:j + 1]) if rank1 is None else (rank1 == float(j))
            rowcnt = jnp.where(in_row, cnt[j:j + 1], rowcnt)
        c2_ref[h, :, cs] = pltpu.bitcast(rank2.astype(BF16), jnp.int32)
        p2_ref[h, :, cs] = pltpu.bitcast(jnp.exp(s2 - t2[0:1]).astype(BF16), jnp.int32)
        cnt_ref[h, :, cs] = _splat_words(rowcnt)
        ka_ref[h, :, cs] = _splat_words(jnp.exp(s1 - t1[0:1]) / zsum)

    def head(h, carry):
        base = pl.multiple_of(h * PEER_QDIM, PEER_QDIM)
        s1_all = _dot(keys_ref[h], qt_ref[pl.ds(base, PEER_HALF), :])
        s2_all = _dot(keys_ref[PEER_HEADS + h], qt_ref[pl.ds(base + PEER_HALF, PEER_HALF), :])
        chunks = []
        for c in range(s1_all.shape[1] // LANES):
            cs = slice(c * LANES, (c + 1) * LANES)
            s1, s2 = s1_all[:, cs], s2_all[:, cs]
            t1, _, ok1 = _top16_values(s1, with_rank=False)
            t2, rank2, ok2 = _top16_values(s2, with_rank=True)
            cnt, zsum, ok3 = _pair_counts(t1, t2)
            store(h, cs, s1, s2, t1, None, t2, rank2, cnt, zsum)
            chunks.append((cs, s1, s2, jnp.min(ok1 * ok2 * ok3)))

        for cs, s1, s2, exact in chunks:
            @pl.when(exact < 0.5)
            def _(cs=cs, s1=s1, s2=s2):
                t1x, rank1x = _top16(s1)
                t2x, rank2x = _top16(s2)
                cntx, zx = _pair_top16(t1x, t2x)
                store(h, cs, s1, s2, t1x, rank1x, t2x, rank2x, cntx, zx)

        return carry

    lax.fori_loop(0, PEER_HEADS, head, 0)


PEER_SCORE_TN = 4 * LANES


def _peer_score(h2t, wqt, keys):
    ntok = h2t.shape[1]
    tn = PEER_SCORE_TN
    nw = PEER_N_KEYS // _words_per_row()
    tbl = lambda rows: jax.ShapeDtypeStruct((PEER_HEADS, rows, ntok), jnp.int32)
    ospec = lambda rows: pl.BlockSpec((PEER_HEADS, rows, tn), lambda i: (0, 0, i))
    return pl.pallas_call(
        _peer_score_kernel,
        grid=(ntok // tn,),
        in_specs=[pl.BlockSpec((h2t.shape[0], tn), lambda i: (0, i)),
                  pl.BlockSpec(wqt.shape, lambda i: (0, 0)),
                  pl.BlockSpec(keys.shape, lambda i: (0, 0, 0))],
        out_specs=[ospec(nw), ospec(nw), ospec(PEER_N_KEYS), ospec(PEER_N_KEYS)],
        out_shape=[tbl(nw), tbl(nw), tbl(PEER_N_KEYS), tbl(PEER_N_KEYS)],
        scratch_shapes=[pltpu.VMEM((PEER_HEADS * PEER_QDIM, tn), BF16)],
        compiler_params=_cparams("parallel"),
        name="peer_score",
    )(h2t, wqt, keys)


PEER_KEYS_PER_STEP = 8


PEER_KEY_GROUP = 2


def _peer_dense_kernel(h2t_ref, c2_ref, p2_ref, cnt_ref, ka_ref, u_ref, vt_ref, x1_ref, modc_ref, modl_ref, fg_ref,
                       o_ref, acc_ref, pre_ref, hid_ref, *, ctx_rows, rows_per_sample, final):
    D = D_MODEL
    NK = PEER_N_KEYS
    tn = h2t_ref.shape[1]
    e = pl.program_id(1)

    @pl.when(e == 0)
    def _():
        acc_ref[...] = jnp.zeros_like(acc_ref)

    pre_ref[...] = _dot(pltpu.bitcast(u_ref[...], BF16), pltpu.bitcast(h2t_ref[...], BF16))

    kps = PEER_KEYS_PER_STEP
    row = lambda ref, i, h: ref[h, i:i + 1, :]
    cnt_rows = [[row(cnt_ref, i, h) for h in range(PEER_HEADS)] for i in range(kps)]
    ka_rows = [[row(ka_ref, i, h) for h in range(PEER_HEADS)] for i in range(kps)]
    wpr = _words_per_row()

    def key_rows(i, h, cs):
        rep = lambda words: pltpu.bitcast(jnp.broadcast_to(words[:, cs], (BF16_ROWS // wpr, LANES)), BF16)
        return rep(cnt_rows[i][h]), rep(ka_rows[i][h])

    for c in range(tn // LANES):
        cs = slice(c * LANES, (c + 1) * LANES)
        for g in range(0, kps, PEER_KEY_GROUP):
            group = range(g, g + PEER_KEY_GROUP)
            w = {i: None for i in group}
            for h in range(PEER_HEADS):
                c2 = pltpu.bitcast(c2_ref[h, :, cs], BF16).reshape(NK // BF16_ROWS, BF16_ROWS, LANES)
                p2 = pltpu.bitcast(p2_ref[h, :, cs], BF16).reshape(NK // BF16_ROWS, BF16_ROWS, LANES)
                for i in group:
                    cnt, kaw = key_rows(i, h, cs)
                    t = jnp.where(c2 < cnt[None], p2, 0.0) * kaw[None]
                    w[i] = t if w[i] is None else w[i] + t
            for i in group:
                rows = slice(i * NK, (i + 1) * NK)
                x = pre_ref[rows, cs].astype(BF16)
                act = 0.5 * x * (1.0 + lax.erf(x * (2.0 ** -0.5)))
                hid_ref[rows, cs] = act * w[i].reshape(NK, LANES)

    acc_ref[...] += _dot(pltpu.bitcast(vt_ref[...], BF16), hid_ref[...])

    @pl.when(e == pl.num_programs(1) - 1)
    def _():
        y = acc_ref[...].T
        g_lat = modl_ref[0][:, 5 * D:6 * D]
        if ctx_rows:
            g_ctx = modc_ref[0][:, 5 * D:6 * D]
            tiles = rows_per_sample // tn
            r0 = (pl.program_id(0) % tiles) * tn
            row = r0 + lax.broadcasted_iota(jnp.int32, (tn, D), 0)
            gate = jnp.where(row < ctx_rows, g_ctx, g_lat)
        else:
            gate = g_lat
        x2 = x1_ref[...] + gate * y
        if final:
            x2 = _rms(x2, fg_ref[...])
        o_ref[...] = x2


def _peer_dense(h2t, sel, u, vt, x1, modrows, fg, *, tn, rows_per_sample, ctx_rows, final):
    ntok, D = x1.shape
    wpr = _words_per_row()
    etile = PEER_KEYS_PER_STEP * PEER_N_KEYS
    kspec = pl.BlockSpec((PEER_HEADS, PEER_KEYS_PER_STEP, tn), lambda i, s: (0, s, i))
    ne = u.shape[0] * wpr // etile
    tiles = rows_per_sample // tn
    tspec = lambda t: pl.BlockSpec((PEER_HEADS, t.shape[1], tn), lambda i, s: (0, 0, i))
    return pl.pallas_call(
        functools.partial(_peer_dense_kernel, ctx_rows=ctx_rows, rows_per_sample=rows_per_sample, final=final),
        grid=(ntok // tn, ne),
        in_specs=[pl.BlockSpec((D // wpr, tn), lambda i, s: (0, i)),
                  tspec(sel[0]), tspec(sel[1]), kspec, kspec,
                  pl.BlockSpec((etile // wpr, D), lambda i, s: (s, 0)),
                  pl.BlockSpec((D // wpr, etile), lambda i, s: (0, s)),
                  pl.BlockSpec((tn, D), lambda i, s: (i, 0), pipeline_mode=pl.Buffered(1)),
                  pl.BlockSpec((1, 1, 6 * D), lambda i, s: (2 * (i // tiles), 0, 0)),
                  pl.BlockSpec((1, 1, 6 * D), lambda i, s: (2 * (i // tiles) + 1, 0, 0)),
                  pl.BlockSpec((1, D), lambda i, s: (0, 0))],
        out_specs=pl.BlockSpec((tn, D), lambda i, s: (i, 0)),
        out_shape=jax.ShapeDtypeStruct((ntok, D), F32),
        scratch_shapes=[pltpu.VMEM((D, tn), F32),
                        pltpu.VMEM((etile, tn), F32),
                        pltpu.VMEM((etile, tn), BF16)],
        compiler_params=_cparams("parallel", "arbitrary"),
        name="peer_dense",
    )(h2t, *sel, u, vt, x1, modrows, modrows, fg)


def _pack_rows_kernel(x_ref, o_ref):
    o_ref[...] = pltpu.bitcast(x_ref[...].astype(BF16), jnp.int32)


def _pack_rows(x):
    R, C = x.shape
    wpr = _words_per_row()
    br, bc = 512, 1024
    return pl.pallas_call(
        _pack_rows_kernel,
        grid=(R // br, C // bc),
        in_specs=[pl.BlockSpec((br, bc), lambda i, j: (i, j))],
        out_specs=pl.BlockSpec((br // wpr, bc), lambda i, j: (i, j)),
        out_shape=jax.ShapeDtypeStruct((R // wpr, C), jnp.int32),
        compiler_params=_cparams("parallel", "parallel"),
        name="pack_rows",
    )(x)


def _peer_tile(rows_per_sample):
    for tn in (1024, 768, 512, 256):
        if rows_per_sample % tn == 0:
            return tn
    raise ValueError(rows_per_sample)


_SWAP = np.concatenate([np.arange(16, 32), np.arange(0, 16), np.arange(48, 64), np.arange(32, 48)])


def _layout_w_in(w):
    D = w.shape[0]
    idx = np.cumsum(IN_SPLITS)[:-1]
    p, q, k, v, r, lrf, lrb, qdn, kvdn, krope, gz = jnp.split(w, idx, axis=1)
    zeros = lambda n: jnp.zeros((D, n), w.dtype)
    misc = jnp.concatenate([krope, lrf, lrb, zeros(LANES - MLA_ROPE - 2 * GLA_GATE_RANK)], axis=1)
    sw = jnp.concatenate([krope[:, _SWAP], zeros(LANES - MLA_ROPE)], axis=1)
    out = jnp.concatenate([p, q, k, v, r, gz, kvdn, misc, sw, qdn], axis=1)
    return jnp.concatenate([out, zeros(Z_COLS - out.shape[1])], axis=1).astype(BF16)


def _layout_gate_w(gate_w, row0):
    HK = GLA_HEADS * GLA_DK
    pad = jnp.zeros((LANES, HK), gate_w.dtype)
    return pad.at[row0:row0 + GLA_GATE_RANK].set(gate_w).astype(BF16)


def _layout_w_uq(w):
    H = MLA_HEADS
    w3 = w.reshape(MLA_Q_RANK, H, MLA_NOPE + MLA_ROPE)
    qn, qr = w3[..., :MLA_NOPE], w3[..., MLA_NOPE:]
    z = jnp.zeros((MLA_Q_RANK, H, LANES - MLA_ROPE), w.dtype)
    main = jnp.concatenate([qn, qr, z], axis=-1).reshape(MLA_Q_RANK, H * MQ)
    swp = jnp.concatenate([qr[..., _SWAP], z], axis=-1).reshape(MLA_Q_RANK, H * LANES)
    return jnp.concatenate([main, swp], axis=1).astype(BF16)


def _rope_tables(seq):
    half = MLA_ROPE // 2
    t = jnp.arange(seq)
    inv = ROPE_BASE ** (-jnp.arange(0, half, 2, dtype=F32) / half)
    ar = (t // GRID_W).astype(F32)[:, None] * inv
    ac = (t % GRID_W).astype(F32)[:, None] * inv
    cos = jnp.concatenate([jnp.cos(ar), jnp.cos(ar), jnp.cos(ac), jnp.cos(ac)], axis=1)
    sin = jnp.concatenate([-jnp.sin(ar), jnp.sin(ar), -jnp.sin(ac), jnp.sin(ac)], axis=1)
    cos = jnp.concatenate([jnp.ones((CTX_LEN, MLA_ROPE), F32), cos], axis=0)
    sin = jnp.concatenate([jnp.zeros((CTX_LEN, MLA_ROPE), F32), sin], axis=0)
    pad = jnp.zeros((CTX_LEN + seq, LANES - MLA_ROPE), F32)
    return jnp.concatenate([cos, pad], axis=1), jnp.concatenate([sin, pad], axis=1)


def kernel(x, c, ctx, c_ctx, ada_w, ada_b, norm1_g, norm2_g, w_in, pool_w, pool_scale, gla_gate_w, gla_gate_b, gla_norm_g, mla_q_norm_g, mla_kv_norm_g, mla_w_uq, mla_w_ukv, branch_w, w_out, peer_wq, peer_keys, peer_u, peer_v, final_norm_g):
    B, T, D = x.shape
    R = CTX_LEN + T
    assert D == D_MODEL and ctx.shape[1] == CTX_LEN == TM and T % TM == 0 and T % GRID_W == 0

    nrow = -(-(B + 1) // 8) * 8
    cc = jnp.concatenate([c, c_ctx[None], jnp.zeros((nrow - B - 1, D), F32)], axis=0)
    mod = _ada(cc, ada_w, ada_b)
    modrows = jnp.stack([jnp.broadcast_to(mod[:, B:B + 1], (DEPTH, B, 6 * D)), mod[:, :B]], axis=2)
    modrows = modrows.reshape(DEPTH, 2 * B, 1, 6 * D)

    cos, sin = _rope_tables(T)
    xall = jnp.concatenate([ctx, x], axis=1)
    row = lambda a: a.reshape(1, -1)

    for i in range(DEPTH):
        with_ctx = i < DEPTH - 1
        final = i == DEPTH - 1
        z = _in_proj(xall, modrows[i], row(norm1_g[i]), _layout_w_in(w_in[i]))

        pool_o = _pool(z, pool_w[i].astype(BF16), row(pool_scale[i]))

        o_b = _gla(z, _layout_gate_w(gla_gate_w[i, 1], MISC_LRB), row(gla_gate_b[i, 1]), fwd=False)
        gla_o = _gla(z, _layout_gate_w(gla_gate_w[i, 0], MISC_LRF), row(gla_gate_b[i, 0]), fwd=True,
                     ob=o_b, gn=row(gla_norm_g[i]))

        q, k, v = _mla_up(z, cos, sin, row(mla_q_norm_g[i]), row(mla_kv_norm_g[i]),
                          _layout_w_uq(mla_w_uq[i]), mla_w_ukv[i].astype(BF16))
        mla_o = _attn(q, k, v, with_ctx=with_ctx)

        x1, h2t = _merge(pool_o, gla_o, mla_o, z, xall, modrows[i], branch_w[i].astype(BF16),
                        w_out[i].astype(BF16), row(norm2_g[i]), with_ctx=with_ctx)

        rows = x1.shape[1]
        keys = peer_keys[i].reshape(2 * PEER_HEADS, PEER_N_KEYS, PEER_HALF).astype(BF16)
        sel = _peer_score(h2t, peer_wq[i].T.astype(BF16), keys)
        xall = _peer_dense(h2t, sel, _pack_rows(peer_u[i]), _pack_rows(peer_v[i].T), x1.reshape(B * rows, D),
                           modrows[i], row(final_norm_g), tn=_peer_tile(rows), rows_per_sample=rows,
                           ctx_rows=CTX_LEN if with_ctx else 0, final=final).reshape(B, rows, D)
    return xall
```

```python
import functools

import jax
import jax.numpy as jnp
import numpy as np
from jax import lax
from jax.experimental import pallas as pl
from jax.experimental.pallas import tpu as pltpu

F32 = jnp.float32
BF16 = jnp.bfloat16

D_MODEL = 1024
DEPTH = 2
CTX_LEN = 256
GRID_W = 64
NORM_EPS = 1e-6
POOL_WIDTH = 1024
POOL_WINDOWS = (2, 4, 8, 16)
POOL_GROUP = POOL_WIDTH // len(POOL_WINDOWS)
GLA_HEADS = 4
GLA_DK = 128
GLA_DV = 256
GLA_GATE_RANK = 16
GLA_TAU = 16.0
GLA_CHUNK = 64
MLA_HEADS = 8
MLA_Q_RANK = 384
MLA_KV_RANK = 256
MLA_NOPE = 128
MLA_ROPE = 64
MLA_V = 128
ROPE_BASE = 10000.0
N_BRANCH = 3
PEER_HEADS = 8
PEER_N_KEYS = 128
PEER_TOPK = 16
PEER_QDIM = 256
PEER_HALF = PEER_QDIM // 2
IN_SPLITS = (POOL_WIDTH, GLA_HEADS * GLA_DK, GLA_HEADS * GLA_DK, GLA_HEADS * GLA_DV, GLA_HEADS * GLA_DV,
             GLA_GATE_RANK, GLA_GATE_RANK, MLA_Q_RANK, MLA_KV_RANK, MLA_ROPE, N_BRANCH * D_MODEL)

LANES = 128
BF16_ROWS = 16
TM = 256
VMEM_LIMIT = 56 * 1024 * 1024

COL_P = 0
COL_Q = 1024
COL_K = 1536
COL_V = 2048
COL_R = 3072
COL_GZ = 4096
COL_KVDN = 7168
COL_MISC = 7424
COL_SW = 7552
COL_QDN = 7680
Z_COLS = 8192
IN_NBLK = 4096
MISC_LRF = MLA_ROPE
MISC_LRB = MLA_ROPE + GLA_GATE_RANK
MQ = MLA_NOPE + LANES


def _cparams(*sem):
    return pltpu.CompilerParams(dimension_semantics=sem, vmem_limit_bytes=VMEM_LIMIT)


def _rms(x, g):
    return x * lax.rsqrt(jnp.mean(x * x, axis=-1, keepdims=True) + NORM_EPS) * g


def _dot(a, b):
    return jnp.dot(a, b, preferred_element_type=F32)


def _dot_nt(a, b):
    return lax.dot_general(a, b, (((1,), (1,)), ((), ())), preferred_element_type=F32)


def _dot_tn(a, b):
    return lax.dot_general(a, b, (((0,), (0,)), ((), ())), preferred_element_type=F32)


def _ada_kernel(c_ref, w_ref, b_ref, o_ref):
    c = c_ref[...]
    a = (c * jax.nn.sigmoid(c)).astype(BF16)
    o_ref[0] = _dot(a, w_ref[0].astype(BF16)) + b_ref[0]


def _ada(cc, ada_w, ada_b):
    L, D, N = ada_w.shape
    rows = cc.shape[0]
    nb = N // D
    return pl.pallas_call(
        _ada_kernel,
        grid=(L, nb),
        in_specs=[pl.BlockSpec((rows, D), lambda l, j: (0, 0)),
                  pl.BlockSpec((1, D, D), lambda l, j: (l, 0, j)),
                  pl.BlockSpec((1, 1, D), lambda l, j: (l, 0, j))],
        out_specs=pl.BlockSpec((1, rows, D), lambda l, j: (l, 0, j)),
        out_shape=jax.ShapeDtypeStruct((L, rows, N), F32),
        compiler_params=_cparams("parallel", "parallel"),
        name="ada_mod",
    )(cc, ada_w, ada_b.reshape(L, 1, N))


def _in_proj_kernel(x_ref, mod_ref, g_ref, w_ref, z_ref):
    D = D_MODEL
    x = x_ref[0]
    mod = mod_ref[0]
    h = _rms(x, g_ref[...]) * (1.0 + mod[:, D:2 * D]) + mod[:, 0:D]
    z_ref[0] = _dot(h.astype(BF16), w_ref[...]).astype(BF16)


def _in_proj(xall, modrows, g, w):
    B, R, D = xall.shape
    nt = R // TM
    nn = Z_COLS // IN_NBLK
    return pl.pallas_call(
        _in_proj_kernel,
        grid=(nn, B, nt),
        in_specs=[pl.BlockSpec((1, TM, D), lambda n, b, t: (b, t, 0)),
                  pl.BlockSpec((1, 1, 6 * D), lambda n, b, t: (2 * b + jnp.minimum(t, 1), 0, 0)),
                  pl.BlockSpec((1, D), lambda n, b, t: (0, 0)),
                  pl.BlockSpec((D, IN_NBLK), lambda n, b, t: (0, n))],
        out_specs=pl.BlockSpec((1, TM, IN_NBLK), lambda n, b, t: (b, t, n)),
        out_shape=jax.ShapeDtypeStruct((B, R, Z_COLS), BF16),
        compiler_params=_cparams("parallel", "parallel", "parallel"),
        name="in_proj",
    )(xall, modrows, g, w)


def _pool_kernel(u_ref, w_ref, sc_ref, o_ref):
    R = u_ref.shape[1]
    row = lax.broadcasted_iota(jnp.int32, (R, POOL_GROUP), 0)
    seg_lo = jnp.where(row < CTX_LEN, 0, CTX_LEN)
    seg_hi = jnp.where(row < CTX_LEN, CTX_LEN, R)
    for gi, win in enumerate(POOL_WINDOWS):
        sl = slice(gi * POOL_GROUP, (gi + 1) * POOL_GROUP)
        u = u_ref[0, :, sl].astype(F32)
        lo_off, hi_off = win // 2, win - win // 2
        acc = jnp.zeros_like(u)
        for d in range(-lo_off, hi_off):
            shifted = u if d == 0 else pltpu.roll(u, (R - d) % R, axis=0)
            ok = (row + d >= seg_lo) & (row + d < seg_hi)
            acc = acc + jnp.where(ok, shifted, 0.0)
        cnt = (jnp.minimum(row + hi_off, seg_hi) - jnp.maximum(row - lo_off, seg_lo)).astype(F32)
        diff = (acc / cnt - u).astype(BF16)
        o_ref[0, :, sl] = (_dot(diff, w_ref[gi]) * sc_ref[:, sl]).astype(BF16)


def _pool(z, pool_w, pool_scale):
    B, R, _ = z.shape
    return pl.pallas_call(
        _pool_kernel,
        grid=(B,),
        in_specs=[pl.BlockSpec((1, R, POOL_WIDTH), lambda b: (b, 0, COL_P // POOL_WIDTH)),
                  pl.BlockSpec(pool_w.shape, lambda b: (0, 0, 0)),
                  pl.BlockSpec((1, POOL_WIDTH), lambda b: (0, 0))],
        out_specs=pl.BlockSpec((1, R, POOL_WIDTH), lambda b: (b, 0, 0)),
        out_shape=jax.ShapeDtypeStruct((B, R, POOL_WIDTH), BF16),
        compiler_params=_cparams("parallel"),
        name="pool",
    )(z, pool_w, pool_scale)


def _split3(x):
    a = x.astype(BF16)
    r = x - a.astype(F32)
    b = r.astype(BF16)
    c = (r - b.astype(F32)).astype(BF16)
    return a, b, c


def _gla_kernel(*refs, fwd):
    if fwd:
        q_ref, k_ref, v_ref, misc_ref, gw_ref, gb_ref, ob_ref, r_ref, gn_ref, o_ref, st_ref = refs
    else:
        q_ref, k_ref, v_ref, misc_ref, gw_ref, gb_ref, o_ref, st_ref = refs
    C = GLA_CHUNK
    nchunk = TM // C
    HK = GLA_HEADS * GLA_DK

    @pl.when(pl.program_id(1) == 0)
    def _():
        st_ref[...] = jnp.zeros_like(st_ref)

    zg = _dot(misc_ref[0], gw_ref[...]) + gb_ref[...]
    g = (jnp.minimum(zg, 0.0) - jnp.log1p(jnp.exp(-jnp.abs(zg)))) * (1.0 / GLA_TAU)

    row = lax.broadcasted_iota(jnp.int32, (TM, TM), 0)
    col = lax.broadcasted_iota(jnp.int32, (TM, TM), 1)
    shift = C.bit_length() - 1
    same = (row >> shift) == (col >> shift)
    tri = same & ((col <= row) if fwd else (col >= row))
    ones = jnp.concatenate([jnp.where(tri, 1.0, 0.0), jnp.where(same, 1.0, 0.0)], axis=0).astype(BF16)
    g1, g2, g3 = _split3(g)
    both = _dot(ones, g1) + _dot(ones, g2) + _dot(ones, g3)
    b = both[:TM]
    bt = both[TM:]

    qe = q_ref[0].astype(F32) * (GLA_DK ** -0.5) * jnp.exp(b)
    ke = (k_ref[0].astype(F32) * jnp.exp(-b)).astype(BF16)
    kd = (k_ref[0].astype(F32) * jnp.exp(bt - b)).astype(BF16)
    qe = qe.astype(BF16)
    dec = jnp.exp(bt)
    v = v_ref[0]

    order = range(nchunk) if fwd else range(nchunk - 1, -1, -1)
    for h in range(GLA_HEADS):
        ks = slice(h * GLA_DK, (h + 1) * GLA_DK)
        vs = slice(h * GLA_DV, (h + 1) * GLA_DV)
        att = jnp.where(tri, _dot_nt(qe[:, ks], ke[:, ks]), 0.0).astype(BF16)
        o_in = _dot(att, v[:, vs])
        st = st_ref[h]
        parts = [None] * nchunk
        for j in order:
            rs = slice(j * C, (j + 1) * C)
            parts[j] = o_in[rs] + _dot_nt(qe[rs, ks], st.astype(BF16))
            st = st * dec[j * C:j * C + 1, ks] + _dot_tn(v[rs, vs], kd[rs, ks])
        st_ref[h] = st
        o = jnp.concatenate(parts, axis=0)
        if fwd:
            o = o + ob_ref[0, :, vs].astype(F32)
            r = r_ref[0, :, vs].astype(F32)
            o = _rms(o, gn_ref[...]) * (r * jax.nn.sigmoid(r))
        o_ref[0, :, vs] = o.astype(BF16)


def _gla(z, gw, gb, *, fwd, ob=None, gn=None):
    B, R, _ = z.shape
    nt = R // TM
    HK, HV = GLA_HEADS * GLA_DK, GLA_HEADS * GLA_DV
    if fwd:
        tile = lambda c: c
    else:
        tile = lambda c: jnp.where(c == 0, 0, nt - c)
    in_specs = [pl.BlockSpec((1, TM, HK), lambda b, c: (b, tile(c), COL_Q // HK)),
                pl.BlockSpec((1, TM, HK), lambda b, c: (b, tile(c), COL_K // HK)),
                pl.BlockSpec((1, TM, HV), lambda b, c: (b, tile(c), COL_V // HV)),
                pl.BlockSpec((1, TM, LANES), lambda b, c: (b, tile(c), COL_MISC // LANES)),
                pl.BlockSpec((LANES, HK), lambda b, c: (0, 0)),
                pl.BlockSpec((1, HK), lambda b, c: (0, 0))]
    args = [z, z, z, z, gw, gb]
    if fwd:
        in_specs += [pl.BlockSpec((1, TM, HV), lambda b, c: (b, tile(c), 0)),
                     pl.BlockSpec((1, TM, HV), lambda b, c: (b, tile(c), COL_R // HV)),
                     pl.BlockSpec((1, GLA_DV), lambda b, c: (0, 0))]
        args += [ob, z, gn]
    return pl.pallas_call(
        functools.partial(_gla_kernel, fwd=fwd),
        grid=(B, nt),
        in_specs=in_specs,
        out_specs=pl.BlockSpec((1, TM, HV), lambda b, c: (b, tile(c), 0)),
        out_shape=jax.ShapeDtypeStruct((B, R, HV), BF16),
        scratch_shapes=[pltpu.VMEM((GLA_HEADS, GLA_DV, GLA_DK), F32)],
        compiler_params=_cparams("parallel", "arbitrary"),
        name="gla_fwd" if fwd else "gla_bwd",
    )(*args)


def _mla_up_kernel(qdn_ref, kvdn_ref, misc_ref, sw_ref, cos_ref, sin_ref, qg_ref, kvg_ref, wq_ref, wkv_ref,
                   q_ref, k_ref, v_ref):
    H = MLA_HEADS
    scale = (MLA_NOPE + MLA_ROPE) ** -0.5 * float(np.log2(np.e))
    cos = cos_ref[...]
    sin = sin_ref[...]
    qn = _rms(qdn_ref[0].astype(F32), qg_ref[...]).astype(BF16)
    qall = _dot(qn, wq_ref[...])
    for h in range(H):
        base = h * MQ
        q_ref[0, :, base:base + MLA_NOPE] = (qall[:, base:base + MLA_NOPE] * scale).astype(BF16)
        rot = (qall[:, base + MLA_NOPE:base + MQ] * cos
               + qall[:, H * MQ + h * LANES:H * MQ + (h + 1) * LANES] * sin)
        q_ref[0, :, base + MLA_NOPE:base + MQ] = (rot * scale).astype(BF16)
    kvn = _rms(kvdn_ref[0].astype(F32), kvg_ref[...]).astype(BF16)
    kvall = _dot(kvn, wkv_ref[...])
    kr = (misc_ref[0].astype(F32) * cos + sw_ref[0].astype(F32) * sin).astype(BF16)
    for h in range(H):
        src = h * (MLA_NOPE + MLA_V)
        k_ref[0, :, h * MQ:h * MQ + MLA_NOPE] = kvall[:, src:src + MLA_NOPE].astype(BF16)
        k_ref[0, :, h * MQ + MLA_NOPE:(h + 1) * MQ] = kr
        v_ref[0, :, h * MLA_V:(h + 1) * MLA_V] = kvall[:, src + MLA_NOPE:src + MLA_NOPE + MLA_V].astype(BF16)


def _mla_up(z, cos, sin, qg, kvg, wq, wkv):
    B, R, _ = z.shape
    nt = R // TM
    H = MLA_HEADS
    const = lambda b, t: (0, 0)
    return pl.pallas_call(
        _mla_up_kernel,
        grid=(B, nt),
        in_specs=[pl.BlockSpec((1, TM, MLA_Q_RANK), lambda b, t: (b, t, COL_QDN // MLA_Q_RANK)),
                  pl.BlockSpec((1, TM, MLA_KV_RANK), lambda b, t: (b, t, COL_KVDN // MLA_KV_RANK)),
                  pl.BlockSpec((1, TM, LANES), lambda b, t: (b, t, COL_MISC // LANES)),
                  pl.BlockSpec((1, TM, LANES), lambda b, t: (b, t, COL_SW // LANES)),
                  pl.BlockSpec((TM, LANES), lambda b, t: (t, 0)),
                  pl.BlockSpec((TM, LANES), lambda b, t: (t, 0)),
                  pl.BlockSpec(qg.shape, const),
                  pl.BlockSpec(kvg.shape, const),
                  pl.BlockSpec(wq.shape, const),
                  pl.BlockSpec(wkv.shape, const)],
        out_specs=[pl.BlockSpec((1, TM, H * MQ), lambda b, t: (b, t, 0)),
                   pl.BlockSpec((1, TM, H * MQ), lambda b, t: (b, t, 0)),
                   pl.BlockSpec((1, TM, H * MLA_V), lambda b, t: (b, t, 0))],
        out_shape=[jax.ShapeDtypeStruct((B, R, H * MQ), BF16),
                   jax.ShapeDtypeStruct((B, R, H * MQ), BF16),
                   jax.ShapeDtypeStruct((B, R, H * MLA_V), BF16)],
        compiler_params=_cparams("parallel", "parallel"),
        name="mla_up",
    )(z, z, z, z, cos, sin, qg, kvg, wq, wkv)


ATTN_HEADS = 8


def _attn_kernel(q_ref, k_ref, v_ref, o_ref, *, q_off):
    R = k_ref.shape[1]

    def attend(nk):
        for j in range(ATTN_HEADS):
            s = _dot_nt(q_ref[0, :, j * MQ:(j + 1) * MQ], k_ref[0, :nk, j * MQ:(j + 1) * MQ])
            p = jnp.exp2(s - jnp.max(s, axis=-1, keepdims=True))
            l = jnp.sum(p, axis=-1, keepdims=True)
            o = _dot(p.astype(BF16), v_ref[0, :nk, j * MLA_V:(j + 1) * MLA_V]) / l
            o_ref[0, :, j * MLA_V:(j + 1) * MLA_V] = o.astype(BF16)

    if q_off == 0:
        qi = pl.program_id(2)

        @pl.when(qi == 0)
        def _():
            attend(CTX_LEN)

        @pl.when(qi > 0)
        def _():
            attend(R)
    else:
        attend(R)


def _attn(q, k, v, *, with_ctx):
    B, R, _ = q.shape
    q_off = 0 if with_ctx else CTX_LEN // TM
    nq = R // TM - q_off
    G = ATTN_HEADS
    return pl.pallas_call(
        functools.partial(_attn_kernel, q_off=q_off),
        grid=(B, MLA_HEADS // G, nq),
        in_specs=[pl.BlockSpec((1, TM, G * MQ), lambda b, h, i: (b, i + q_off, h)),
                  pl.BlockSpec((1, R, G * MQ), lambda b, h, i: (b, 0, h)),
                  pl.BlockSpec((1, R, G * MLA_V), lambda b, h, i: (b, 0, h))],
        out_specs=pl.BlockSpec((1, TM, G * MLA_V), lambda b, h, i: (b, i + q_off, h)),
        out_shape=jax.ShapeDtypeStruct((B, R, MLA_HEADS * MLA_V), BF16),
        compiler_params=_cparams("parallel", "parallel", "arbitrary"),
        name="mla_attn",
    )(q, k, v)


def _merge_kernel(po_ref, go_ref, mo_ref, gz0_ref, gz1_ref, gz2_ref, x_ref, mod_ref, bw_ref, wo_ref, g_ref,
                  x1_ref, h2t_ref):
    D = D_MODEL
    m = None
    for i, (o_ref, gz_ref) in enumerate(((po_ref, gz0_ref), (go_ref, gz1_ref), (mo_ref, gz2_ref))):
        t = jax.nn.sigmoid(gz_ref[0].astype(F32)) * _dot(o_ref[0], bw_ref[i])
        m = t if m is None else m + t
    y = _dot(m.astype(BF16), wo_ref[...])
    mod = mod_ref[0]
    x1 = x_ref[0] + mod[:, 2 * D:3 * D] * y
    x1_ref[0] = x1
    h2 = _rms(x1, g_ref[...]) * (1.0 + mod[:, 4 * D:5 * D]) + mod[:, 3 * D:4 * D]
    h2t_ref[...] = pltpu.bitcast(h2.T.astype(BF16), jnp.int32)


def _merge(pool_o, gla_o, mla_o, z, xall, modrows, bw, wo, g, *, with_ctx):
    B, R, D = xall.shape
    t_off = 0 if with_ctx else CTX_LEN // TM
    nt = R // TM - t_off
    tok = lambda b, t: (b, t + t_off, 0)
    gzb = COL_GZ // D
    wpr = _words_per_row()
    mo_off = t_off if mla_o.shape[1] == R else 0
    return pl.pallas_call(
        _merge_kernel,
        grid=(B, nt),
        in_specs=[pl.BlockSpec((1, TM, D), tok),
                  pl.BlockSpec((1, TM, D), tok),
                  pl.BlockSpec((1, TM, D), lambda b, t: (b, t + mo_off, 0)),
                  pl.BlockSpec((1, TM, D), lambda b, t: (b, t + t_off, gzb)),
                  pl.BlockSpec((1, TM, D), lambda b, t: (b, t + t_off, gzb + 1)),
                  pl.BlockSpec((1, TM, D), lambda b, t: (b, t + t_off, gzb + 2)),
                  pl.BlockSpec((1, TM, D), tok),
                  pl.BlockSpec((1, 1, 6 * D), lambda b, t: (2 * b + jnp.minimum(t + t_off, 1), 0, 0)),
                  pl.BlockSpec(bw.shape, lambda b, t: (0, 0, 0)),
                  pl.BlockSpec(wo.shape, lambda b, t: (0, 0)),
                  pl.BlockSpec((1, D), lambda b, t: (0, 0))],
        out_specs=[pl.BlockSpec((1, TM, D), lambda b, t: (b, t, 0)),
                   pl.BlockSpec((D // wpr, TM), lambda b, t: (0, b * nt + t))],
        out_shape=[jax.ShapeDtypeStruct((B, nt * TM, D), F32),
                   jax.ShapeDtypeStruct((D // wpr, B * nt * TM), jnp.int32)],
        compiler_params=_cparams("parallel", "parallel"),
        name="merge",
    )(pool_o, gla_o, mla_o, z, z, z, xall, modrows, bw, wo, g)


def _top16(s):
    nk, n = s.shape
    iota = lax.broadcasted_iota(jnp.int32, (nk, n), 0).astype(F32)
    slot = lax.broadcasted_iota(jnp.int32, (PEER_TOPK, n), 0)
    rank = jnp.full((nk, n), float(PEER_TOPK), F32)
    vals = jnp.zeros((PEER_TOPK, n), F32)
    for j in range(PEER_TOPK):
        m = jnp.max(s, axis=0, keepdims=True)
        idx = jnp.min(jnp.where(s == m, iota, float(nk)), axis=0, keepdims=True)
        hit = iota == idx
        rank = jnp.where(hit, float(j), rank)
        s = jnp.where(hit, -jnp.inf, s)
        vals = jnp.where(slot == j, m, vals)
    return vals, rank


PAIR_SHORT = PEER_TOPK // 2
PAIR_ROWS = PEER_TOPK + (PEER_TOPK - 1) * PAIR_SHORT
PAIR_VALID = sum(PEER_TOPK // (i + 1) for i in range(PEER_TOPK))


def _pair_rows(t2):
    return jnp.concatenate([t2] + [t2[0:PAIR_SHORT]] * (PEER_TOPK - 1), axis=0)


def _pair_candidates(t1, t2):
    K = PEER_TOPK
    n = t1.shape[1]
    first = jnp.concatenate([jnp.broadcast_to(t1[0:1], (K, n))]
                            + [jnp.broadcast_to(t1[i:i + 1], (PAIR_SHORT, n)) for i in range(1, K)], axis=0)
    r = lax.broadcasted_iota(jnp.int32, (PAIR_ROWS, n), 0)
    sh = PAIR_SHORT.bit_length() - 1
    i_of = jnp.where(r < K, 0, ((r - K) >> sh) + 1)
    j_of = jnp.where(r < K, r, (r - K) & (PAIR_SHORT - 1))
    cand = jnp.where((i_of + 1) * (j_of + 1) <= K, first + _pair_rows(t2), -jnp.inf)
    return cand, (i_of * K + j_of).astype(F32)


def _top16_values(s, with_rank):
    n = s.shape[1]
    slot = lax.broadcasted_iota(jnp.int32, (PEER_TOPK, n), 0)
    vals = jnp.zeros((PEER_TOPK, n), F32)
    rank = jnp.full(s.shape, float(PEER_TOPK), F32) if with_rank else None
    for j in range(PEER_TOPK):
        m = jnp.max(s, axis=0, keepdims=True)
        hit = s == m
        if with_rank:
            rank = jnp.where(hit, float(j), rank)
        s = jnp.where(hit, -jnp.inf, s)
        vals = jnp.where(slot == j, m, vals)
    used = jnp.sum(jnp.where(s == -jnp.inf, 1.0, 0.0), axis=0, keepdims=True)
    return vals, rank, jnp.where(used == float(PEER_TOPK), 1.0, 0.0)


PAIR_GROUPS = (((0, 0, 8),), ((0, 8, 8),), ((1, 0, 8),), ((2, 0, 5), (4, 0, 3)), ((3, 0, 4), (5, 0, 2), (6, 0, 2)),
               ((7, 0, 2),))
PAIR_SUB = 8
assert sorted([(i, j0 + k) for g in PAIR_GROUPS for i, j0, ln in g for k in range(ln)]
              + [(i, 0) for i in range(PAIR_SUB, PEER_TOPK)]) == sorted(
    (i, j) for i in range(PEER_TOPK) for j in range(PEER_TOPK) if (i + 1) * (j + 1) <= PEER_TOPK)


def _pair_counts(t1, t2):
    K = PEER_TOPK
    S = PAIR_SUB
    n = t1.shape[1]
    sub = lax.broadcasted_iota(jnp.int32, (S, n), 0)
    groups = []
    for runs in PAIR_GROUPS:
        val, s = None, 0
        for i, j0, length in runs:
            src = t2[j0 // S * S:(j0 // S + 1) * S]
            shift = (s - j0 % S) % S
            b = src if shift == 0 else pltpu.roll(src, shift, axis=0)
            v = t1[i:i + 1] + b
            val = v if val is None else jnp.where(sub >= s, v, val)
            s += length
        groups.append(val if s == S else jnp.where(sub < s, val, -jnp.inf))
    groups.append(t1[S:K] + t2[0:1])
    cand0 = jnp.concatenate(groups, axis=0)
    n_invalid = sum(S - sum(ln for _, _, ln in g) for g in PAIR_GROUPS)

    cand = cand0
    zsum = jnp.zeros((1, n), F32)
    best = m = None
    for j in range(K):
        m = jnp.max(cand, axis=0, keepdims=True)
        cand = jnp.where(cand == m, -jnp.inf, cand)
        if j == 0:
            best = m
        zsum = zsum + jnp.exp(m - best)
    used = jnp.sum(jnp.where(cand == -jnp.inf, 1.0, 0.0), axis=0, keepdims=True)
    exact = jnp.where(used == float(n_invalid + K), 1.0, 0.0)

    picked = jnp.where(cand0 >= m, 1.0, 0.0)
    low = jnp.zeros((S, n), F32)
    for gi, runs in enumerate(PAIR_GROUPS):
        blk, s = picked[gi * S:(gi + 1) * S], 0
        for i, _, length in runs:
            part = blk if length == S else jnp.where((sub >= s) & (sub < s + length), blk, 0.0)
            low = low + jnp.where(sub == i, jnp.sum(part, axis=0, keepdims=True), 0.0)
            s += length
    return jnp.concatenate([low, picked[len(PAIR_GROUPS) * S:]], axis=0), zsum, exact


def _pair_top16(t1, t2):
    K = PEER_TOPK
    n = t1.shape[1]
    cand, flat = _pair_candidates(t1, t2)
    irow = lax.broadcasted_iota(jnp.int32, (K, n), 0).astype(F32)
    cnt = jnp.zeros((K, n), F32)
    zsum = jnp.zeros((1, n), F32)
    best = None
    for j in range(K):
        m = jnp.max(cand, axis=0, keepdims=True)
        idx = jnp.min(jnp.where(cand == m, flat, float(K * K)), axis=0, keepdims=True)
        cand = jnp.where(flat == idx, -jnp.inf, cand)
        cnt = cnt + jnp.where(irow == jnp.floor(idx * (1.0 / K)), 1.0, 0.0)
        if j == 0:
            best = m
        zsum = zsum + jnp.exp(m - best)
    return cnt, zsum


def _words_per_row():
    return 4 // jnp.dtype(BF16).itemsize


def _splat_words(x):
    bits = lax.bitcast_convert_type(x.astype(BF16).astype(F32), jnp.int32)
    if _words_per_row() == 1:
        return bits
    return bits | lax.shift_right_logical(bits, 16)


def _peer_score_kernel(h2t_ref, wqt_ref, keys_ref, c2_ref, p2_ref, cnt_ref, ka_ref, qt_ref):
    K = PEER_TOPK
    qt_ref[...] = _dot(wqt_ref[...], pltpu.bitcast(h2t_ref[...], BF16)).astype(BF16)

    def store(h, cs, s1, s2, t1, rank1, t2, rank2, cnt, zsum):
        rowcnt = jnp.zeros_like(s1)
        for j in range(K):
            in_row = (s1 == t1[j:j + 1]) if rank1 is None else (rank1 == float(j))
            rowcnt = jnp.where(in_row, cnt[j:j + 1], rowcnt)
        c2_ref[h, :, cs] = pltpu.bitcast(rank2.astype(BF16), jnp.int32)
        p2_ref[h, :, cs] = pltpu.bitcast(jnp.exp(s2 - t2[0:1]).astype(BF16), jnp.int32)
        cnt_ref[h, :, cs] = _splat_words(rowcnt)
        ka_ref[h, :, cs] = _splat_words(jnp.exp(s1 - t1[0:1]) / zsum)

    def head(h, carry):
        base = pl.multiple_of(h * PEER_QDIM, PEER_QDIM)
        s1_all = _dot(keys_ref[h], qt_ref[pl.ds(base, PEER_HALF), :])
        s2_all = _dot(keys_ref[PEER_HEADS + h], qt_ref[pl.ds(base + PEER_HALF, PEER_HALF), :])
        chunks = []
        for c in range(s1_all.shape[1] // LANES):
            cs = slice(c * LANES, (c + 1) * LANES)
            s1, s2 = s1_all[:, cs], s2_all[:, cs]
            t1, _, ok1 = _top16_values(s1, with_rank=False)
            t2, rank2, ok2 = _top16_values(s2, with_rank=True)
            cnt, zsum, ok3 = _pair_counts(t1, t2)
            store(h, cs, s1, s2, t1, None, t2, rank2, cnt, zsum)
            chunks.append((cs, s1, s2, jnp.min(ok1 * ok2 * ok3)))

        for cs, s1, s2, exact in chunks:
            @pl.when(exact < 0.5)
            def _(cs=cs, s1=s1, s2=s2):
                t1x, rank1x = _top16(s1)
                t2x, rank2x = _top16(s2)
                cntx, zx = _pair_top16(t1x, t2x)
                store(h, cs, s1, s2, t1x, rank1x, t2x, rank2x, cntx, zx)

        return carry

    lax.fori_loop(0, PEER_HEADS, head, 0)


PEER_SCORE_TN = 4 * LANES


def _peer_score(h2t, wqt, keys):
    ntok = h2t.shape[1]
    tn = PEER_SCORE_TN
    nw = PEER_N_KEYS // _words_per_row()
    tbl = lambda rows: jax.ShapeDtypeStruct((PEER_HEADS, rows, ntok), jnp.int32)
    ospec = lambda rows: pl.BlockSpec((PEER_HEADS, rows, tn), lambda i: (0, 0, i))
    return pl.pallas_call(
        _peer_score_kernel,
        grid=(ntok // tn,),
        in_specs=[pl.BlockSpec((h2t.shape[0], tn), lambda i: (0, i)),
                  pl.BlockSpec(wqt.shape, lambda i: (0, 0)),
                  pl.BlockSpec(keys.shape, lambda i: (0, 0, 0))],
        out_specs=[ospec(nw), ospec(nw), ospec(PEER_N_KEYS), ospec(PEER_N_KEYS)],
        out_shape=[tbl(nw), tbl(nw), tbl(PEER_N_KEYS), tbl(PEER_N_KEYS)],
        scratch_shapes=[pltpu.VMEM((PEER_HEADS * PEER_QDIM, tn), BF16)],
        compiler_params=_cparams("parallel"),
        name="peer_score",
    )(h2t, wqt, keys)


PEER_KEYS_PER_STEP = 8


PEER_KEY_GROUP = 2


def _peer_dense_kernel(h2t_ref, c2_ref, p2_ref, cnt_ref, ka_ref, u_ref, vt_ref, x1_ref, modc_ref, modl_ref, fg_ref,
                       o_ref, acc_ref, pre_ref, hid_ref, *, ctx_rows, rows_per_sample, final):
    D = D_MODEL
    NK = PEER_N_KEYS
    tn = h2t_ref.shape[1]
    e = pl.program_id(1)

    @pl.when(e == 0)
    def _():
        acc_ref[...] = jnp.zeros_like(acc_ref)

    pre_ref[...] = _dot(pltpu.bitcast(u_ref[...], BF16), pltpu.bitcast(h2t_ref[...], BF16))

    kps = PEER_KEYS_PER_STEP
    row = lambda ref, i, h: ref[h, i:i + 1, :]
    cnt_rows = [[row(cnt_ref, i, h) for h in range(PEER_HEADS)] for i in range(kps)]
    ka_rows = [[row(ka_ref, i, h) for h in range(PEER_HEADS)] for i in range(kps)]
    wpr = _words_per_row()

    def key_rows(i, h, cs):
        rep = lambda words: pltpu.bitcast(jnp.broadcast_to(words[:, cs], (BF16_ROWS // wpr, LANES)), BF16)
        return rep(cnt_rows[i][h]), rep(ka_rows[i][h])

    for c in range(tn // LANES):
        cs = slice(c * LANES, (c + 1) * LANES)
        for g in range(0, kps, PEER_KEY_GROUP):
            group = range(g, g + PEER_KEY_GROUP)
            w = {i: None for i in group}
            for h in range(PEER_HEADS):
                c2 = pltpu.bitcast(c2_ref[h, :, cs], BF16).reshape(NK // BF16_ROWS, BF16_ROWS, LANES)
                p2 = pltpu.bitcast(p2_ref[h, :, cs], BF16).reshape(NK // BF16_ROWS, BF16_ROWS, LANES)
                for i in group:
                    cnt, kaw = key_rows(i, h, cs)
                    t = jnp.where(c2 < cnt[None], p2, 0.0) * kaw[None]
                    w[i] = t if w[i] is None else w[i] + t
            for i in group:
                rows = slice(i * NK, (i + 1) * NK)
                x = pre_ref[rows, cs].astype(BF16)
                act = 0.5 * x * (1.0 + lax.erf(x * (2.0 ** -0.5)))
                hid_ref[rows, cs] = act * w[i].reshape(NK, LANES)

    acc_ref[...] += _dot(pltpu.bitcast(vt_ref[...], BF16), hid_ref[...])

    @pl.when(e == pl.num_programs(1) - 1)
    def _():
        y = acc_ref[...].T
        g_lat = modl_ref[0][:, 5 * D:6 * D]
        if ctx_rows:
            g_ctx = modc_ref[0][:, 5 * D:6 * D]
            tiles = rows_per_sample // tn
            r0 = (pl.program_id(0) % tiles) * tn
            row = r0 + lax.broadcasted_iota(jnp.int32, (tn, D), 0)
            gate = jnp.where(row < ctx_rows, g_ctx, g_lat)
        else:
            gate = g_lat
        x2 = x1_ref[...] + gate * y
        if final:
            x2 = _rms(x2, fg_ref[...])
        o_ref[...] = x2


def _peer_dense(h2t, sel, u, vt, x1, modrows, fg, *, tn, rows_per_sample, ctx_rows, final):
    ntok, D = x1.shape
    wpr = _words_per_row()
    etile = PEER_KEYS_PER_STEP * PEER_N_KEYS
    kspec = pl.BlockSpec((PEER_HEADS, PEER_KEYS_PER_STEP, tn), lambda i, s: (0, s, i))
    ne = u.shape[0] * wpr // etile
    tiles = rows_per_sample // tn
    tspec = lambda t: pl.BlockSpec((PEER_HEADS, t.shape[1], tn), lambda i, s: (0, 0, i))
    return pl.pallas_call(
        functools.partial(_peer_dense_kernel, ctx_rows=ctx_rows, rows_per_sample=rows_per_sample, final=final),
        grid=(ntok // tn, ne),
        in_specs=[pl.BlockSpec((D // wpr, tn), lambda i, s: (0, i)),
                  tspec(sel[0]), tspec(sel[1]), kspec, kspec,
                  pl.BlockSpec((etile // wpr, D), lambda i, s: (s, 0)),
                  pl.BlockSpec((D // wpr, etile), lambda i, s: (0, s)),
                  pl.BlockSpec((tn, D), lambda i, s: (i, 0), pipeline_mode=pl.Buffered(1)),
                  pl.BlockSpec((1, 1, 6 * D), lambda i, s: (2 * (i // tiles), 0, 0)),
                  pl.BlockSpec((1, 1, 6 * D), lambda i, s: (2 * (i // tiles) + 1, 0, 0)),
                  pl.BlockSpec((1, D), lambda i, s: (0, 0))],
        out_specs=pl.BlockSpec((tn, D), lambda i, s: (i, 0)),
        out_shape=jax.ShapeDtypeStruct((ntok, D), F32),
        scratch_shapes=[pltpu.VMEM((D, tn), F32),
                        pltpu.VMEM((etile, tn), F32),
                        pltpu.VMEM((etile, tn), BF16)],
        compiler_params=_cparams("parallel", "arbitrary"),
        name="peer_dense",
    )(h2t, *sel, u, vt, x1, modrows, modrows, fg)


def _pack_rows_kernel(x_ref, o_ref):
    o_ref[...] = pltpu.bitcast(x_ref[...].astype(BF16), jnp.int32)


def _pack_rows(x):
    R, C = x.shape
    wpr = _words_per_row()
    br, bc = 512, 1024
    return pl.pallas_call(
        _pack_rows_kernel,
        grid=(R // br, C // bc),
        in_specs=[pl.BlockSpec((br, bc), lambda i, j: (i, j))],
        out_specs=pl.BlockSpec((br // wpr, bc), lambda i, j: (i, j)),
        out_shape=jax.ShapeDtypeStruct((R // wpr, C), jnp.int32),
        compiler_params=_cparams("parallel", "parallel"),
        name="pack_rows",
    )(x)


def _peer_tile(rows_per_sample):
    for tn in (1024, 768, 512, 256):
        if rows_per_sample % tn == 0:
            return tn
    raise ValueError(rows_per_sample)


_SWAP = np.concatenate([np.arange(16, 32), np.arange(0, 16), np.arange(48, 64), np.arange(32, 48)])


def _layout_w_in(w):
    D = w.shape[0]
    idx = np.cumsum(IN_SPLITS)[:-1]
    p, q, k, v, r, lrf, lrb, qdn, kvdn, krope, gz = jnp.split(w, idx, axis=1)
    zeros = lambda n: jnp.zeros((D, n), w.dtype)
    misc = jnp.concatenate([krope, lrf, lrb, zeros(LANES - MLA_ROPE - 2 * GLA_GATE_RANK)], axis=1)
    sw = jnp.concatenate([krope[:, _SWAP], zeros(LANES - MLA_ROPE)], axis=1)
    out = jnp.concatenate([p, q, k, v, r, gz, kvdn, misc, sw, qdn], axis=1)
    return jnp.concatenate([out, zeros(Z_COLS - out.shape[1])], axis=1).astype(BF16)


def _layout_gate_w(gate_w, row0):
    HK = GLA_HEADS * GLA_DK
    pad = jnp.zeros((LANES, HK), gate_w.dtype)
    return pad.at[row0:row0 + GLA_GATE_RANK].set(gate_w).astype(BF16)


def _layout_w_uq(w):
    H = MLA_HEADS
    w3 = w.reshape(MLA_Q_RANK, H, MLA_NOPE + MLA_ROPE)
    qn, qr = w3[..., :MLA_NOPE], w3[..., MLA_NOPE:]
    z = jnp.zeros((MLA_Q_RANK, H, LANES - MLA_ROPE), w.dtype)
    main = jnp.concatenate([qn, qr, z], axis=-1).reshape(MLA_Q_RANK, H * MQ)
    swp = jnp.concatenate([qr[..., _SWAP], z], axis=-1).reshape(MLA_Q_RANK, H * LANES)
    return jnp.concatenate([main, swp], axis=1).astype(BF16)


def _rope_tables(seq):
    half = MLA_ROPE // 2
    t = jnp.arange(seq)
    inv = ROPE_BASE ** (-jnp.arange(0, half, 2, dtype=F32) / half)
    ar = (t // GRID_W).astype(F32)[:, None] * inv
    ac = (t % GRID_W).astype(F32)[:, None] * inv
    cos = jnp.concatenate([jnp.cos(ar), jnp.cos(ar), jnp.cos(ac), jnp.cos(ac)], axis=1)
    sin = jnp.concatenate([-jnp.sin(ar), jnp.sin(ar), -jnp.sin(ac), jnp.sin(ac)], axis=1)
    cos = jnp.concatenate([jnp.ones((CTX_LEN, MLA_ROPE), F32), cos], axis=0)
    sin = jnp.concatenate([jnp.zeros((CTX_LEN, MLA_ROPE), F32), sin], axis=0)
    pad = jnp.zeros((CTX_LEN + seq, LANES - MLA_ROPE), F32)
    return jnp.concatenate([cos, pad], axis=1), jnp.concatenate([sin, pad], axis=1)


def kernel(x, c, ctx, c_ctx, ada_w, ada_b, norm1_g, norm2_g, w_in, pool_w, pool_scale, gla_gate_w, gla_gate_b, gla_norm_g, mla_q_norm_g, mla_kv_norm_g, mla_w_uq, mla_w_ukv, branch_w, w_out, peer_wq, peer_keys, peer_u, peer_v, final_norm_g):
    B, T, D = x.shape
    R = CTX_LEN + T
    assert D == D_MODEL and ctx.shape[1] == CTX_LEN == TM and T % TM == 0 and T % GRID_W == 0

    nrow = -(-(B + 1) // 8) * 8
    cc = jnp.concatenate([c, c_ctx[None], jnp.zeros((nrow - B - 1, D), F32)], axis=0)
    mod = _ada(cc, ada_w, ada_b)
    modrows = jnp.stack([jnp.broadcast_to(mod[:, B:B + 1], (DEPTH, B, 6 * D)), mod[:, :B]], axis=2)
    modrows = modrows.reshape(DEPTH, 2 * B, 1, 6 * D)

    cos, sin = _rope_tables(T)
    xall = jnp.concatenate([ctx, x], axis=1)
    row = lambda a: a.reshape(1, -1)

    for i in range(DEPTH):
        with_ctx = i < DEPTH - 1
        final = i == DEPTH - 1
        z = _in_proj(xall, modrows[i], row(norm1_g[i]), _layout_w_in(w_in[i]))

        pool_o = _pool(z, pool_w[i].astype(BF16), row(pool_scale[i]))

        o_b = _gla(z, _layout_gate_w(gla_gate_w[i, 1], MISC_LRB), row(gla_gate_b[i, 1]), fwd=False)
        gla_o = _gla(z, _layout_gate_w(gla_gate_w[i, 0], MISC_LRF), row(gla_gate_b[i, 0]), fwd=True,
                     ob=o_b, gn=row(gla_norm_g[i]))

        q, k, v = _mla_up(z, cos, sin, row(mla_q_norm_g[i]), row(mla_kv_norm_g[i]),
                          _layout_w_uq(mla_w_uq[i]), mla_w_ukv[i].astype(BF16))
        mla_o = _attn(q, k, v, with_ctx=with_ctx)

        x1, h2t = _merge(pool_o, gla_o, mla_o, z, xall, modrows[i], branch_w[i].astype(BF16),
                        w_out[i].astype(BF16), row(norm2_g[i]), with_ctx=with_ctx)

        rows = x1.shape[1]
        keys = peer_keys[i].reshape(2 * PEER_HEADS, PEER_N_KEYS, PEER_HALF).astype(BF16)
        sel = _peer_score(h2t, peer_wq[i].T.astype(BF16), keys)
        xall = _peer_dense(h2t, sel, _pack_rows(peer_u[i]), _pack_rows(peer_v[i].T), x1.reshape(B * rows, D),
                           modrows[i], row(final_norm_g), tn=_peer_tile(rows), rows_per_sample=rows,
                           ctx_rows=CTX_LEN if with_ctx else 0, final=final).reshape(B, rows, D)
    return xall
```

```python
import functools

import jax
import jax.numpy as jnp
import numpy as np
from jax import lax
from jax.experimental import pallas as pl
from jax.experimental.pallas import tpu as pltpu

F32 = jnp.float32
BF16 = jnp.bfloat16

D_MODEL = 1024
DEPTH = 2
CTX_LEN = 256
GRID_W = 64
NORM_EPS = 1e-6
POOL_WIDTH = 1024
POOL_WINDOWS = (2, 4, 8, 16)
POOL_GROUP = POOL_WIDTH // len(POOL_WINDOWS)
GLA_HEADS = 4
GLA_DK = 128
GLA_DV = 256
GLA_GATE_RANK = 16
GLA_TAU = 16.0
GLA_CHUNK = 64
MLA_HEADS = 8
MLA_Q_RANK = 384
MLA_KV_RANK = 256
MLA_NOPE = 128
MLA_ROPE = 64
MLA_V = 128
ROPE_BASE = 10000.0
N_BRANCH = 3
PEER_HEADS = 8
PEER_N_KEYS = 128
PEER_TOPK = 16
PEER_QDIM = 256
PEER_HALF = PEER_QDIM // 2
IN_SPLITS = (POOL_WIDTH, GLA_HEADS * GLA_DK, GLA_HEADS * GLA_DK, GLA_HEADS * GLA_DV, GLA_HEADS * GLA_DV,
             GLA_GATE_RANK, GLA_GATE_RANK, MLA_Q_RANK, MLA_KV_RANK, MLA_ROPE, N_BRANCH * D_MODEL)

LANES = 128
BF16_ROWS = 16
TM = 256
VMEM_LIMIT = 56 * 1024 * 1024

COL_P = 0
COL_Q = 1024
COL_K = 1536
COL_V = 2048
COL_R = 3072
COL_GZ = 4096
COL_KVDN = 7168
COL_MISC = 7424
COL_SW = 7552
COL_QDN = 7680
Z_COLS = 8192
IN_NBLK = 4096
MISC_LRF = MLA_ROPE
MISC_LRB = MLA_ROPE + GLA_GATE_RANK
MQ = MLA_NOPE + LANES


def _cparams(*sem):
    return pltpu.CompilerParams(dimension_semantics=sem, vmem_limit_bytes=VMEM_LIMIT)


def _rms(x, g):
    return x * lax.rsqrt(jnp.mean(x * x, axis=-1, keepdims=True) + NORM_EPS) * g


def _dot(a, b):
    return jnp.dot(a, b, preferred_element_type=F32)


def _dot_nt(a, b):
    return lax.dot_general(a, b, (((1,), (1,)), ((), ())), preferred_element_type=F32)


def _dot_tn(a, b):
    return lax.dot_general(a, b, (((0,), (0,)), ((), ())), preferred_element_type=F32)


def _ada_kernel(c_ref, w_ref, b_ref, o_ref):
    c = c_ref[...]
    a = (c * jax.nn.sigmoid(c)).astype(BF16)
    o_ref[0] = _dot(a, w_ref[0].astype(BF16)) + b_ref[0]


def _ada(cc, ada_w, ada_b):
    L, D, N = ada_w.shape
    rows = cc.shape[0]
    nb = N // D
    return pl.pallas_call(
        _ada_kernel,
        grid=(L, nb),
        in_specs=[pl.BlockSpec((rows, D), lambda l, j: (0, 0)),
                  pl.BlockSpec((1, D, D), lambda l, j: (l, 0, j)),
                  pl.BlockSpec((1, 1, D), lambda l, j: (l, 0, j))],
        out_specs=pl.BlockSpec((1, rows, D), lambda l, j: (l, 0, j)),
        out_shape=jax.ShapeDtypeStruct((L, rows, N), F32),
        compiler_params=_cparams("parallel", "parallel"),
        name="ada_mod",
    )(cc, ada_w, ada_b.reshape(L, 1, N))


def _in_proj_kernel(x_ref, mod_ref, g_ref, w_ref, z_ref):
    D = D_MODEL
    x = x_ref[0]
    mod = mod_ref[0]
    h = _rms(x, g_ref[...]) * (1.0 + mod[:, D:2 * D]) + mod[:, 0:D]
    z_ref[0] = _dot(h.astype(BF16), w_ref[...]).astype(BF16)


def _in_proj(xall, modrows, g, w):
    B, R, D = xall.shape
    nt = R // TM
    nn = Z_COLS // IN_NBLK
    return pl.pallas_call(
        _in_proj_kernel,
        grid=(nn, B, nt),
        in_specs=[pl.BlockSpec((1, TM, D), lambda n, b, t: (b, t, 0)),
                  pl.BlockSpec((1, 1, 6 * D), lambda n, b, t: (2 * b + jnp.minimum(t, 1), 0, 0)),
                  pl.BlockSpec((1, D), lambda n, b, t: (0, 0)),
                  pl.BlockSpec((D, IN_NBLK), lambda n, b, t: (0, n))],
        out_specs=pl.BlockSpec((1, TM, IN_NBLK), lambda n, b, t: (b, t, n)),
        out_shape=jax.ShapeDtypeStruct((B, R, Z_COLS), BF16),
        compiler_params=_cparams("parallel", "parallel", "parallel"),
        name="in_proj",
    )(xall, modrows, g, w)


def _pool_kernel(u_ref, w_ref, sc_ref, o_ref):
    R = u_ref.shape[1]
    row = lax.broadcasted_iota(jnp.int32, (R, POOL_GROUP), 0)
    seg_lo = jnp.where(row < CTX_LEN, 0, CTX_LEN)
    seg_hi = jnp.where(row < CTX_LEN, CTX_LEN, R)
    for gi, win in enumerate(POOL_WINDOWS):
        sl = slice(gi * POOL_GROUP, (gi + 1) * POOL_GROUP)
        u = u_ref[0, :, sl].astype(F32)
        lo_off, hi_off = win // 2, win - win // 2
        acc = jnp.zeros_like(u)
        for d in range(-lo_off, hi_off):
            shifted = u if d == 0 else pltpu.roll(u, (R - d) % R, axis=0)
            ok = (row + d >= seg_lo) & (row + d < seg_hi)
            acc = acc + jnp.where(ok, shifted, 0.0)
        cnt = (jnp.minimum(row + hi_off, seg_hi) - jnp.maximum(row - lo_off, seg_lo)).astype(F32)
        diff = (acc / cnt - u).astype(BF16)
        o_ref[0, :, sl] = (_dot(diff, w_ref[gi]) * sc_ref[:, sl]).astype(BF16)


def _pool(z, pool_w, pool_scale):
    B, R, _ = z.shape
    return pl.pallas_call(
        _pool_kernel,
        grid=(B,),
        in_specs=[pl.BlockSpec((1, R, POOL_WIDTH), lambda b: (b, 0, COL_P // POOL_WIDTH)),
                  pl.BlockSpec(pool_w.shape, lambda b: (0, 0, 0)),
                  pl.BlockSpec((1, POOL_WIDTH), lambda b: (0, 0))],
        out_specs=pl.BlockSpec((1, R, POOL_WIDTH), lambda b: (b, 0, 0)),
        out_shape=jax.ShapeDtypeStruct((B, R, POOL_WIDTH), BF16),
        compiler_params=_cparams("parallel"),
        name="pool",
    )(z, pool_w, pool_scale)


def _split3(x):
    a = x.astype(BF16)
    r = x - a.astype(F32)
    b = r.astype(BF16)
    c = (r - b.astype(F32)).astype(BF16)
    return a, b, c


def _gla_kernel(*refs, fwd):
    if fwd:
        q_ref, k_ref, v_ref, misc_ref, gw_ref, gb_ref, ob_ref, r_ref, gn_ref, o_ref, st_ref = refs
    else:
        q_ref, k_ref, v_ref, misc_ref, gw_ref, gb_ref, o_ref, st_ref = refs
    C = GLA_CHUNK
    nchunk = TM // C
    HK = GLA_HEADS * GLA_DK

    @pl.when(pl.program_id(1) == 0)
    def _():
        st_ref[...] = jnp.zeros_like(st_ref)

    zg = _dot(misc_ref[0], gw_ref[...]) + gb_ref[...]
    g = (jnp.minimum(zg, 0.0) - jnp.log1p(jnp.exp(-jnp.abs(zg)))) * (1.0 / GLA_TAU)

    row = lax.broadcasted_iota(jnp.int32, (TM, TM), 0)
    col = lax.broadcasted_iota(jnp.int32, (TM, TM), 1)
    shift = C.bit_length() - 1
    same = (row >> shift) == (col >> shift)
    tri = same & ((col <= row) if fwd else (col >= row))
    ones = jnp.concatenate([jnp.where(tri, 1.0, 0.0), jnp.where(same, 1.0, 0.0)], axis=0).astype(BF16)
    g1, g2, g3 = _split3(g)
    both = _dot(ones, g1) + _dot(ones, g2) + _dot(ones, g3)
    b = both[:TM]
    bt = both[TM:]

    qe = q_ref[0].astype(F32) * (GLA_DK ** -0.5) * jnp.exp(b)
    ke = (k_ref[0].astype(F32) * jnp.exp(-b)).astype(BF16)
    kd = (k_ref[0].astype(F32) * jnp.exp(bt - b)).astype(BF16)
    qe = qe.astype(BF16)
    dec = jnp.exp(bt)
    v = v_ref[0]

    order = range(nchunk) if fwd else range(nchunk - 1, -1, -1)
    for h in range(GLA_HEADS):
        ks = slice(h * GLA_DK, (h + 1) * GLA_DK)
        vs = slice(h * GLA_DV, (h + 1) * GLA_DV)
        att = jnp.where(tri, _dot_nt(qe[:, ks], ke[:, ks]), 0.0).astype(BF16)
        o_in = _dot(att, v[:, vs])
        st = st_ref[h]
        parts = [None] * nchunk
        for j in order:
            rs = slice(j * C, (j + 1) * C)
            parts[j] = o_in[rs] + _dot_nt(qe[rs, ks], st.astype(BF16))
            st = st * dec[j * C:j * C + 1, ks] + _dot_tn(v[rs, vs], kd[rs, ks])
        st_ref[h] = st
        o = jnp.concatenate(parts, axis=0)
        if fwd:
            o = o + ob_ref[0, :, vs].astype(F32)
            r = r_ref[0, :, vs].astype(F32)
            o = _rms(o, gn_ref[...]) * (r * jax.nn.sigmoid(r))
        o_ref[0, :, vs] = o.astype(BF16)


def _gla(z, gw, gb, *, fwd, ob=None, gn=None):
    B, R, _ = z.shape
    nt = R // TM
    HK, HV = GLA_HEADS * GLA_DK, GLA_HEADS * GLA_DV
    if fwd:
        tile = lambda c: c
    else:
        tile = lambda c: jnp.where(c == 0, 0, nt - c)
    in_specs = [pl.BlockSpec((1, TM, HK), lambda b, c: (b, tile(c), COL_Q // HK)),
                pl.BlockSpec((1, TM, HK), lambda b, c: (b, tile(c), COL_K // HK)),
                pl.BlockSpec((1, TM, HV), lambda b, c: (b, tile(c), COL_V // HV)),
                pl.BlockSpec((1, TM, LANES), lambda b, c: (b, tile(c), COL_MISC // LANES)),
                pl.BlockSpec((LANES, HK), lambda b, c: (0, 0)),
                pl.BlockSpec((1, HK), lambda b, c: (0, 0))]
    args = [z, z, z, z, gw, gb]
    if fwd:
        in_specs += [pl.BlockSpec((1, TM, HV), lambda b, c: (b, tile(c), 0)),
                     pl.BlockSpec((1, TM, HV), lambda b, c: (b, tile(c), COL_R // HV)),
                     pl.BlockSpec((1, GLA_DV), lambda b, c: (0, 0))]
        args += [ob, z, gn]
    return pl.pallas_call(
        functools.partial(_gla_kernel, fwd=fwd),
        grid=(B, nt),
        in_specs=in_specs,
        out_specs=pl.BlockSpec((1, TM, HV), lambda b, c: (b, tile(c), 0)),
        out_shape=jax.ShapeDtypeStruct((B, R, HV), BF16),
        scratch_shapes=[pltpu.VMEM((GLA_HEADS, GLA_DV, GLA_DK), F32)],
        compiler_params=_cparams("parallel", "arbitrary"),
        name="gla_fwd" if fwd else "gla_bwd",
    )(*args)


def _mla_up_kernel(qdn_ref, kvdn_ref, misc_ref, sw_ref, cos_ref, sin_ref, qg_ref, kvg_ref, wq_ref, wkv_ref,
                   q_ref, k_ref, v_ref):
    H = MLA_HEADS
    scale = (MLA_NOPE + MLA_ROPE) ** -0.5 * float(np.log2(np.e))
    cos = cos_ref[...]
    sin = sin_ref[...]
    qn = _rms(qdn_ref[0].astype(F32), qg_ref[...]).astype(BF16)
    qall = _dot(qn, wq_ref[...])
    for h in range(H):
        base = h * MQ
        q_ref[0, :, base:base + MLA_NOPE] = (qall[:, base:base + MLA_NOPE] * scale).astype(BF16)
        rot = (qall[:, base + MLA_NOPE:base + MQ] * cos
               + qall[:, H * MQ + h * LANES:H * MQ + (h + 1) * LANES] * sin)
        q_ref[0, :, base + MLA_NOPE:base + MQ] = (rot * scale).astype(BF16)
    kvn = _rms(kvdn_ref[0].astype(F32), kvg_ref[...]).astype(BF16)
    kvall = _dot(kvn, wkv_ref[...])
    kr = (misc_ref[0].astype(F32) * cos + sw_ref[0].astype(F32) * sin).astype(BF16)
    for h in range(H):
        src = h * (MLA_NOPE + MLA_V)
        k_ref[0, :, h * MQ:h * MQ + MLA_NOPE] = kvall[:, src:src + MLA_NOPE].astype(BF16)
        k_ref[0, :, h * MQ + MLA_NOPE:(h + 1) * MQ] = kr
        v_ref[0, :, h * MLA_V:(h + 1) * MLA_V] = kvall[:, src + MLA_NOPE:src + MLA_NOPE + MLA_V].astype(BF16)


def _mla_up(z, cos, sin, qg, kvg, wq, wkv):
    B, R, _ = z.shape
    nt = R // TM
    H = MLA_HEADS
    const = lambda b, t: (0, 0)
    return pl.pallas_call(
        _mla_up_kernel,
        grid=(B, nt),
        in_specs=[pl.BlockSpec((1, TM, MLA_Q_RANK), lambda b, t: (b, t, COL_QDN // MLA_Q_RANK)),
                  pl.BlockSpec((1, TM, MLA_KV_RANK), lambda b, t: (b, t, COL_KVDN // MLA_KV_RANK)),
                  pl.BlockSpec((1, TM, LANES), lambda b, t: (b, t, COL_MISC // LANES)),
                  pl.BlockSpec((1, TM, LANES), lambda b, t: (b, t, COL_SW // LANES)),
                  pl.BlockSpec((TM, LANES), lambda b, t: (t, 0)),
                  pl.BlockSpec((TM, LANES), lambda b, t: (t, 0)),
                  pl.BlockSpec(qg.shape, const),
                  pl.BlockSpec(kvg.shape, const),
                  pl.BlockSpec(wq.shape, const),
                  pl.BlockSpec(wkv.shape, const)],
        out_specs=[pl.BlockSpec((1, TM, H * MQ), lambda b, t: (b, t, 0)),
                   pl.BlockSpec((1, TM, H * MQ), lambda b, t: (b, t, 0)),
                   pl.BlockSpec((1, TM, H * MLA_V), lambda b, t: (b, t, 0))],
        out_shape=[jax.ShapeDtypeStruct((B, R, H * MQ), BF16),
                   jax.ShapeDtypeStruct((B, R, H * MQ), BF16),
                   jax.ShapeDtypeStruct((B, R, H * MLA_V), BF16)],
        compiler_params=_cparams("parallel", "parallel"),
        name="mla_up",
    )(z, z, z, z, cos, sin, qg, kvg, wq, wkv)


ATTN_HEADS = 8


def _attn_kernel(q_ref, k_ref, v_ref, o_ref, *, q_off):
    R = k_ref.shape[1]

    def attend(nk):
        for j in range(ATTN_HEADS):
            s = _dot_nt(q_ref[0, :, j * MQ:(j + 1) * MQ], k_ref[0, :nk, j * MQ:(j + 1) * MQ])
            p = jnp.exp2(s - jnp.max(s, axis=-1, keepdims=True))
            l = jnp.sum(p, axis=-1, keepdims=True)
            o = _dot(p.astype(BF16), v_ref[0, :nk, j * MLA_V:(j + 1) * MLA_V]) / l
            o_ref[0, :, j * MLA_V:(j + 1) * MLA_V] = o.astype(BF16)

    if q_off == 0:
        qi = pl.program_id(2)

        @pl.when(qi == 0)
        def _():
            attend(CTX_LEN)

        @pl.when(qi > 0)
        def _():
            attend(R)
    else:
        attend(R)


def _attn(q, k, v, *, with_ctx):
    B, R, _ = q.shape
    q_off = 0 if with_ctx else CTX_LEN // TM
    nq = R // TM - q_off
    G = ATTN_HEADS
    return pl.pallas_call(
        functools.partial(_attn_kernel, q_off=q_off),
        grid=(B, MLA_HEADS // G, nq),
        in_specs=[pl.BlockSpec((1, TM, G * MQ), lambda b, h, i: (b, i + q_off, h)),
                  pl.BlockSpec((1, R, G * MQ), lambda b, h, i: (b, 0, h)),
                  pl.BlockSpec((1, R, G * MLA_V), lambda b, h, i: (b, 0, h))],
        out_specs=pl.BlockSpec((1, TM, G * MLA_V), lambda b, h, i: (b, i + q_off, h)),
        out_shape=jax.ShapeDtypeStruct((B, R, MLA_HEADS * MLA_V), BF16),
        compiler_params=_cparams("parallel", "parallel", "arbitrary"),
        name="mla_attn",
    )(q, k, v)


def _merge_kernel(po_ref, go_ref, mo_ref, gz0_ref, gz1_ref, gz2_ref, x_ref, mod_ref, bw_ref, wo_ref, g_ref,
                  x1_ref, h2t_ref):
    D = D_MODEL
    m = None
    for i, (o_ref, gz_ref) in enumerate(((po_ref, gz0_ref), (go_ref, gz1_ref), (mo_ref, gz2_ref))):
        t = jax.nn.sigmoid(gz_ref[0].astype(F32)) * _dot(o_ref[0], bw_ref[i])
        m = t if m is None else m + t
    y = _dot(m.astype(BF16), wo_ref[...])
    mod = mod_ref[0]
    x1 = x_ref[0] + mod[:, 2 * D:3 * D] * y
    x1_ref[0] = x1
    h2 = _rms(x1, g_ref[...]) * (1.0 + mod[:, 4 * D:5 * D]) + mod[:, 3 * D:4 * D]
    h2t_ref[...] = pltpu.bitcast(h2.T.astype(BF16), jnp.int32)


def _merge(pool_o, gla_o, mla_o, z, xall, modrows, bw, wo, g, *, with_ctx):
    B, R, D = xall.shape
    t_off = 0 if with_ctx else CTX_LEN // TM
    nt = R // TM - t_off
    tok = lambda b, t: (b, t + t_off, 0)
    gzb = COL_GZ // D
    wpr = _words_per_row()
    mo_off = t_off if mla_o.shape[1] == R else 0
    return pl.pallas_call(
        _merge_kernel,
        grid=(B, nt),
        in_specs=[pl.BlockSpec((1, TM, D), tok),
                  pl.BlockSpec((1, TM, D), tok),
                  pl.BlockSpec((1, TM, D), lambda b, t: (b, t + mo_off, 0)),
                  pl.BlockSpec((1, TM, D), lambda b, t: (b, t + t_off, gzb)),
                  pl.BlockSpec((1, TM, D), lambda b, t: (b, t + t_off, gzb + 1)),
                  pl.BlockSpec((1, TM, D), lambda b, t: (b, t + t_off, gzb + 2)),
                  pl.BlockSpec((1, TM, D), tok),
                  pl.BlockSpec((1, 1, 6 * D), lambda b, t: (2 * b + jnp.minimum(t + t_off, 1), 0, 0)),
                  pl.BlockSpec(bw.shape, lambda b, t: (0, 0, 0)),
                  pl.BlockSpec(wo.shape, lambda b, t: (0, 0)),
                  pl.BlockSpec((1, D), lambda b, t: (0, 0))],
        out_specs=[pl.BlockSpec((1, TM, D), lambda b, t: (b, t, 0)),
                   pl.BlockSpec((D // wpr, TM), lambda b, t: (0, b * nt + t))],
        out_shape=[jax.ShapeDtypeStruct((B, nt * TM, D), F32),
                   jax.ShapeDtypeStruct((D // wpr, B * nt * TM), jnp.int32)],
        compiler_params=_cparams("parallel", "parallel"),
        name="merge",
    )(pool_o, gla_o, mla_o, z, z, z, xall, modrows, bw, wo, g)


def _top16(s):
    nk, n = s.shape
    iota = lax.broadcasted_iota(jnp.int32, (nk, n), 0).astype(F32)
    slot = lax.broadcasted_iota(jnp.int32, (PEER_TOPK, n), 0)
    rank = jnp.full((nk, n), float(PEER_TOPK), F32)
    vals = jnp.zeros((PEER_TOPK, n), F32)
    for j in range(PEER_TOPK):
        m = jnp.max(s, axis=0, keepdims=True)
        idx = jnp.min(jnp.where(s == m, iota, float(nk)), axis=0, keepdims=True)
        hit = iota == idx
        rank = jnp.where(hit, float(j), rank)
        s = jnp.where(hit, -jnp.inf, s)
        vals = jnp.where(slot == j, m, vals)
    return vals, rank


PAIR_SHORT = PEER_TOPK // 2
PAIR_ROWS = PEER_TOPK + (PEER_TOPK - 1) * PAIR_SHORT


def _pair_rows(t2):
    return jnp.concatenate([t2] + [t2[0:PAIR_SHORT]] * (PEER_TOPK - 1), axis=0)


def _pair_candidates(t1, t2):
    K = PEER_TOPK
    n = t1.shape[1]
    first = jnp.concatenate([jnp.broadcast_to(t1[0:1], (K, n))]
                            + [jnp.broadcast_to(t1[i:i + 1], (PAIR_SHORT, n)) for i in range(1, K)], axis=0)
    r = lax.broadcasted_iota(jnp.int32, (PAIR_ROWS, n), 0)
    sh = PAIR_SHORT.bit_length() - 1
    i_of = jnp.where(r < K, 0, ((r - K) >> sh) + 1)
    j_of = jnp.where(r < K, r, (r - K) & (PAIR_SHORT - 1))
    cand = jnp.where((i_of + 1) * (j_of + 1) <= K, first + _pair_rows(t2), -jnp.inf)
    return cand, (i_of * K + j_of).astype(F32)


def _top16_values(s, with_rank):
    n = s.shape[1]
    slot = lax.broadcasted_iota(jnp.int32, (PEER_TOPK, n), 0)
    vals = jnp.zeros((PEER_TOPK, n), F32)
    rank = jnp.full(s.shape, float(PEER_TOPK), F32) if with_rank else None
    for j in range(PEER_TOPK):
        m = jnp.max(s, axis=0, keepdims=True)
        hit = s == m
        if with_rank:
            rank = jnp.where(hit, float(j), rank)
        s = jnp.where(hit, -jnp.inf, s)
        vals = jnp.where(slot == j, m, vals)
    used = jnp.sum(jnp.where(s == -jnp.inf, 1.0, 0.0), axis=0, keepdims=True)
    return vals, rank, jnp.where(used == float(PEER_TOPK), 1.0, 0.0)


PAIR_GROUPS = (((0, 0, 8),), ((0, 8, 8),), ((1, 0, 8),), ((2, 0, 5), (4, 0, 3)), ((3, 0, 4), (5, 0, 2), (6, 0, 2)),
               ((7, 0, 2),))
PAIR_SUB = 8
assert sorted([(i, j0 + k) for g in PAIR_GROUPS for i, j0, ln in g for k in range(ln)]
              + [(i, 0) for i in range(PAIR_SUB, PEER_TOPK)]) == sorted(
    (i, j) for i in range(PEER_TOPK) for j in range(PEER_TOPK) if (i + 1) * (j + 1) <= PEER_TOPK)


def _pair_counts(t1, t2):
    K = PEER_TOPK
    S = PAIR_SUB
    n = t1.shape[1]
    sub = lax.broadcasted_iota(jnp.int32, (S, n), 0)
    groups = []
    for runs in PAIR_GROUPS:
        val, s = None, 0
        for i, j0, length in runs:
            src = t2[j0 // S * S:(j0 // S + 1) * S]
            shift = (s - j0 % S) % S
            b = src if shift == 0 else pltpu.roll(src, shift, axis=0)
            v = t1[i:i + 1] + b
            val = v if val is None else jnp.where(sub >= s, v, val)
            s += length
        groups.append(val if s == S else jnp.where(sub < s, val, -jnp.inf))
    groups.append(t1[S:K] + t2[0:1])
    cand0 = jnp.concatenate(groups, axis=0)
    n_invalid = sum(S - sum(ln for _, _, ln in g) for g in PAIR_GROUPS)

    cand = cand0
    zsum = jnp.zeros((1, n), F32)
    best = m = None
    for j in range(K):
        m = jnp.max(cand, axis=0, keepdims=True)
        cand = jnp.where(cand == m, -jnp.inf, cand)
        if j == 0:
            best = m
        zsum = zsum + jnp.exp(m - best)
    used = jnp.sum(jnp.where(cand == -jnp.inf, 1.0, 0.0), axis=0, keepdims=True)
    exact = jnp.where(used == float(n_invalid + K), 1.0, 0.0)

    picked = jnp.where(cand0 >= m, 1.0, 0.0)
    low = jnp.zeros((S, n), F32)
    for gi, runs in enumerate(PAIR_GROUPS):
        blk, s = picked[gi * S:(gi + 1) * S], 0
        for i, _, length in runs:
            part = blk if length == S else jnp.where((sub >= s) & (sub < s + length), blk, 0.0)
            low = low + jnp.where(sub == i, jnp.sum(part, axis=0, keepdims=True), 0.0)
            s += length
    return jnp.concatenate([low, picked[len(PAIR_GROUPS) * S:]], axis=0), zsum, exact


def _pair_top16(t1, t2):
    K = PEER_TOPK
    n = t1.shape[1]
    cand, flat = _pair_candidates(t1, t2)
    irow = lax.broadcasted_iota(jnp.int32, (K, n), 0).astype(F32)
    cnt = jnp.zeros((K, n), F32)
    zsum = jnp.zeros((1, n), F32)
    best = None
    for j in range(K):
        m = jnp.max(cand, axis=0, keepdims=True)
        idx = jnp.min(jnp.where(cand == m, flat, float(K * K)), axis=0, keepdims=True)
        cand = jnp.where(flat == idx, -jnp.inf, cand)
        cnt = cnt + jnp.where(irow == jnp.floor(idx * (1.0 / K)), 1.0, 0.0)
        if j == 0:
            best = m
        zsum = zsum + jnp.exp(m - best)
    return cnt, zsum


def _words_per_row():
    return 4 // jnp.dtype(BF16).itemsize


def _splat_words(x):
    bits = lax.bitcast_convert_type(x.astype(BF16).astype(F32), jnp.int32)
    if _words_per_row() == 1:
        return bits
    return bits | lax.shift_right_logical(bits, 16)


def _peer_score_kernel(h2t_ref, wqt_ref, keys_ref, c2_ref, p2_ref, cnt_ref, ka_ref, qt_ref):
    K = PEER_TOPK
    qt_ref[...] = _dot(wqt_ref[...], pltpu.bitcast(h2t_ref[...], BF16)).astype(BF16)

    def store(h, cs, s1, s2, t1, rank1, t2, rank2, cnt, zsum):
        rowcnt = jnp.zeros_like(s1)
        for j in range(K):
            in_row = (s1 == t1[j:j + 1]) if rank1 is None else (rank1 == float(j))
            rowcnt = jnp.where(in_row, cnt[j:j + 1], rowcnt)
        c2_ref[h, :, cs] = pltpu.bitcast(rank2.astype(BF16), jnp.int32)
        p2_ref[h, :, cs] = pltpu.bitcast(jnp.exp(s2 - t2[0:1]).astype(BF16), jnp.int32)
        cnt_ref[h, :, cs] = _splat_words(rowcnt)
        ka_ref[h, :, cs] = _splat_words(jnp.exp(s1 - t1[0:1]) / zsum)

    def head(h, carry):
        base = pl.multiple_of(h * PEER_QDIM, PEER_QDIM)
        s1_all = _dot(keys_ref[h], qt_ref[pl.ds(base, PEER_HALF), :])
        s2_all = _dot(keys_ref[PEER_HEADS + h], qt_ref[pl.ds(base + PEER_HALF, PEER_HALF), :])
        chunks = []
        for c in range(s1_all.shape[1] // LANES):
            cs = slice(c * LANES, (c + 1) * LANES)
            s1, s2 = s1_all[:, cs], s2_all[:, cs]
            t1, _, ok1 = _top16_values(s1, with_rank=False)
            t2, rank2, ok2 = _top16_values(s2, with_rank=True)
            cnt, zsum, ok3 = _pair_counts(t1, t2)
            store(h, cs, s1, s2, t1, None, t2, rank2, cnt, zsum)
            chunks.append((cs, s1, s2, jnp.min(ok1 * ok2 * ok3)))

        for cs, s1, s2, exact in chunks:
            @pl.when(exact < 0.5)
            def _(cs=cs, s1=s1, s2=s2):
                t1x, rank1x = _top16(s1)
                t2x, rank2x = _top16(s2)
                cntx, zx = _pair_top16(t1x, t2x)
                store(h, cs, s1, s2, t1x, rank1x, t2x, rank2x, cntx, zx)

        return carry

    lax.fori_loop(0, PEER_HEADS, head, 0)


PEER_SCORE_TN = 4 * LANES


def _peer_score(h2t, wqt, keys):
    ntok = h2t.shape[1]
    tn = PEER_SCORE_TN
    nw = PEER_N_KEYS // _words_per_row()
    tbl = lambda rows: jax.ShapeDtypeStruct((PEER_HEADS, rows, ntok), jnp.int32)
    ospec = lambda rows: pl.BlockSpec((PEER_HEADS, rows, tn), lambda i: (0, 0, i))
    return pl.pallas_call(
        _peer_score_kernel,
        grid=(ntok // tn,),
        in_specs=[pl.BlockSpec((h2t.shape[0], tn), lambda i: (0, i)),
                  pl.BlockSpec(wqt.shape, lambda i: (0, 0)),
                  pl.BlockSpec(keys.shape, lambda i: (0, 0, 0))],
        out_specs=[ospec(nw), ospec(nw), ospec(PEER_N_KEYS), ospec(PEER_N_KEYS)],
        out_shape=[tbl(nw), tbl(nw), tbl(PEER_N_KEYS), tbl(PEER_N_KEYS)],
        scratch_shapes=[pltpu.VMEM((PEER_HEADS * PEER_QDIM, tn), BF16)],
        compiler_params=_cparams("parallel"),
        name="peer_score",
    )(h2t, wqt, keys)


PEER_STEP_VMEM = 27 * 1024 * 1024


def _peer_keys_per_step(tn):
    per_key = PEER_N_KEYS * (tn * (4 + 2) + 2 * 2 * D_MODEL * 2)
    kps = 1
    while 2 * kps * per_key <= PEER_STEP_VMEM and 2 * kps <= PEER_N_KEYS:
        kps *= 2
    return kps


PEER_KEY_GROUP = 2


def _peer_dense_kernel(h2t_ref, c2_ref, p2_ref, cnt_ref, ka_ref, u_ref, vt_ref, x1_ref, modc_ref, modl_ref, fg_ref,
                       o_ref, acc_ref, pre_ref, hid_ref, *, ctx_rows, rows_per_sample, final):
    D = D_MODEL
    NK = PEER_N_KEYS
    tn = h2t_ref.shape[1]
    e = pl.program_id(1)

    @pl.when(e == 0)
    def _():
        acc_ref[...] = jnp.zeros_like(acc_ref)

    pre_ref[...] = _dot(pltpu.bitcast(u_ref[...], BF16), pltpu.bitcast(h2t_ref[...], BF16))

    kps = _peer_keys_per_step(tn)
    row = lambda ref, i, h: ref[h, i:i + 1, :]
    cnt_rows = [[row(cnt_ref, i, h) for h in range(PEER_HEADS)] for i in range(kps)]
    ka_rows = [[row(ka_ref, i, h) for h in range(PEER_HEADS)] for i in range(kps)]
    wpr = _words_per_row()

    def key_rows(i, h, cs):
        rep = lambda words: pltpu.bitcast(jnp.broadcast_to(words[:, cs], (BF16_ROWS // wpr, LANES)), BF16)
        return rep(cnt_rows[i][h]), rep(ka_rows[i][h])

    for c in range(tn // LANES):
        cs = slice(c * LANES, (c + 1) * LANES)
        for g in range(0, kps, PEER_KEY_GROUP):
            group = range(g, g + PEER_KEY_GROUP)
            w = {i: None for i in group}
            for h in range(PEER_HEADS):
                c2 = pltpu.bitcast(c2_ref[h, :, cs], BF16).reshape(NK // BF16_ROWS, BF16_ROWS, LANES)
                p2 = pltpu.bitcast(p2_ref[h, :, cs], BF16).reshape(NK // BF16_ROWS, BF16_ROWS, LANES)
                for i in group:
                    cnt, kaw = key_rows(i, h, cs)
                    t = jnp.where(c2 < cnt[None], p2, 0.0) * kaw[None]
                    w[i] = t if w[i] is None else w[i] + t
            for i in group:
                rows = slice(i * NK, (i + 1) * NK)
                x = pre_ref[rows, cs].astype(BF16)
                act = 0.5 * x * (1.0 + lax.erf(x * (2.0 ** -0.5)))
                hid_ref[rows, cs] = act * w[i].reshape(NK, LANES)

    acc_ref[...] += _dot(pltpu.bitcast(vt_ref[...], BF16), hid_ref[...])

    @pl.when(e == pl.num_programs(1) - 1)
    def _():
        y = acc_ref[...].T
        g_lat = modl_ref[0][:, 5 * D:6 * D]
        if ctx_rows:
            g_ctx = modc_ref[0][:, 5 * D:6 * D]
            tiles = rows_per_sample // tn
            r0 = (pl.program_id(0) % tiles) * tn
            row = r0 + lax.broadcasted_iota(jnp.int32, (tn, D), 0)
            gate = jnp.where(row < ctx_rows, g_ctx, g_lat)
        else:
            gate = g_lat
        x2 = x1_ref[...] + gate * y
        if final:
            x2 = _rms(x2, fg_ref[...])
        o_ref[...] = x2


def _peer_dense(h2t, sel, u, vt, x1, modrows, fg, *, tn, rows_per_sample, ctx_rows, final):
    ntok, D = x1.shape
    wpr = _words_per_row()
    kps = _peer_keys_per_step(tn)
    etile = kps * PEER_N_KEYS
    kspec = pl.BlockSpec((PEER_HEADS, kps, tn), lambda i, s: (0, s, i))
    ne = u.shape[0] * wpr // etile
    tiles = rows_per_sample // tn
    tspec = lambda t: pl.BlockSpec((PEER_HEADS, t.shape[1], tn), lambda i, s: (0, 0, i))
    return pl.pallas_call(
        functools.partial(_peer_dense_kernel, ctx_rows=ctx_rows, rows_per_sample=rows_per_sample, final=final),
        grid=(ntok // tn, ne),
        in_specs=[pl.BlockSpec((D // wpr, tn), lambda i, s: (0, i)),
                  tspec(sel[0]), tspec(sel[1]), kspec, kspec,
                  pl.BlockSpec((etile // wpr, D), lambda i, s: (s, 0)),
                  pl.BlockSpec((D // wpr, etile), lambda i, s: (0, s)),
                  pl.BlockSpec((tn, D), lambda i, s: (i, 0), pipeline_mode=pl.Buffered(1)),
                  pl.BlockSpec((1, 1, 6 * D), lambda i, s: (2 * (i // tiles), 0, 0)),
                  pl.BlockSpec((1, 1, 6 * D), lambda i, s: (2 * (i // tiles) + 1, 0, 0)),
                  pl.BlockSpec((1, D), lambda i, s: (0, 0))],
        out_specs=pl.BlockSpec((tn, D), lambda i, s: (i, 0)),
        out_shape=jax.ShapeDtypeStruct((ntok, D), F32),
        scratch_shapes=[pltpu.VMEM((D, tn), F32),
                        pltpu.VMEM((etile, tn), F32),
                        pltpu.VMEM((etile, tn), BF16)],
        compiler_params=_cparams("parallel", "arbitrary"),
        name="peer_dense",
    )(h2t, *sel, u, vt, x1, modrows, modrows, fg)


def _pack_rows_kernel(x_ref, o_ref):
    o_ref[...] = pltpu.bitcast(x_ref[...].astype(BF16), jnp.int32)


def _pack_rows(x):
    R, C = x.shape
    wpr = _words_per_row()
    br, bc = 512, 1024
    return pl.pallas_call(
        _pack_rows_kernel,
        grid=(R // br, C // bc),
        in_specs=[pl.BlockSpec((br, bc), lambda i, j: (i, j))],
        out_specs=pl.BlockSpec((br // wpr, bc), lambda i, j: (i, j)),
        out_shape=jax.ShapeDtypeStruct((R // wpr, C), jnp.int32),
        compiler_params=_cparams("parallel", "parallel"),
        name="pack_rows",
    )(x)


def _peer_tile(rows_per_sample):
    for tn in (1024, 768, 512, 256):
        if rows_per_sample % tn == 0:
            return tn
    raise ValueError(rows_per_sample)


_SWAP = np.concatenate([np.arange(16, 32), np.arange(0, 16), np.arange(48, 64), np.arange(32, 48)])


def _layout_w_in(w):
    D = w.shape[0]
    idx = np.cumsum(IN_SPLITS)[:-1]
    p, q, k, v, r, lrf, lrb, qdn, kvdn, krope, gz = jnp.split(w, idx, axis=1)
    zeros = lambda n: jnp.zeros((D, n), w.dtype)
    misc = jnp.concatenate([krope, lrf, lrb, zeros(LANES - MLA_ROPE - 2 * GLA_GATE_RANK)], axis=1)
    sw = jnp.concatenate([krope[:, _SWAP], zeros(LANES - MLA_ROPE)], axis=1)
    out = jnp.concatenate([p, q, k, v, r, gz, kvdn, misc, sw, qdn], axis=1)
    return jnp.concatenate([out, zeros(Z_COLS - out.shape[1])], axis=1).astype(BF16)


def _layout_gate_w(gate_w, row0):
    HK = GLA_HEADS * GLA_DK
    pad = jnp.zeros((LANES, HK), gate_w.dtype)
    return pad.at[row0:row0 + GLA_GATE_RANK].set(gate_w).astype(BF16)


def _layout_w_uq(w):
    H = MLA_HEADS
    w3 = w.reshape(MLA_Q_RANK, H, MLA_NOPE + MLA_ROPE)
    qn, qr = w3[..., :MLA_NOPE], w3[..., MLA_NOPE:]
    z = jnp.zeros((MLA_Q_RANK, H, LANES - MLA_ROPE), w.dtype)
    main = jnp.concatenate([qn, qr, z], axis=-1).reshape(MLA_Q_RANK, H * MQ)
    swp = jnp.concatenate([qr[..., _SWAP], z], axis=-1).reshape(MLA_Q_RANK, H * LANES)
    return jnp.concatenate([main, swp], axis=1).astype(BF16)


def _rope_tables(seq):
    half = MLA_ROPE // 2
    t = jnp.arange(seq)
    inv = ROPE_BASE ** (-jnp.arange(0, half, 2, dtype=F32) / half)
    ar = (t // GRID_W).astype(F32)[:, None] * inv
    ac = (t % GRID_W).astype(F32)[:, None] * inv
    cos = jnp.concatenate([jnp.cos(ar), jnp.cos(ar), jnp.cos(ac), jnp.cos(ac)], axis=1)
    sin = jnp.concatenate([-jnp.sin(ar), jnp.sin(ar), -jnp.sin(ac), jnp.sin(ac)], axis=1)
    cos = jnp.concatenate([jnp.ones((CTX_LEN, MLA_ROPE), F32), cos], axis=0)
    sin = jnp.concatenate([jnp.zeros((CTX_LEN, MLA_ROPE), F32), sin], axis=0)
    pad = jnp.zeros((CTX_LEN + seq, LANES - MLA_ROPE), F32)
    return jnp.concatenate([cos, pad], axis=1), jnp.concatenate([sin, pad], axis=1)


def kernel(x, c, ctx, c_ctx, ada_w, ada_b, norm1_g, norm2_g, w_in, pool_w, pool_scale, gla_gate_w, gla_gate_b, gla_norm_g, mla_q_norm_g, mla_kv_norm_g, mla_w_uq, mla_w_ukv, branch_w, w_out, peer_wq, peer_keys, peer_u, peer_v, final_norm_g):
    B, T, D = x.shape
    R = CTX_LEN + T
    assert D == D_MODEL and ctx.shape[1] == CTX_LEN == TM and T % TM == 0 and T % GRID_W == 0

    nrow = -(-(B + 1) // 8) * 8
    cc = jnp.concatenate([c, c_ctx[None], jnp.zeros((nrow - B - 1, D), F32)], axis=0)
    mod = _ada(cc, ada_w, ada_b)
    modrows = jnp.stack([jnp.broadcast_to(mod[:, B:B + 1], (DEPTH, B, 6 * D)), mod[:, :B]], axis=2)
    modrows = modrows.reshape(DEPTH, 2 * B, 1, 6 * D)

    cos, sin = _rope_tables(T)
    xall = jnp.concatenate([ctx, x], axis=1)
    row = lambda a: a.reshape(1, -1)

    for i in range(DEPTH):
        with_ctx = i < DEPTH - 1
        final = i == DEPTH - 1
        z = _in_proj(xall, modrows[i], row(norm1_g[i]), _layout_w_in(w_in[i]))

        pool_o = _pool(z, pool_w[i].astype(BF16), row(pool_scale[i]))

        o_b = _gla(z, _layout_gate_w(gla_gate_w[i, 1], MISC_LRB), row(gla_gate_b[i, 1]), fwd=False)
        gla_o = _gla(z, _layout_gate_w(gla_gate_w[i, 0], MISC_LRF), row(gla_gate_b[i, 0]), fwd=True,
                     ob=o_b, gn=row(gla_norm_g[i]))

        q, k, v = _mla_up(z, cos, sin, row(mla_q_norm_g[i]), row(mla_kv_norm_g[i]),
                          _layout_w_uq(mla_w_uq[i]), mla_w_ukv[i].astype(BF16))
        mla_o = _attn(q, k, v, with_ctx=with_ctx)

        x1, h2t = _merge(pool_o, gla_o, mla_o, z, xall, modrows[i], branch_w[i].astype(BF16),
                        w_out[i].astype(BF16), row(norm2_g[i]), with_ctx=with_ctx)

        rows = x1.shape[1]
        keys = peer_keys[i].reshape(2 * PEER_HEADS, PEER_N_KEYS, PEER_HALF).astype(BF16)
        sel = _peer_score(h2t, peer_wq[i].T.astype(BF16), keys)
        xall = _peer_dense(h2t, sel, _pack_rows(peer_u[i]), _pack_rows(peer_v[i].T), x1.reshape(B * rows, D),
                           modrows[i], row(final_norm_g), tn=_peer_tile(rows), rows_per_sample=rows,
                           ctx_rows=CTX_LEN if with_ctx else 0, final=final).reshape(B, rows, D)
    return xall
```

```python
import functools

import jax
import jax.numpy as jnp
import numpy as np
from jax import lax
from jax.experimental import pallas as pl
from jax.experimental.pallas import tpu as pltpu

F32 = jnp.float32
BF16 = jnp.bfloat16

D_MODEL = 1024
DEPTH = 2
CTX_LEN = 256
GRID_W = 64
NORM_EPS = 1e-6
POOL_WIDTH = 1024
POOL_WINDOWS = (2, 4, 8, 16)
POOL_GROUP = POOL_WIDTH // len(POOL_WINDOWS)
GLA_HEADS = 4
GLA_DK = 128
GLA_DV = 256
GLA_GATE_RANK = 16
GLA_TAU = 16.0
GLA_CHUNK = 64
MLA_HEADS = 8
MLA_Q_RANK = 384
MLA_KV_RANK = 256
MLA_NOPE = 128
MLA_ROPE = 64
MLA_V = 128
ROPE_BASE = 10000.0
N_BRANCH = 3
PEER_HEADS = 8
PEER_N_KEYS = 128
PEER_TOPK = 16
PEER_QDIM = 256
PEER_HALF = PEER_QDIM // 2
IN_SPLITS = (POOL_WIDTH, GLA_HEADS * GLA_DK, GLA_HEADS * GLA_DK, GLA_HEADS * GLA_DV, GLA_HEADS * GLA_DV,
             GLA_GATE_RANK, GLA_GATE_RANK, MLA_Q_RANK, MLA_KV_RANK, MLA_ROPE, N_BRANCH * D_MODEL)

LANES = 128
BF16_ROWS = 16
TM = 256
VMEM_LIMIT = 56 * 1024 * 1024

COL_P = 0
COL_Q = 1024
COL_K = 1536
COL_V = 2048
COL_R = 3072
COL_GZ = 4096
COL_KVDN = 7168
COL_MISC = 7424
COL_SW = 7552
COL_QDN = 7680
Z_COLS = 8192
IN_NBLK = 8192
MISC_LRF = MLA_ROPE
MISC_LRB = MLA_ROPE + GLA_GATE_RANK
MQ = MLA_NOPE + LANES


def _cparams(*sem):
    return pltpu.CompilerParams(dimension_semantics=sem, vmem_limit_bytes=VMEM_LIMIT)


def _rms(x, g):
    return x * lax.rsqrt(jnp.mean(x * x, axis=-1, keepdims=True) + NORM_EPS) * g


def _dot(a, b):
    return jnp.dot(a, b, preferred_element_type=F32)


def _dot_nt(a, b):
    return lax.dot_general(a, b, (((1,), (1,)), ((), ())), preferred_element_type=F32)


def _dot_tn(a, b):
    return lax.dot_general(a, b, (((0,), (0,)), ((), ())), preferred_element_type=F32)


def _ada_kernel(c_ref, w_ref, b_ref, o_ref):
    c = c_ref[...]
    a = (c * jax.nn.sigmoid(c)).astype(BF16)
    o_ref[0] = _dot(a, w_ref[0].astype(BF16)) + b_ref[0]


def _ada(cc, ada_w, ada_b):
    L, D, N = ada_w.shape
    rows = cc.shape[0]
    nb = N // D
    return pl.pallas_call(
        _ada_kernel,
        grid=(L, nb),
        in_specs=[pl.BlockSpec((rows, D), lambda l, j: (0, 0)),
                  pl.BlockSpec((1, D, D), lambda l, j: (l, 0, j)),
                  pl.BlockSpec((1, 1, D), lambda l, j: (l, 0, j))],
        out_specs=pl.BlockSpec((1, rows, D), lambda l, j: (l, 0, j)),
        out_shape=jax.ShapeDtypeStruct((L, rows, N), F32),
        compiler_params=_cparams("parallel", "parallel"),
        name="ada_mod",
    )(cc, ada_w, ada_b.reshape(L, 1, N))


def _in_proj_kernel(x_ref, mod_ref, g_ref, w_ref, z_ref):
    D = D_MODEL
    x = x_ref[0]
    mod = mod_ref[0]
    h = _rms(x, g_ref[...]) * (1.0 + mod[:, D:2 * D]) + mod[:, 0:D]
    z_ref[0] = _dot(h.astype(BF16), w_ref[...]).astype(BF16)


def _in_proj(xall, modrows, g, w):
    B, R, D = xall.shape
    nt = R // TM
    nn = Z_COLS // IN_NBLK
    return pl.pallas_call(
        _in_proj_kernel,
        grid=(nn, B, nt),
        in_specs=[pl.BlockSpec((1, TM, D), lambda n, b, t: (b, t, 0)),
                  pl.BlockSpec((1, 1, 6 * D), lambda n, b, t: (2 * b + jnp.minimum(t, 1), 0, 0)),
                  pl.BlockSpec((1, D), lambda n, b, t: (0, 0)),
                  pl.BlockSpec((D, IN_NBLK), lambda n, b, t: (0, n))],
        out_specs=pl.BlockSpec((1, TM, IN_NBLK), lambda n, b, t: (b, t, n)),
        out_shape=jax.ShapeDtypeStruct((B, R, Z_COLS), BF16),
        compiler_params=_cparams("parallel", "parallel", "parallel"),
        name="in_proj",
    )(xall, modrows, g, w)


def _pool_kernel(u_ref, w_ref, sc_ref, o_ref):
    R = u_ref.shape[1]
    row = lax.broadcasted_iota(jnp.int32, (R, POOL_GROUP), 0)
    seg_lo = jnp.where(row < CTX_LEN, 0, CTX_LEN)
    seg_hi = jnp.where(row < CTX_LEN, CTX_LEN, R)
    for gi, win in enumerate(POOL_WINDOWS):
        sl = slice(gi * POOL_GROUP, (gi + 1) * POOL_GROUP)
        u = u_ref[0, :, sl].astype(F32)
        lo_off, hi_off = win // 2, win - win // 2
        acc = jnp.zeros_like(u)
        for d in range(-lo_off, hi_off):
            shifted = u if d == 0 else pltpu.roll(u, (R - d) % R, axis=0)
            ok = (row + d >= seg_lo) & (row + d < seg_hi)
            acc = acc + jnp.where(ok, shifted, 0.0)
        cnt = (jnp.minimum(row + hi_off, seg_hi) - jnp.maximum(row - lo_off, seg_lo)).astype(F32)
        diff = (acc / cnt - u).astype(BF16)
        o_ref[0, :, sl] = (_dot(diff, w_ref[gi]) * sc_ref[:, sl]).astype(BF16)


def _pool(z, pool_w, pool_scale):
    B, R, _ = z.shape
    return pl.pallas_call(
        _pool_kernel,
        grid=(B,),
        in_specs=[pl.BlockSpec((1, R, POOL_WIDTH), lambda b: (b, 0, COL_P // POOL_WIDTH)),
                  pl.BlockSpec(pool_w.shape, lambda b: (0, 0, 0)),
                  pl.BlockSpec((1, POOL_WIDTH), lambda b: (0, 0))],
        out_specs=pl.BlockSpec((1, R, POOL_WIDTH), lambda b: (b, 0, 0)),
        out_shape=jax.ShapeDtypeStruct((B, R, POOL_WIDTH), BF16),
        compiler_params=_cparams("parallel"),
        name="pool",
    )(z, pool_w, pool_scale)


def _split3(x):
    a = x.astype(BF16)
    r = x - a.astype(F32)
    b = r.astype(BF16)
    c = (r - b.astype(F32)).astype(BF16)
    return a, b, c


def _gla_kernel(*refs, fwd):
    if fwd:
        q_ref, k_ref, v_ref, misc_ref, gw_ref, gb_ref, ob_ref, r_ref, gn_ref, o_ref, st_ref = refs
    else:
        q_ref, k_ref, v_ref, misc_ref, gw_ref, gb_ref, o_ref, st_ref = refs
    C = GLA_CHUNK
    nchunk = TM // C
    HK = GLA_HEADS * GLA_DK

    @pl.when(pl.program_id(1) == 0)
    def _():
        st_ref[...] = jnp.zeros_like(st_ref)

    zg = _dot(misc_ref[0], gw_ref[...]) + gb_ref[...]
    g = (jnp.minimum(zg, 0.0) - jnp.log1p(jnp.exp(-jnp.abs(zg)))) * (1.0 / GLA_TAU)

    row = lax.broadcasted_iota(jnp.int32, (TM, TM), 0)
    col = lax.broadcasted_iota(jnp.int32, (TM, TM), 1)
    shift = C.bit_length() - 1
    same = (row >> shift) == (col >> shift)
    tri = same & ((col <= row) if fwd else (col >= row))
    ones = jnp.concatenate([jnp.where(tri, 1.0, 0.0), jnp.where(same, 1.0, 0.0)], axis=0).astype(BF16)
    g1, g2, g3 = _split3(g)
    both = _dot(ones, g1) + _dot(ones, g2) + _dot(ones, g3)
    b = both[:TM]
    bt = both[TM:]

    qe = q_ref[0].astype(F32) * (GLA_DK ** -0.5) * jnp.exp(b)
    ke = (k_ref[0].astype(F32) * jnp.exp(-b)).astype(BF16)
    kd = (k_ref[0].astype(F32) * jnp.exp(bt - b)).astype(BF16)
    qe = qe.astype(BF16)
    dec = jnp.exp(bt)
    v = v_ref[0]

    order = range(nchunk) if fwd else range(nchunk - 1, -1, -1)
    for h in range(GLA_HEADS):
        ks = slice(h * GLA_DK, (h + 1) * GLA_DK)
        vs = slice(h * GLA_DV, (h + 1) * GLA_DV)
        att = jnp.where(tri, _dot_nt(qe[:, ks], ke[:, ks]), 0.0).astype(BF16)
        o_in = _dot(att, v[:, vs])
        st = st_ref[h]
        parts = [None] * nchunk
        for j in order:
            rs = slice(j * C, (j + 1) * C)
            parts[j] = o_in[rs] + _dot_nt(qe[rs, ks], st.astype(BF16))
            st = st * dec[j * C:j * C + 1, ks] + _dot_tn(v[rs, vs], kd[rs, ks])
        st_ref[h] = st
        o = jnp.concatenate(parts, axis=0)
        if fwd:
            o = o + ob_ref[0, :, vs].astype(F32)
            r = r_ref[0, :, vs].astype(F32)
            o = _rms(o, gn_ref[...]) * (r * jax.nn.sigmoid(r))
        o_ref[0, :, vs] = o.astype(BF16)


def _gla(z, gw, gb, *, fwd, ob=None, gn=None):
    B, R, _ = z.shape
    nt = R // TM
    HK, HV = GLA_HEADS * GLA_DK, GLA_HEADS * GLA_DV
    if fwd:
        tile = lambda c: c
    else:
        tile = lambda c: jnp.where(c == 0, 0, nt - c)
    in_specs = [pl.BlockSpec((1, TM, HK), lambda b, c: (b, tile(c), COL_Q // HK)),
                pl.BlockSpec((1, TM, HK), lambda b, c: (b, tile(c), COL_K // HK)),
                pl.BlockSpec((1, TM, HV), lambda b, c: (b, tile(c), COL_V // HV)),
                pl.BlockSpec((1, TM, LANES), lambda b, c: (b, tile(c), COL_MISC // LANES)),
                pl.BlockSpec((LANES, HK), lambda b, c: (0, 0)),
                pl.BlockSpec((1, HK), lambda b, c: (0, 0))]
    args = [z, z, z, z, gw, gb]
    if fwd:
        in_specs += [pl.BlockSpec((1, TM, HV), lambda b, c: (b, tile(c), 0)),
                     pl.BlockSpec((1, TM, HV), lambda b, c: (b, tile(c), COL_R // HV)),
                     pl.BlockSpec((1, GLA_DV), lambda b, c: (0, 0))]
        args += [ob, z, gn]
    return pl.pallas_call(
        functools.partial(_gla_kernel, fwd=fwd),
        grid=(B, nt),
        in_specs=in_specs,
        out_specs=pl.BlockSpec((1, TM, HV), lambda b, c: (b, tile(c), 0)),
        out_shape=jax.ShapeDtypeStruct((B, R, HV), BF16),
        scratch_shapes=[pltpu.VMEM((GLA_HEADS, GLA_DV, GLA_DK), F32)],
        compiler_params=_cparams("parallel", "arbitrary"),
        name="gla_fwd" if fwd else "gla_bwd",
    )(*args)


def _mla_up_kernel(qdn_ref, kvdn_ref, misc_ref, sw_ref, cos_ref, sin_ref, qg_ref, kvg_ref, wq_ref, wkv_ref,
                   q_ref, k_ref, v_ref):
    H = MLA_HEADS
    scale = (MLA_NOPE + MLA_ROPE) ** -0.5 * float(np.log2(np.e))
    cos = cos_ref[...]
    sin = sin_ref[...]
    qn = _rms(qdn_ref[0].astype(F32), qg_ref[...]).astype(BF16)
    qall = _dot(qn, wq_ref[...])
    for h in range(H):
        base = h * MQ
        q_ref[0, :, base:base + MLA_NOPE] = (qall[:, base:base + MLA_NOPE] * scale).astype(BF16)
        rot = (qall[:, base + MLA_NOPE:base + MQ] * cos
               + qall[:, H * MQ + h * LANES:H * MQ + (h + 1) * LANES] * sin)
        q_ref[0, :, base + MLA_NOPE:base + MQ] = (rot * scale).astype(BF16)
    kvn = _rms(kvdn_ref[0].astype(F32), kvg_ref[...]).astype(BF16)
    kvall = _dot(kvn, wkv_ref[...])
    kr = (misc_ref[0].astype(F32) * cos + sw_ref[0].astype(F32) * sin).astype(BF16)
    for h in range(H):
        src = h * (MLA_NOPE + MLA_V)
        k_ref[0, :, h * MQ:h * MQ + MLA_NOPE] = kvall[:, src:src + MLA_NOPE].astype(BF16)
        k_ref[0, :, h * MQ + MLA_NOPE:(h + 1) * MQ] = kr
        v_ref[0, :, h * MLA_V:(h + 1) * MLA_V] = kvall[:, src + MLA_NOPE:src + MLA_NOPE + MLA_V].astype(BF16)


def _mla_up(z, cos, sin, qg, kvg, wq, wkv):
    B, R, _ = z.shape
    nt = R // TM
    H = MLA_HEADS
    const = lambda b, t: (0, 0)
    return pl.pallas_call(
        _mla_up_kernel,
        grid=(B, nt),
        in_specs=[pl.BlockSpec((1, TM, MLA_Q_RANK), lambda b, t: (b, t, COL_QDN // MLA_Q_RANK)),
                  pl.BlockSpec((1, TM, MLA_KV_RANK), lambda b, t: (b, t, COL_KVDN // MLA_KV_RANK)),
                  pl.BlockSpec((1, TM, LANES), lambda b, t: (b, t, COL_MISC // LANES)),
                  pl.BlockSpec((1, TM, LANES), lambda b, t: (b, t, COL_SW // LANES)),
                  pl.BlockSpec((TM, LANES), lambda b, t: (t, 0)),
                  pl.BlockSpec((TM, LANES), lambda b, t: (t, 0)),
                  pl.BlockSpec(qg.shape, const),
                  pl.BlockSpec(kvg.shape, const),
                  pl.BlockSpec(wq.shape, const),
                  pl.BlockSpec(wkv.shape, const)],
        out_specs=[pl.BlockSpec((1, TM, H * MQ), lambda b, t: (b, t, 0)),
                   pl.BlockSpec((1, TM, H * MQ), lambda b, t: (b, t, 0)),
                   pl.BlockSpec((1, TM, H * MLA_V), lambda b, t: (b, t, 0))],
        out_shape=[jax.ShapeDtypeStruct((B, R, H * MQ), BF16),
                   jax.ShapeDtypeStruct((B, R, H * MQ), BF16),
                   jax.ShapeDtypeStruct((B, R, H * MLA_V), BF16)],
        compiler_params=_cparams("parallel", "parallel"),
        name="mla_up",
    )(z, z, z, z, cos, sin, qg, kvg, wq, wkv)


ATTN_HEADS = 8


def _attn_kernel(q_ref, k_ref, v_ref, o_ref, *, q_off):
    R = k_ref.shape[1]

    def attend(nk):
        for j in range(ATTN_HEADS):
            s = _dot_nt(q_ref[0, :, j * MQ:(j + 1) * MQ], k_ref[0, :nk, j * MQ:(j + 1) * MQ])
            p = jnp.exp2(s - jnp.max(s, axis=-1, keepdims=True))
            l = jnp.sum(p, axis=-1, keepdims=True)
            o = _dot(p.astype(BF16), v_ref[0, :nk, j * MLA_V:(j + 1) * MLA_V]) / l
            o_ref[0, :, j * MLA_V:(j + 1) * MLA_V] = o.astype(BF16)

    if q_off == 0:
        qi = pl.program_id(2)

        @pl.when(qi == 0)
        def _():
            attend(CTX_LEN)

        @pl.when(qi > 0)
        def _():
            attend(R)
    else:
        attend(R)


def _attn(q, k, v, *, with_ctx):
    B, R, _ = q.shape
    q_off = 0 if with_ctx else CTX_LEN // TM
    nq = R // TM - q_off
    G = ATTN_HEADS
    return pl.pallas_call(
        functools.partial(_attn_kernel, q_off=q_off),
        grid=(B, MLA_HEADS // G, nq),
        in_specs=[pl.BlockSpec((1, TM, G * MQ), lambda b, h, i: (b, i + q_off, h)),
                  pl.BlockSpec((1, R, G * MQ), lambda b, h, i: (b, 0, h)),
                  pl.BlockSpec((1, R, G * MLA_V), lambda b, h, i: (b, 0, h))],
        out_specs=pl.BlockSpec((1, TM, G * MLA_V), lambda b, h, i: (b, i + q_off, h)),
        out_shape=jax.ShapeDtypeStruct((B, R, MLA_HEADS * MLA_V), BF16),
        compiler_params=_cparams("parallel", "parallel", "arbitrary"),
        name="mla_attn",
    )(q, k, v)


def _merge_kernel(po_ref, go_ref, mo_ref, gz0_ref, gz1_ref, gz2_ref, x_ref, mod_ref, bw_ref, wo_ref, g_ref,
                  x1_ref, h2t_ref):
    D = D_MODEL
    m = None
    for i, (o_ref, gz_ref) in enumerate(((po_ref, gz0_ref), (go_ref, gz1_ref), (mo_ref, gz2_ref))):
        t = jax.nn.sigmoid(gz_ref[0].astype(F32)) * _dot(o_ref[0], bw_ref[i])
        m = t if m is None else m + t
    y = _dot(m.astype(BF16), wo_ref[...])
    mod = mod_ref[0]
    x1 = x_ref[0] + mod[:, 2 * D:3 * D] * y
    x1_ref[0] = x1
    h2 = _rms(x1, g_ref[...]) * (1.0 + mod[:, 4 * D:5 * D]) + mod[:, 3 * D:4 * D]
    h2t_ref[...] = pltpu.bitcast(h2.T.astype(BF16), jnp.int32)


def _merge(pool_o, gla_o, mla_o, z, xall, modrows, bw, wo, g, *, with_ctx):
    B, R, D = xall.shape
    t_off = 0 if with_ctx else CTX_LEN // TM
    nt = R // TM - t_off
    tok = lambda b, t: (b, t + t_off, 0)
    gzb = COL_GZ // D
    wpr = _words_per_row()
    mo_off = t_off if mla_o.shape[1] == R else 0
    return pl.pallas_call(
        _merge_kernel,
        grid=(B, nt),
        in_specs=[pl.BlockSpec((1, TM, D), tok),
                  pl.BlockSpec((1, TM, D), tok),
                  pl.BlockSpec((1, TM, D), lambda b, t: (b, t + mo_off, 0)),
                  pl.BlockSpec((1, TM, D), lambda b, t: (b, t + t_off, gzb)),
                  pl.BlockSpec((1, TM, D), lambda b, t: (b, t + t_off, gzb + 1)),
                  pl.BlockSpec((1, TM, D), lambda b, t: (b, t + t_off, gzb + 2)),
                  pl.BlockSpec((1, TM, D), tok),
                  pl.BlockSpec((1, 1, 6 * D), lambda b, t: (2 * b + jnp.minimum(t + t_off, 1), 0, 0)),
                  pl.BlockSpec(bw.shape, lambda b, t: (0, 0, 0)),
                  pl.BlockSpec(wo.shape, lambda b, t: (0, 0)),
                  pl.BlockSpec((1, D), lambda b, t: (0, 0))],
        out_specs=[pl.BlockSpec((1, TM, D), lambda b, t: (b, t, 0)),
                   pl.BlockSpec((D // wpr, TM), lambda b, t: (0, b * nt + t))],
        out_shape=[jax.ShapeDtypeStruct((B, nt * TM, D), F32),
                   jax.ShapeDtypeStruct((D // wpr, B * nt * TM), jnp.int32)],
        compiler_params=_cparams("parallel", "parallel"),
        name="merge",
    )(pool_o, gla_o, mla_o, z, z, z, xall, modrows, bw, wo, g)


def _top16(s):
    nk, n = s.shape
    iota = lax.broadcasted_iota(jnp.int32, (nk, n), 0).astype(F32)
    slot = lax.broadcasted_iota(jnp.int32, (PEER_TOPK, n), 0)
    rank = jnp.full((nk, n), float(PEER_TOPK), F32)
    vals = jnp.zeros((PEER_TOPK, n), F32)
    for j in range(PEER_TOPK):
        m = jnp.max(s, axis=0, keepdims=True)
        idx = jnp.min(jnp.where(s == m, iota, float(nk)), axis=0, keepdims=True)
        hit = iota == idx
        rank = jnp.where(hit, float(j), rank)
        s = jnp.where(hit, -jnp.inf, s)
        vals = jnp.where(slot == j, m, vals)
    return vals, rank


PAIR_SHORT = PEER_TOPK // 2
PAIR_ROWS = PEER_TOPK + (PEER_TOPK - 1) * PAIR_SHORT


def _pair_rows(t2):
    return jnp.concatenate([t2] + [t2[0:PAIR_SHORT]] * (PEER_TOPK - 1), axis=0)


def _pair_candidates(t1, t2):
    K = PEER_TOPK
    n = t1.shape[1]
    first = jnp.concatenate([jnp.broadcast_to(t1[0:1], (K, n))]
                            + [jnp.broadcast_to(t1[i:i + 1], (PAIR_SHORT, n)) for i in range(1, K)], axis=0)
    r = lax.broadcasted_iota(jnp.int32, (PAIR_ROWS, n), 0)
    sh = PAIR_SHORT.bit_length() - 1
    i_of = jnp.where(r < K, 0, ((r - K) >> sh) + 1)
    j_of = jnp.where(r < K, r, (r - K) & (PAIR_SHORT - 1))
    cand = jnp.where((i_of + 1) * (j_of + 1) <= K, first + _pair_rows(t2), -jnp.inf)
    return cand, (i_of * K + j_of).astype(F32)


def _top16_values(s, with_rank):
    n = s.shape[1]
    slot = lax.broadcasted_iota(jnp.int32, (PEER_TOPK, n), 0)
    vals = jnp.zeros((PEER_TOPK, n), F32)
    rank = jnp.full(s.shape, float(PEER_TOPK), F32) if with_rank else None
    for j in range(PEER_TOPK):
        m = jnp.max(s, axis=0, keepdims=True)
        hit = s == m
        if with_rank:
            rank = jnp.where(hit, float(j), rank)
        s = jnp.where(hit, -jnp.inf, s)
        vals = jnp.where(slot == j, m, vals)
    used = jnp.sum(jnp.where(s == -jnp.inf, 1.0, 0.0), axis=0, keepdims=True)
    return vals, rank, jnp.where(used == float(PEER_TOPK), 1.0, 0.0)


PAIR_GROUPS = (((0, 0, 8),), ((0, 8, 8),), ((1, 0, 8),), ((2, 0, 5), (4, 0, 3)), ((3, 0, 4), (5, 0, 2), (6, 0, 2)),
               ((7, 0, 2),))
PAIR_SUB = 8
assert sorted([(i, j0 + k) for g in PAIR_GROUPS for i, j0, ln in g for k in range(ln)]
              + [(i, 0) for i in range(PAIR_SUB, PEER_TOPK)]) == sorted(
    (i, j) for i in range(PEER_TOPK) for j in range(PEER_TOPK) if (i + 1) * (j + 1) <= PEER_TOPK)


def _pair_counts(t1, t2):
    K = PEER_TOPK
    S = PAIR_SUB
    n = t1.shape[1]
    sub = lax.broadcasted_iota(jnp.int32, (S, n), 0)
    groups = []
    for runs in PAIR_GROUPS:
        val, s = None, 0
        for i, j0, length in runs:
            src = t2[j0 // S * S:(j0 // S + 1) * S]
            shift = (s - j0 % S) % S
            b = src if shift == 0 else pltpu.roll(src, shift, axis=0)
            v = t1[i:i + 1] + b
            val = v if val is None else jnp.where(sub >= s, v, val)
            s += length
        groups.append(val if s == S else jnp.where(sub < s, val, -jnp.inf))
    groups.append(t1[S:K] + t2[0:1])
    cand0 = jnp.concatenate(groups, axis=0)
    n_invalid = sum(S - sum(ln for _, _, ln in g) for g in PAIR_GROUPS)

    cand = cand0
    zsum = jnp.zeros((1, n), F32)
    best = m = None
    for j in range(K):
        m = jnp.max(cand, axis=0, keepdims=True)
        cand = jnp.where(cand == m, -jnp.inf, cand)
        if j == 0:
            best = m
        zsum = zsum + jnp.exp(m - best)
    used = jnp.sum(jnp.where(cand == -jnp.inf, 1.0, 0.0), axis=0, keepdims=True)
    exact = jnp.where(used == float(n_invalid + K), 1.0, 0.0)

    picked = jnp.where(cand0 >= m, 1.0, 0.0)
    low = jnp.zeros((S, n), F32)
    for gi, runs in enumerate(PAIR_GROUPS):
        blk, s = picked[gi * S:(gi + 1) * S], 0
        for i, _, length in runs:
            part = blk if length == S else jnp.where((sub >= s) & (sub < s + length), blk, 0.0)
            low = low + jnp.where(sub == i, jnp.sum(part, axis=0, keepdims=True), 0.0)
            s += length
    return jnp.concatenate([low, picked[len(PAIR_GROUPS) * S:]], axis=0), zsum, exact


def _pair_top16(t1, t2):
    K = PEER_TOPK
    n = t1.shape[1]
    cand, flat = _pair_candidates(t1, t2)
    irow = lax.broadcasted_iota(jnp.int32, (K, n), 0).astype(F32)
    cnt = jnp.zeros((K, n), F32)
    zsum = jnp.zeros((1, n), F32)
    best = None
    for j in range(K):
        m = jnp.max(cand, axis=0, keepdims=True)
        idx = jnp.min(jnp.where(cand == m, flat, float(K * K)), axis=0, keepdims=True)
        cand = jnp.where(flat == idx, -jnp.inf, cand)
        cnt = cnt + jnp.where(irow == jnp.floor(idx * (1.0 / K)), 1.0, 0.0)
        if j == 0:
            best = m
        zsum = zsum + jnp.exp(m - best)
    return cnt, zsum


def _words_per_row():
    return 4 // jnp.dtype(BF16).itemsize


def _splat_words(x):
    bits = lax.bitcast_convert_type(x.astype(BF16).astype(F32), jnp.int32)
    if _words_per_row() == 1:
        return bits
    return bits | lax.shift_right_logical(bits, 16)


def _peer_score_kernel(h2t_ref, wqt_ref, keys_ref, c2_ref, p2_ref, cnt_ref, ka_ref, qt_ref):
    K = PEER_TOPK
    qt_ref[...] = _dot(wqt_ref[...], pltpu.bitcast(h2t_ref[...], BF16)).astype(BF16)

    def store(h, cs, s1, s2, t1, rank1, t2, rank2, cnt, zsum):
        rowcnt = jnp.zeros_like(s1)
        for j in range(K):
            in_row = (s1 == t1[j:j + 1]) if rank1 is None else (rank1 == float(j))
            rowcnt = jnp.where(in_row, cnt[j:j + 1], rowcnt)
        c2_ref[h, :, cs] = pltpu.bitcast(rank2.astype(BF16), jnp.int32)
        p2_ref[h, :, cs] = pltpu.bitcast(jnp.exp(s2 - t2[0:1]).astype(BF16), jnp.int32)
        cnt_ref[h, :, cs] = _splat_words(rowcnt)
        ka_ref[h, :, cs] = _splat_words(jnp.exp(s1 - t1[0:1]) / zsum)

    def head(h, carry):
        base = pl.multiple_of(h * PEER_QDIM, PEER_QDIM)
        s1_all = _dot(keys_ref[h], qt_ref[pl.ds(base, PEER_HALF), :])
        s2_all = _dot(keys_ref[PEER_HEADS + h], qt_ref[pl.ds(base + PEER_HALF, PEER_HALF), :])
        chunks = []
        for c in range(s1_all.shape[1] // LANES):
            cs = slice(c * LANES, (c + 1) * LANES)
            s1, s2 = s1_all[:, cs], s2_all[:, cs]
            t1, _, ok1 = _top16_values(s1, with_rank=False)
            t2, rank2, ok2 = _top16_values(s2, with_rank=True)
            cnt, zsum, ok3 = _pair_counts(t1, t2)
            store(h, cs, s1, s2, t1, None, t2, rank2, cnt, zsum)
            chunks.append((cs, s1, s2, jnp.min(ok1 * ok2 * ok3)))

        for cs, s1, s2, exact in chunks:
            @pl.when(exact < 0.5)
            def _(cs=cs, s1=s1, s2=s2):
                t1x, rank1x = _top16(s1)
                t2x, rank2x = _top16(s2)
                cntx, zx = _pair_top16(t1x, t2x)
                store(h, cs, s1, s2, t1x, rank1x, t2x, rank2x, cntx, zx)

        return carry

    lax.fori_loop(0, PEER_HEADS, head, 0)


PEER_SCORE_TN = 4 * LANES


def _peer_score(h2t, wqt, keys):
    ntok = h2t.shape[1]
    tn = PEER_SCORE_TN
    nw = PEER_N_KEYS // _words_per_row()
    tbl = lambda rows: jax.ShapeDtypeStruct((PEER_HEADS, rows, ntok), jnp.int32)
    ospec = lambda rows: pl.BlockSpec((PEER_HEADS, rows, tn), lambda i: (0, 0, i))
    return pl.pallas_call(
        _peer_score_kernel,
        grid=(ntok // tn,),
        in_specs=[pl.BlockSpec((h2t.shape[0], tn), lambda i: (0, i)),
                  pl.BlockSpec(wqt.shape, lambda i: (0, 0)),
                  pl.BlockSpec(keys.shape, lambda i: (0, 0, 0))],
        out_specs=[ospec(nw), ospec(nw), ospec(PEER_N_KEYS), ospec(PEER_N_KEYS)],
        out_shape=[tbl(nw), tbl(nw), tbl(PEER_N_KEYS), tbl(PEER_N_KEYS)],
        scratch_shapes=[pltpu.VMEM((PEER_HEADS * PEER_QDIM, tn), BF16)],
        compiler_params=_cparams("parallel"),
        name="peer_score",
    )(h2t, wqt, keys)


PEER_STEP_VMEM = 27 * 1024 * 1024


def _peer_keys_per_step(tn):
    per_key = PEER_N_KEYS * (tn * (4 + 2) + 2 * 2 * D_MODEL * 2)
    kps = 1
    while 2 * kps * per_key <= PEER_STEP_VMEM and 2 * kps <= PEER_N_KEYS:
        kps *= 2
    return kps


PEER_KEY_GROUP = 2


def _peer_dense_kernel(h2t_ref, c2_ref, p2_ref, cnt_ref, ka_ref, u_ref, vt_ref, x1_ref, modc_ref, modl_ref, fg_ref,
                       o_ref, acc_ref, pre_ref, hid_ref, *, ctx_rows, rows_per_sample, final):
    D = D_MODEL
    NK = PEER_N_KEYS
    tn = h2t_ref.shape[1]
    e = pl.program_id(1)

    @pl.when(e == 0)
    def _():
        acc_ref[...] = jnp.zeros_like(acc_ref)

    pre_ref[...] = _dot(pltpu.bitcast(u_ref[...], BF16), pltpu.bitcast(h2t_ref[...], BF16))

    kps = _peer_keys_per_step(tn)
    row = lambda ref, i, h: ref[h, i:i + 1, :]
    cnt_rows = [[row(cnt_ref, i, h) for h in range(PEER_HEADS)] for i in range(kps)]
    ka_rows = [[row(ka_ref, i, h) for h in range(PEER_HEADS)] for i in range(kps)]
    wpr = _words_per_row()

    def key_rows(i, h, cs):
        rep = lambda words: pltpu.bitcast(jnp.broadcast_to(words[:, cs], (BF16_ROWS // wpr, LANES)), BF16)
        return rep(cnt_rows[i][h]), rep(ka_rows[i][h])

    for c in range(tn // LANES):
        cs = slice(c * LANES, (c + 1) * LANES)
        for g in range(0, kps, PEER_KEY_GROUP):
            group = range(g, g + PEER_KEY_GROUP)
            w = {i: None for i in group}
            for h in range(PEER_HEADS):
                c2 = pltpu.bitcast(c2_ref[h, :, cs], BF16).reshape(NK // BF16_ROWS, BF16_ROWS, LANES)
                p2 = pltpu.bitcast(p2_ref[h, :, cs], BF16).reshape(NK // BF16_ROWS, BF16_ROWS, LANES)
                for i in group:
                    cnt, kaw = key_rows(i, h, cs)
                    t = jnp.where(c2 < cnt[None], p2, 0.0) * kaw[None]
                    w[i] = t if w[i] is None else w[i] + t
            for i in group:
                rows = slice(i * NK, (i + 1) * NK)
                x = pre_ref[rows, cs].astype(BF16)
                act = 0.5 * x * (1.0 + lax.erf(x * (2.0 ** -0.5)))
                hid_ref[rows, cs] = act * w[i].reshape(NK, LANES)

    acc_ref[...] += _dot(pltpu.bitcast(vt_ref[...], BF16), hid_ref[...])

    @pl.when(e == pl.num_programs(1) - 1)
    def _():
        y = acc_ref[...].T
        g_lat = modl_ref[0][:, 5 * D:6 * D]
        if ctx_rows:
            g_ctx = modc_ref[0][:, 5 * D:6 * D]
            tiles = rows_per_sample // tn
            r0 = (pl.program_id(0) % tiles) * tn
            row = r0 + lax.broadcasted_iota(jnp.int32, (tn, D), 0)
            gate = jnp.where(row < ctx_rows, g_ctx, g_lat)
        else:
            gate = g_lat
        x2 = x1_ref[...] + gate * y
        if final:
            x2 = _rms(x2, fg_ref[...])
        o_ref[...] = x2


def _peer_dense(h2t, sel, u, vt, x1, modrows, fg, *, tn, rows_per_sample, ctx_rows, final):
    ntok, D = x1.shape
    wpr = _words_per_row()
    kps = _peer_keys_per_step(tn)
    etile = kps * PEER_N_KEYS
    kspec = pl.BlockSpec((PEER_HEADS, kps, tn), lambda i, s: (0, s, i))
    ne = u.shape[0] * wpr // etile
    tiles = rows_per_sample // tn
    tspec = lambda t: pl.BlockSpec((PEER_HEADS, t.shape[1], tn), lambda i, s: (0, 0, i))
    return pl.pallas_call(
        functools.partial(_peer_dense_kernel, ctx_rows=ctx_rows, rows_per_sample=rows_per_sample, final=final),
        grid=(ntok // tn, ne),
        in_specs=[pl.BlockSpec((D // wpr, tn), lambda i, s: (0, i)),
                  tspec(sel[0]), tspec(sel[1]), kspec, kspec,
                  pl.BlockSpec((etile // wpr, D), lambda i, s: (s, 0)),
                  pl.BlockSpec((D // wpr, etile), lambda i, s: (0, s)),
                  pl.BlockSpec((tn, D), lambda i, s: (i, 0), pipeline_mode=pl.Buffered(1)),
                  pl.BlockSpec((1, 1, 6 * D), lambda i, s: (2 * (i // tiles), 0, 0)),
                  pl.BlockSpec((1, 1, 6 * D), lambda i, s: (2 * (i // tiles) + 1, 0, 0)),
                  pl.BlockSpec((1, D), lambda i, s: (0, 0))],
        out_specs=pl.BlockSpec((tn, D), lambda i, s: (i, 0)),
        out_shape=jax.ShapeDtypeStruct((ntok, D), F32),
        scratch_shapes=[pltpu.VMEM((D, tn), F32),
                        pltpu.VMEM((etile, tn), F32),
                        pltpu.VMEM((etile, tn), BF16)],
        compiler_params=_cparams("parallel", "arbitrary"),
        name="peer_dense",
    )(h2t, *sel, u, vt, x1, modrows, modrows, fg)


def _pack_rows_kernel(x_ref, o_ref):
    o_ref[...] = pltpu.bitcast(x_ref[...].astype(BF16), jnp.int32)


def _pack_rows(x):
    R, C = x.shape
    wpr = _words_per_row()
    br, bc = 512, 1024
    return pl.pallas_call(
        _pack_rows_kernel,
        grid=(R // br, C // bc),
        in_specs=[pl.BlockSpec((br, bc), lambda i, j: (i, j))],
        out_specs=pl.BlockSpec((br // wpr, bc), lambda i, j: (i, j)),
        out_shape=jax.ShapeDtypeStruct((R // wpr, C), jnp.int32),
        compiler_params=_cparams("parallel", "parallel"),
        name="pack_rows",
    )(x)


def _peer_tile(rows_per_sample):
    for tn in (1024, 768, 512, 256):
        if rows_per_sample % tn == 0:
            return tn
    raise ValueError(rows_per_sample)


_SWAP = np.concatenate([np.arange(16, 32), np.arange(0, 16), np.arange(48, 64), np.arange(32, 48)])


def _layout_w_in(w):
    D = w.shape[0]
    idx = np.cumsum(IN_SPLITS)[:-1]
    p, q, k, v, r, lrf, lrb, qdn, kvdn, krope, gz = jnp.split(w, idx, axis=1)
    zeros = lambda n: jnp.zeros((D, n), w.dtype)
    misc = jnp.concatenate([krope, lrf, lrb, zeros(LANES - MLA_ROPE - 2 * GLA_GATE_RANK)], axis=1)
    sw = jnp.concatenate([krope[:, _SWAP], zeros(LANES - MLA_ROPE)], axis=1)
    out = jnp.concatenate([p, q, k, v, r, gz, kvdn, misc, sw, qdn], axis=1)
    return jnp.concatenate([out, zeros(Z_COLS - out.shape[1])], axis=1).astype(BF16)


def _layout_gate_w(gate_w, row0):
    HK = GLA_HEADS * GLA_DK
    pad = jnp.zeros((LANES, HK), gate_w.dtype)
    return pad.at[row0:row0 + GLA_GATE_RANK].set(gate_w).astype(BF16)


def _layout_w_uq(w):
    H = MLA_HEADS
    w3 = w.reshape(MLA_Q_RANK, H, MLA_NOPE + MLA_ROPE)
    qn, qr = w3[..., :MLA_NOPE], w3[..., MLA_NOPE:]
    z = jnp.zeros((MLA_Q_RANK, H, LANES - MLA_ROPE), w.dtype)
    main = jnp.concatenate([qn, qr, z], axis=-1).reshape(MLA_Q_RANK, H * MQ)
    swp = jnp.concatenate([qr[..., _SWAP], z], axis=-1).reshape(MLA_Q_RANK, H * LANES)
    return jnp.concatenate([main, swp], axis=1).astype(BF16)


def _rope_tables(seq):
    half = MLA_ROPE // 2
    t = jnp.arange(seq)
    inv = ROPE_BASE ** (-jnp.arange(0, half, 2, dtype=F32) / half)
    ar = (t // GRID_W).astype(F32)[:, None] * inv
    ac = (t % GRID_W).astype(F32)[:, None] * inv
    cos = jnp.concatenate([jnp.cos(ar), jnp.cos(ar), jnp.cos(ac), jnp.cos(ac)], axis=1)
    sin = jnp.concatenate([-jnp.sin(ar), jnp.sin(ar), -jnp.sin(ac), jnp.sin(ac)], axis=1)
    cos = jnp.concatenate([jnp.ones((CTX_LEN, MLA_ROPE), F32), cos], axis=0)
    sin = jnp.concatenate([jnp.zeros((CTX_LEN, MLA_ROPE), F32), sin], axis=0)
    pad = jnp.zeros((CTX_LEN + seq, LANES - MLA_ROPE), F32)
    return jnp.concatenate([cos, pad], axis=1), jnp.concatenate([sin, pad], axis=1)


def kernel(x, c, ctx, c_ctx, ada_w, ada_b, norm1_g, norm2_g, w_in, pool_w, pool_scale, gla_gate_w, gla_gate_b, gla_norm_g, mla_q_norm_g, mla_kv_norm_g, mla_w_uq, mla_w_ukv, branch_w, w_out, peer_wq, peer_keys, peer_u, peer_v, final_norm_g):
    B, T, D = x.shape
    R = CTX_LEN + T
    assert D == D_MODEL and ctx.shape[1] == CTX_LEN == TM and T % TM == 0 and T % GRID_W == 0

    nrow = -(-(B + 1) // 8) * 8
    cc = jnp.concatenate([c, c_ctx[None], jnp.zeros((nrow - B - 1, D), F32)], axis=0)
    mod = _ada(cc, ada_w, ada_b)
    modrows = jnp.stack([jnp.broadcast_to(mod[:, B:B + 1], (DEPTH, B, 6 * D)), mod[:, :B]], axis=2)
    modrows = modrows.reshape(DEPTH, 2 * B, 1, 6 * D)

    cos, sin = _rope_tables(T)
    xall = jnp.concatenate([ctx, x], axis=1)
    row = lambda a: a.reshape(1, -1)

    for i in range(DEPTH):
        with_ctx = i < DEPTH - 1
        final = i == DEPTH - 1
        z = _in_proj(xall, modrows[i], row(norm1_g[i]), _layout_w_in(w_in[i]))

        pool_o = _pool(z, pool_w[i].astype(BF16), row(pool_scale[i]))

        o_b = _gla(z, _layout_gate_w(gla_gate_w[i, 1], MISC_LRB), row(gla_gate_b[i, 1]), fwd=False)
        gla_o = _gla(z, _layout_gate_w(gla_gate_w[i, 0], MISC_LRF), row(gla_gate_b[i, 0]), fwd=True,
                     ob=o_b, gn=row(gla_norm_g[i]))

        q, k, v = _mla_up(z, cos, sin, row(mla_q_norm_g[i]), row(mla_kv_norm_g[i]),
                          _layout_w_uq(mla_w_uq[i]), mla_w_ukv[i].astype(BF16))
        mla_o = _attn(q, k, v, with_ctx=with_ctx)

        x1, h2t = _merge(pool_o, gla_o, mla_o, z, xall, modrows[i], branch_w[i].astype(BF16),
                        w_out[i].astype(BF16), row(norm2_g[i]), with_ctx=with_ctx)

        rows = x1.shape[1]
        keys = peer_keys[i].reshape(2 * PEER_HEADS, PEER_N_KEYS, PEER_HALF).astype(BF16)
        sel = _peer_score(h2t, peer_wq[i].T.astype(BF16), keys)
        xall = _peer_dense(h2t, sel, _pack_rows(peer_u[i]), _pack_rows(peer_v[i].T), x1.reshape(B * rows, D),
                           modrows[i], row(final_norm_g), tn=_peer_tile(rows), rows_per_sample=rows,
                           ctx_rows=CTX_LEN if with_ctx else 0, final=final).reshape(B, rows, D)
    return xall
```

```python
import functools

import jax
import jax.numpy as jnp
import numpy as np
from jax import lax
from jax.experimental import pallas as pl
from jax.experimental.pallas import tpu as pltpu

F32 = jnp.float32
BF16 = jnp.bfloat16

D_MODEL = 1024
DEPTH = 2
CTX_LEN = 256
GRID_W = 64
NORM_EPS = 1e-6
POOL_WIDTH = 1024
POOL_WINDOWS = (2, 4, 8, 16)
POOL_GROUP = POOL_WIDTH // len(POOL_WINDOWS)
GLA_HEADS = 4
GLA_DK = 128
GLA_DV = 256
GLA_GATE_RANK = 16
GLA_TAU = 16.0
GLA_CHUNK = 64
MLA_HEADS = 8
MLA_Q_RANK = 384
MLA_KV_RANK = 256
MLA_NOPE = 128
MLA_ROPE = 64
MLA_V = 128
ROPE_BASE = 10000.0
N_BRANCH = 3
PEER_HEADS = 8
PEER_N_KEYS = 128
PEER_TOPK = 16
PEER_QDIM = 256
PEER_HALF = PEER_QDIM // 2
IN_SPLITS = (POOL_WIDTH, GLA_HEADS * GLA_DK, GLA_HEADS * GLA_DK, GLA_HEADS * GLA_DV, GLA_HEADS * GLA_DV,
             GLA_GATE_RANK, GLA_GATE_RANK, MLA_Q_RANK, MLA_KV_RANK, MLA_ROPE, N_BRANCH * D_MODEL)

LANES = 128
BF16_ROWS = 16
TM = 256
VMEM_LIMIT = 56 * 1024 * 1024

COL_P = 0
COL_Q = 1024
COL_K = 1536
COL_V = 2048
COL_R = 3072
COL_GZ = 4096
COL_KVDN = 7168
COL_MISC = 7424
COL_SW = 7552
COL_QDN = 7680
Z_COLS = 8192
IN_NBLK = 8192
MISC_LRF = MLA_ROPE
MISC_LRB = MLA_ROPE + GLA_GATE_RANK
MQ = MLA_NOPE + LANES


def _cparams(*sem):
    return pltpu.CompilerParams(dimension_semantics=sem, vmem_limit_bytes=VMEM_LIMIT)


def _rms(x, g):
    return x * lax.rsqrt(jnp.mean(x * x, axis=-1, keepdims=True) + NORM_EPS) * g


def _dot(a, b):
    return jnp.dot(a, b, preferred_element_type=F32)


def _dot_nt(a, b):
    return lax.dot_general(a, b, (((1,), (1,)), ((), ())), preferred_element_type=F32)


def _dot_tn(a, b):
    return lax.dot_general(a, b, (((0,), (0,)), ((), ())), preferred_element_type=F32)


def _ada_kernel(c_ref, w_ref, b_ref, o_ref):
    c = c_ref[...]
    a = (c * jax.nn.sigmoid(c)).astype(BF16)
    o_ref[0] = _dot(a, w_ref[0].astype(BF16)) + b_ref[0]


def _ada(cc, ada_w, ada_b):
    L, D, N = ada_w.shape
    rows = cc.shape[0]
    nb = N // D
    return pl.pallas_call(
        _ada_kernel,
        grid=(L, nb),
        in_specs=[pl.BlockSpec((rows, D), lambda l, j: (0, 0)),
                  pl.BlockSpec((1, D, D), lambda l, j: (l, 0, j)),
                  pl.BlockSpec((1, 1, D), lambda l, j: (l, 0, j))],
        out_specs=pl.BlockSpec((1, rows, D), lambda l, j: (l, 0, j)),
        out_shape=jax.ShapeDtypeStruct((L, rows, N), F32),
        compiler_params=_cparams("parallel", "parallel"),
        name="ada_mod",
    )(cc, ada_w, ada_b.reshape(L, 1, N))


def _in_proj_kernel(x_ref, mod_ref, g_ref, w_ref, z_ref):
    D = D_MODEL
    x = x_ref[0]
    mod = mod_ref[0]
    h = _rms(x, g_ref[...]) * (1.0 + mod[:, D:2 * D]) + mod[:, 0:D]
    z_ref[0] = _dot(h.astype(BF16), w_ref[...]).astype(BF16)


def _in_proj(xall, modrows, g, w):
    B, R, D = xall.shape
    nt = R // TM
    nn = Z_COLS // IN_NBLK
    return pl.pallas_call(
        _in_proj_kernel,
        grid=(nn, B, nt),
        in_specs=[pl.BlockSpec((1, TM, D), lambda n, b, t: (b, t, 0)),
                  pl.BlockSpec((1, 1, 6 * D), lambda n, b, t: (2 * b + jnp.minimum(t, 1), 0, 0)),
                  pl.BlockSpec((1, D), lambda n, b, t: (0, 0)),
                  pl.BlockSpec((D, IN_NBLK), lambda n, b, t: (0, n))],
        out_specs=pl.BlockSpec((1, TM, IN_NBLK), lambda n, b, t: (b, t, n)),
        out_shape=jax.ShapeDtypeStruct((B, R, Z_COLS), BF16),
        compiler_params=_cparams("parallel", "parallel", "parallel"),
        name="in_proj",
    )(xall, modrows, g, w)


def _pool_kernel(u_ref, w_ref, sc_ref, o_ref):
    R = u_ref.shape[1]
    row = lax.broadcasted_iota(jnp.int32, (R, POOL_GROUP), 0)
    seg_lo = jnp.where(row < CTX_LEN, 0, CTX_LEN)
    seg_hi = jnp.where(row < CTX_LEN, CTX_LEN, R)
    for gi, win in enumerate(POOL_WINDOWS):
        sl = slice(gi * POOL_GROUP, (gi + 1) * POOL_GROUP)
        u = u_ref[0, :, sl].astype(F32)
        lo_off, hi_off = win // 2, win - win // 2
        acc = jnp.zeros_like(u)
        for d in range(-lo_off, hi_off):
            shifted = u if d == 0 else pltpu.roll(u, (R - d) % R, axis=0)
            ok = (row + d >= seg_lo) & (row + d < seg_hi)
            acc = acc + jnp.where(ok, shifted, 0.0)
        cnt = (jnp.minimum(row + hi_off, seg_hi) - jnp.maximum(row - lo_off, seg_lo)).astype(F32)
        diff = (acc / cnt - u).astype(BF16)
        o_ref[0, :, sl] = (_dot(diff, w_ref[gi]) * sc_ref[:, sl]).astype(BF16)


def _pool(z, pool_w, pool_scale):
    B, R, _ = z.shape
    return pl.pallas_call(
        _pool_kernel,
        grid=(B,),
        in_specs=[pl.BlockSpec((1, R, POOL_WIDTH), lambda b: (b, 0, COL_P // POOL_WIDTH)),
                  pl.BlockSpec(pool_w.shape, lambda b: (0, 0, 0)),
                  pl.BlockSpec((1, POOL_WIDTH), lambda b: (0, 0))],
        out_specs=pl.BlockSpec((1, R, POOL_WIDTH), lambda b: (b, 0, 0)),
        out_shape=jax.ShapeDtypeStruct((B, R, POOL_WIDTH), BF16),
        compiler_params=_cparams("parallel"),
        name="pool",
    )(z, pool_w, pool_scale)


def _split3(x):
    a = x.astype(BF16)
    r = x - a.astype(F32)
    b = r.astype(BF16)
    c = (r - b.astype(F32)).astype(BF16)
    return a, b, c


def _gla_kernel(*refs, fwd):
    if fwd:
        q_ref, k_ref, v_ref, misc_ref, gw_ref, gb_ref, ob_ref, r_ref, gn_ref, o_ref, st_ref = refs
    else:
        q_ref, k_ref, v_ref, misc_ref, gw_ref, gb_ref, o_ref, st_ref = refs
    C = GLA_CHUNK
    nchunk = TM // C
    HK = GLA_HEADS * GLA_DK

    @pl.when(pl.program_id(1) == 0)
    def _():
        st_ref[...] = jnp.zeros_like(st_ref)

    zg = _dot(misc_ref[0], gw_ref[...]) + gb_ref[...]
    g = (jnp.minimum(zg, 0.0) - jnp.log1p(jnp.exp(-jnp.abs(zg)))) * (1.0 / GLA_TAU)

    row = lax.broadcasted_iota(jnp.int32, (TM, TM), 0)
    col = lax.broadcasted_iota(jnp.int32, (TM, TM), 1)
    shift = C.bit_length() - 1
    same = (row >> shift) == (col >> shift)
    tri = same & ((col <= row) if fwd else (col >= row))
    ones = jnp.concatenate([jnp.where(tri, 1.0, 0.0), jnp.where(same, 1.0, 0.0)], axis=0).astype(BF16)
    g1, g2, g3 = _split3(g)
    both = _dot(ones, g1) + _dot(ones, g2) + _dot(ones, g3)
    b = both[:TM]
    bt = both[TM:]

    qe = q_ref[0].astype(F32) * (GLA_DK ** -0.5) * jnp.exp(b)
    ke = (k_ref[0].astype(F32) * jnp.exp(-b)).astype(BF16)
    kd = (k_ref[0].astype(F32) * jnp.exp(bt - b)).astype(BF16)
    qe = qe.astype(BF16)
    dec = jnp.exp(bt)
    v = v_ref[0]

    order = range(nchunk) if fwd else range(nchunk - 1, -1, -1)
    for h in range(GLA_HEADS):
        ks = slice(h * GLA_DK, (h + 1) * GLA_DK)
        vs = slice(h * GLA_DV, (h + 1) * GLA_DV)
        att = jnp.where(tri, _dot_nt(qe[:, ks], ke[:, ks]), 0.0).astype(BF16)
        o_in = _dot(att, v[:, vs])
        st = st_ref[h]
        parts = [None] * nchunk
        for j in order:
            rs = slice(j * C, (j + 1) * C)
            parts[j] = o_in[rs] + _dot_nt(qe[rs, ks], st.astype(BF16))
            st = st * dec[j * C:j * C + 1, ks] + _dot_tn(v[rs, vs], kd[rs, ks])
        st_ref[h] = st
        o = jnp.concatenate(parts, axis=0)
        if fwd:
            o = o + ob_ref[0, :, vs].astype(F32)
            r = r_ref[0, :, vs].astype(F32)
            o = _rms(o, gn_ref[...]) * (r * jax.nn.sigmoid(r))
        o_ref[0, :, vs] = o.astype(BF16)


def _gla(z, gw, gb, *, fwd, ob=None, gn=None):
    B, R, _ = z.shape
    nt = R // TM
    HK, HV = GLA_HEADS * GLA_DK, GLA_HEADS * GLA_DV
    if fwd:
        tile = lambda c: c
    else:
        tile = lambda c: jnp.where(c == 0, 0, nt - c)
    in_specs = [pl.BlockSpec((1, TM, HK), lambda b, c: (b, tile(c), COL_Q // HK)),
                pl.BlockSpec((1, TM, HK), lambda b, c: (b, tile(c), COL_K // HK)),
                pl.BlockSpec((1, TM, HV), lambda b, c: (b, tile(c), COL_V // HV)),
                pl.BlockSpec((1, TM, LANES), lambda b, c: (b, tile(c), COL_MISC // LANES)),
                pl.BlockSpec((LANES, HK), lambda b, c: (0, 0)),
                pl.BlockSpec((1, HK), lambda b, c: (0, 0))]
    args = [z, z, z, z, gw, gb]
    if fwd:
        in_specs += [pl.BlockSpec((1, TM, HV), lambda b, c: (b, tile(c), 0)),
                     pl.BlockSpec((1, TM, HV), lambda b, c: (b, tile(c), COL_R // HV)),
                     pl.BlockSpec((1, GLA_DV), lambda b, c: (0, 0))]
        args += [ob, z, gn]
    return pl.pallas_call(
        functools.partial(_gla_kernel, fwd=fwd),
        grid=(B, nt),
        in_specs=in_specs,
        out_specs=pl.BlockSpec((1, TM, HV), lambda b, c: (b, tile(c), 0)),
        out_shape=jax.ShapeDtypeStruct((B, R, HV), BF16),
        scratch_shapes=[pltpu.VMEM((GLA_HEADS, GLA_DV, GLA_DK), F32)],
        compiler_params=_cparams("parallel", "arbitrary"),
        name="gla_fwd" if fwd else "gla_bwd",
    )(*args)


def _mla_up_kernel(qdn_ref, kvdn_ref, misc_ref, sw_ref, cos_ref, sin_ref, qg_ref, kvg_ref, wq_ref, wkv_ref,
                   q_ref, k_ref, v_ref):
    H = MLA_HEADS
    scale = (MLA_NOPE + MLA_ROPE) ** -0.5 * float(np.log2(np.e))
    cos = cos_ref[...]
    sin = sin_ref[...]
    qn = _rms(qdn_ref[0].astype(F32), qg_ref[...]).astype(BF16)
    qall = _dot(qn, wq_ref[...])
    for h in range(H):
        base = h * MQ
        q_ref[0, :, base:base + MLA_NOPE] = (qall[:, base:base + MLA_NOPE] * scale).astype(BF16)
        rot = (qall[:, base + MLA_NOPE:base + MQ] * cos
               + qall[:, H * MQ + h * LANES:H * MQ + (h + 1) * LANES] * sin)
        q_ref[0, :, base + MLA_NOPE:base + MQ] = (rot * scale).astype(BF16)
    kvn = _rms(kvdn_ref[0].astype(F32), kvg_ref[...]).astype(BF16)
    kvall = _dot(kvn, wkv_ref[...])
    kr = (misc_ref[0].astype(F32) * cos + sw_ref[0].astype(F32) * sin).astype(BF16)
    for h in range(H):
        src = h * (MLA_NOPE + MLA_V)
        k_ref[0, :, h * MQ:h * MQ + MLA_NOPE] = kvall[:, src:src + MLA_NOPE].astype(BF16)
        k_ref[0, :, h * MQ + MLA_NOPE:(h + 1) * MQ] = kr
        v_ref[0, :, h * MLA_V:(h + 1) * MLA_V] = kvall[:, src + MLA_NOPE:src + MLA_NOPE + MLA_V].astype(BF16)


def _mla_up(z, cos, sin, qg, kvg, wq, wkv):
    B, R, _ = z.shape
    nt = R // TM
    H = MLA_HEADS
    const = lambda b, t: (0, 0)
    return pl.pallas_call(
        _mla_up_kernel,
        grid=(B, nt),
        in_specs=[pl.BlockSpec((1, TM, MLA_Q_RANK), lambda b, t: (b, t, COL_QDN // MLA_Q_RANK)),
                  pl.BlockSpec((1, TM, MLA_KV_RANK), lambda b, t: (b, t, COL_KVDN // MLA_KV_RANK)),
                  pl.BlockSpec((1, TM, LANES), lambda b, t: (b, t, COL_MISC // LANES)),
                  pl.BlockSpec((1, TM, LANES), lambda b, t: (b, t, COL_SW // LANES)),
                  pl.BlockSpec((TM, LANES), lambda b, t: (t, 0)),
                  pl.BlockSpec((TM, LANES), lambda b, t: (t, 0)),
                  pl.BlockSpec(qg.shape, const),
                  pl.BlockSpec(kvg.shape, const),
                  pl.BlockSpec(wq.shape, const),
                  pl.BlockSpec(wkv.shape, const)],
        out_specs=[pl.BlockSpec((1, TM, H * MQ), lambda b, t: (b, t, 0)),
                   pl.BlockSpec((1, TM, H * MQ), lambda b, t: (b, t, 0)),
                   pl.BlockSpec((1, TM, H * MLA_V), lambda b, t: (b, t, 0))],
        out_shape=[jax.ShapeDtypeStruct((B, R, H * MQ), BF16),
                   jax.ShapeDtypeStruct((B, R, H * MQ), BF16),
                   jax.ShapeDtypeStruct((B, R, H * MLA_V), BF16)],
        compiler_params=_cparams("parallel", "parallel"),
        name="mla_up",
    )(z, z, z, z, cos, sin, qg, kvg, wq, wkv)


ATTN_HEADS = 8


def _attn_kernel(q_ref, k_ref, v_ref, o_ref, *, q_off):
    R = k_ref.shape[1]

    def attend(nk):
        for j in range(ATTN_HEADS):
            s = _dot_nt(q_ref[0, :, j * MQ:(j + 1) * MQ], k_ref[0, :nk, j * MQ:(j + 1) * MQ])
            p = jnp.exp2(s - jnp.max(s, axis=-1, keepdims=True))
            l = jnp.sum(p, axis=-1, keepdims=True)
            o = _dot(p.astype(BF16), v_ref[0, :nk, j * MLA_V:(j + 1) * MLA_V]) / l
            o_ref[0, :, j * MLA_V:(j + 1) * MLA_V] = o.astype(BF16)

    if q_off == 0:
        qi = pl.program_id(2)

        @pl.when(qi == 0)
        def _():
            attend(CTX_LEN)

        @pl.when(qi > 0)
        def _():
            attend(R)
    else:
        attend(R)


def _attn(q, k, v, *, with_ctx):
    B, R, _ = q.shape
    q_off = 0 if with_ctx else CTX_LEN // TM
    nq = R // TM - q_off
    G = ATTN_HEADS
    return pl.pallas_call(
        functools.partial(_attn_kernel, q_off=q_off),
        grid=(B, MLA_HEADS // G, nq),
        in_specs=[pl.BlockSpec((1, TM, G * MQ), lambda b, h, i: (b, i + q_off, h)),
                  pl.BlockSpec((1, R, G * MQ), lambda b, h, i: (b, 0, h)),
                  pl.BlockSpec((1, R, G * MLA_V), lambda b, h, i: (b, 0, h))],
        out_specs=pl.BlockSpec((1, TM, G * MLA_V), lambda b, h, i: (b, i + q_off, h)),
        out_shape=jax.ShapeDtypeStruct((B, R, MLA_HEADS * MLA_V), BF16),
        compiler_params=_cparams("parallel", "parallel", "arbitrary"),
        name="mla_attn",
    )(q, k, v)


def _merge_kernel(po_ref, go_ref, mo_ref, gz0_ref, gz1_ref, gz2_ref, x_ref, mod_ref, bw_ref, wo_ref, g_ref,
                  x1_ref, h2t_ref):
    D = D_MODEL
    m = None
    for i, (o_ref, gz_ref) in enumerate(((po_ref, gz0_ref), (go_ref, gz1_ref), (mo_ref, gz2_ref))):
        t = jax.nn.sigmoid(gz_ref[0].astype(F32)) * _dot(o_ref[0], bw_ref[i])
        m = t if m is None else m + t
    y = _dot(m.astype(BF16), wo_ref[...])
    mod = mod_ref[0]
    x1 = x_ref[0] + mod[:, 2 * D:3 * D] * y
    x1_ref[0] = x1
    h2 = _rms(x1, g_ref[...]) * (1.0 + mod[:, 4 * D:5 * D]) + mod[:, 3 * D:4 * D]
    h2t_ref[...] = pltpu.bitcast(h2.T.astype(BF16), jnp.int32)


def _merge(pool_o, gla_o, mla_o, z, xall, modrows, bw, wo, g, *, with_ctx):
    B, R, D = xall.shape
    t_off = 0 if with_ctx else CTX_LEN // TM
    nt = R // TM - t_off
    tok = lambda b, t: (b, t + t_off, 0)
    gzb = COL_GZ // D
    wpr = _words_per_row()
    mo_off = t_off if mla_o.shape[1] == R else 0
    return pl.pallas_call(
        _merge_kernel,
        grid=(B, nt),
        in_specs=[pl.BlockSpec((1, TM, D), tok),
                  pl.BlockSpec((1, TM, D), tok),
                  pl.BlockSpec((1, TM, D), lambda b, t: (b, t + mo_off, 0)),
                  pl.BlockSpec((1, TM, D), lambda b, t: (b, t + t_off, gzb)),
                  pl.BlockSpec((1, TM, D), lambda b, t: (b, t + t_off, gzb + 1)),
                  pl.BlockSpec((1, TM, D), lambda b, t: (b, t + t_off, gzb + 2)),
                  pl.BlockSpec((1, TM, D), tok),
                  pl.BlockSpec((1, 1, 6 * D), lambda b, t: (2 * b + jnp.minimum(t + t_off, 1), 0, 0)),
                  pl.BlockSpec(bw.shape, lambda b, t: (0, 0, 0)),
                  pl.BlockSpec(wo.shape, lambda b, t: (0, 0)),
                  pl.BlockSpec((1, D), lambda b, t: (0, 0))],
        out_specs=[pl.BlockSpec((1, TM, D), lambda b, t: (b, t, 0)),
                   pl.BlockSpec((D // wpr, TM), lambda b, t: (0, b * nt + t))],
        out_shape=[jax.ShapeDtypeStruct((B, nt * TM, D), F32),
                   jax.ShapeDtypeStruct((D // wpr, B * nt * TM), jnp.int32)],
        compiler_params=_cparams("parallel", "parallel"),
        name="merge",
    )(pool_o, gla_o, mla_o, z, z, z, xall, modrows, bw, wo, g)


def _top16(s):
    nk, n = s.shape
    iota = lax.broadcasted_iota(jnp.int32, (nk, n), 0).astype(F32)
    slot = lax.broadcasted_iota(jnp.int32, (PEER_TOPK, n), 0)
    rank = jnp.full((nk, n), float(PEER_TOPK), F32)
    vals = jnp.zeros((PEER_TOPK, n), F32)
    for j in range(PEER_TOPK):
        m = jnp.max(s, axis=0, keepdims=True)
        idx = jnp.min(jnp.where(s == m, iota, float(nk)), axis=0, keepdims=True)
        hit = iota == idx
        rank = jnp.where(hit, float(j), rank)
        s = jnp.where(hit, -jnp.inf, s)
        vals = jnp.where(slot == j, m, vals)
    return vals, rank


PAIR_SHORT = PEER_TOPK // 2
PAIR_ROWS = PEER_TOPK + (PEER_TOPK - 1) * PAIR_SHORT


def _pair_rows(t2):
    return jnp.concatenate([t2] + [t2[0:PAIR_SHORT]] * (PEER_TOPK - 1), axis=0)


def _pair_candidates(t1, t2):
    K = PEER_TOPK
    n = t1.shape[1]
    first = jnp.concatenate([jnp.broadcast_to(t1[0:1], (K, n))]
                            + [jnp.broadcast_to(t1[i:i + 1], (PAIR_SHORT, n)) for i in range(1, K)], axis=0)
    r = lax.broadcasted_iota(jnp.int32, (PAIR_ROWS, n), 0)
    sh = PAIR_SHORT.bit_length() - 1
    i_of = jnp.where(r < K, 0, ((r - K) >> sh) + 1)
    j_of = jnp.where(r < K, r, (r - K) & (PAIR_SHORT - 1))
    cand = jnp.where((i_of + 1) * (j_of + 1) <= K, first + _pair_rows(t2), -jnp.inf)
    return cand, (i_of * K + j_of).astype(F32)


def _top16_values(s, with_rank):
    n = s.shape[1]
    slot = lax.broadcasted_iota(jnp.int32, (PEER_TOPK, n), 0)
    vals = jnp.zeros((PEER_TOPK, n), F32)
    rank = jnp.full(s.shape, float(PEER_TOPK), F32) if with_rank else None
    for j in range(PEER_TOPK):
        m = jnp.max(s, axis=0, keepdims=True)
        hit = s == m
        if with_rank:
            rank = jnp.where(hit, float(j), rank)
        s = jnp.where(hit, -jnp.inf, s)
        vals = jnp.where(slot == j, m, vals)
    used = jnp.sum(jnp.where(s == -jnp.inf, 1.0, 0.0), axis=0, keepdims=True)
    return vals, rank, jnp.where(used == float(PEER_TOPK), 1.0, 0.0)


PAIR_GROUPS = (((0, 0, 8),), ((0, 8, 8),), ((1, 0, 8),), ((2, 0, 5), (4, 0, 3)), ((3, 0, 4), (5, 0, 2), (6, 0, 2)),
               ((7, 0, 2),))
PAIR_SUB = 8
assert sorted([(i, j0 + k) for g in PAIR_GROUPS for i, j0, ln in g for k in range(ln)]
              + [(i, 0) for i in range(PAIR_SUB, PEER_TOPK)]) == sorted(
    (i, j) for i in range(PEER_TOPK) for j in range(PEER_TOPK) if (i + 1) * (j + 1) <= PEER_TOPK)


def _pair_counts(t1, t2):
    K = PEER_TOPK
    S = PAIR_SUB
    n = t1.shape[1]
    sub = lax.broadcasted_iota(jnp.int32, (S, n), 0)
    groups = []
    for runs in PAIR_GROUPS:
        val, s = None, 0
        for i, j0, length in runs:
            src = t2[j0 // S * S:(j0 // S + 1) * S]
            shift = (s - j0 % S) % S
            b = src if shift == 0 else pltpu.roll(src, shift, axis=0)
            v = t1[i:i + 1] + b
            val = v if val is None else jnp.where(sub >= s, v, val)
            s += length
        groups.append(val if s == S else jnp.where(sub < s, val, -jnp.inf))
    groups.append(t1[S:K] + t2[0:1])
    cand0 = jnp.concatenate(groups, axis=0)
    n_invalid = sum(S - sum(ln for _, _, ln in g) for g in PAIR_GROUPS)

    cand = cand0
    zsum = jnp.zeros((1, n), F32)
    best = m = None
    for j in range(K):
        m = jnp.max(cand, axis=0, keepdims=True)
        cand = jnp.where(cand == m, -jnp.inf, cand)
        if j == 0:
            best = m
        zsum = zsum + jnp.exp(m - best)
    used = jnp.sum(jnp.where(cand == -jnp.inf, 1.0, 0.0), axis=0, keepdims=True)
    exact = jnp.where(used == float(n_invalid + K), 1.0, 0.0)

    picked = jnp.where(cand0 >= m, 1.0, 0.0)
    low = jnp.zeros((S, n), F32)
    for gi, runs in enumerate(PAIR_GROUPS):
        blk, s = picked[gi * S:(gi + 1) * S], 0
        for i, _, length in runs:
            part = blk if length == S else jnp.where((sub >= s) & (sub < s + length), blk, 0.0)
            low = low + jnp.where(sub == i, jnp.sum(part, axis=0, keepdims=True), 0.0)
            s += length
    return jnp.concatenate([low, picked[len(PAIR_GROUPS) * S:]], axis=0), zsum, exact


def _pair_top16(t1, t2):
    K = PEER_TOPK
    n = t1.shape[1]
    cand, flat = _pair_candidates(t1, t2)
    irow = lax.broadcasted_iota(jnp.int32, (K, n), 0).astype(F32)
    cnt = jnp.zeros((K, n), F32)
    zsum = jnp.zeros((1, n), F32)
    best = None
    for j in range(K):
        m = jnp.max(cand, axis=0, keepdims=True)
        idx = jnp.min(jnp.where(cand == m, flat, float(K * K)), axis=0, keepdims=True)
        cand = jnp.where(flat == idx, -jnp.inf, cand)
        cnt = cnt + jnp.where(irow == jnp.floor(idx * (1.0 / K)), 1.0, 0.0)
        if j == 0:
            best = m
        zsum = zsum + jnp.exp(m - best)
    return cnt, zsum


def _words_per_row():
    return 4 // jnp.dtype(BF16).itemsize


def _splat_words(x):
    bits = lax.bitcast_convert_type(x.astype(BF16).astype(F32), jnp.int32)
    if _words_per_row() == 1:
        return bits
    return bits | lax.shift_right_logical(bits, 16)


def _peer_score_kernel(h2t_ref, wqt_ref, keys_ref, c2_ref, p2_ref, cnt_ref, ka_ref, qt_ref):
    K = PEER_TOPK
    qt_ref[...] = _dot(wqt_ref[...], pltpu.bitcast(h2t_ref[...], BF16)).astype(BF16)

    def store(h, cs, s1, s2, t1, rank1, t2, rank2, cnt, zsum):
        rowcnt = jnp.zeros_like(s1)
        for j in range(K):
            in_row = (s1 == t1[j:j + 1]) if rank1 is None else (rank1 == float(j))
            rowcnt = jnp.where(in_row, cnt[j:j + 1], rowcnt)
        c2_ref[h, :, cs] = pltpu.bitcast(rank2.astype(BF16), jnp.int32)
        p2_ref[h, :, cs] = pltpu.bitcast(jnp.exp(s2 - t2[0:1]).astype(BF16), jnp.int32)
        cnt_ref[h, :, cs] = _splat_words(rowcnt)
        ka_ref[h, :, cs] = _splat_words(jnp.exp(s1 - t1[0:1]) / zsum)

    def head(h, carry):
        base = pl.multiple_of(h * PEER_QDIM, PEER_QDIM)
        s1_all = _dot(keys_ref[h], qt_ref[pl.ds(base, PEER_HALF), :])
        s2_all = _dot(keys_ref[PEER_HEADS + h], qt_ref[pl.ds(base + PEER_HALF, PEER_HALF), :])
        chunks = []
        for c in range(s1_all.shape[1] // LANES):
            cs = slice(c * LANES, (c + 1) * LANES)
            s1, s2 = s1_all[:, cs], s2_all[:, cs]
            t1, _, ok1 = _top16_values(s1, with_rank=False)
            t2, rank2, ok2 = _top16_values(s2, with_rank=True)
            cnt, zsum, ok3 = _pair_counts(t1, t2)
            store(h, cs, s1, s2, t1, None, t2, rank2, cnt, zsum)
            chunks.append((cs, s1, s2, jnp.min(ok1 * ok2 * ok3)))

        for cs, s1, s2, exact in chunks:
            @pl.when(exact < 0.5)
            def _(cs=cs, s1=s1, s2=s2):
                t1x, rank1x = _top16(s1)
                t2x, rank2x = _top16(s2)
                cntx, zx = _pair_top16(t1x, t2x)
                store(h, cs, s1, s2, t1x, rank1x, t2x, rank2x, cntx, zx)

        return carry

    lax.fori_loop(0, PEER_HEADS, head, 0)


PEER_SCORE_TN = 8 * LANES


def _peer_score(h2t, wqt, keys):
    ntok = h2t.shape[1]
    tn = PEER_SCORE_TN if ntok % PEER_SCORE_TN == 0 else 2 * LANES
    assert ntok % tn == 0
    nw = PEER_N_KEYS // _words_per_row()
    tbl = lambda rows: jax.ShapeDtypeStruct((PEER_HEADS, rows, ntok), jnp.int32)
    ospec = lambda rows: pl.BlockSpec((PEER_HEADS, rows, tn), lambda i: (0, 0, i))
    return pl.pallas_call(
        _peer_score_kernel,
        grid=(ntok // tn,),
        in_specs=[pl.BlockSpec((h2t.shape[0], tn), lambda i: (0, i)),
                  pl.BlockSpec(wqt.shape, lambda i: (0, 0)),
                  pl.BlockSpec(keys.shape, lambda i: (0, 0, 0))],
        out_specs=[ospec(nw), ospec(nw), ospec(PEER_N_KEYS), ospec(PEER_N_KEYS)],
        out_shape=[tbl(nw), tbl(nw), tbl(PEER_N_KEYS), tbl(PEER_N_KEYS)],
        scratch_shapes=[pltpu.VMEM((PEER_HEADS * PEER_QDIM, tn), BF16)],
        compiler_params=_cparams("parallel"),
        name="peer_score",
    )(h2t, wqt, keys)


PEER_STEP_VMEM = 27 * 1024 * 1024


def _peer_keys_per_step(tn):
    per_key = PEER_N_KEYS * (tn * (4 + 2) + 2 * 2 * D_MODEL * 2)
    kps = 1
    while 2 * kps * per_key <= PEER_STEP_VMEM and 2 * kps <= PEER_N_KEYS:
        kps *= 2
    return kps


PEER_KEY_GROUP = 2


def _peer_dense_kernel(h2t_ref, c2_ref, p2_ref, cnt_ref, ka_ref, u_ref, vt_ref, x1_ref, modc_ref, modl_ref, fg_ref,
                       o_ref, acc_ref, pre_ref, hid_ref, *, ctx_rows, rows_per_sample, final):
    D = D_MODEL
    NK = PEER_N_KEYS
    tn = h2t_ref.shape[1]
    e = pl.program_id(1)

    @pl.when(e == 0)
    def _():
        acc_ref[...] = jnp.zeros_like(acc_ref)

    pre_ref[...] = _dot(pltpu.bitcast(u_ref[...], BF16), pltpu.bitcast(h2t_ref[...], BF16))

    kps = _peer_keys_per_step(tn)
    row = lambda ref, i, h: ref[h, i:i + 1, :]
    cnt_rows = [[row(cnt_ref, i, h) for h in range(PEER_HEADS)] for i in range(kps)]
    ka_rows = [[row(ka_ref, i, h) for h in range(PEER_HEADS)] for i in range(kps)]
    wpr = _words_per_row()

    def key_rows(i, h, cs):
        rep = lambda words: pltpu.bitcast(jnp.broadcast_to(words[:, cs], (BF16_ROWS // wpr, LANES)), BF16)
        return rep(cnt_rows[i][h]), rep(ka_rows[i][h])

    for c in range(tn // LANES):
        cs = slice(c * LANES, (c + 1) * LANES)
        for g in range(0, kps, PEER_KEY_GROUP):
            group = range(g, g + PEER_KEY_GROUP)
            w = {i: None for i in group}
            for h in range(PEER_HEADS):
                c2 = pltpu.bitcast(c2_ref[h, :, cs], BF16).reshape(NK // BF16_ROWS, BF16_ROWS, LANES)
                p2 = pltpu.bitcast(p2_ref[h, :, cs], BF16).reshape(NK // BF16_ROWS, BF16_ROWS, LANES)
                for i in group:
                    cnt, kaw = key_rows(i, h, cs)
                    t = jnp.where(c2 < cnt[None], p2, 0.0) * kaw[None]
                    w[i] = t if w[i] is None else w[i] + t
            for i in group:
                rows = slice(i * NK, (i + 1) * NK)
                x = pre_ref[rows, cs].astype(BF16)
                act = 0.5 * x * (1.0 + lax.erf(x * (2.0 ** -0.5)))
                hid_ref[rows, cs] = act * w[i].reshape(NK, LANES)

    acc_ref[...] += _dot(pltpu.bitcast(vt_ref[...], BF16), hid_ref[...])

    @pl.when(e == pl.num_programs(1) - 1)
    def _():
        y = acc_ref[...].T
        g_lat = modl_ref[0][:, 5 * D:6 * D]
        if ctx_rows:
            g_ctx = modc_ref[0][:, 5 * D:6 * D]
            tiles = rows_per_sample // tn
            r0 = (pl.program_id(0) % tiles) * tn
            row = r0 + lax.broadcasted_iota(jnp.int32, (tn, D), 0)
            gate = jnp.where(row < ctx_rows, g_ctx, g_lat)
        else:
            gate = g_lat
        x2 = x1_ref[...] + gate * y
        if final:
            x2 = _rms(x2, fg_ref[...])
        o_ref[...] = x2


def _peer_dense(h2t, sel, u, vt, x1, modrows, fg, *, tn, rows_per_sample, ctx_rows, final):
    ntok, D = x1.shape
    wpr = _words_per_row()
    kps = _peer_keys_per_step(tn)
    etile = kps * PEER_N_KEYS
    kspec = pl.BlockSpec((PEER_HEADS, kps, tn), lambda i, s: (0, s, i))
    ne = u.shape[0] * wpr // etile
    tiles = rows_per_sample // tn
    tspec = lambda t: pl.BlockSpec((PEER_HEADS, t.shape[1], tn), lambda i, s: (0, 0, i))
    return pl.pallas_call(
        functools.partial(_peer_dense_kernel, ctx_rows=ctx_rows, rows_per_sample=rows_per_sample, final=final),
        grid=(ntok // tn, ne),
        in_specs=[pl.BlockSpec((D // wpr, tn), lambda i, s: (0, i)),
                  tspec(sel[0]), tspec(sel[1]), kspec, kspec,
                  pl.BlockSpec((etile // wpr, D), lambda i, s: (s, 0)),
                  pl.BlockSpec((D // wpr, etile), lambda i, s: (0, s)),
                  pl.BlockSpec((tn, D), lambda i, s: (i, 0), pipeline_mode=pl.Buffered(1)),
                  pl.BlockSpec((1, 1, 6 * D), lambda i, s: (2 * (i // tiles), 0, 0)),
                  pl.BlockSpec((1, 1, 6 * D), lambda i, s: (2 * (i // tiles) + 1, 0, 0)),
                  pl.BlockSpec((1, D), lambda i, s: (0, 0))],
        out_specs=pl.BlockSpec((tn, D), lambda i, s: (i, 0)),
        out_shape=jax.ShapeDtypeStruct((ntok, D), F32),
        scratch_shapes=[pltpu.VMEM((D, tn), F32),
                        pltpu.VMEM((etile, tn), F32),
                        pltpu.VMEM((etile, tn), BF16)],
        compiler_params=_cparams("parallel", "arbitrary"),
        name="peer_dense",
    )(h2t, *sel, u, vt, x1, modrows, modrows, fg)


def _pack_rows_kernel(x_ref, o_ref):
    o_ref[...] = pltpu.bitcast(x_ref[...].astype(BF16), jnp.int32)


def _pack_rows(x):
    R, C = x.shape
    wpr = _words_per_row()
    br, bc = 512, 1024
    return pl.pallas_call(
        _pack_rows_kernel,
        grid=(R // br, C // bc),
        in_specs=[pl.BlockSpec((br, bc), lambda i, j: (i, j))],
        out_specs=pl.BlockSpec((br // wpr, bc), lambda i, j: (i, j)),
        out_shape=jax.ShapeDtypeStruct((R // wpr, C), jnp.int32),
        compiler_params=_cparams("parallel", "parallel"),
        name="pack_rows",
    )(x)


def _peer_tile(rows_per_sample):
    for tn in (1024, 768, 512, 256):
        if rows_per_sample % tn == 0:
            return tn
    raise ValueError(rows_per_sample)


_SWAP = np.concatenate([np.arange(16, 32), np.arange(0, 16), np.arange(48, 64), np.arange(32, 48)])


def _layout_w_in(w):
    D = w.shape[0]
    idx = np.cumsum(IN_SPLITS)[:-1]
    p, q, k, v, r, lrf, lrb, qdn, kvdn, krope, gz = jnp.split(w, idx, axis=1)
    zeros = lambda n: jnp.zeros((D, n), w.dtype)
    misc = jnp.concatenate([krope, lrf, lrb, zeros(LANES - MLA_ROPE - 2 * GLA_GATE_RANK)], axis=1)
    sw = jnp.concatenate([krope[:, _SWAP], zeros(LANES - MLA_ROPE)], axis=1)
    out = jnp.concatenate([p, q, k, v, r, gz, kvdn, misc, sw, qdn], axis=1)
    return jnp.concatenate([out, zeros(Z_COLS - out.shape[1])], axis=1).astype(BF16)


def _layout_gate_w(gate_w, row0):
    HK = GLA_HEADS * GLA_DK
    pad = jnp.zeros((LANES, HK), gate_w.dtype)
    return pad.at[row0:row0 + GLA_GATE_RANK].set(gate_w).astype(BF16)


def _layout_w_uq(w):
    H = MLA_HEADS
    w3 = w.reshape(MLA_Q_RANK, H, MLA_NOPE + MLA_ROPE)
    qn, qr = w3[..., :MLA_NOPE], w3[..., MLA_NOPE:]
    z = jnp.zeros((MLA_Q_RANK, H, LANES - MLA_ROPE), w.dtype)
    main = jnp.concatenate([qn, qr, z], axis=-1).reshape(MLA_Q_RANK, H * MQ)
    swp = jnp.concatenate([qr[..., _SWAP], z], axis=-1).reshape(MLA_Q_RANK, H * LANES)
    return jnp.concatenate([main, swp], axis=1).astype(BF16)


def _rope_tables(seq):
    half = MLA_ROPE // 2
    t = jnp.arange(seq)
    inv = ROPE_BASE ** (-jnp.arange(0, half, 2, dtype=F32) / half)
    ar = (t // GRID_W).astype(F32)[:, None] * inv
    ac = (t % GRID_W).astype(F32)[:, None] * inv
    cos = jnp.concatenate([jnp.cos(ar), jnp.cos(ar), jnp.cos(ac), jnp.cos(ac)], axis=1)
    sin = jnp.concatenate([-jnp.sin(ar), jnp.sin(ar), -jnp.sin(ac), jnp.sin(ac)], axis=1)
    cos = jnp.concatenate([jnp.ones((CTX_LEN, MLA_ROPE), F32), cos], axis=0)
    sin = jnp.concatenate([jnp.zeros((CTX_LEN, MLA_ROPE), F32), sin], axis=0)
    pad = jnp.zeros((CTX_LEN + seq, LANES - MLA_ROPE), F32)
    return jnp.concatenate([cos, pad], axis=1), jnp.concatenate([sin, pad], axis=1)


def kernel(x, c, ctx, c_ctx, ada_w, ada_b, norm1_g, norm2_g, w_in, pool_w, pool_scale, gla_gate_w, gla_gate_b, gla_norm_g, mla_q_norm_g, mla_kv_norm_g, mla_w_uq, mla_w_ukv, branch_w, w_out, peer_wq, peer_keys, peer_u, peer_v, final_norm_g):
    B, T, D = x.shape
    R = CTX_LEN + T
    assert D == D_MODEL and ctx.shape[1] == CTX_LEN == TM and T % TM == 0 and T % GRID_W == 0

    nrow = -(-(B + 1) // 8) * 8
    cc = jnp.concatenate([c, c_ctx[None], jnp.zeros((nrow - B - 1, D), F32)], axis=0)
    mod = _ada(cc, ada_w, ada_b)
    modrows = jnp.stack([jnp.broadcast_to(mod[:, B:B + 1], (DEPTH, B, 6 * D)), mod[:, :B]], axis=2)
    modrows = modrows.reshape(DEPTH, 2 * B, 1, 6 * D)

    cos, sin = _rope_tables(T)
    xall = jnp.concatenate([ctx, x], axis=1)
    row = lambda a: a.reshape(1, -1)

    for i in range(DEPTH):
        with_ctx = i < DEPTH - 1
        final = i == DEPTH - 1
        z = _in_proj(xall, modrows[i], row(norm1_g[i]), _layout_w_in(w_in[i]))

        pool_o = _pool(z, pool_w[i].astype(BF16), row(pool_scale[i]))

        o_b = _gla(z, _layout_gate_w(gla_gate_w[i, 1], MISC_LRB), row(gla_gate_b[i, 1]), fwd=False)
        gla_o = _gla(z, _layout_gate_w(gla_gate_w[i, 0], MISC_LRF), row(gla_gate_b[i, 0]), fwd=True,
                     ob=o_b, gn=row(gla_norm_g[i]))

        q, k, v = _mla_up(z, cos, sin, row(mla_q_norm_g[i]), row(mla_kv_norm_g[i]),
                          _layout_w_uq(mla_w_uq[i]), mla_w_ukv[i].astype(BF16))
        mla_o = _attn(q, k, v, with_ctx=with_ctx)

        x1, h2t = _merge(pool_o, gla_o, mla_o, z, xall, modrows[i], branch_w[i].astype(BF16),
                        w_out[i].astype(BF16), row(norm2_g[i]), with_ctx=with_ctx)

        rows = x1.shape[1]
        keys = peer_keys[i].reshape(2 * PEER_HEADS, PEER_N_KEYS, PEER_HALF).astype(BF16)
        sel = _peer_score(h2t, peer_wq[i].T.astype(BF16), keys)
        xall = _peer_dense(h2t, sel, _pack_rows(peer_u[i]), _pack_rows(peer_v[i].T), x1.reshape(B * rows, D),
                           modrows[i], row(final_norm_g), tn=_peer_tile(rows), rows_per_sample=rows,
                           ctx_rows=CTX_LEN if with_ctx else 0, final=final).reshape(B, rows, D)
    return xall
```
